```python
import math
import jax, jax.numpy as jnp
from jax import lax
import numpy as np

D_MODEL = 1024
BATCH = 4
SEQ = 8192
DEPTH = 1

EPS = 1e-6
ROPE_THETA = 10000.0
MOBA_HEADS = 8
MOBA_HEAD_DIM = 64
MOBA_BLOCK = 256
MOBA_TOPK = 3
MOBA_QBLOCK = 128
MOBA_W = MOBA_HEADS * MOBA_HEAD_DIM
SSD_D_INNER = D_MODEL
SSD_HEAD_DIM = 64
SSD_HEADS = SSD_D_INNER // SSD_HEAD_DIM
SSD_GROUPS = 4
SSD_STATE = 128
SSD_CONV = 4
SSD_CHUNK = 256
SSD_BC_W = SSD_GROUPS * SSD_STATE
SSD_XBC_W = SSD_D_INNER + 2 * SSD_BC_W
MEM_LEN = 256
MEM_HEADS = 4
MEM_HEAD_DIM = 128
MEM_W = MEM_HEADS * MEM_HEAD_DIM
N_BRANCH = 3
IN_SIZES = (MOBA_W, MOBA_W, MOBA_W, SSD_D_INNER, SSD_XBC_W, SSD_HEADS, MEM_W, N_BRANCH * D_MODEL)
IN_COLS = sum(IN_SIZES)
PAD_MULT = 256
MOE_GROUPS = 4
MOE_EXPERTS_PER_GROUP = 8
MOE_EXPERTS = MOE_GROUPS * MOE_EXPERTS_PER_GROUP
MOE_TOPK = 2
MOE_D_FF = 512
MOE_BLOCK = 128

kernel_name = 'hybrid_moba_ssd_memory_hmoe_layer'


def rmsnorm(u, g):
    uf = u.astype(jnp.float32)
    r = lax.rsqrt(jnp.mean(uf * uf, axis=-1, keepdims=True) + EPS)
    return (uf * r).astype(u.dtype) * g


def rope(u):
    s_len, d = u.shape[1], u.shape[-1]
    half = d // 2
    inv = ROPE_THETA ** (-jnp.arange(half, dtype=jnp.float32) / half)
    ang = jnp.arange(s_len, dtype=jnp.float32)[:, None] * inv[None, :]
    cos = jnp.cos(ang)[None, :, None, :]
    sin = jnp.sin(ang)[None, :, None, :]
    uf = u.astype(jnp.float32)
    u1, u2 = uf[..., :half], uf[..., half:]
    return jnp.concatenate([u1 * cos - u2 * sin, u2 * cos + u1 * sin], axis=-1).astype(u.dtype)


def moba_attention(q, k, v):
    bsz, s_len, nh, d = q.shape
    L = MOBA_BLOCK
    nb = s_len // L
    scale = d ** -0.5
    q = q.transpose(0, 2, 1, 3)
    kb = k.transpose(0, 2, 1, 3).reshape(bsz, nh, nb, L, d)
    vb = v.transpose(0, 2, 1, 3).reshape(bsz, nh, nb, L, d)
    k_mean = jnp.mean(kb.astype(jnp.float32), axis=3)
    gate = jnp.einsum('bhsd,bhnd->bhsn', q.astype(jnp.float32), k_mean)
    q_blk = jnp.arange(s_len) // L
    past = jnp.arange(nb)[None, :] < q_blk[:, None]
    gate = jnp.where(past[None, None], gate, -jnp.inf)
    n_sel = min(MOBA_TOPK, max(nb - 1, 1))
    g_val, g_idx = lax.top_k(gate, n_sel)
    g_valid = jnp.isfinite(g_val)
    b_ix = jnp.arange(bsz)[:, None, None]
    h_ix = jnp.arange(nh)[None, :, None]
    QB = MOBA_QBLOCK

    def one_block(i):
        t0 = i * QB
        qc = lax.dynamic_slice_in_dim(q, t0, QB, axis=2).astype(jnp.float32) * scale
        ic = lax.dynamic_slice_in_dim(g_idx, t0, QB, axis=2)
        vc = lax.dynamic_slice_in_dim(g_valid, t0, QB, axis=2)
        own = t0 // L
        k_own = lax.dynamic_index_in_dim(kb, own, axis=2, keepdims=False).astype(jnp.float32)
        v_own = lax.dynamic_index_in_dim(vb, own, axis=2, keepdims=False).astype(jnp.float32)
        s_own = jnp.einsum('bhqd,bhkd->bhqk', qc, k_own)
        causal = (own * L + jnp.arange(L))[None, :] <= (t0 + jnp.arange(QB))[:, None]
        s_own = jnp.where(causal[None, None], s_own, -jnp.inf)
        s_parts = []
        for n in range(n_sel):
            k_n = kb[b_ix, h_ix, ic[..., n]].astype(jnp.float32)
            s_n = jnp.einsum('bhqd,bhqkd->bhqk', qc, k_n)
            s_parts.append(jnp.where(vc[..., n, None], s_n, -jnp.inf))
        scores = jnp.concatenate(s_parts + [s_own], axis=-1)
        p = jax.nn.softmax(scores, axis=-1)
        out = jnp.einsum('bhqk,bhkd->bhqd', p[..., n_sel * L:], v_own)
        for n in range(n_sel):
            v_n = vb[b_ix, h_ix, ic[..., n]].astype(jnp.float32)
            out = out + jnp.einsum('bhqk,bhqkd->bhqd', p[..., n * L:(n + 1) * L], v_n)
        return out.astype(q.dtype)

    outs = lax.map(one_block, jnp.arange(s_len // QB))
    return outs.transpose(1, 0, 3, 2, 4).reshape(bsz, s_len, nh * d)


def causal_dwconv(u, w, b):
    kw = w.shape[0]
    out = lax.conv_general_dilated(u, w[:, None, :], window_strides=(1,), padding=[(kw - 1, 0)],
                                   dimension_numbers=('NWC', 'WIO', 'NWC'), feature_group_count=u.shape[-1])
    return out + b


def ssd_scan(x, dt, A, Bm, Cm):
    bsz, s_len, nh, hp = x.shape
    G, N = Bm.shape[2], Bm.shape[3]
    R = nh // G
    Q = SSD_CHUNK
    nc = s_len // Q
    x = x.astype(jnp.float32).reshape(bsz, nc, Q, G, R, hp)
    dt = dt.astype(jnp.float32).reshape(bsz, nc, Q, G, R)
    Bm = Bm.astype(jnp.float32).reshape(bsz, nc, Q, G, N)
    Cm = Cm.astype(jnp.float32).reshape(bsz, nc, Q, G, N)
    a_cum = jnp.cumsum(dt * A.reshape(G, R), axis=2)
    x_dt = x * dt[..., None]
    a_t = a_cum.transpose(0, 1, 3, 4, 2)
    seg = a_t[..., :, None] - a_t[..., None, :]
    tril = jnp.tril(jnp.ones((Q, Q), dtype=bool))
    Lmat = jnp.exp(jnp.where(tril, seg, -jnp.inf))
    cb = jnp.einsum('bcign,bcjgn->bcgij', Cm, Bm)
    y_diag = jnp.einsum('bcgrij,bcjgrp->bcigrp', cb[:, :, :, None] * Lmat, x_dt)
    decay = jnp.exp(a_cum[:, :, -1:] - a_cum)
    states = jnp.einsum('bcjgn,bcjgrp->bcgrpn', Bm, decay[..., None] * x_dt)
    chunk_decay = jnp.exp(a_cum[:, :, -1])

    def step(h, inp):
        st, dec = inp
        return h * dec[..., None, None] + st, h

    h0 = jnp.zeros((bsz, G, R, hp, N), jnp.float32)
    _, h_in = lax.scan(step, h0, (states.transpose(1, 0, 2, 3, 4, 5), chunk_decay.transpose(1, 0, 2, 3)))
    h_in = h_in.transpose(1, 0, 2, 3, 4, 5)
    y_off = jnp.einsum('bcign,bcgrpn->bcigrp', Cm, h_in) * jnp.exp(a_cum)[..., None]
    return (y_diag + y_off).reshape(bsz, s_len, nh * hp)


def memory_attention(qm, mem, g_mem, w_mem_kv, q_gain, k_gain):
    bsz, s_len, _ = qm.shape
    m_len = mem.shape[1]
    kv = rmsnorm(mem, g_mem) @ w_mem_kv
    km = rmsnorm(kv[..., :MEM_W].reshape(bsz, m_len, MEM_HEADS, MEM_HEAD_DIM), k_gain)
    vm = kv[..., MEM_W:].reshape(bsz, m_len, MEM_HEADS, MEM_HEAD_DIM)
    qh = rmsnorm(qm.reshape(bsz, s_len, MEM_HEADS, MEM_HEAD_DIM), q_gain)
    s = jnp.einsum('bshd,bmhd->bhsm', qh.astype(jnp.float32), km.astype(jnp.float32)) * (MEM_HEAD_DIM ** -0.5)
    p = jax.nn.softmax(s, axis=-1)
    o = jnp.einsum('bhsm,bmhd->bshd', p, vm.astype(jnp.float32))
    return o.reshape(bsz, s_len, MEM_W).astype(qm.dtype)


def hierarchical_moe(h2, w_router_group, w_router_expert, w_gate, w_up, w_down):
    n_tok, d = h2.shape
    grp_prob = jax.nn.softmax((h2 @ w_router_group).astype(jnp.float32), axis=-1)
    g_sel = jnp.argmax(grp_prob, axis=-1)
    p_g = jnp.max(grp_prob, axis=-1)
    e_logits = (h2 @ w_router_expert).astype(jnp.float32).reshape(n_tok, MOE_GROUPS, MOE_EXPERTS_PER_GROUP)
    e_logits = jnp.take_along_axis(e_logits, g_sel[:, None, None], axis=1)[:, 0]
    e_prob = jax.nn.softmax(e_logits, axis=-1)
    w_top, e_top = lax.top_k(e_prob, MOE_TOPK)
    w_top = w_top / jnp.sum(w_top, axis=-1, keepdims=True)
    comb = p_g[:, None] * w_top
    expert_id = g_sel[:, None] * MOE_EXPERTS_PER_GROUP + e_top
    A = n_tok * MOE_TOPK
    e_flat = expert_id.reshape(A).astype(jnp.int32)
    tok_flat = jnp.repeat(jnp.arange(n_tok, dtype=jnp.int32), MOE_TOPK)
    w_flat = comb.reshape(A)
    order = jnp.argsort(e_flat)
    e_s, tok_s, w_s = e_flat[order], tok_flat[order], w_flat[order]
    counts = jnp.bincount(e_flat, length=MOE_EXPERTS)
    padded = ((counts + MOE_BLOCK - 1) // MOE_BLOCK) * MOE_BLOCK
    pad_end = jnp.cumsum(padded)
    pad_start = pad_end - padded
    raw_start = jnp.cumsum(counts) - counts
    dest = pad_start[e_s] + jnp.arange(A, dtype=jnp.int32) - raw_start[e_s]
    n_blocks = (A + MOE_BLOCK - 1) // MOE_BLOCK + MOE_EXPERTS
    P = n_blocks * MOE_BLOCK
    buf_tok = jnp.zeros((P,), jnp.int32).at[dest].set(tok_s)
    buf_w = jnp.zeros((P,), h2.dtype).at[dest].set(w_s.astype(h2.dtype))
    blk_start = jnp.arange(n_blocks, dtype=jnp.int32) * MOE_BLOCK
    blk_expert = jnp.minimum(jnp.sum(pad_end[None, :] <= blk_start[:, None], axis=1), MOE_EXPERTS - 1)
    xb = h2[buf_tok].reshape(n_blocks, MOE_BLOCK, d)

    def expert_block(args):
        xi, e = args
        hid = jax.nn.silu(xi @ w_gate[e]) * (xi @ w_up[e])
        return hid @ w_down[e]

    yb = lax.map(expert_block, (xb, blk_expert)).reshape(P, d)
    return jnp.zeros((n_tok, d), h2.dtype).at[buf_tok].add(yb * buf_w[:, None])


def setup_inputs(seed: int = 0) -> dict:
    key = jax.random.key(seed)
    ks = jax.random.split(key, 32)
    f32 = jnp.float32

    def nrm(k, shape, fan_in):
        return jax.random.normal(k, shape, f32) * fan_in ** -0.5

    def gain(k, n):
        return 1.0 + 0.02 * jax.random.normal(k, (n,), f32)

    dt0 = jnp.exp(jax.random.uniform(ks[8], (SSD_HEADS,), f32) * (math.log(0.1) - math.log(0.001)) + math.log(0.001))
    return {
        'x': jax.random.normal(ks[0], (BATCH, SEQ, D_MODEL), f32),
        'mem': jax.random.normal(ks[1], (BATCH, MEM_LEN, D_MODEL), f32),
        'g_mix': gain(ks[2], D_MODEL),
        'w_in': nrm(ks[3], (D_MODEL, IN_COLS), D_MODEL),
        'moba_q_norm': gain(ks[4], MOBA_HEAD_DIM),
        'moba_k_norm': gain(ks[5], MOBA_HEAD_DIM),
        'conv_w': nrm(ks[6], (SSD_CONV, SSD_XBC_W), SSD_CONV),
        'conv_b': 0.01 * jax.random.normal(ks[7], (SSD_XBC_W,), f32),
        'dt_bias': dt0 + jnp.log(-jnp.expm1(-dt0)),
        'a_log': jnp.log(jax.random.uniform(ks[9], (SSD_HEADS,), f32, minval=1.0, maxval=16.0)),
        'd_skip': 1.0 + 0.1 * jax.random.normal(ks[10], (SSD_HEADS,), f32),
        'ssd_norm': gain(ks[11], SSD_D_INNER),
        'g_mem': gain(ks[12], D_MODEL),
        'w_mem_kv': nrm(ks[13], (D_MODEL, 2 * MEM_W), D_MODEL),
        'mem_q_norm': gain(ks[14], MEM_HEAD_DIM),
        'mem_k_norm': gain(ks[15], MEM_HEAD_DIM),
        'w_o_moba': nrm(ks[16], (MOBA_W, D_MODEL), MOBA_W),
        'w_o_ssd': nrm(ks[17], (SSD_D_INNER, D_MODEL), SSD_D_INNER),
        'w_o_mem': nrm(ks[18], (MEM_W, D_MODEL), MEM_W),
        'w_out': nrm(ks[19], (D_MODEL, D_MODEL), D_MODEL),
        'g_ffn': gain(ks[20], D_MODEL),
        'w_router_group': nrm(ks[21], (D_MODEL, MOE_GROUPS), D_MODEL),
        'w_router_expert': nrm(ks[22], (D_MODEL, MOE_EXPERTS), D_MODEL),
        'w_gate': nrm(ks[23], (MOE_EXPERTS, D_MODEL, MOE_D_FF), D_MODEL),
        'w_up': nrm(ks[24], (MOE_EXPERTS, D_MODEL, MOE_D_FF), D_MODEL),
        'w_down': nrm(ks[25], (MOE_EXPERTS, MOE_D_FF, D_MODEL), MOE_D_FF),
    }


def reference(x, mem, g_mix, w_in, moba_q_norm, moba_k_norm, conv_w, conv_b, dt_bias, a_log, d_skip,
              ssd_norm, g_mem, w_mem_kv, mem_q_norm, mem_k_norm, w_o_moba, w_o_ssd, w_o_mem, w_out,
              g_ffn, w_router_group, w_router_expert, w_gate, w_up, w_down):
    bsz, s_len, d = x.shape
    s_pad = ((s_len + PAD_MULT - 1) // PAD_MULT) * PAD_MULT
    for _layer in range(DEPTH):
        h = rmsnorm(x, g_mix)
        h = jnp.pad(h, ((0, 0), (0, s_pad - s_len), (0, 0)))
        proj = h @ w_in
        offs = np.cumsum((0,) + IN_SIZES)
        parts = [proj[..., int(offs[i]):int(offs[i + 1])] for i in range(len(IN_SIZES))]
        q_a, k_a, v_a, z, xbc, dt_raw, q_m, gate_logits = parts
        q_a = rope(rmsnorm(q_a.reshape(bsz, s_pad, MOBA_HEADS, MOBA_HEAD_DIM), moba_q_norm))
        k_a = rope(rmsnorm(k_a.reshape(bsz, s_pad, MOBA_HEADS, MOBA_HEAD_DIM), moba_k_norm))
        v_a = v_a.reshape(bsz, s_pad, MOBA_HEADS, MOBA_HEAD_DIM)
        o_a = moba_attention(q_a, k_a, v_a)
        xbc = jax.nn.silu(causal_dwconv(xbc, conv_w, conv_b))
        xs = xbc[..., :SSD_D_INNER].reshape(bsz, s_pad, SSD_HEADS, SSD_HEAD_DIM)
        Bm = xbc[..., SSD_D_INNER:SSD_D_INNER + SSD_BC_W].reshape(bsz, s_pad, SSD_GROUPS, SSD_STATE)
        Cm = xbc[..., SSD_D_INNER + SSD_BC_W:].reshape(bsz, s_pad, SSD_GROUPS, SSD_STATE)
        dt = jax.nn.softplus(dt_raw.astype(jnp.float32) + dt_bias.astype(jnp.float32))
        A = -jnp.exp(a_log.astype(jnp.float32))
        y = ssd_scan(xs, dt, A, Bm, Cm)
        y = y + (d_skip.astype(jnp.float32)[:, None] * xs.astype(jnp.float32)).reshape(bsz, s_pad, SSD_D_INNER)
        y = y * jax.nn.silu(z.astype(jnp.float32))
        y = rmsnorm(y.reshape(bsz, s_pad, SSD_GROUPS, SSD_D_INNER // SSD_GROUPS),
                    ssd_norm.astype(jnp.float32).reshape(SSD_GROUPS, SSD_D_INNER // SSD_GROUPS))
        o_s = y.reshape(bsz, s_pad, SSD_D_INNER).astype(h.dtype)
        o_m = memory_attention(q_m, mem, g_mem, w_mem_kv, mem_q_norm, mem_k_norm)
        gates = jax.nn.sigmoid(gate_logits.astype(jnp.float32)).astype(h.dtype).reshape(bsz, s_pad, N_BRANCH, d)
        merged = gates[:, :, 0] * (o_a @ w_o_moba) + gates[:, :, 1] * (o_s @ w_o_ssd) + gates[:, :, 2] * (o_m @ w_o_mem)
        x = x + (merged @ w_out)[:, :s_len]
        h2 = rmsnorm(x, g_ffn).reshape(bsz * s_len, d)
        x = x + hierarchical_moe(h2, w_router_group, w_router_expert, w_gate, w_up, w_down).reshape(bsz, s_len, d)
    return x
```

```python
import functools

import jax
import jax.numpy as jnp
from jax import lax
from jax.experimental import pallas as pl
from jax.experimental.pallas import tpu as pltpu

F32 = jnp.float32
BF16 = jnp.bfloat16
HIGHEST = lax.Precision.HIGHEST

EPS = 1e-6
ROPE_THETA = 10000.0
MOBA_HEADS = 8
MOBA_HEAD_DIM = 64
MOBA_BLOCK = 256
MOBA_TOPK = 3
SSD_HEAD_DIM = 64
SSD_HEADS = 16
SSD_GROUPS = 4
SSD_STATE = 128
SSD_CONV = 4
SSD_CHUNK = 256
MEM_HEADS = 4
MEM_HEAD_DIM = 128
MOE_GROUPS = 4
MOE_EXPERTS_PER_GROUP = 8
MOE_EXPERTS = MOE_GROUPS * MOE_EXPERTS_PER_GROUP
MOE_TOPK = 2

LANES = 128
MASKED = -1e30
MOE_ROWS = 256
VMEM_LIMIT = 56 * 1024 * 1024


def _params(*sem):
    return pltpu.CompilerParams(dimension_semantics=sem, vmem_limit_bytes=VMEM_LIMIT)


def _sigmoid(x):
    return 1.0 / (1.0 + jnp.exp(-x))


def _nt_dot(a, b, precision=None):
    return lax.dot_general(a, b, (((1,), (1,)), ((), ())), precision=precision, preferred_element_type=F32)


def _rmsnorm_kernel(x_ref, g_ref, o_ref):
    x = x_ref[...]
    r = lax.rsqrt(jnp.mean(x * x, axis=-1, keepdims=True) + EPS)
    o_ref[...] = ((x * r) * g_ref[...]).astype(o_ref.dtype)


def _rmsnorm(x2, g, tm=1024):
    n, d = x2.shape
    return pl.pallas_call(
        _rmsnorm_kernel,
        grid=(n // tm,),
        in_specs=[pl.BlockSpec((tm, d), lambda i: (i, 0)), pl.BlockSpec((1, d), lambda i: (0, 0))],
        out_specs=pl.BlockSpec((tm, d), lambda i: (i, 0)),
        out_shape=jax.ShapeDtypeStruct((n, d), BF16),
        compiler_params=_params("parallel"),
        name="rmsnorm",
    )(x2, g.reshape(1, d))


def _proj_kernel(h_ref, w_ref, o_ref):
    o_ref[...] = jnp.dot(h_ref[...], w_ref[...], preferred_element_type=F32).astype(o_ref.dtype)


def _proj(h, w, out_dtype, tm=1024, tn=512, name="proj"):
    n, k = h.shape
    c = w.shape[1]
    tn = min(tn, c)
    return pl.pallas_call(
        _proj_kernel,
        grid=(n // tm, c // tn),
        in_specs=[pl.BlockSpec((tm, k), lambda i, j: (i, 0)), pl.BlockSpec((k, tn), lambda i, j: (0, j))],
        out_specs=pl.BlockSpec((tm, tn), lambda i, j: (i, j)),
        out_shape=jax.ShapeDtypeStruct((n, c), out_dtype),
        compiler_params=_params("parallel", "arbitrary"),
        name=name,
    )(h, w)


def _qk_kernel(h_ref, w_ref, gain_ref, gsum_ref, cos_ref, sin_ref, o_ref):
    u = jnp.dot(h_ref[...], w_ref[...], preferred_element_type=F32)
    sq = u * u
    hi = sq.astype(BF16)
    lo = (sq - hi.astype(F32)).astype(BF16)
    ss = jnp.dot(hi, gsum_ref[...], preferred_element_type=F32) + jnp.dot(lo, gsum_ref[...], preferred_element_type=F32)
    r = lax.rsqrt(ss * (1.0 / MOBA_HEAD_DIM) + EPS)
    un = (u * r) * gain_ref[0]
    width = un.shape[1]
    half = MOBA_HEAD_DIM // 2
    lane = lax.broadcasted_iota(jnp.int32, un.shape, 1)
    first = (lane & (MOBA_HEAD_DIM - 1)) < half
    partner = jnp.where(first, pltpu.roll(un, width - half, axis=1), pltpu.roll(un, half, axis=1))
    reps = width // cos_ref.shape[1]
    cos = jnp.concatenate([cos_ref[...]] * reps, axis=1)
    sin = jnp.concatenate([sin_ref[...]] * reps, axis=1)
    o_ref[...] = un * cos + partner * sin


def _qk_proj(h, w_qk, gains, cos_t, sin_t, seq, tm=512):
    n, k = h.shape
    c = w_qk.shape[1]
    tn = c // 2
    tiles_per_seq = seq // tm
    head = jnp.arange(tn) // MOBA_HEAD_DIM
    gsum = (head[:, None] == head[None, :]).astype(BF16)
    return pl.pallas_call(
        _qk_kernel,
        grid=(n // tm, 2),
        in_specs=[
            pl.BlockSpec((tm, k), lambda i, j: (i, 0)),
            pl.BlockSpec((k, tn), lambda i, j: (0, j)),
            pl.BlockSpec((1, 1, tn), lambda i, j: (j, 0, 0)),
            pl.BlockSpec((tn, tn), lambda i, j: (0, 0)),
            pl.BlockSpec((tm, LANES), lambda i, j: (i % tiles_per_seq, 0)),
            pl.BlockSpec((tm, LANES), lambda i, j: (i % tiles_per_seq, 0)),
        ],
        out_specs=pl.BlockSpec((tm, tn), lambda i, j: (i, j)),
        out_shape=jax.ShapeDtypeStruct((n, c), F32),
        compiler_params=_params("parallel", "arbitrary"),
        name="qk_proj",
    )(h, w_qk, gains, gsum, cos_t, sin_t)


def _kmean_kernel(k_ref, km_ref, kb_ref):
    k = k_ref[0]
    km_ref[0, 0] = jnp.mean(k, axis=0, keepdims=True)
    kb_ref[0] = k.astype(BF16)


def _kmean(k3):
    b, s, w = k3.shape
    nb = s // MOBA_BLOCK
    return pl.pallas_call(
        _kmean_kernel,
        grid=(b, nb),
        in_specs=[pl.BlockSpec((1, MOBA_BLOCK, w), lambda bi, n: (bi, n, 0))],
        out_specs=[
            pl.BlockSpec((1, 1, 1, w), lambda bi, n: (bi, n, 0, 0)),
            pl.BlockSpec((1, MOBA_BLOCK, w), lambda bi, n: (bi, n, 0)),
        ],
        out_shape=[jax.ShapeDtypeStruct((b, nb, 1, w), F32), jax.ShapeDtypeStruct((b, s, w), BF16)],
        compiler_params=_params("parallel", "parallel"),
        name="moba_kmean",
    )(k3)


def _moba_kernel(q_ref, k_ref, v_ref, km_ref, o_ref):
    blk = MOBA_BLOCK
    i = pl.program_id(2)
    q = q_ref[0]
    kmean = km_ref[0]
    lane = lax.broadcasted_iota(jnp.int32, (blk, LANES), 1)
    lanef = lane.astype(F32)
    row = lax.broadcasted_iota(jnp.int32, (blk, blk), 0)
    col = lax.broadcasted_iota(jnp.int32, (blk, blk), 1)
    causal = col <= row
    scale = MOBA_HEAD_DIM ** -0.5
    zeros_oh = jnp.zeros((blk, LANES), BF16)
    outs = []
    for hh in range(2):
        hmask = (lane >= hh * MOBA_HEAD_DIM) & (lane < (hh + 1) * MOBA_HEAD_DIM)
        qh = jnp.where(hmask, q, 0.0)
        gate = _nt_dot(qh, kmean, precision=HIGHEST)
        g = jnp.where(lane < i, gate, -jnp.inf)
        bias = jnp.full((blk, LANES), MASKED, F32)
        for _ in range(MOBA_TOPK):
            m = jnp.max(g, axis=-1, keepdims=True)
            hit = (g == m) & (m > -jnp.inf)
            first = jnp.min(jnp.where(hit, lanef, float(LANES)), axis=-1, keepdims=True)
            sel = lanef == first
            bias = jnp.where(sel, 0.0, bias)
            g = jnp.where(sel, -jnp.inf, g)
        qa = jnp.concatenate([(qh * scale).astype(BF16), bias.astype(BF16)], axis=1)

        start = pl.multiple_of(i * blk, blk)
        ka = jnp.concatenate([k_ref[0, pl.ds(start, blk), :], zeros_oh], axis=1)
        s = jnp.where(causal, _nt_dot(qa, ka), MASKED)
        m0 = jnp.max(s, axis=-1, keepdims=True)
        p = jnp.exp(s - m0)
        l0 = jnp.sum(p, axis=-1, keepdims=True)
        acc0 = jnp.dot(p.astype(BF16), v_ref[0, pl.ds(start, blk), :], preferred_element_type=F32)

        def past_block(n, carry, qa=qa):
            m_run, l_run, acc = carry
            st = pl.multiple_of(n * blk, blk)
            onehot = jnp.where(lane == n, 1.0, 0.0).astype(BF16)
            ka_n = jnp.concatenate([k_ref[0, pl.ds(st, blk), :], onehot], axis=1)
            s_n = _nt_dot(qa, ka_n)
            m_new = jnp.maximum(m_run, jnp.max(s_n, axis=-1, keepdims=True))
            alpha = jnp.exp(m_run - m_new)
            p_n = jnp.exp(s_n - m_new)
            l_new = alpha * l_run + jnp.sum(p_n, axis=-1, keepdims=True)
            acc_new = alpha * acc + jnp.dot(p_n.astype(BF16), v_ref[0, pl.ds(st, blk), :], preferred_element_type=F32)
            return m_new, l_new, acc_new

        _, l_fin, acc_fin = lax.fori_loop(0, i, past_block, (m0, l0, acc0))
        outs.append(acc_fin / l_fin)
    o_ref[0] = jnp.where(lane < MOBA_HEAD_DIM, outs[0], outs[1]).astype(o_ref.dtype)


def _moba_attention(q3, kb3, v3, kmean_pad):
    b, s, w = q3.shape
    nq = s // MOBA_BLOCK
    pairs = w // LANES
    return pl.pallas_call(
        _moba_kernel,
        grid=(b, pairs, nq),
        in_specs=[
            pl.BlockSpec((1, MOBA_BLOCK, LANES), lambda bi, p, i: (bi, i, p)),
            pl.BlockSpec((1, s, LANES), lambda bi, p, i: (bi, 0, p)),
            pl.BlockSpec((1, s, LANES), lambda bi, p, i: (bi, 0, p)),
            pl.BlockSpec((1, LANES, LANES), lambda bi, p, i: (bi, 0, p)),
        ],
        out_specs=pl.BlockSpec((1, MOBA_BLOCK, LANES), lambda bi, p, i: (bi, i, p)),
        out_shape=jax.ShapeDtypeStruct((b, s, w), BF16),
        compiler_params=_params("parallel", "parallel", "arbitrary"),
        name="moba_attention",
    )(q3, kb3, v3, kmean_pad)


def _ssd_kernel(xbc_ref, z_ref, dt_ref, cw_ref, cb_ref, dtb_ref, alog_ref, dskip_ref, gn_ref, ex_ref, o_ref,
                state_ref, ext_ref):
    q = SSD_CHUNK
    d_in = SSD_HEADS * SSD_HEAD_DIM
    bc_w = SSD_GROUPS * SSD_STATE
    gw = d_in // SSD_GROUPS
    c = pl.program_id(1)

    @pl.when(c == 0)
    def _():
        state_ref[...] = jnp.zeros_like(state_ref)
        ext_ref[0:8, :] = jnp.zeros((8, ext_ref.shape[1]), F32)

    u = xbc_ref[0]
    ext_ref[8:8 + q, :] = u
    acc = cb_ref[...] + cw_ref[3:4, :] * u
    for kk in range(SSD_CONV - 1):
        off = 8 - (SSD_CONV - 1 - kk)
        acc = acc + cw_ref[kk:kk + 1, :] * ext_ref[off:off + q, :]
    ext_ref[0:8, :] = u[q - 8:q, :]
    act = acc * _sigmoid(acc)
    xs = act[:, :d_in]
    bm = act[:, d_in:d_in + bc_w]
    cm = act[:, d_in + bc_w:]

    dt_in = dt_ref[0] + dtb_ref[...]
    dt = jnp.maximum(dt_in, 0.0) + jnp.log1p(jnp.exp(-jnp.abs(dt_in)))
    a = dt * (-jnp.exp(alog_ref[...]))
    row = lax.broadcasted_iota(jnp.int32, (q, q), 0)
    col = lax.broadcasted_iota(jnp.int32, (q, q), 1)
    tril = col <= row
    a_cum = jnp.dot(tril.astype(F32), a, precision=HIGHEST, preferred_element_type=F32)
    a_cum_t = a_cum.T
    ex = ex_ref[...]
    dt_x = jnp.dot(dt, ex, precision=HIGHEST, preferred_element_type=F32)
    acum_x = jnp.dot(a_cum, ex, precision=HIGHEST, preferred_element_type=F32)
    alast_x = acum_x[q - 1:q, :]
    x_dt = xs * dt_x
    xd = (x_dt * jnp.exp(alast_x - acum_x)).astype(BF16)
    x_dt_b = x_dt.astype(BF16)
    e_acum = jnp.exp(acum_x)
    e_alast = jnp.exp(alast_x)
    lane2 = lax.broadcasted_iota(jnp.int32, (q, 2 * SSD_HEAD_DIM), 1)
    heads_per_group = SSD_HEADS // SSD_GROUPS

    for g in range(SSD_GROUPS):
        bg = bm[:, g * SSD_STATE:(g + 1) * SSD_STATE]
        cg = cm[:, g * SSD_STATE:(g + 1) * SSD_STATE].astype(BF16)
        cb = _nt_dot(cg, bg.astype(BF16))
        h_in = state_ref[g]
        y_g = jnp.dot(cg, h_in.astype(BF16), preferred_element_type=F32) * e_acum[:, g * gw:(g + 1) * gw]
        parts = []
        for pr in range(heads_per_group // 2):
            xp = x_dt_b[:, g * gw + pr * 2 * SSD_HEAD_DIM:g * gw + (pr + 1) * 2 * SSD_HEAD_DIM]
            ys = []
            for hh in range(2):
                hd = g * heads_per_group + pr * 2 + hh
                seg = a_cum[:, hd:hd + 1] - a_cum_t[hd:hd + 1, :]
                lmat = jnp.exp(jnp.where(tril, seg, -jnp.inf))
                ys.append(jnp.dot((cb * lmat).astype(BF16), xp, preferred_element_type=F32))
            parts.append(jnp.where(lane2 < SSD_HEAD_DIM, ys[0], ys[1]))
        y_g = y_g + jnp.concatenate(parts, axis=1)
        st = jnp.dot(bg.T.astype(BF16), xd[:, g * gw:(g + 1) * gw], preferred_element_type=F32)
        state_ref[g] = h_in * e_alast[:, g * gw:(g + 1) * gw] + st

        sl = slice(g * gw, (g + 1) * gw)
        y_g = y_g + dskip_ref[:, sl] * xs[:, sl]
        zz = z_ref[0, :, sl]
        y_g = y_g * (zz * _sigmoid(zz))
        r = lax.rsqrt(jnp.mean(y_g * y_g, axis=-1, keepdims=True) + EPS)
        o_ref[0, :, sl] = ((y_g * r) * gn_ref[:, sl]).astype(o_ref.dtype)


def _ssd(xbc3, z3, dt3, conv_w, conv_b, dt_bias, a_log, d_skip, ssd_norm):
    b, s, xw = xbc3.shape
    d_in = SSD_HEADS * SSD_HEAD_DIM
    nc = s // SSD_CHUNK
    pad = LANES - SSD_HEADS
    dtb = jnp.pad(dt_bias.astype(F32), (0, pad)).reshape(1, LANES)
    alog = jnp.pad(a_log.astype(F32), (0, pad)).reshape(1, LANES)
    dskip = jnp.repeat(d_skip.astype(F32), SSD_HEAD_DIM).reshape(1, d_in)
    expand = (jnp.arange(LANES)[:, None] == (jnp.arange(d_in) // SSD_HEAD_DIM)[None, :]).astype(F32)
    const = lambda shape: pl.BlockSpec(shape, lambda bi, ci: (0,) * len(shape))
    return pl.pallas_call(
        _ssd_kernel,
        grid=(b, nc),
        in_specs=[
            pl.BlockSpec((1, SSD_CHUNK, xw), lambda bi, ci: (bi, ci, 0)),
            pl.BlockSpec((1, SSD_CHUNK, d_in), lambda bi, ci: (bi, ci, 0)),
            pl.BlockSpec((1, SSD_CHUNK, LANES), lambda bi, ci: (bi, ci, 0)),
            const((SSD_CONV, xw)), const((1, xw)), const((1, LANES)), const((1, LANES)),
            const((1, d_in)), const((1, d_in)), const((LANES, d_in)),
        ],
        out_specs=pl.BlockSpec((1, SSD_CHUNK, d_in), lambda bi, ci: (bi, ci, 0)),
        out_shape=jax.ShapeDtypeStruct((b, s, d_in), BF16),
        scratch_shapes=[
            pltpu.VMEM((SSD_GROUPS, SSD_STATE, d_in // SSD_GROUPS), F32),
            pltpu.VMEM((SSD_CHUNK + 8, xw), F32),
        ],
        compiler_params=_params("parallel", "arbitrary"),
        name="ssd_scan",
    )(xbc3, z3, dt3, conv_w, conv_b.reshape(1, xw), dtb, alog, dskip, ssd_norm.reshape(1, d_in), expand)


def _memkv_kernel(mem_ref, g_ref, w_ref, kg_ref, km_ref, vm_ref):
    m = mem_ref[0]
    r = lax.rsqrt(jnp.mean(m * m, axis=-1, keepdims=True) + EPS)
    mn = ((m * r) * g_ref[...]).astype(BF16)
    kv = jnp.dot(mn, w_ref[...], preferred_element_type=F32)
    mw = MEM_HEADS * MEM_HEAD_DIM
    for hd in range(MEM_HEADS):
        sl = slice(hd * MEM_HEAD_DIM, (hd + 1) * MEM_HEAD_DIM)
        kh = kv[:, sl]
        rk = lax.rsqrt(jnp.mean(kh * kh, axis=-1, keepdims=True) + EPS)
        km_ref[0, :, sl] = ((kh * rk) * kg_ref[...]).astype(km_ref.dtype)
    vm_ref[0] = kv[:, mw:].astype(vm_ref.dtype)


def _memkv(mem, g_mem, w_kv, k_gain):
    b, m, d = mem.shape
    mw = MEM_HEADS * MEM_HEAD_DIM
    return pl.pallas_call(
        _memkv_kernel,
        grid=(b,),
        in_specs=[
            pl.BlockSpec((1, m, d), lambda bi: (bi, 0, 0)),
            pl.BlockSpec((1, d), lambda bi: (0, 0)),
            pl.BlockSpec((d, 2 * mw), lambda bi: (0, 0)),
            pl.BlockSpec((1, MEM_HEAD_DIM), lambda bi: (0, 0)),
        ],
        out_specs=[pl.BlockSpec((1, m, mw), lambda bi: (bi, 0, 0)), pl.BlockSpec((1, m, mw), lambda bi: (bi, 0, 0))],
        out_shape=[jax.ShapeDtypeStruct((b, m, mw), BF16), jax.ShapeDtypeStruct((b, m, mw), BF16)],
        compiler_params=_params("parallel"),
        name="mem_kv",
    )(mem, g_mem.reshape(1, d), w_kv, k_gain.reshape(1, MEM_HEAD_DIM))


def _memattn_kernel(h_ref, w_ref, qg_ref, km_ref, vm_ref, o_ref):
    qm = jnp.dot(h_ref[...], w_ref[...], preferred_element_type=F32)
    scale = MEM_HEAD_DIM ** -0.5
    for hd in range(MEM_HEADS):
        sl = slice(hd * MEM_HEAD_DIM, (hd + 1) * MEM_HEAD_DIM)
        qh = qm[:, sl]
        r = lax.rsqrt(jnp.mean(qh * qh, axis=-1, keepdims=True) + EPS)
        qn = ((qh * r) * qg_ref[...]).astype(BF16)
        s = _nt_dot(qn, km_ref[0, :, sl]) * scale
        p = jnp.exp(s - jnp.max(s, axis=-1, keepdims=True))
        l = jnp.sum(p, axis=-1, keepdims=True)
        o = jnp.dot(p.astype(BF16), vm_ref[0, :, sl], preferred_element_type=F32)
        o_ref[:, sl] = (o / l).astype(o_ref.dtype)


def _memattn(h, w_qm, q_gain, km, vm, seq, tm=512):
    n, d = h.shape
    b, m, mw = km.shape
    tiles_per_seq = seq // tm
    return pl.pallas_call(
        _memattn_kernel,
        grid=(n // tm,),
        in_specs=[
            pl.BlockSpec((tm, d), lambda i: (i, 0)),
            pl.BlockSpec((d, mw), lambda i: (0, 0)),
            pl.BlockSpec((1, MEM_HEAD_DIM), lambda i: (0, 0)),
            pl.BlockSpec((1, m, mw), lambda i: (i // tiles_per_seq, 0, 0)),
            pl.BlockSpec((1, m, mw), lambda i: (i // tiles_per_seq, 0, 0)),
        ],
        out_specs=pl.BlockSpec((tm, mw), lambda i: (i, 0)),
        out_shape=jax.ShapeDtypeStruct((n, mw), BF16),
        compiler_params=_params("parallel"),
        name="mem_attention",
    )(h, w_qm, q_gain.reshape(1, MEM_HEAD_DIM), km, vm)


def _merge_kernel(x_ref, h_ref, oa_ref, os_ref, om_ref, wg_ref, wa_ref, ws_ref, wm_ref, wo_ref, gf_ref, wr_ref,
                  x1_ref, h2_ref, route_ref):
    d = x_ref.shape[1]
    gates = _sigmoid(jnp.dot(h_ref[...], wg_ref[...], preferred_element_type=F32))
    merged = gates[:, :d] * jnp.dot(oa_ref[...], wa_ref[...], preferred_element_type=F32)
    merged = merged + gates[:, d:2 * d] * jnp.dot(os_ref[...], ws_ref[...], preferred_element_type=F32)
    merged = merged + gates[:, 2 * d:] * jnp.dot(om_ref[...], wm_ref[...], preferred_element_type=F32)
    x1 = x_ref[...] + jnp.dot(merged.astype(BF16), wo_ref[...], preferred_element_type=F32)
    x1_ref[...] = x1
    r = lax.rsqrt(jnp.mean(x1 * x1, axis=-1, keepdims=True) + EPS)
    h2 = (x1 * r) * gf_ref[...]
    h2_ref[...] = h2

    lg = jnp.dot(h2, wr_ref[...], precision=HIGHEST, preferred_element_type=F32)
    lanef = lax.broadcasted_iota(jnp.int32, lg.shape, 1).astype(F32)
    big = float(LANES)
    gmask = lanef < MOE_GROUPS
    gl = jnp.where(gmask, lg, -jnp.inf)
    gmax = jnp.max(gl, axis=-1, keepdims=True)
    p_g = 1.0 / jnp.sum(jnp.exp(gl - gmax), axis=-1, keepdims=True)
    g_sel = jnp.min(jnp.where(gl == gmax, lanef, big), axis=-1, keepdims=True)
    lo = MOE_GROUPS + MOE_EXPERTS_PER_GROUP * g_sel
    el = jnp.where((lanef >= lo) & (lanef < lo + MOE_EXPERTS_PER_GROUP), lg, -jnp.inf)
    m1 = jnp.max(el, axis=-1, keepdims=True)
    i1 = jnp.min(jnp.where(el == m1, lanef, big), axis=-1, keepdims=True)
    el2 = jnp.where(lanef == i1, -jnp.inf, el)
    m2 = jnp.max(el2, axis=-1, keepdims=True)
    i2 = jnp.min(jnp.where(el2 == m2, lanef, big), axis=-1, keepdims=True)
    e2 = jnp.exp(m2 - m1)
    w1 = 1.0 / (1.0 + e2)
    w2 = e2 / (1.0 + e2)
    route = jnp.where(lanef == 0, i1 - MOE_GROUPS, 0.0)
    route = jnp.where(lanef == 1, i2 - MOE_GROUPS, route)
    route = jnp.where(lanef == 2, p_g * w1, route)
    route = jnp.where(lanef == 3, p_g * w2, route)
    route_ref[...] = route


def _merge(x2, h, o_a, o_s, o_m, w_gates, w_a, w_s, w_m, w_out, g_ffn, w_router, tm=512):
    n, d = x2.shape
    full = lambda arr: pl.BlockSpec(arr.shape, lambda i: (0,) * arr.ndim)
    tile = lambda arr: pl.BlockSpec((tm, arr.shape[1]), lambda i: (i, 0))
    gf = g_ffn.reshape(1, d)
    return pl.pallas_call(
        _merge_kernel,
        grid=(n // tm,),
        in_specs=[tile(x2), tile(h), tile(o_a), tile(o_s), tile(o_m), full(w_gates), full(w_a), full(w_s), full(w_m),
                  full(w_out), full(gf), full(w_router)],
        out_specs=[pl.BlockSpec((tm, d), lambda i: (i, 0)), pl.BlockSpec((tm, d), lambda i: (i, 0)),
                   pl.BlockSpec((tm, LANES), lambda i: (i, 0))],
        out_shape=[jax.ShapeDtypeStruct((n, d), F32), jax.ShapeDtypeStruct((n, d), F32),
                   jax.ShapeDtypeStruct((n, LANES), F32)],
        compiler_params=_params("parallel"),
        name="merge_router",
    )(x2, h, o_a, o_s, o_m, w_gates, w_a, w_s, w_m, w_out, gf, w_router)


def _moe_kernel(be_ref, ba_ref, nu_ref, h2_hbm, wg_ref, wu_ref, wd_ref, y2_hbm, xbuf, ybuf, gsem, ssem):
    rows = xbuf.shape[1]
    nblocks = pl.num_programs(0)
    i = pl.program_id(0)
    nused = nu_ref[0]
    slot = lax.rem(i, 2)

    def gather_copy(blk, sl, r):
        a = ba_ref[blk * rows + r]
        tok = lax.shift_right_arithmetic(jnp.maximum(a, 0), 1)
        return pltpu.make_async_copy(h2_hbm.at[pl.ds(tok, 1)], xbuf.at[sl, pl.ds(r, 1)], gsem.at[sl])

    def scatter_copy(blk, sl, r):
        a = ba_ref[blk * rows + r]
        return a, pltpu.make_async_copy(ybuf.at[sl, pl.ds(r, 1)], y2_hbm.at[pl.ds(jnp.maximum(a, 0), 1)], ssem.at[sl])

    def gather_start(blk, sl):
        def body(r, carry):
            gather_copy(blk, sl, r).start()
            return carry
        lax.fori_loop(0, rows, body, 0)

    def gather_wait(blk, sl):
        def body(r, carry):
            gather_copy(blk, sl, r).wait()
            return carry
        lax.fori_loop(0, rows, body, 0)

    def scatter_start(blk, sl):
        def body(r, carry):
            a, cp = scatter_copy(blk, sl, r)

            @pl.when(a >= 0)
            def _():
                cp.start()
            return carry
        lax.fori_loop(0, rows, body, 0)

    def scatter_wait(blk, sl):
        def body(r, carry):
            a, cp = scatter_copy(blk, sl, r)

            @pl.when(a >= 0)
            def _():
                cp.wait()
            return carry
        lax.fori_loop(0, rows, body, 0)

    @pl.when(i == 0)
    def _():
        gather_start(0, 0)

    @pl.when(i + 1 < nused)
    def _():
        gather_start(i + 1, 1 - slot)

    @pl.when(i < nused)
    def _():
        gather_wait(i, slot)

        @pl.when(i >= 2)
        def _():
            scatter_wait(i - 2, slot)

        x = xbuf[slot].astype(BF16)
        gate = jnp.dot(x, wg_ref[0], preferred_element_type=F32)
        up = jnp.dot(x, wu_ref[0], preferred_element_type=F32)
        hid = (gate * _sigmoid(gate)) * up
        ybuf[slot] = jnp.dot(hid.astype(BF16), wd_ref[0], preferred_element_type=F32)
        scatter_start(i, slot)

    @pl.when(i == nblocks - 1)
    def _():
        @pl.when(nused >= 2)
        def _():
            scatter_wait(nused - 2, lax.rem(nused, 2))

        scatter_wait(nused - 1, lax.rem(nused - 1, 2))


def _moe_experts(h2, blk_expert, buf_a, nused, w_gate, w_up, w_down):
    n, d = h2.shape
    e, _, f = w_gate.shape
    nblocks = blk_expert.shape[0]
    grid_spec = pltpu.PrefetchScalarGridSpec(
        num_scalar_prefetch=3,
        grid=(nblocks,),
        in_specs=[
            pl.BlockSpec(memory_space=pl.ANY),
            pl.BlockSpec((1, d, f), lambda i, be, ba, nu: (be[i], 0, 0)),
            pl.BlockSpec((1, d, f), lambda i, be, ba, nu: (be[i], 0, 0)),
            pl.BlockSpec((1, f, d), lambda i, be, ba, nu: (be[i], 0, 0)),
        ],
        out_specs=pl.BlockSpec(memory_space=pl.ANY),
        scratch_shapes=[
            pltpu.VMEM((2, MOE_ROWS, d), F32),
            pltpu.VMEM((2, MOE_ROWS, d), F32),
            pltpu.SemaphoreType.DMA((2,)),
            pltpu.SemaphoreType.DMA((2,)),
        ],
    )
    return pl.pallas_call(
        _moe_kernel,
        grid_spec=grid_spec,
        out_shape=jax.ShapeDtypeStruct((n * MOE_TOPK, d), F32),
        compiler_params=_params("arbitrary"),
        name="moe_experts",
    )(blk_expert, buf_a, nused, h2, w_gate, w_up, w_down)


def _combine_kernel(x1_ref, y_ref, route_ref, o_ref):
    d = x1_ref.shape[1]
    route = route_ref[...]
    o_ref[...] = x1_ref[...] + (route[:, 2:3] * y_ref[:, :d] + route[:, 3:4] * y_ref[:, d:])


def _combine(x1, y2, route, tm=512):
    n, d = x1.shape
    ypair = y2.reshape(n, MOE_TOPK * d)
    return pl.pallas_call(
        _combine_kernel,
        grid=(n // tm,),
        in_specs=[pl.BlockSpec((tm, d), lambda i: (i, 0)), pl.BlockSpec((tm, MOE_TOPK * d), lambda i: (i, 0)),
                  pl.BlockSpec((tm, LANES), lambda i: (i, 0))],
        out_specs=pl.BlockSpec((tm, d), lambda i: (i, 0)),
        out_shape=jax.ShapeDtypeStruct((n, d), F32),
        compiler_params=_params("parallel"),
        name="moe_combine",
    )(x1, ypair, route)


def _dispatch_tables(route, n_tok):
    n_assign = n_tok * MOE_TOPK
    e_flat = route[:, :MOE_TOPK].astype(jnp.int32).reshape(n_assign)
    order = jnp.argsort(e_flat).astype(jnp.int32)
    e_sorted = e_flat[order]
    counts = jnp.bincount(e_flat, length=MOE_EXPERTS).astype(jnp.int32)
    padded = ((counts + MOE_ROWS - 1) // MOE_ROWS) * MOE_ROWS
    pad_end = jnp.cumsum(padded)
    pad_start = pad_end - padded
    raw_start = jnp.cumsum(counts) - counts
    dest = pad_start[e_sorted] + jnp.arange(n_assign, dtype=jnp.int32) - raw_start[e_sorted]
    nblocks = n_assign // MOE_ROWS + MOE_EXPERTS
    buf_a = jnp.full((nblocks * MOE_ROWS,), -1, jnp.int32).at[dest].set(order)
    blk_start = jnp.arange(nblocks, dtype=jnp.int32) * MOE_ROWS
    blk_expert = jnp.minimum(jnp.sum(pad_end[None, :] <= blk_start[:, None], axis=1), MOE_EXPERTS - 1).astype(jnp.int32)
    nused = (pad_end[-1] // MOE_ROWS).astype(jnp.int32).reshape(1)
    return blk_expert, buf_a, nused


def kernel(x, mem, g_mix, w_in, moba_q_norm, moba_k_norm, conv_w, conv_b, dt_bias, a_log, d_skip, ssd_norm, g_mem,
           w_mem_kv, mem_q_norm, mem_k_norm, w_o_moba, w_o_ssd, w_o_mem, w_out, g_ffn, w_router_group,
           w_router_expert, w_gate, w_up, w_down):
    bsz, seq, d = x.shape
    assert seq % MOBA_BLOCK == 0 and seq % SSD_CHUNK == 0
    n_tok = bsz * seq
    moba_w = MOBA_HEADS * MOBA_HEAD_DIM
    d_in = SSD_HEADS * SSD_HEAD_DIM
    xbc_w = d_in + 2 * SSD_GROUPS * SSD_STATE
    mem_w = MEM_HEADS * MEM_HEAD_DIM
    sizes = (moba_w, moba_w, moba_w, d_in, xbc_w, SSD_HEADS, mem_w, 3 * d)
    offs = [0]
    for sz in sizes:
        offs.append(offs[-1] + sz)
    w_in_b = w_in.astype(BF16)
    w_qk = w_in_b[:, offs[0]:offs[2]]
    w_v = w_in_b[:, offs[2]:offs[3]]
    w_z = w_in_b[:, offs[3]:offs[4]]
    w_xbc = w_in_b[:, offs[4]:offs[5]]
    w_dt = jnp.pad(w_in_b[:, offs[5]:offs[6]], ((0, 0), (0, LANES - SSD_HEADS)))
    w_qm = w_in_b[:, offs[6]:offs[7]]
    w_gates = w_in_b[:, offs[7]:offs[8]]

    x2 = x.reshape(n_tok, d)
    h = _rmsnorm(x2, g_mix)

    half = MOBA_HEAD_DIM // 2
    inv = ROPE_THETA ** (-jnp.arange(half, dtype=F32) / half)
    ang = jnp.arange(seq, dtype=F32)[:, None] * inv[None, :]
    cos_t = jnp.tile(jnp.cos(ang), (1, LANES // half))
    sin_t = jnp.tile(jnp.concatenate([-jnp.sin(ang), jnp.sin(ang)], axis=1), (1, LANES // MOBA_HEAD_DIM))
    gains = jnp.stack([jnp.tile(moba_q_norm, MOBA_HEADS), jnp.tile(moba_k_norm, MOBA_HEADS)]).reshape(2, 1, moba_w)

    qk = _qk_proj(h, w_qk, gains, cos_t, sin_t, seq)
    q3 = qk[:, :moba_w].reshape(bsz, seq, moba_w)
    k3 = qk[:, moba_w:].reshape(bsz, seq, moba_w)
    v3 = _proj(h, w_v, BF16, name="v_proj").reshape(bsz, seq, moba_w)
    kmean, kb3 = _kmean(k3)
    nb = seq // MOBA_BLOCK
    kmean_pad = jnp.pad(kmean.reshape(bsz, nb, moba_w), ((0, 0), (0, LANES - nb), (0, 0)))
    o_a = _moba_attention(q3, kb3, v3, kmean_pad).reshape(n_tok, moba_w)

    z3 = _proj(h, w_z, F32, name="z_proj").reshape(bsz, seq, d_in)
    xbc3 = _proj(h, w_xbc, F32, name="xbc_proj").reshape(bsz, seq, xbc_w)
    dt3 = _proj(h, w_dt, F32, name="dt_proj").reshape(bsz, seq, LANES)
    o_s = _ssd(xbc3, z3, dt3, conv_w, conv_b, dt_bias, a_log, d_skip, ssd_norm).reshape(n_tok, d_in)

    km, vm = _memkv(mem, g_mem, w_mem_kv.astype(BF16), mem_k_norm)
    o_m = _memattn(h, w_qm, mem_q_norm, km, vm, seq)

    w_router = jnp.pad(jnp.concatenate([w_router_group, w_router_expert], axis=1),
                       ((0, 0), (0, LANES - MOE_GROUPS - MOE_EXPERTS)))
    x1, h2, route = _merge(x2, h, o_a, o_s, o_m, w_gates, w_o_moba.astype(BF16), w_o_ssd.astype(BF16),
                           w_o_mem.astype(BF16), w_out.astype(BF16), g_ffn, w_router)

    blk_expert, buf_a, nused = _dispatch_tables(route, n_tok)
    y2 = _moe_experts(h2, blk_expert, buf_a, nused, w_gate.astype(BF16), w_up.astype(BF16), w_down.astype(BF16))
    out = _combine(x1, y2, route)
    return out.reshape(bsz, seq, d)
```

```python
import functools

import jax
import jax.numpy as jnp
from jax import lax
from jax.experimental import pallas as pl
from jax.experimental.pallas import tpu as pltpu

F32 = jnp.float32
BF16 = jnp.bfloat16
HIGHEST = lax.Precision.HIGHEST

EPS = 1e-6
ROPE_THETA = 10000.0
MOBA_HEADS = 8
MOBA_HEAD_DIM = 64
MOBA_BLOCK = 256
MOBA_TOPK = 3
SSD_HEAD_DIM = 64
SSD_HEADS = 16
SSD_GROUPS = 4
SSD_STATE = 128
SSD_CONV = 4
SSD_CHUNK = 256
MEM_HEADS = 4
MEM_HEAD_DIM = 128
MOE_GROUPS = 4
MOE_EXPERTS_PER_GROUP = 8
MOE_EXPERTS = MOE_GROUPS * MOE_EXPERTS_PER_GROUP
MOE_TOPK = 2

LANES = 128
MASKED = -1e30
MOBA_MAX_BLOCKS = 32
MOBA_UNROLL = 4
MOE_ROWS = 256
MOE_DMA_UNROLL = 8
VMEM_LIMIT = 56 * 1024 * 1024


def _params(*sem):
    return pltpu.CompilerParams(dimension_semantics=sem, vmem_limit_bytes=VMEM_LIMIT)


def _sigmoid(x):
    return 1.0 / (1.0 + jnp.exp(-x))


def _nt_dot(a, b, precision=None):
    return lax.dot_general(a, b, (((1,), (1,)), ((), ())), precision=precision, preferred_element_type=F32)


def _rmsnorm_kernel(x_ref, g_ref, o_ref):
    x = x_ref[...]
    r = lax.rsqrt(jnp.mean(x * x, axis=-1, keepdims=True) + EPS)
    o_ref[...] = ((x * r) * g_ref[...]).astype(o_ref.dtype)


def _rmsnorm(x2, g, tm=1024):
    n, d = x2.shape
    return pl.pallas_call(
        _rmsnorm_kernel,
        grid=(n // tm,),
        in_specs=[pl.BlockSpec((tm, d), lambda i: (i, 0)), pl.BlockSpec((1, d), lambda i: (0, 0))],
        out_specs=pl.BlockSpec((tm, d), lambda i: (i, 0)),
        out_shape=jax.ShapeDtypeStruct((n, d), BF16),
        compiler_params=_params("parallel"),
        name="rmsnorm",
    )(x2, g.reshape(1, d))


def _proj_kernel(h_ref, w_ref, o_ref):
    o_ref[...] = jnp.dot(h_ref[...], w_ref[...], preferred_element_type=F32).astype(o_ref.dtype)


def _proj(h, w, out_dtype, tm=1024, tn=512, name="proj"):
    n, k = h.shape
    c = w.shape[1]
    tn = min(tn, c)
    return pl.pallas_call(
        _proj_kernel,
        grid=(n // tm, c // tn),
        in_specs=[pl.BlockSpec((tm, k), lambda i, j: (i, 0)), pl.BlockSpec((k, tn), lambda i, j: (0, j))],
        out_specs=pl.BlockSpec((tm, tn), lambda i, j: (i, j)),
        out_shape=jax.ShapeDtypeStruct((n, c), out_dtype),
        compiler_params=_params("parallel", "arbitrary"),
        name=name,
    )(h, w)


def _qk_kernel(h_ref, w_ref, gain_ref, gsum_ref, cos_ref, sin_ref, o_ref):
    u = jnp.dot(h_ref[...], w_ref[...], preferred_element_type=F32)
    sq = u * u
    hi = sq.astype(BF16)
    lo = (sq - hi.astype(F32)).astype(BF16)
    ss = jnp.dot(hi, gsum_ref[...], preferred_element_type=F32) + jnp.dot(lo, gsum_ref[...], preferred_element_type=F32)
    r = lax.rsqrt(ss * (1.0 / MOBA_HEAD_DIM) + EPS)
    un = (u * r) * gain_ref[0]
    width = un.shape[1]
    half = MOBA_HEAD_DIM // 2
    lane = lax.broadcasted_iota(jnp.int32, un.shape, 1)
    first = (lane & (MOBA_HEAD_DIM - 1)) < half
    partner = jnp.where(first, pltpu.roll(un, width - half, axis=1), pltpu.roll(un, half, axis=1))
    reps = width // cos_ref.shape[1]
    cos = jnp.concatenate([cos_ref[...]] * reps, axis=1)
    sin = jnp.concatenate([sin_ref[...]] * reps, axis=1)
    o_ref[...] = un * cos + partner * sin


def _qk_proj(h, w_qk, gains, cos_t, sin_t, seq, tm=512):
    n, k = h.shape
    c = w_qk.shape[1]
    tn = c // 2
    tiles_per_seq = seq // tm
    head = jnp.arange(tn) // MOBA_HEAD_DIM
    gsum = (head[:, None] == head[None, :]).astype(BF16)
    return pl.pallas_call(
        _qk_kernel,
        grid=(n // tm, 2),
        in_specs=[
            pl.BlockSpec((tm, k), lambda i, j: (i, 0)),
            pl.BlockSpec((k, tn), lambda i, j: (0, j)),
            pl.BlockSpec((1, 1, tn), lambda i, j: (j, 0, 0)),
            pl.BlockSpec((tn, tn), lambda i, j: (0, 0)),
            pl.BlockSpec((tm, LANES), lambda i, j: (i % tiles_per_seq, 0)),
            pl.BlockSpec((tm, LANES), lambda i, j: (i % tiles_per_seq, 0)),
        ],
        out_specs=pl.BlockSpec((tm, tn), lambda i, j: (i, j)),
        out_shape=jax.ShapeDtypeStruct((n, c), F32),
        compiler_params=_params("parallel", "arbitrary"),
        name="qk_proj",
    )(h, w_qk, gains, gsum, cos_t, sin_t)


def _moba_prep_kernel(k_ref, v_ref, km_ref, ka_ref, vt_ref):
    n = pl.program_id(1)
    k = k_ref[0]
    km_ref[0, 0] = jnp.mean(k, axis=0, keepdims=True)
    lane = lax.broadcasted_iota(jnp.int32, (MOBA_BLOCK, LANES), 1)
    own = lane < MOBA_HEAD_DIM
    onehot = jnp.where(lane == MOBA_HEAD_DIM + n, 1.0, 0.0)
    for pr in range(MOBA_HEADS // 2):
        pair = slice(pr * LANES, (pr + 1) * LANES)
        kp = k[:, pair]
        ka_ref[0, 2 * pr, 0] = jnp.where(own, kp, onehot).astype(BF16)
        ka_ref[0, 2 * pr + 1, 0] = jnp.where(own, pltpu.roll(kp, MOBA_HEAD_DIM, axis=1), onehot).astype(BF16)
        vp_t = v_ref[0, :, pair].astype(F32).T
        vt_ref[0, 2 * pr, 0] = vp_t[:MOBA_HEAD_DIM].astype(BF16)
        vt_ref[0, 2 * pr + 1, 0] = vp_t[MOBA_HEAD_DIM:].astype(BF16)


def _moba_prep(qk3, v3):
    b, s, w = v3.shape
    nb = s // MOBA_BLOCK
    return pl.pallas_call(
        _moba_prep_kernel,
        grid=(b, nb),
        in_specs=[pl.BlockSpec((1, MOBA_BLOCK, w), lambda bi, n: (bi, n, 1)),
                  pl.BlockSpec((1, MOBA_BLOCK, w), lambda bi, n: (bi, n, 0))],
        out_specs=[
            pl.BlockSpec((1, 1, 1, w), lambda bi, n: (bi, n, 0, 0)),
            pl.BlockSpec((1, MOBA_HEADS, 1, MOBA_BLOCK, LANES), lambda bi, n: (bi, 0, n, 0, 0)),
            pl.BlockSpec((1, MOBA_HEADS, 1, MOBA_HEAD_DIM, MOBA_BLOCK), lambda bi, n: (bi, 0, n, 0, 0)),
        ],
        out_shape=[jax.ShapeDtypeStruct((b, nb, 1, w), F32),
                   jax.ShapeDtypeStruct((b, MOBA_HEADS, nb, MOBA_BLOCK, LANES), BF16),
                   jax.ShapeDtypeStruct((b, MOBA_HEADS, nb, MOBA_HEAD_DIM, MOBA_BLOCK), BF16)],
        compiler_params=_params("parallel", "parallel"),
        name="moba_prep",
    )(qk3, v3)


def _moba_kernel(q_ref, ka_ref, vt_ref, km_ref, o_ref, qa_ref, sa_ref, sb_ref, gma_ref, gmb_ref, m_ref, l_ref,
                 acc_ref):
    blk = MOBA_BLOCK
    nsel = MOBA_MAX_BLOCKS
    i = pl.program_id(2)
    ngroups = (i + MOBA_UNROLL - 1) // MOBA_UNROLL
    q_t = q_ref[0].T
    kmean = km_ref[0]
    lane = lax.broadcasted_iota(jnp.int32, (LANES, LANES), 1)
    blk_row = lax.broadcasted_iota(jnp.int32, (nsel, blk), 0)
    blk_rowf = blk_row.astype(F32)
    key_pos = lax.broadcasted_iota(jnp.int32, (blk, blk), 0)
    qry_pos = lax.broadcasted_iota(jnp.int32, (blk, blk), 1)
    causal = key_pos <= qry_pos
    qscale = (MOBA_HEAD_DIM ** -0.5) * 1.4426950408889634

    def fold(s, op):
        return op(s.reshape(blk // 8, 8, blk), axis=0)

    for hh in range(2):
        hmask = (lane >= hh * MOBA_HEAD_DIM) & (lane < (hh + 1) * MOBA_HEAD_DIM)
        gate = jnp.dot(jnp.where(hmask, kmean, 0.0), q_t, precision=HIGHEST, preferred_element_type=F32)[:nsel]
        g = jnp.where(blk_row < i, gate, -jnp.inf)
        bias = jnp.full((nsel, blk), MASKED, F32)
        for _ in range(MOBA_TOPK):
            m = jnp.max(g, axis=0, keepdims=True)
            hit = (g == m) & (m > -jnp.inf)
            first = jnp.min(jnp.where(hit, blk_rowf, float(nsel)), axis=0, keepdims=True)
            sel = blk_rowf == first
            bias = jnp.where(sel, 0.0, bias)
            g = jnp.where(sel, -jnp.inf, g)
        qs = q_t[hh * MOBA_HEAD_DIM:(hh + 1) * MOBA_HEAD_DIM] * qscale
        pad = jnp.zeros((LANES - MOBA_HEAD_DIM - nsel, blk), F32)
        qa_ref[hh] = jnp.concatenate([qs, bias, pad], axis=0).astype(BF16)
        qa_own = jnp.concatenate([qs, jnp.zeros((LANES - MOBA_HEAD_DIM, blk), F32)], axis=0).astype(BF16)
        s_own = jnp.where(causal, jnp.dot(ka_ref[0, hh, i], qa_own, preferred_element_type=F32), MASKED)
        m_own = jnp.max(fold(s_own, jnp.max), axis=0, keepdims=True)
        p_own = jnp.exp2(s_own - m_own)
        m_ref[hh] = jnp.broadcast_to(m_own, (8, blk))
        l_ref[hh] = fold(p_own, jnp.sum)
        acc_ref[hh] = jnp.dot(vt_ref[0, hh, i], p_own.astype(BF16), preferred_element_type=F32)

    def score_group(grp, s_ref, gm_ref):
        for hh in range(2):
            gmax = None
            for u in range(MOBA_UNROLL):
                s = jnp.dot(ka_ref[0, hh, grp * MOBA_UNROLL + u], qa_ref[hh], preferred_element_type=F32)
                s_ref[hh, u] = s
                gmax = fold(s, jnp.max) if gmax is None else jnp.maximum(gmax, fold(s, jnp.max))
            gm_ref[hh] = gmax

    def value_group(grp, s_ref, gm_ref):
        for hh in range(2):
            m_old = m_ref[hh][0:1]
            m_new = jnp.maximum(m_old, jnp.max(gm_ref[hh], axis=0, keepdims=True))
            alpha = jnp.exp2(m_old - m_new)
            l8 = l_ref[hh] * alpha
            acc = acc_ref[hh] * alpha
            for u in range(MOBA_UNROLL):
                p = jnp.exp2(s_ref[hh, u] - m_new)
                l8 = l8 + fold(p, jnp.sum)
                acc = acc + jnp.dot(vt_ref[0, hh, grp * MOBA_UNROLL + u], p.astype(BF16),
                                    preferred_element_type=F32)
            m_ref[hh] = jnp.broadcast_to(m_new, (8, blk))
            l_ref[hh] = l8
            acc_ref[hh] = acc

    @pl.when(ngroups > 0)
    def _():
        score_group(0, sa_ref, gma_ref)

    npairs = jnp.maximum(ngroups - 1, 0) // 2

    def body(t, carry):
        score_group(2 * t + 1, sb_ref, gmb_ref)
        value_group(2 * t, sa_ref, gma_ref)
        score_group(2 * t + 2, sa_ref, gma_ref)
        value_group(2 * t + 1, sb_ref, gmb_ref)
        return carry

    lax.fori_loop(0, npairs, body, 0)
    left = ngroups - 2 * npairs

    @pl.when(left == 2)
    def _():
        score_group(2 * npairs + 1, sb_ref, gmb_ref)
        value_group(2 * npairs, sa_ref, gma_ref)
        value_group(2 * npairs + 1, sb_ref, gmb_ref)

    @pl.when(left == 1)
    def _():
        value_group(2 * npairs, sa_ref, gma_ref)

    outs = [acc_ref[hh] / jnp.sum(l_ref[hh], axis=0, keepdims=True) for hh in range(2)]
    o_ref[0] = jnp.concatenate(outs, axis=0).T.astype(o_ref.dtype)


def _moba_attention(qk3, ka, vt, kmean_pad):
    b, s, w = qk3.shape[0], qk3.shape[1], qk3.shape[2] // 2
    nq = s // MOBA_BLOCK
    pairs = w // LANES
    return pl.pallas_call(
        _moba_kernel,
        grid=(b, pairs, nq),
        in_specs=[
            pl.BlockSpec((1, MOBA_BLOCK, LANES), lambda bi, p, i: (bi, i, p)),
            pl.BlockSpec((1, 2, nq, MOBA_BLOCK, LANES), lambda bi, p, i: (bi, p, 0, 0, 0)),
            pl.BlockSpec((1, 2, nq, MOBA_HEAD_DIM, MOBA_BLOCK), lambda bi, p, i: (bi, p, 0, 0, 0)),
            pl.BlockSpec((1, LANES, LANES), lambda bi, p, i: (bi, 0, p)),
        ],
        out_specs=pl.BlockSpec((1, MOBA_BLOCK, LANES), lambda bi, p, i: (bi, i, p)),
        out_shape=jax.ShapeDtypeStruct((b, s, w), BF16),
        scratch_shapes=[
            pltpu.VMEM((2, LANES, MOBA_BLOCK), BF16),
            pltpu.VMEM((2, MOBA_UNROLL, MOBA_BLOCK, MOBA_BLOCK), F32),
            pltpu.VMEM((2, MOBA_UNROLL, MOBA_BLOCK, MOBA_BLOCK), F32),
            pltpu.VMEM((2, 8, MOBA_BLOCK), F32),
            pltpu.VMEM((2, 8, MOBA_BLOCK), F32),
            pltpu.VMEM((2, 8, MOBA_BLOCK), F32),
            pltpu.VMEM((2, 8, MOBA_BLOCK), F32),
            pltpu.VMEM((2, MOBA_HEAD_DIM, MOBA_BLOCK), F32),
        ],
        compiler_params=_params("parallel", "parallel", "arbitrary"),
        name="moba_attention",
    )(qk3, ka, vt, kmean_pad)


def _ssd_kernel(xbc_ref, z_ref, dt_ref, cw_ref, cb_ref, dtb_ref, alog_ref, dskip_ref, gn_ref, ex_ref, o_ref,
                state_ref, ext_ref):
    q = SSD_CHUNK
    d_in = SSD_HEADS * SSD_HEAD_DIM
    bc_w = SSD_GROUPS * SSD_STATE
    gw = d_in // SSD_GROUPS
    c = pl.program_id(1)

    @pl.when(c == 0)
    def _():
        state_ref[...] = jnp.zeros_like(state_ref)
        ext_ref[0:8, :] = jnp.zeros((8, ext_ref.shape[1]), F32)

    u = xbc_ref[0]
    ext_ref[8:8 + q, :] = u
    acc = cb_ref[...] + cw_ref[3:4, :] * u
    for kk in range(SSD_CONV - 1):
        off = 8 - (SSD_CONV - 1 - kk)
        acc = acc + cw_ref[kk:kk + 1, :] * ext_ref[off:off + q, :]
    ext_ref[0:8, :] = u[q - 8:q, :]
    act = acc * _sigmoid(acc)
    xs = act[:, :d_in]
    bm = act[:, d_in:d_in + bc_w]
    cm = act[:, d_in + bc_w:]

    dt_in = dt_ref[0] + dtb_ref[...]
    dt = jnp.maximum(dt_in, 0.0) + jnp.log1p(jnp.exp(-jnp.abs(dt_in)))
    a = dt * (-jnp.exp(alog_ref[...]))
    row = lax.broadcasted_iota(jnp.int32, (q, q), 0)
    col = lax.broadcasted_iota(jnp.int32, (q, q), 1)
    tril = col <= row
    a_cum = jnp.dot(tril.astype(F32), a, precision=HIGHEST, preferred_element_type=F32)
    a_cum_t = a_cum.T
    ex = ex_ref[...]
    dt_x = jnp.dot(dt, ex, precision=HIGHEST, preferred_element_type=F32)
    acum_x = jnp.dot(a_cum, ex, precision=HIGHEST, preferred_element_type=F32)
    alast_x = acum_x[q - 1:q, :]
    x_dt = xs * dt_x
    xd = (x_dt * jnp.exp(alast_x - acum_x)).astype(BF16)
    x_dt_b = x_dt.astype(BF16)
    e_acum = jnp.exp(acum_x)
    e_alast = jnp.exp(alast_x)
    lane2 = lax.broadcasted_iota(jnp.int32, (q, 2 * SSD_HEAD_DIM), 1)
    heads_per_group = SSD_HEADS // SSD_GROUPS

    for g in range(SSD_GROUPS):
        bg = bm[:, g * SSD_STATE:(g + 1) * SSD_STATE]
        cg = cm[:, g * SSD_STATE:(g + 1) * SSD_STATE].astype(BF16)
        cb = _nt_dot(cg, bg.astype(BF16))
        h_in = state_ref[g]
        y_g = jnp.dot(cg, h_in.astype(BF16), preferred_element_type=F32) * e_acum[:, g * gw:(g + 1) * gw]
        parts = []
        for pr in range(heads_per_group // 2):
            xp = x_dt_b[:, g * gw + pr * 2 * SSD_HEAD_DIM:g * gw + (pr + 1) * 2 * SSD_HEAD_DIM]
            ys = []
            for hh in range(2):
                hd = g * heads_per_group + pr * 2 + hh
                seg = a_cum[:, hd:hd + 1] - a_cum_t[hd:hd + 1, :]
                lmat = jnp.exp(jnp.where(tril, seg, -jnp.inf))
                ys.append(jnp.dot((cb * lmat).astype(BF16), xp, preferred_element_type=F32))
            parts.append(jnp.where(lane2 < SSD_HEAD_DIM, ys[0], ys[1]))
        y_g = y_g + jnp.concatenate(parts, axis=1)
        st = jnp.dot(bg.T.astype(BF16), xd[:, g * gw:(g + 1) * gw], preferred_element_type=F32)
        state_ref[g] = h_in * e_alast[:, g * gw:(g + 1) * gw] + st

        sl = slice(g * gw, (g + 1) * gw)
        y_g = y_g + dskip_ref[:, sl] * xs[:, sl]
        zz = z_ref[0, :, sl]
        y_g = y_g * (zz * _sigmoid(zz))
        r = lax.rsqrt(jnp.mean(y_g * y_g, axis=-1, keepdims=True) + EPS)
        o_ref[0, :, sl] = ((y_g * r) * gn_ref[:, sl]).astype(o_ref.dtype)


def _ssd(xbc3, z3, dt3, conv_w, conv_b, dt_bias, a_log, d_skip, ssd_norm):
    b, s, xw = xbc3.shape
    d_in = SSD_HEADS * SSD_HEAD_DIM
    nc = s // SSD_CHUNK
    pad = LANES - SSD_HEADS
    dtb = jnp.pad(dt_bias.astype(F32), (0, pad)).reshape(1, LANES)
    alog = jnp.pad(a_log.astype(F32), (0, pad)).reshape(1, LANES)
    dskip = jnp.repeat(d_skip.astype(F32), SSD_HEAD_DIM).reshape(1, d_in)
    expand = (jnp.arange(LANES)[:, None] == (jnp.arange(d_in) // SSD_HEAD_DIM)[None, :]).astype(F32)
    const = lambda shape: pl.BlockSpec(shape, lambda bi, ci: (0,) * len(shape))
    return pl.pallas_call(
        _ssd_kernel,
        grid=(b, nc),
        in_specs=[
            pl.BlockSpec((1, SSD_CHUNK, xw), lambda bi, ci: (bi, ci, 0)),
            pl.BlockSpec((1, SSD_CHUNK, d_in), lambda bi, ci: (bi, ci, 0)),
            pl.BlockSpec((1, SSD_CHUNK, LANES), lambda bi, ci: (bi, ci, 0)),
            const((SSD_CONV, xw)), const((1, xw)), const((1, LANES)), const((1, LANES)),
            const((1, d_in)), const((1, d_in)), const((LANES, d_in)),
        ],
        out_specs=pl.BlockSpec((1, SSD_CHUNK, d_in), lambda bi, ci: (bi, ci, 0)),
        out_shape=jax.ShapeDtypeStruct((b, s, d_in), BF16),
        scratch_shapes=[
            pltpu.VMEM((SSD_GROUPS, SSD_STATE, d_in // SSD_GROUPS), F32),
            pltpu.VMEM((SSD_CHUNK + 8, xw), F32),
        ],
        compiler_params=_params("parallel", "arbitrary"),
        name="ssd_scan",
    )(xbc3, z3, dt3, conv_w, conv_b.reshape(1, xw), dtb, alog, dskip, ssd_norm.reshape(1, d_in), expand)


def _memkv_kernel(mem_ref, g_ref, w_ref, kg_ref, km_ref, vm_ref):
    m = mem_ref[0]
    r = lax.rsqrt(jnp.mean(m * m, axis=-1, keepdims=True) + EPS)
    mn = ((m * r) * g_ref[...]).astype(BF16)
    kv = jnp.dot(mn, w_ref[...], preferred_element_type=F32)
    mw = MEM_HEADS * MEM_HEAD_DIM
    for hd in range(MEM_HEADS):
        sl = slice(hd * MEM_HEAD_DIM, (hd + 1) * MEM_HEAD_DIM)
        kh = kv[:, sl]
        rk = lax.rsqrt(jnp.mean(kh * kh, axis=-1, keepdims=True) + EPS)
        km_ref[0, :, sl] = ((kh * rk) * kg_ref[...]).astype(km_ref.dtype)
    vm_ref[0] = kv[:, mw:].astype(vm_ref.dtype)


def _memkv(mem, g_mem, w_kv, k_gain):
    b, m, d = mem.shape
    mw = MEM_HEADS * MEM_HEAD_DIM
    return pl.pallas_call(
        _memkv_kernel,
        grid=(b,),
        in_specs=[
            pl.BlockSpec((1, m, d), lambda bi: (bi, 0, 0)),
            pl.BlockSpec((1, d), lambda bi: (0, 0)),
            pl.BlockSpec((d, 2 * mw), lambda bi: (0, 0)),
            pl.BlockSpec((1, MEM_HEAD_DIM), lambda bi: (0, 0)),
        ],
        out_specs=[pl.BlockSpec((1, m, mw), lambda bi: (bi, 0, 0)), pl.BlockSpec((1, m, mw), lambda bi: (bi, 0, 0))],
        out_shape=[jax.ShapeDtypeStruct((b, m, mw), BF16), jax.ShapeDtypeStruct((b, m, mw), BF16)],
        compiler_params=_params("parallel"),
        name="mem_kv",
    )(mem, g_mem.reshape(1, d), w_kv, k_gain.reshape(1, MEM_HEAD_DIM))


def _memattn_kernel(h_ref, w_ref, qg_ref, km_ref, vm_ref, o_ref):
    qm = jnp.dot(h_ref[...], w_ref[...], preferred_element_type=F32)
    scale = MEM_HEAD_DIM ** -0.5
    for hd in range(MEM_HEADS):
        sl = slice(hd * MEM_HEAD_DIM, (hd + 1) * MEM_HEAD_DIM)
        qh = qm[:, sl]
        r = lax.rsqrt(jnp.mean(qh * qh, axis=-1, keepdims=True) + EPS)
        qn = ((qh * r) * qg_ref[...]).astype(BF16)
        s = _nt_dot(qn, km_ref[0, :, sl]) * scale
        p = jnp.exp(s - jnp.max(s, axis=-1, keepdims=True))
        l = jnp.sum(p, axis=-1, keepdims=True)
        o = jnp.dot(p.astype(BF16), vm_ref[0, :, sl], preferred_element_type=F32)
        o_ref[:, sl] = (o / l).astype(o_ref.dtype)


def _memattn(h, w_qm, q_gain, km, vm, seq, tm=512):
    n, d = h.shape
    b, m, mw = km.shape
    tiles_per_seq = seq // tm
    return pl.pallas_call(
        _memattn_kernel,
        grid=(n // tm,),
        in_specs=[
            pl.BlockSpec((tm, d), lambda i: (i, 0)),
            pl.BlockSpec((d, mw), lambda i: (0, 0)),
            pl.BlockSpec((1, MEM_HEAD_DIM), lambda i: (0, 0)),
            pl.BlockSpec((1, m, mw), lambda i: (i // tiles_per_seq, 0, 0)),
            pl.BlockSpec((1, m, mw), lambda i: (i // tiles_per_seq, 0, 0)),
        ],
        out_specs=pl.BlockSpec((tm, mw), lambda i: (i, 0)),
        out_shape=jax.ShapeDtypeStruct((n, mw), BF16),
        compiler_params=_params("parallel"),
        name="mem_attention",
    )(h, w_qm, q_gain.reshape(1, MEM_HEAD_DIM), km, vm)


def _merge_kernel(x_ref, h_ref, oa_ref, os_ref, om_ref, wg_ref, wa_ref, ws_ref, wm_ref, wo_ref, gf_ref, wr_ref,
                  x1_ref, h2_ref, route_ref):
    d = x_ref.shape[1]
    gates = _sigmoid(jnp.dot(h_ref[...], wg_ref[...], preferred_element_type=F32))
    merged = gates[:, :d] * jnp.dot(oa_ref[...], wa_ref[...], preferred_element_type=F32)
    merged = merged + gates[:, d:2 * d] * jnp.dot(os_ref[...], ws_ref[...], preferred_element_type=F32)
    merged = merged + gates[:, 2 * d:] * jnp.dot(om_ref[...], wm_ref[...], preferred_element_type=F32)
    x1 = x_ref[...] + jnp.dot(merged.astype(BF16), wo_ref[...], preferred_element_type=F32)
    x1_ref[...] = x1
    r = lax.rsqrt(jnp.mean(x1 * x1, axis=-1, keepdims=True) + EPS)
    h2 = (x1 * r) * gf_ref[...]
    for c in range(d // LANES):
        h2_ref[:, c, :] = h2[:, c * LANES:(c + 1) * LANES]

    lg = jnp.dot(h2, wr_ref[...], precision=HIGHEST, preferred_element_type=F32)
    lanef = lax.broadcasted_iota(jnp.int32, lg.shape, 1).astype(F32)
    big = float(LANES)
    gmask = lanef < MOE_GROUPS
    gl = jnp.where(gmask, lg, -jnp.inf)
    gmax = jnp.max(gl, axis=-1, keepdims=True)
    p_g = 1.0 / jnp.sum(jnp.exp(gl - gmax), axis=-1, keepdims=True)
    g_sel = jnp.min(jnp.where(gl == gmax, lanef, big), axis=-1, keepdims=True)
    lo = MOE_GROUPS + MOE_EXPERTS_PER_GROUP * g_sel
    el = jnp.where((lanef >= lo) & (lanef < lo + MOE_EXPERTS_PER_GROUP), lg, -jnp.inf)
    m1 = jnp.max(el, axis=-1, keepdims=True)
    i1 = jnp.min(jnp.where(el == m1, lanef, big), axis=-1, keepdims=True)
    el2 = jnp.where(lanef == i1, -jnp.inf, el)
    m2 = jnp.max(el2, axis=-1, keepdims=True)
    i2 = jnp.min(jnp.where(el2 == m2, lanef, big), axis=-1, keepdims=True)
    e2 = jnp.exp(m2 - m1)
    w1 = 1.0 / (1.0 + e2)
    w2 = e2 / (1.0 + e2)
    route = jnp.where(lanef == 0, i1 - MOE_GROUPS, 0.0)
    route = jnp.where(lanef == 1, i2 - MOE_GROUPS, route)
    route = jnp.where(lanef == 2, p_g * w1, route)
    route = jnp.where(lanef == 3, p_g * w2, route)
    route_ref[...] = route


def _merge(x2, h, o_a, o_s, o_m, w_gates, w_a, w_s, w_m, w_out, g_ffn, w_router, tm=512):
    n, d = x2.shape
    full = lambda arr: pl.BlockSpec(arr.shape, lambda i: (0,) * arr.ndim)
    tile = lambda arr: pl.BlockSpec((tm, arr.shape[1]), lambda i: (i, 0))
    gf = g_ffn.reshape(1, d)
    return pl.pallas_call(
        _merge_kernel,
        grid=(n // tm,),
        in_specs=[tile(x2), tile(h), tile(o_a), tile(o_s), tile(o_m), full(w_gates), full(w_a), full(w_s), full(w_m),
                  full(w_out), full(gf), full(w_router)],
        out_specs=[pl.BlockSpec((tm, d), lambda i: (i, 0)),
                   pl.BlockSpec((tm, d // LANES, LANES), lambda i: (i, 0, 0)),
                   pl.BlockSpec((tm, LANES), lambda i: (i, 0))],
        out_shape=[jax.ShapeDtypeStruct((n, d), F32), jax.ShapeDtypeStruct((n, d // LANES, LANES), F32),
                   jax.ShapeDtypeStruct((n, LANES), F32)],
        compiler_params=_params("parallel"),
        name="merge_router",
    )(x2, h, o_a, o_s, o_m, w_gates, w_a, w_s, w_m, w_out, gf, w_router)


def _moe_kernel(be_ref, j0_ref, nv_ref, nu_ref, ord_ref, h2_hbm, wg_ref, wu_ref, wd_ref, y2_hbm, xbuf, ybuf, gsem,
                ssem):
    rows = xbuf.shape[1]
    n_tok = h2_hbm.shape[0]
    nblocks = pl.num_programs(0)
    i = pl.program_id(0)
    nused = nu_ref[0]
    slot = lax.rem(i, 2)

    def per_row(blk, fn):
        j0 = j0_ref[blk]
        last = nv_ref[blk] - 1

        def body(c, carry):
            for k in range(MOE_DMA_UNROLL):
                r = c * MOE_DMA_UNROLL + k
                fn(r, ord_ref[j0 + jnp.minimum(r, last)], r <= last, k % 2)
            return carry
        lax.fori_loop(0, rows // MOE_DMA_UNROLL, body, 0)

    def gather_start(blk, sl):
        def fn(r, a, real, prio):
            tok = lax.shift_right_logical(a, 1)
            pltpu.make_async_copy(h2_hbm.at[pl.ds(tok, 1)], xbuf.at[sl, pl.ds(r, 1)], gsem.at[sl]).start(priority=prio)
        per_row(blk, fn)

    def gather_wait(sl):
        pltpu.make_async_copy(h2_hbm.at[pl.ds(0, rows)], xbuf.at[sl], gsem.at[sl]).wait()

    def scatter_start(blk, sl):
        def fn(r, a, real, prio):
            dst = jnp.where(real, (a & 1) * n_tok + lax.shift_right_logical(a, 1), MOE_TOPK * n_tok + sl * rows + r)
            pltpu.make_async_copy(ybuf.at[sl, pl.ds(r, 1)], y2_hbm.at[pl.ds(dst, 1)], ssem.at[sl]).start(priority=prio)
        per_row(blk, fn)

    def scatter_wait(sl):
        pltpu.make_async_copy(ybuf.at[sl], y2_hbm.at[pl.ds(0, rows)], ssem.at[sl]).wait()

    @pl.when(i == 0)
    def _():
        gather_start(0, 0)
        ybuf[...] = jnp.zeros_like(ybuf)
        for sl in range(2):
            dump = y2_hbm.at[pl.ds(MOE_TOPK * n_tok + sl * rows, rows)]
            cp = pltpu.make_async_copy(ybuf.at[sl], dump, ssem.at[sl])
            cp.start()
            cp.wait()

    @pl.when(i + 1 < nused)
    def _():
        gather_start(i + 1, 1 - slot)

    @pl.when(i < nused)
    def _():
        gather_wait(slot)

        @pl.when(i >= 2)
        def _():
            scatter_wait(slot)

        chunks = xbuf.shape[2]
        x = jnp.concatenate([xbuf[slot, :, c, :] for c in range(chunks)], axis=1).astype(BF16)
        gate = jnp.dot(x, wg_ref[0], preferred_element_type=F32)
        up = jnp.dot(x, wu_ref[0], preferred_element_type=F32)
        hid = (gate * _sigmoid(gate)) * up
        y = jnp.dot(hid.astype(BF16), wd_ref[0], preferred_element_type=F32)
        for c in range(chunks):
            ybuf[slot, :, c, :] = y[:, c * LANES:(c + 1) * LANES]
        scatter_start(i, slot)

    @pl.when(i == nblocks - 1)
    def _():
        @pl.when(nused >= 2)
        def _():
            scatter_wait(lax.rem(nused, 2))

        scatter_wait(lax.rem(nused - 1, 2))


def _moe_experts(h2, tables, w_gate, w_up, w_down):
    n = h2.shape[0]
    e, d, f = w_gate.shape
    nblocks = tables[0].shape[0]
    wmap = lambda i, be, j0, nv, nu, od: (be[i], 0, 0)
    grid_spec = pltpu.PrefetchScalarGridSpec(
        num_scalar_prefetch=5,
        grid=(nblocks,),
        in_specs=[
            pl.BlockSpec(memory_space=pl.ANY),
            pl.BlockSpec((1, d, f), wmap),
            pl.BlockSpec((1, d, f), wmap),
            pl.BlockSpec((1, f, d), wmap),
        ],
        out_specs=pl.BlockSpec(memory_space=pl.ANY),
        scratch_shapes=[
            pltpu.VMEM((2, MOE_ROWS, d // LANES, LANES), F32),
            pltpu.VMEM((2, MOE_ROWS, d // LANES, LANES), F32),
            pltpu.SemaphoreType.DMA((2,)),
            pltpu.SemaphoreType.DMA((2,)),
        ],
    )
    return pl.pallas_call(
        _moe_kernel,
        grid_spec=grid_spec,
        out_shape=jax.ShapeDtypeStruct((n * MOE_TOPK + 2 * MOE_ROWS, d // LANES, LANES), F32),
        compiler_params=_params("arbitrary"),
        name="moe_experts",
    )(*tables, h2, w_gate, w_up, w_down)


def _combine_kernel(x1_ref, y0_ref, y1_ref, route_ref, o_ref):
    route = route_ref[...]
    w0 = route[:, 2:3]
    w1 = route[:, 3:4]
    for c in range(y0_ref.shape[1]):
        sl = slice(c * LANES, (c + 1) * LANES)
        o_ref[:, sl] = x1_ref[:, sl] + (w0 * y0_ref[:, c, :] + w1 * y1_ref[:, c, :])


def _combine(x1, y2, route, tm=512):
    n, d = x1.shape
    tiles = n // tm
    ytile = (tm, d // LANES, LANES)
    return pl.pallas_call(
        _combine_kernel,
        grid=(tiles,),
        in_specs=[pl.BlockSpec((tm, d), lambda i: (i, 0)), pl.BlockSpec(ytile, lambda i: (i, 0, 0)),
                  pl.BlockSpec(ytile, lambda i: (tiles + i, 0, 0)), pl.BlockSpec((tm, LANES), lambda i: (i, 0))],
        out_specs=pl.BlockSpec((tm, d), lambda i: (i, 0)),
        out_shape=jax.ShapeDtypeStruct((n, d), F32),
        compiler_params=_params("parallel"),
        name="moe_combine",
    )(x1, y2, y2, route)


def _dispatch_tables(route, n_tok):
    n_assign = n_tok * MOE_TOPK
    e_flat = route[:, :MOE_TOPK].astype(jnp.int32).reshape(n_assign)
    order = jnp.argsort(e_flat).astype(jnp.int32)
    counts = jnp.sum(e_flat[:, None] == jnp.arange(MOE_EXPERTS, dtype=jnp.int32)[None, :], axis=0, dtype=jnp.int32)
    blocks_per_expert = (counts + MOE_ROWS - 1) // MOE_ROWS
    blk_end = jnp.cumsum(blocks_per_expert)
    raw_start = jnp.cumsum(counts) - counts
    nblocks = n_assign // MOE_ROWS + MOE_EXPERTS
    blk = jnp.arange(nblocks, dtype=jnp.int32)
    blk_expert = jnp.minimum(jnp.sum(blk_end[None, :] <= blk[:, None], axis=1), MOE_EXPERTS - 1).astype(jnp.int32)
    within = blk - (blk_end - blocks_per_expert)[blk_expert]
    blk_j0 = (raw_start[blk_expert] + within * MOE_ROWS).astype(jnp.int32)
    blk_nvalid = jnp.clip(counts[blk_expert] - within * MOE_ROWS, 0, MOE_ROWS).astype(jnp.int32)
    nused = blk_end[-1].astype(jnp.int32).reshape(1)
    return blk_expert, blk_j0, blk_nvalid, nused, order


def kernel(x, mem, g_mix, w_in, moba_q_norm, moba_k_norm, conv_w, conv_b, dt_bias, a_log, d_skip, ssd_norm, g_mem,
           w_mem_kv, mem_q_norm, mem_k_norm, w_o_moba, w_o_ssd, w_o_mem, w_out, g_ffn, w_router_group,
           w_router_expert, w_gate, w_up, w_down):
    bsz, seq, d = x.shape
    assert seq % MOBA_BLOCK == 0 and seq % SSD_CHUNK == 0
    n_tok = bsz * seq
    moba_w = MOBA_HEADS * MOBA_HEAD_DIM
    d_in = SSD_HEADS * SSD_HEAD_DIM
    xbc_w = d_in + 2 * SSD_GROUPS * SSD_STATE
    mem_w = MEM_HEADS * MEM_HEAD_DIM
    sizes = (moba_w, moba_w, moba_w, d_in, xbc_w, SSD_HEADS, mem_w, 3 * d)
    offs = [0]
    for sz in sizes:
        offs.append(offs[-1] + sz)
    w_in_b = w_in.astype(BF16)
    w_qk = w_in_b[:, offs[0]:offs[2]]
    w_v = w_in_b[:, offs[2]:offs[3]]
    w_z = w_in_b[:, offs[3]:offs[4]]
    w_xbc = w_in_b[:, offs[4]:offs[5]]
    w_dt = jnp.pad(w_in_b[:, offs[5]:offs[6]], ((0, 0), (0, LANES - SSD_HEADS)))
    w_qm = w_in_b[:, offs[6]:offs[7]]
    w_gates = w_in_b[:, offs[7]:offs[8]]

    x2 = x.reshape(n_tok, d)
    h = _rmsnorm(x2, g_mix)

    half = MOBA_HEAD_DIM // 2
    inv = ROPE_THETA ** (-jnp.arange(half, dtype=F32) / half)
    ang = jnp.arange(seq, dtype=F32)[:, None] * inv[None, :]
    cos_t = jnp.tile(jnp.cos(ang), (1, LANES // half))
    sin_t = jnp.tile(jnp.concatenate([-jnp.sin(ang), jnp.sin(ang)], axis=1), (1, LANES // MOBA_HEAD_DIM))
    gains = jnp.stack([jnp.tile(moba_q_norm, MOBA_HEADS), jnp.tile(moba_k_norm, MOBA_HEADS)]).reshape(2, 1, moba_w)

    qk = _qk_proj(h, w_qk, gains, cos_t, sin_t, seq)
    qk3 = qk.reshape(bsz, seq, 2 * moba_w)
    v3 = _proj(h, w_v, BF16, name="v_proj").reshape(bsz, seq, moba_w)
    kmean, ka, vt = _moba_prep(qk3, v3)
    nb = seq // MOBA_BLOCK
    assert nb <= MOBA_MAX_BLOCKS and nb % MOBA_UNROLL == 0
    kmean_pad = jnp.pad(kmean.reshape(bsz, nb, moba_w), ((0, 0), (0, LANES - nb), (0, 0)))
    o_a = _moba_attention(qk3, ka, vt, kmean_pad).reshape(n_tok, moba_w)

    z3 = _proj(h, w_z, F32, name="z_proj").reshape(bsz, seq, d_in)
    xbc3 = _proj(h, w_xbc, F32, name="xbc_proj").reshape(bsz, seq, xbc_w)
    dt3 = _proj(h, w_dt, F32, name="dt_proj").reshape(bsz, seq, LANES)
    o_s = _ssd(xbc3, z3, dt3, conv_w, conv_b, dt_bias, a_log, d_skip, ssd_norm).reshape(n_tok, d_in)

    km, vm = _memkv(mem, g_mem, w_mem_kv.astype(BF16), mem_k_norm)
    o_m = _memattn(h, w_qm, mem_q_norm, km, vm, seq)

    w_router = jnp.pad(jnp.concatenate([w_router_group, w_router_expert], axis=1),
                       ((0, 0), (0, LANES - MOE_GROUPS - MOE_EXPERTS)))
    x1, h2, route = _merge(x2, h, o_a, o_s, o_m, w_gates, w_o_moba.astype(BF16), w_o_ssd.astype(BF16),
                           w_o_mem.astype(BF16), w_out.astype(BF16), g_ffn, w_router)

    tables = _dispatch_tables(route, n_tok)
    y2 = _moe_experts(h2, tables, w_gate.astype(BF16), w_up.astype(BF16), w_down.astype(BF16))
    out = _combine(x1, y2, route)
    return out.reshape(bsz, seq, d)
```

```python
import functools

import jax
import jax.numpy as jnp
from jax import lax
from jax.experimental import pallas as pl
from jax.experimental.pallas import tpu as pltpu

F32 = jnp.float32
BF16 = jnp.bfloat16
HIGHEST = lax.Precision.HIGHEST

EPS = 1e-6
ROPE_THETA = 10000.0
MOBA_HEADS = 8
MOBA_HEAD_DIM = 64
MOBA_BLOCK = 256
MOBA_TOPK = 3
SSD_HEAD_DIM = 64
SSD_HEADS = 16
SSD_GROUPS = 4
SSD_STATE = 128
SSD_CONV = 4
SSD_CHUNK = 256
MEM_HEADS = 4
MEM_HEAD_DIM = 128
MOE_GROUPS = 4
MOE_EXPERTS_PER_GROUP = 8
MOE_EXPERTS = MOE_GROUPS * MOE_EXPERTS_PER_GROUP
MOE_TOPK = 2

LANES = 128
MASKED = -1e30
MOBA_MAX_BLOCKS = 32
MOBA_UNROLL = 4
MOE_ROWS = 256
MOE_DMA_UNROLL = 8
VMEM_LIMIT = 56 * 1024 * 1024


def _params(*sem):
    return pltpu.CompilerParams(dimension_semantics=sem, vmem_limit_bytes=VMEM_LIMIT)


def _sigmoid(x):
    return 1.0 / (1.0 + jnp.exp(-x))


def _split3_dot(x, sel):
    hi = x.astype(BF16)
    rest = x - hi.astype(F32)
    mid = rest.astype(BF16)
    lo = (rest - mid.astype(F32)).astype(BF16)
    return (jnp.dot(hi, sel, preferred_element_type=F32) + jnp.dot(mid, sel, preferred_element_type=F32)
            + jnp.dot(lo, sel, preferred_element_type=F32))


def _nt_dot(a, b, precision=None):
    return lax.dot_general(a, b, (((1,), (1,)), ((), ())), precision=precision, preferred_element_type=F32)


def _softplus(x):
    return jnp.maximum(x, 0.0) + jnp.log1p(jnp.exp(-jnp.abs(x)))


def _inproj_kernel(x_ref, gmix_ref, wqk_ref, wv_ref, wz_ref, wxbc_ref, wdt_ref, gain_ref, gsum_ref, cos_ref, sin_ref,
                   cw_ref, cb_ref, dtb_ref, h_ref, qk_ref, v_ref, zs_ref, xbc_ref, dt_ref, ext_ref, *,
                   tiles_per_seq):
    tm = x_ref.shape[0]
    x = x_ref[...]
    r = lax.rsqrt(jnp.mean(x * x, axis=-1, keepdims=True) + EPS)
    h = ((x * r) * gmix_ref[...]).astype(BF16)
    h_ref[...] = h

    width = gsum_ref.shape[0]
    half = MOBA_HEAD_DIM // 2
    lane = lax.broadcasted_iota(jnp.int32, (tm, width), 1)
    first = (lane & (MOBA_HEAD_DIM - 1)) < half
    reps = width // cos_ref.shape[1]
    cos = jnp.concatenate([cos_ref[...]] * reps, axis=1)
    sin = jnp.concatenate([sin_ref[...]] * reps, axis=1)
    for j in range(2):
        u = jnp.dot(h, wqk_ref[:, j * width:(j + 1) * width], preferred_element_type=F32)
        sq = u * u
        hi = sq.astype(BF16)
        lo = (sq - hi.astype(F32)).astype(BF16)
        ss = (jnp.dot(hi, gsum_ref[...], preferred_element_type=F32)
              + jnp.dot(lo, gsum_ref[...], preferred_element_type=F32))
        un = (u * lax.rsqrt(ss * (1.0 / MOBA_HEAD_DIM) + EPS)) * gain_ref[j]
        partner = jnp.where(first, pltpu.roll(un, width - half, axis=1), pltpu.roll(un, half, axis=1))
        qk_ref[:, j * width:(j + 1) * width] = un * cos + partner * sin

    v_ref[...] = jnp.dot(h, wv_ref[...], preferred_element_type=F32).astype(v_ref.dtype)
    z = jnp.dot(h, wz_ref[...], preferred_element_type=F32)
    zs_ref[...] = (z * _sigmoid(z)).astype(zs_ref.dtype)
    dt_ref[...] = _softplus(jnp.dot(h, wdt_ref[...], preferred_element_type=F32) + dtb_ref[...])

    @pl.when(pl.program_id(0) % tiles_per_seq == 0)
    def _():
        ext_ref[0:8, :] = jnp.zeros((8, ext_ref.shape[1]), F32)

    cchunk = 512
    for c0 in range(0, wxbc_ref.shape[1], cchunk):
        cs = slice(c0, c0 + cchunk)
        u = jnp.dot(h, wxbc_ref[:, cs], preferred_element_type=F32)
        ext_ref[8:8 + tm, cs] = u
        acc = cb_ref[:, cs] + cw_ref[SSD_CONV - 1:SSD_CONV, cs] * u
        for kk in range(SSD_CONV - 1):
            off = 8 - (SSD_CONV - 1 - kk)
            acc = acc + cw_ref[kk:kk + 1, cs] * ext_ref[off:off + tm, cs]
        ext_ref[0:8, cs] = u[tm - 8:tm, :]
        xbc_ref[:, cs] = (acc * _sigmoid(acc)).astype(xbc_ref.dtype)


def _inproj(x2, g_mix, w_qk, w_v, w_z, w_xbc, w_dt, gains, cos_t, sin_t, conv_w, conv_b, dt_bias, seq, tm=512):
    n, d = x2.shape
    width = w_qk.shape[1] // 2
    tiles_per_seq = seq // tm
    head = jnp.arange(width) // MOBA_HEAD_DIM
    gsum = (head[:, None] == head[None, :]).astype(BF16)
    dtb = jnp.pad(dt_bias.astype(F32), (0, LANES - dt_bias.shape[0])).reshape(1, LANES)
    full = lambda arr: pl.BlockSpec(arr.shape, lambda i: (0,) * arr.ndim)
    tile = lambda cols: pl.BlockSpec((tm, cols), lambda i: (i, 0))
    rope = pl.BlockSpec((tm, LANES), lambda i: (i % tiles_per_seq, 0))
    consts = (g_mix.reshape(1, d), w_qk, w_v, w_z, w_xbc, w_dt, gains, gsum)
    tail = (conv_w, conv_b.reshape(1, -1), dtb)
    outs = ((d, BF16), (2 * width, F32), (w_v.shape[1], BF16), (w_z.shape[1], BF16), (w_xbc.shape[1], BF16),
            (LANES, F32))
    return pl.pallas_call(
        functools.partial(_inproj_kernel, tiles_per_seq=tiles_per_seq),
        grid=(n // tm,),
        in_specs=[tile(d)] + [full(a) for a in consts] + [rope, rope] + [full(a) for a in tail],
        out_specs=[tile(c) for c, _ in outs],
        out_shape=[jax.ShapeDtypeStruct((n, c), dt) for c, dt in outs],
        scratch_shapes=[pltpu.VMEM((tm + 8, w_xbc.shape[1]), F32)],
        compiler_params=_params("arbitrary"),
        name="in_proj",
    )(x2, *consts, cos_t, sin_t, *tail)


def _moba_prep_kernel(k_ref, v_ref, km_ref, ka_ref, vt_ref):
    n = pl.program_id(1)
    k = k_ref[0]
    km_ref[0, 0] = jnp.mean(k, axis=0, keepdims=True)
    lane = lax.broadcasted_iota(jnp.int32, (MOBA_BLOCK, LANES), 1)
    own = lane < MOBA_HEAD_DIM
    onehot = jnp.where(lane == MOBA_HEAD_DIM + n, 1.0, 0.0)
    for pr in range(MOBA_HEADS // 2):
        pair = slice(pr * LANES, (pr + 1) * LANES)
        kp = k[:, pair]
        ka_ref[0, 2 * pr, 0] = jnp.where(own, kp, onehot).astype(BF16)
        ka_ref[0, 2 * pr + 1, 0] = jnp.where(own, pltpu.roll(kp, MOBA_HEAD_DIM, axis=1), onehot).astype(BF16)
        vp_t = v_ref[0, :, pair].astype(F32).T
        vt_ref[0, 2 * pr, 0] = vp_t[:MOBA_HEAD_DIM].astype(BF16)
        vt_ref[0, 2 * pr + 1, 0] = vp_t[MOBA_HEAD_DIM:].astype(BF16)


def _moba_prep(qk3, v3):
    b, s, w = v3.shape
    nb = s // MOBA_BLOCK
    return pl.pallas_call(
        _moba_prep_kernel,
        grid=(b, nb),
        in_specs=[pl.BlockSpec((1, MOBA_BLOCK, w), lambda bi, n: (bi, n, 1)),
                  pl.BlockSpec((1, MOBA_BLOCK, w), lambda bi, n: (bi, n, 0))],
        out_specs=[
            pl.BlockSpec((1, 1, 1, w), lambda bi, n: (bi, n, 0, 0)),
            pl.BlockSpec((1, MOBA_HEADS, 1, MOBA_BLOCK, LANES), lambda bi, n: (bi, 0, n, 0, 0)),
            pl.BlockSpec((1, MOBA_HEADS, 1, MOBA_HEAD_DIM, MOBA_BLOCK), lambda bi, n: (bi, 0, n, 0, 0)),
        ],
        out_shape=[jax.ShapeDtypeStruct((b, nb, 1, w), F32),
                   jax.ShapeDtypeStruct((b, MOBA_HEADS, nb, MOBA_BLOCK, LANES), BF16),
                   jax.ShapeDtypeStruct((b, MOBA_HEADS, nb, MOBA_HEAD_DIM, MOBA_BLOCK), BF16)],
        compiler_params=_params("parallel", "parallel"),
        name="moba_prep",
    )(qk3, v3)


def _moba_select_kernel(q_ref, km_ref, qa_ref):
    blk = MOBA_BLOCK
    nsel = MOBA_MAX_BLOCKS
    i = pl.program_id(1)
    qscale = (MOBA_HEAD_DIM ** -0.5) * 1.4426950408889634
    q_t = (q_ref[0] * qscale).T
    kmean = km_ref[0]
    lane = lax.broadcasted_iota(jnp.int32, (nsel, LANES), 1)
    blk_row = lax.broadcasted_iota(jnp.int32, (nsel, blk), 0)
    blk_rowf = blk_row.astype(F32)
    pad = jnp.zeros((LANES - MOBA_HEAD_DIM - nsel, blk), F32)
    for hd in range(MOBA_HEADS):
        pr, hh = hd // 2, hd % 2
        hmask = (lane >= hh * MOBA_HEAD_DIM) & (lane < (hh + 1) * MOBA_HEAD_DIM)
        km_h = jnp.where(hmask, kmean[:, pr * LANES:(pr + 1) * LANES], 0.0)
        gate = jnp.dot(km_h, q_t[pr * LANES:(pr + 1) * LANES], precision=HIGHEST, preferred_element_type=F32)
        g = jnp.where(blk_row < i, gate, -jnp.inf)
        bias = jnp.full((nsel, blk), MASKED, F32)
        for _ in range(MOBA_TOPK):
            m = jnp.max(g, axis=0, keepdims=True)
            hit = (g == m) & (m > -jnp.inf)
            first = jnp.min(jnp.where(hit, blk_rowf, float(nsel)), axis=0, keepdims=True)
            sel = blk_rowf == first
            bias = jnp.where(sel, 0.0, bias)
            g = jnp.where(sel, -jnp.inf, g)
        qs = q_t[hd * MOBA_HEAD_DIM:(hd + 1) * MOBA_HEAD_DIM]
        qa_ref[0, hd, 0] = jnp.concatenate([qs, bias, pad], axis=0).astype(BF16)


def _moba_select(qk3, kmean_pad):
    b, s, w = qk3.shape[0], qk3.shape[1], qk3.shape[2] // 2
    nq = s // MOBA_BLOCK
    return pl.pallas_call(
        _moba_select_kernel,
        grid=(b, nq),
        in_specs=[pl.BlockSpec((1, MOBA_BLOCK, w), lambda bi, i: (bi, i, 0)),
                  pl.BlockSpec((1, MOBA_MAX_BLOCKS, w), lambda bi, i: (bi, 0, 0))],
        out_specs=pl.BlockSpec((1, MOBA_HEADS, 1, LANES, MOBA_BLOCK), lambda bi, i: (bi, 0, i, 0, 0)),
        out_shape=jax.ShapeDtypeStruct((b, MOBA_HEADS, nq, LANES, MOBA_BLOCK), BF16),
        compiler_params=_params("parallel", "parallel"),
        name="moba_select",
    )(qk3, kmean_pad)


def _moba_kernel(qa_ref, ka_ref, vt_ref, o_ref, sa_ref, sb_ref, gma_ref, gmb_ref, m_ref, l_ref, acc_ref):
    blk = MOBA_BLOCK
    i = pl.program_id(2)
    ngroups = (i + MOBA_UNROLL) // MOBA_UNROLL
    key_pos = lax.broadcasted_iota(jnp.int32, (blk, blk), 0)
    qry_pos = lax.broadcasted_iota(jnp.int32, (blk, blk), 1)
    causal = key_pos <= qry_pos
    feat_row = lax.broadcasted_iota(jnp.int32, (LANES, blk), 0) < MOBA_HEAD_DIM

    def fold(s, op):
        return op(s.reshape(blk // 8, 8, blk), axis=0)

    for hh in range(2):
        m_ref[hh] = jnp.full((8, blk), MASKED, F32)
        l_ref[hh] = jnp.zeros((8, blk), F32)
        acc_ref[hh] = jnp.zeros((MOBA_HEAD_DIM, blk), F32)

    def score_group(grp, s_ref, gm_ref, first=False):
        for hh in range(2):
            gmax = None
            for u in range(MOBA_UNROLL):
                if first and u == 0:
                    qa_own = jnp.where(feat_row, qa_ref[0, hh, 0], jnp.zeros((), BF16))
                    s = jnp.where(causal, jnp.dot(ka_ref[0, hh, i], qa_own, preferred_element_type=F32), MASKED)
                else:
                    n = grp * MOBA_UNROLL + (u - 1)
                    s = jnp.dot(ka_ref[0, hh, n], qa_ref[0, hh, 0], preferred_element_type=F32)
                s_ref[hh, u] = s
                gmax = fold(s, jnp.max) if gmax is None else jnp.maximum(gmax, fold(s, jnp.max))
            gm_ref[hh] = gmax

    def value_group(grp, s_ref, gm_ref):
        for hh in range(2):
            m_old = m_ref[hh][0:1]
            m_new = jnp.maximum(m_old, jnp.max(gm_ref[hh], axis=0, keepdims=True))
            alpha = jnp.exp2(m_old - m_new)
            l8 = l_ref[hh] * alpha
            acc = acc_ref[hh] * alpha
            for u in range(MOBA_UNROLL):
                n = grp * MOBA_UNROLL + (u - 1)
                if u == 0:
                    n = jnp.where(grp == 0, i, n)
                p = jnp.exp2(s_ref[hh, u] - m_new)
                l8 = l8 + fold(p, jnp.sum)
                acc = acc + jnp.dot(vt_ref[0, hh, n], p.astype(BF16), preferred_element_type=F32)
            m_ref[hh] = jnp.broadcast_to(m_new, (8, blk))
            l_ref[hh] = l8
            acc_ref[hh] = acc

    score_group(0, sa_ref, gma_ref, first=True)

    npairs = (ngroups - 1) // 2

    def body(t, carry):
        score_group(2 * t + 1, sb_ref, gmb_ref)
        value_group(2 * t, sa_ref, gma_ref)
        score_group(2 * t + 2, sa_ref, gma_ref)
        value_group(2 * t + 1, sb_ref, gmb_ref)
        return carry

    lax.fori_loop(0, npairs, body, 0)
    left = ngroups - 2 * npairs

    @pl.when(left == 2)
    def _():
        score_group(2 * npairs + 1, sb_ref, gmb_ref)
        value_group(2 * npairs, sa_ref, gma_ref)
        value_group(2 * npairs + 1, sb_ref, gmb_ref)

    @pl.when(left == 1)
    def _():
        value_group(2 * npairs, sa_ref, gma_ref)

    outs = [acc_ref[hh] / jnp.sum(l_ref[hh], axis=0, keepdims=True) for hh in range(2)]
    o_ref[0] = jnp.concatenate(outs, axis=0).T.astype(o_ref.dtype)


def _moba_attention(qa, ka, vt):
    b, heads, nq = qa.shape[:3]
    s = nq * MOBA_BLOCK
    pairs = heads // 2
    return pl.pallas_call(
        _moba_kernel,
        grid=(b, pairs, nq),
        in_specs=[
            pl.BlockSpec((1, 2, 1, LANES, MOBA_BLOCK), lambda bi, p, i: (bi, p, i, 0, 0)),
            pl.BlockSpec((1, 2, nq, MOBA_BLOCK, LANES), lambda bi, p, i: (bi, p, 0, 0, 0)),
            pl.BlockSpec((1, 2, nq, MOBA_HEAD_DIM, MOBA_BLOCK), lambda bi, p, i: (bi, p, 0, 0, 0)),
        ],
        out_specs=pl.BlockSpec((1, MOBA_BLOCK, LANES), lambda bi, p, i: (bi, i, p)),
        out_shape=jax.ShapeDtypeStruct((b, s, heads * MOBA_HEAD_DIM), BF16),
        scratch_shapes=[
            pltpu.VMEM((2, MOBA_UNROLL, MOBA_BLOCK, MOBA_BLOCK), F32),
            pltpu.VMEM((2, MOBA_UNROLL, MOBA_BLOCK, MOBA_BLOCK), F32),
            pltpu.VMEM((2, 8, MOBA_BLOCK), F32),
            pltpu.VMEM((2, 8, MOBA_BLOCK), F32),
            pltpu.VMEM((2, 8, MOBA_BLOCK), F32),
            pltpu.VMEM((2, 8, MOBA_BLOCK), F32),
            pltpu.VMEM((2, MOBA_HEAD_DIM, MOBA_BLOCK), F32),
        ],
        compiler_params=_params("parallel", "parallel", "arbitrary"),
        name="moba_attention",
    )(qa, ka, vt)


def _ssd_kernel(xbc_ref, zs_ref, dt_ref, alog_ref, dskip_ref, gn_ref, ex_ref, o_ref, state_ref):
    q = SSD_CHUNK
    d_in = SSD_HEADS * SSD_HEAD_DIM
    bc_w = SSD_GROUPS * SSD_STATE
    gw = d_in // SSD_GROUPS
    c = pl.program_id(1)

    @pl.when(c == 0)
    def _():
        state_ref[...] = jnp.zeros_like(state_ref)

    xs = xbc_ref[0, :, :d_in].astype(F32)
    bm = xbc_ref[0, :, d_in:d_in + bc_w]
    cm = xbc_ref[0, :, d_in + bc_w:]
    dt = dt_ref[0]
    a = dt * (-jnp.exp(alog_ref[...]))
    row = lax.broadcasted_iota(jnp.int32, (q, q), 0)
    col = lax.broadcasted_iota(jnp.int32, (q, q), 1)
    tril = col <= row
    a_cum = jnp.dot(tril.astype(F32), a, precision=HIGHEST, preferred_element_type=F32)
    a_cum_t = a_cum.T
    ex = ex_ref[...]
    dt_x = _split3_dot(dt, ex)
    acum_x = _split3_dot(a_cum, ex)
    alast_x = acum_x[q - 1:q, :]
    x_dt = xs * dt_x
    xd = (x_dt * jnp.exp(alast_x - acum_x)).astype(BF16)
    x_dt_b = x_dt.astype(BF16)
    e_acum = jnp.exp(acum_x)
    e_alast = jnp.exp(alast_x)
    lane2 = lax.broadcasted_iota(jnp.int32, (q, 2 * SSD_HEAD_DIM), 1)
    heads_per_group = SSD_HEADS // SSD_GROUPS

    for g in range(SSD_GROUPS):
        bg = bm[:, g * SSD_STATE:(g + 1) * SSD_STATE]
        cg = cm[:, g * SSD_STATE:(g + 1) * SSD_STATE]
        cb = _nt_dot(cg, bg)
        h_in = state_ref[g]
        y_g = jnp.dot(cg, h_in.astype(BF16), preferred_element_type=F32) * e_acum[:, g * gw:(g + 1) * gw]
        parts = []
        for pr in range(heads_per_group // 2):
            xp = x_dt_b[:, g * gw + pr * 2 * SSD_HEAD_DIM:g * gw + (pr + 1) * 2 * SSD_HEAD_DIM]
            ys = []
            for hh in range(2):
                hd = g * heads_per_group + pr * 2 + hh
                seg = a_cum[:, hd:hd + 1] - a_cum_t[hd:hd + 1, :]
                lmat = jnp.exp(jnp.where(tril, seg, -jnp.inf))
                ys.append(jnp.dot((cb * lmat).astype(BF16), xp, preferred_element_type=F32))
            parts.append(jnp.where(lane2 < SSD_HEAD_DIM, ys[0], ys[1]))
        y_g = y_g + jnp.concatenate(parts, axis=1)
        st = jnp.dot(bg.astype(F32).T.astype(BF16), xd[:, g * gw:(g + 1) * gw], preferred_element_type=F32)
        state_ref[g] = h_in * e_alast[:, g * gw:(g + 1) * gw] + st

        sl = slice(g * gw, (g + 1) * gw)
        y_g = y_g + dskip_ref[:, sl] * xs[:, sl]
        y_g = y_g * zs_ref[0, :, sl].astype(F32)
        r = lax.rsqrt(jnp.mean(y_g * y_g, axis=-1, keepdims=True) + EPS)
        o_ref[0, :, sl] = ((y_g * r) * gn_ref[:, sl]).astype(o_ref.dtype)


def _ssd(xbc3, zs3, dt3, a_log, d_skip, ssd_norm):
    b, s, xw = xbc3.shape
    d_in = SSD_HEADS * SSD_HEAD_DIM
    nc = s // SSD_CHUNK
    pad = LANES - SSD_HEADS
    alog = jnp.pad(a_log.astype(F32), (0, pad)).reshape(1, LANES)
    dskip = jnp.repeat(d_skip.astype(F32), SSD_HEAD_DIM).reshape(1, d_in)
    expand = (jnp.arange(LANES)[:, None] == (jnp.arange(d_in) // SSD_HEAD_DIM)[None, :]).astype(BF16)
    const = lambda shape: pl.BlockSpec(shape, lambda bi, ci: (0,) * len(shape))
    return pl.pallas_call(
        _ssd_kernel,
        grid=(b, nc),
        in_specs=[
            pl.BlockSpec((1, SSD_CHUNK, xw), lambda bi, ci: (bi, ci, 0)),
            pl.BlockSpec((1, SSD_CHUNK, d_in), lambda bi, ci: (bi, ci, 0)),
            pl.BlockSpec((1, SSD_CHUNK, LANES), lambda bi, ci: (bi, ci, 0)),
            const((1, LANES)), const((1, d_in)), const((1, d_in)), const((LANES, d_in)),
        ],
        out_specs=pl.BlockSpec((1, SSD_CHUNK, d_in), lambda bi, ci: (bi, ci, 0)),
        out_shape=jax.ShapeDtypeStruct((b, s, d_in), BF16),
        scratch_shapes=[pltpu.VMEM((SSD_GROUPS, SSD_STATE, d_in // SSD_GROUPS), F32)],
        compiler_params=_params("parallel", "arbitrary"),
        name="ssd_scan",
    )(xbc3, zs3, dt3, alog, dskip, ssd_norm.reshape(1, d_in), expand)


def _memkv_kernel(mem_ref, g_ref, w_ref, kg_ref, km_ref, vm_ref):
    m = mem_ref[0]
    r = lax.rsqrt(jnp.mean(m * m, axis=-1, keepdims=True) + EPS)
    mn = ((m * r) * g_ref[...]).astype(BF16)
    kv = jnp.dot(mn, w_ref[...], preferred_element_type=F32)
    mw = MEM_HEADS * MEM_HEAD_DIM
    for hd in range(MEM_HEADS):
        sl = slice(hd * MEM_HEAD_DIM, (hd + 1) * MEM_HEAD_DIM)
        kh = kv[:, sl]
        rk = lax.rsqrt(jnp.mean(kh * kh, axis=-1, keepdims=True) + EPS)
        km_ref[0, :, sl] = ((kh * rk) * kg_ref[...]).astype(km_ref.dtype)
    vm_ref[0] = kv[:, mw:].astype(vm_ref.dtype)


def _memkv(mem, g_mem, w_kv, k_gain):
    b, m, d = mem.shape
    mw = MEM_HEADS * MEM_HEAD_DIM
    return pl.pallas_call(
        _memkv_kernel,
        grid=(b,),
        in_specs=[
            pl.BlockSpec((1, m, d), lambda bi: (bi, 0, 0)),
            pl.BlockSpec((1, d), lambda bi: (0, 0)),
            pl.BlockSpec((d, 2 * mw), lambda bi: (0, 0)),
            pl.BlockSpec((1, MEM_HEAD_DIM), lambda bi: (0, 0)),
        ],
        out_specs=[pl.BlockSpec((1, m, mw), lambda bi: (bi, 0, 0)), pl.BlockSpec((1, m, mw), lambda bi: (bi, 0, 0))],
        out_shape=[jax.ShapeDtypeStruct((b, m, mw), BF16), jax.ShapeDtypeStruct((b, m, mw), BF16)],
        compiler_params=_params("parallel"),
        name="mem_kv",
    )(mem, g_mem.reshape(1, d), w_kv, k_gain.reshape(1, MEM_HEAD_DIM))


def _memattn_kernel(h_ref, w_ref, qg_ref, km_ref, vm_ref, o_ref):
    qm = jnp.dot(h_ref[...], w_ref[...], preferred_element_type=F32)
    scale = MEM_HEAD_DIM ** -0.5
    for hd in range(MEM_HEADS):
        sl = slice(hd * MEM_HEAD_DIM, (hd + 1) * MEM_HEAD_DIM)
        qh = qm[:, sl]
        r = lax.rsqrt(jnp.mean(qh * qh, axis=-1, keepdims=True) + EPS)
        qn = ((qh * r) * qg_ref[...]).astype(BF16)
        s = _nt_dot(qn, km_ref[0, :, sl]) * scale
        p = jnp.exp(s - jnp.max(s, axis=-1, keepdims=True))
        l = jnp.sum(p, axis=-1, keepdims=True)
        o = jnp.dot(p.astype(BF16), vm_ref[0, :, sl], preferred_element_type=F32)
        o_ref[:, sl] = (o / l).astype(o_ref.dtype)


def _memattn(h, w_qm, q_gain, km, vm, seq, tm=512):
    n, d = h.shape
    b, m, mw = km.shape
    tiles_per_seq = seq // tm
    return pl.pallas_call(
        _memattn_kernel,
        grid=(n // tm,),
        in_specs=[
            pl.BlockSpec((tm, d), lambda i: (i, 0)),
            pl.BlockSpec((d, mw), lambda i: (0, 0)),
            pl.BlockSpec((1, MEM_HEAD_DIM), lambda i: (0, 0)),
            pl.BlockSpec((1, m, mw), lambda i: (i // tiles_per_seq, 0, 0)),
            pl.BlockSpec((1, m, mw), lambda i: (i // tiles_per_seq, 0, 0)),
        ],
        out_specs=pl.BlockSpec((tm, mw), lambda i: (i, 0)),
        out_shape=jax.ShapeDtypeStruct((n, mw), BF16),
        compiler_params=_params("parallel"),
        name="mem_attention",
    )(h, w_qm, q_gain.reshape(1, MEM_HEAD_DIM), km, vm)


def _merge_kernel(x_ref, h_ref, oa_ref, os_ref, om_ref, wg_ref, wa_ref, ws_ref, wm_ref, wo_ref, gf_ref, wr_ref,
                  x1_ref, h2_ref, route_ref):
    d = x_ref.shape[1]
    gates = _sigmoid(jnp.dot(h_ref[...], wg_ref[...], preferred_element_type=F32))
    merged = gates[:, :d] * jnp.dot(oa_ref[...], wa_ref[...], preferred_element_type=F32)
    merged = merged + gates[:, d:2 * d] * jnp.dot(os_ref[...], ws_ref[...], preferred_element_type=F32)
    merged = merged + gates[:, 2 * d:] * jnp.dot(om_ref[...], wm_ref[...], preferred_element_type=F32)
    x1 = x_ref[...] + jnp.dot(merged.astype(BF16), wo_ref[...], preferred_element_type=F32)
    x1_ref[...] = x1
    r = lax.rsqrt(jnp.mean(x1 * x1, axis=-1, keepdims=True) + EPS)
    h2 = (x1 * r) * gf_ref[...]
    for c in range(d // LANES):
        h2_ref[:, c, :] = h2[:, c * LANES:(c + 1) * LANES]

    h2_hi = h2.astype(BF16)
    h2_lo = (h2 - h2_hi.astype(F32)).astype(BF16)
    lg = (jnp.dot(h2_hi, wr_ref[0], preferred_element_type=F32) + jnp.dot(h2_hi, wr_ref[1], preferred_element_type=F32)
          + jnp.dot(h2_lo, wr_ref[0], preferred_element_type=F32))
    lanef = lax.broadcasted_iota(jnp.int32, lg.shape, 1).astype(F32)
    big = float(LANES)
    gmask = lanef < MOE_GROUPS
    gl = jnp.where(gmask, lg, -jnp.inf)
    gmax = jnp.max(gl, axis=-1, keepdims=True)
    p_g = 1.0 / jnp.sum(jnp.exp(gl - gmax), axis=-1, keepdims=True)
    g_sel = jnp.min(jnp.where(gl == gmax, lanef, big), axis=-1, keepdims=True)
    lo = MOE_GROUPS + MOE_EXPERTS_PER_GROUP * g_sel
    el = jnp.where((lanef >= lo) & (lanef < lo + MOE_EXPERTS_PER_GROUP), lg, -jnp.inf)
    m1 = jnp.max(el, axis=-1, keepdims=True)
    i1 = jnp.min(jnp.where(el == m1, lanef, big), axis=-1, keepdims=True)
    el2 = jnp.where(lanef == i1, -jnp.inf, el)
    m2 = jnp.max(el2, axis=-1, keepdims=True)
    i2 = jnp.min(jnp.where(el2 == m2, lanef, big), axis=-1, keepdims=True)
    e2 = jnp.exp(m2 - m1)
    w1 = 1.0 / (1.0 + e2)
    w2 = e2 / (1.0 + e2)
    route = jnp.where(lanef == 0, i1 - MOE_GROUPS, 0.0)
    route = jnp.where(lanef == 1, i2 - MOE_GROUPS, route)
    route = jnp.where(lanef == 2, p_g * w1, route)
    route = jnp.where(lanef == 3, p_g * w2, route)
    route_ref[...] = route


def _merge(x2, h, o_a, o_s, o_m, w_gates, w_a, w_s, w_m, w_out, g_ffn, w_router, tm=512):
    n, d = x2.shape
    full = lambda arr: pl.BlockSpec(arr.shape, lambda i: (0,) * arr.ndim)
    tile = lambda arr: pl.BlockSpec((tm, arr.shape[1]), lambda i: (i, 0))
    gf = g_ffn.reshape(1, d)
    return pl.pallas_call(
        _merge_kernel,
        grid=(n // tm,),
        in_specs=[tile(x2), tile(h), tile(o_a), tile(o_s), tile(o_m), full(w_gates), full(w_a), full(w_s), full(w_m),
                  full(w_out), full(gf), full(w_router)],
        out_specs=[pl.BlockSpec((tm, d), lambda i: (i, 0)),
                   pl.BlockSpec((tm, d // LANES, LANES), lambda i: (i, 0, 0)),
                   pl.BlockSpec((tm, LANES), lambda i: (i, 0))],
        out_shape=[jax.ShapeDtypeStruct((n, d), F32), jax.ShapeDtypeStruct((n, d // LANES, LANES), F32),
                   jax.ShapeDtypeStruct((n, LANES), F32)],
        compiler_params=_params("parallel"),
        name="merge_router",
    )(x2, h, o_a, o_s, o_m, w_gates, w_a, w_s, w_m, w_out, gf, w_router)


def _moe_kernel(be_ref, j0_ref, nv_ref, nu_ref, ord_ref, h2_hbm, wg_ref, wu_ref, wd_ref, y2_hbm, xbuf, ybuf, gsem,
                ssem):
    rows = xbuf.shape[1]
    n_tok = h2_hbm.shape[0]
    nblocks = pl.num_programs(0)
    i = pl.program_id(0)
    nused = nu_ref[0]
    slot = lax.rem(i, 2)

    def per_row(blk, fn):
        j0 = j0_ref[blk]
        last = nv_ref[blk] - 1

        def body(c, carry):
            for k in range(MOE_DMA_UNROLL):
                r = c * MOE_DMA_UNROLL + k
                fn(r, ord_ref[j0 + jnp.minimum(r, last)], r <= last, k % 2)
            return carry
        lax.fori_loop(0, rows // MOE_DMA_UNROLL, body, 0)

    def gather_start(blk, sl):
        def fn(r, a, real, prio):
            tok = lax.shift_right_logical(a, 1)
            pltpu.make_async_copy(h2_hbm.at[pl.ds(tok, 1)], xbuf.at[sl, pl.ds(r, 1)], gsem.at[sl]).start(priority=prio)
        per_row(blk, fn)

    def gather_wait(sl):
        pltpu.make_async_copy(h2_hbm.at[pl.ds(0, rows)], xbuf.at[sl], gsem.at[sl]).wait()

    def scatter_start(blk, sl):
        def fn(r, a, real, prio):
            dst = jnp.where(real, (a & 1) * n_tok + lax.shift_right_logical(a, 1), MOE_TOPK * n_tok + sl * rows + r)
            pltpu.make_async_copy(ybuf.at[sl, pl.ds(r, 1)], y2_hbm.at[pl.ds(dst, 1)], ssem.at[sl]).start(priority=prio)
        per_row(blk, fn)

    def scatter_wait(sl):
        pltpu.make_async_copy(ybuf.at[sl], y2_hbm.at[pl.ds(0, rows)], ssem.at[sl]).wait()

    @pl.when(i == 0)
    def _():
        gather_start(0, 0)
        ybuf[...] = jnp.zeros_like(ybuf)
        for sl in range(2):
            dump = y2_hbm.at[pl.ds(MOE_TOPK * n_tok + sl * rows, rows)]
            cp = pltpu.make_async_copy(ybuf.at[sl], dump, ssem.at[sl])
            cp.start()
            cp.wait()

    @pl.when(i + 1 < nused)
    def _():
        gather_start(i + 1, 1 - slot)

    @pl.when(i < nused)
    def _():
        gather_wait(slot)

        @pl.when(i >= 2)
        def _():
            scatter_wait(slot)

        chunks = xbuf.shape[2]
        x = jnp.concatenate([xbuf[slot, :, c, :] for c in range(chunks)], axis=1).astype(BF16)
        gate = jnp.dot(x, wg_ref[0], preferred_element_type=F32)
        up = jnp.dot(x, wu_ref[0], preferred_element_type=F32)
        hid = (gate * _sigmoid(gate)) * up
        y = jnp.dot(hid.astype(BF16), wd_ref[0], preferred_element_type=F32)
        for c in range(chunks):
            ybuf[slot, :, c, :] = y[:, c * LANES:(c + 1) * LANES]
        scatter_start(i, slot)

    @pl.when(i == nblocks - 1)
    def _():
        @pl.when(nused >= 2)
        def _():
            scatter_wait(lax.rem(nused, 2))

        scatter_wait(lax.rem(nused - 1, 2))


def _moe_experts(h2, tables, w_gate, w_up, w_down):
    n = h2.shape[0]
    e, d, f = w_gate.shape
    nblocks = tables[0].shape[0]
    wmap = lambda i, be, j0, nv, nu, od: (be[i], 0, 0)
    grid_spec = pltpu.PrefetchScalarGridSpec(
        num_scalar_prefetch=5,
        grid=(nblocks,),
        in_specs=[
            pl.BlockSpec(memory_space=pl.ANY),
            pl.BlockSpec((1, d, f), wmap),
            pl.BlockSpec((1, d, f), wmap),
            pl.BlockSpec((1, f, d), wmap),
        ],
        out_specs=pl.BlockSpec(memory_space=pl.ANY),
        scratch_shapes=[
            pltpu.VMEM((2, MOE_ROWS, d // LANES, LANES), F32),
            pltpu.VMEM((2, MOE_ROWS, d // LANES, LANES), F32),
            pltpu.SemaphoreType.DMA((2,)),
            pltpu.SemaphoreType.DMA((2,)),
        ],
    )
    return pl.pallas_call(
        _moe_kernel,
        grid_spec=grid_spec,
        out_shape=jax.ShapeDtypeStruct((n * MOE_TOPK + 2 * MOE_ROWS, d // LANES, LANES), F32),
        compiler_params=_params("arbitrary"),
        name="moe_experts",
    )(*tables, h2, w_gate, w_up, w_down)


def _combine_kernel(x1_ref, y0_ref, y1_ref, route_ref, o_ref):
    route = route_ref[...]
    w0 = route[:, 2:3]
    w1 = route[:, 3:4]
    for c in range(y0_ref.shape[1]):
        sl = slice(c * LANES, (c + 1) * LANES)
        o_ref[:, sl] = x1_ref[:, sl] + (w0 * y0_ref[:, c, :] + w1 * y1_ref[:, c, :])


def _combine(x1, y2, route, tm=512):
    n, d = x1.shape
    tiles = n // tm
    ytile = (tm, d // LANES, LANES)
    return pl.pallas_call(
        _combine_kernel,
        grid=(tiles,),
        in_specs=[pl.BlockSpec((tm, d), lambda i: (i, 0)), pl.BlockSpec(ytile, lambda i: (i, 0, 0)),
                  pl.BlockSpec(ytile, lambda i: (tiles + i, 0, 0)), pl.BlockSpec((tm, LANES), lambda i: (i, 0))],
        out_specs=pl.BlockSpec((tm, d), lambda i: (i, 0)),
        out_shape=jax.ShapeDtypeStruct((n, d), F32),
        compiler_params=_params("parallel"),
        name="moe_combine",
    )(x1, y2, y2, route)


def _dispatch_tables(route, n_tok):
    n_assign = n_tok * MOE_TOPK
    e_flat = route[:, :MOE_TOPK].astype(jnp.int32).reshape(n_assign)
    order = jnp.argsort(e_flat).astype(jnp.int32)
    counts = jnp.sum(e_flat[:, None] == jnp.arange(MOE_EXPERTS, dtype=jnp.int32)[None, :], axis=0, dtype=jnp.int32)
    blocks_per_expert = (counts + MOE_ROWS - 1) // MOE_ROWS
    blk_end = jnp.cumsum(blocks_per_expert)
    raw_start = jnp.cumsum(counts) - counts
    nblocks = n_assign // MOE_ROWS + MOE_EXPERTS
    blk = jnp.arange(nblocks, dtype=jnp.int32)
    blk_expert = jnp.minimum(jnp.sum(blk_end[None, :] <= blk[:, None], axis=1), MOE_EXPERTS - 1).astype(jnp.int32)
    within = blk - (blk_end - blocks_per_expert)[blk_expert]
    blk_j0 = (raw_start[blk_expert] + within * MOE_ROWS).astype(jnp.int32)
    blk_nvalid = jnp.clip(counts[blk_expert] - within * MOE_ROWS, 0, MOE_ROWS).astype(jnp.int32)
    nused = blk_end[-1].astype(jnp.int32).reshape(1)
    return blk_expert, blk_j0, blk_nvalid, nused, order


def kernel(x, mem, g_mix, w_in, moba_q_norm, moba_k_norm, conv_w, conv_b, dt_bias, a_log, d_skip, ssd_norm, g_mem,
           w_mem_kv, mem_q_norm, mem_k_norm, w_o_moba, w_o_ssd, w_o_mem, w_out, g_ffn, w_router_group,
           w_router_expert, w_gate, w_up, w_down):
    bsz, seq, d = x.shape
    assert seq % MOBA_BLOCK == 0 and seq % SSD_CHUNK == 0
    n_tok = bsz * seq
    moba_w = MOBA_HEADS * MOBA_HEAD_DIM
    d_in = SSD_HEADS * SSD_HEAD_DIM
    xbc_w = d_in + 2 * SSD_GROUPS * SSD_STATE
    mem_w = MEM_HEADS * MEM_HEAD_DIM
    sizes = (moba_w, moba_w, moba_w, d_in, xbc_w, SSD_HEADS, mem_w, 3 * d)
    offs = [0]
    for sz in sizes:
        offs.append(offs[-1] + sz)
    w_in_b = w_in.astype(BF16)
    w_qk = w_in_b[:, offs[0]:offs[2]]
    w_v = w_in_b[:, offs[2]:offs[3]]
    w_z = w_in_b[:, offs[3]:offs[4]]
    w_xbc = w_in_b[:, offs[4]:offs[5]]
    w_dt = jnp.pad(w_in_b[:, offs[5]:offs[6]], ((0, 0), (0, LANES - SSD_HEADS)))
    w_qm = w_in_b[:, offs[6]:offs[7]]
    w_gates = w_in_b[:, offs[7]:offs[8]]

    x2 = x.reshape(n_tok, d)

    half = MOBA_HEAD_DIM // 2
    inv = ROPE_THETA ** (-jnp.arange(half, dtype=F32) / half)
    ang = jnp.arange(seq, dtype=F32)[:, None] * inv[None, :]
    cos_t = jnp.tile(jnp.cos(ang), (1, LANES // half))
    sin_t = jnp.tile(jnp.concatenate([-jnp.sin(ang), jnp.sin(ang)], axis=1), (1, LANES // MOBA_HEAD_DIM))
    gains = jnp.stack([jnp.tile(moba_q_norm, MOBA_HEADS), jnp.tile(moba_k_norm, MOBA_HEADS)]).reshape(2, 1, moba_w)

    h, qk, v, zs, xbc, dt = _inproj(x2, g_mix, w_qk, w_v, w_z, w_xbc, w_dt, gains, cos_t, sin_t, conv_w, conv_b,
                                    dt_bias, seq)

    qk3 = qk.reshape(bsz, seq, 2 * moba_w)
    v3 = v.reshape(bsz, seq, moba_w)
    kmean, ka, vt = _moba_prep(qk3, v3)
    nb = seq // MOBA_BLOCK
    assert nb <= MOBA_MAX_BLOCKS and nb % MOBA_UNROLL == 0
    kmean_pad = jnp.pad(kmean.reshape(bsz, nb, moba_w), ((0, 0), (0, MOBA_MAX_BLOCKS - nb), (0, 0)))
    qa = _moba_select(qk3, kmean_pad)
    o_a = _moba_attention(qa, ka, vt).reshape(n_tok, moba_w)

    o_s = _ssd(xbc.reshape(bsz, seq, xbc_w), zs.reshape(bsz, seq, d_in), dt.reshape(bsz, seq, LANES), a_log, d_skip,
               ssd_norm).reshape(n_tok, d_in)

    km, vm = _memkv(mem, g_mem, w_mem_kv.astype(BF16), mem_k_norm)
    o_m = _memattn(h, w_qm, mem_q_norm, km, vm, seq)

    w_router = jnp.pad(jnp.concatenate([w_router_group, w_router_expert], axis=1),
                       ((0, 0), (0, LANES - MOE_GROUPS - MOE_EXPERTS)))
    w_router_hi = w_router.astype(BF16)
    w_router = jnp.stack([w_router_hi, (w_router - w_router_hi.astype(F32)).astype(BF16)])
    x1, h2, route = _merge(x2, h, o_a, o_s, o_m, w_gates, w_o_moba.astype(BF16), w_o_ssd.astype(BF16),
                           w_o_mem.astype(BF16), w_out.astype(BF16), g_ffn, w_router)

    tables = _dispatch_tables(route, n_tok)
    y2 = _moe_experts(h2, tables, w_gate.astype(BF16), w_up.astype(BF16), w_down.astype(BF16))
    out = _combine(x1, y2, route)
    return out.reshape(bsz, seq, d)
```

```python
import functools

import jax
import jax.numpy as jnp
from jax import lax
from jax.experimental import pallas as pl
from jax.experimental.pallas import tpu as pltpu

F32 = jnp.float32
BF16 = jnp.bfloat16
HIGHEST = lax.Precision.HIGHEST

EPS = 1e-6
ROPE_THETA = 10000.0
MOBA_HEADS = 8
MOBA_HEAD_DIM = 64
MOBA_BLOCK = 256
MOBA_TOPK = 3
SSD_HEAD_DIM = 64
SSD_HEADS = 16
SSD_GROUPS = 4
SSD_STATE = 128
SSD_CONV = 4
SSD_CHUNK = 256
MEM_HEADS = 4
MEM_HEAD_DIM = 128
MOE_GROUPS = 4
MOE_EXPERTS_PER_GROUP = 8
MOE_EXPERTS = MOE_GROUPS * MOE_EXPERTS_PER_GROUP
MOE_TOPK = 2

LANES = 128
MASKED = -1e30
MOBA_MAX_BLOCKS = 32
MOBA_UNROLL = 4
MOBA_VT_ROWS = MOBA_HEAD_DIM + 16
MOE_ROWS = 256
MOE_DMA_UNROLL = 8
VMEM_LIMIT = 56 * 1024 * 1024


def _params(*sem):
    return pltpu.CompilerParams(dimension_semantics=sem, vmem_limit_bytes=VMEM_LIMIT)


def _sigmoid(x):
    return 1.0 / (1.0 + jnp.exp(-x))


def _split3_dot(x, sel):
    hi = x.astype(BF16)
    rest = x - hi.astype(F32)
    mid = rest.astype(BF16)
    lo = (rest - mid.astype(F32)).astype(BF16)
    return (jnp.dot(hi, sel, preferred_element_type=F32) + jnp.dot(mid, sel, preferred_element_type=F32)
            + jnp.dot(lo, sel, preferred_element_type=F32))


def _nt_dot(a, b, precision=None):
    return lax.dot_general(a, b, (((1,), (1,)), ((), ())), precision=precision, preferred_element_type=F32)


def _softplus(x):
    return jnp.maximum(x, 0.0) + jnp.log1p(jnp.exp(-jnp.abs(x)))


def _inproj_kernel(x_ref, gmix_ref, wqk_ref, wv_ref, wz_ref, wxbc_ref, wdt_ref, gain_ref, gsum_ref, cos_ref, sin_ref,
                   cw_ref, cb_ref, dtb_ref, h_ref, qk_ref, v_ref, zs_ref, xbc_ref, dt_ref, ext_ref, *,
                   tiles_per_seq):
    tm = x_ref.shape[0]
    x = x_ref[...]
    r = lax.rsqrt(jnp.mean(x * x, axis=-1, keepdims=True) + EPS)
    h = ((x * r) * gmix_ref[...]).astype(BF16)
    h_ref[...] = h

    width = gsum_ref.shape[0]
    half = MOBA_HEAD_DIM // 2
    lane = lax.broadcasted_iota(jnp.int32, (tm, width), 1)
    first = (lane & (MOBA_HEAD_DIM - 1)) < half
    reps = width // cos_ref.shape[1]
    cos = jnp.concatenate([cos_ref[...]] * reps, axis=1)
    sin = jnp.concatenate([sin_ref[...]] * reps, axis=1)
    for j in range(2):
        u = jnp.dot(h, wqk_ref[:, j * width:(j + 1) * width], preferred_element_type=F32)
        sq = u * u
        hi = sq.astype(BF16)
        lo = (sq - hi.astype(F32)).astype(BF16)
        ss = (jnp.dot(hi, gsum_ref[...], preferred_element_type=F32)
              + jnp.dot(lo, gsum_ref[...], preferred_element_type=F32))
        un = (u * lax.rsqrt(ss * (1.0 / MOBA_HEAD_DIM) + EPS)) * gain_ref[j]
        partner = jnp.where(first, pltpu.roll(un, width - half, axis=1), pltpu.roll(un, half, axis=1))
        qk_ref[:, j * width:(j + 1) * width] = un * cos + partner * sin

    v_ref[...] = jnp.dot(h, wv_ref[...], preferred_element_type=F32).astype(v_ref.dtype)
    z = jnp.dot(h, wz_ref[...], preferred_element_type=F32)
    zs_ref[...] = (z * _sigmoid(z)).astype(zs_ref.dtype)
    dt_ref[...] = _softplus(jnp.dot(h, wdt_ref[...], preferred_element_type=F32) + dtb_ref[...])

    @pl.when(pl.program_id(0) % tiles_per_seq == 0)
    def _():
        ext_ref[0:8, :] = jnp.zeros((8, ext_ref.shape[1]), F32)

    cchunk = 512
    for c0 in range(0, wxbc_ref.shape[1], cchunk):
        cs = slice(c0, c0 + cchunk)
        u = jnp.dot(h, wxbc_ref[:, cs], preferred_element_type=F32)
        ext_ref[8:8 + tm, cs] = u
        acc = cb_ref[:, cs] + cw_ref[SSD_CONV - 1:SSD_CONV, cs] * u
        for kk in range(SSD_CONV - 1):
            off = 8 - (SSD_CONV - 1 - kk)
            acc = acc + cw_ref[kk:kk + 1, cs] * ext_ref[off:off + tm, cs]
        ext_ref[0:8, cs] = u[tm - 8:tm, :]
        xbc_ref[:, cs] = (acc * _sigmoid(acc)).astype(xbc_ref.dtype)


def _inproj(x2, g_mix, w_qk, w_v, w_z, w_xbc, w_dt, gains, cos_t, sin_t, conv_w, conv_b, dt_bias, seq, tm=512):
    n, d = x2.shape
    width = w_qk.shape[1] // 2
    tiles_per_seq = seq // tm
    head = jnp.arange(width) // MOBA_HEAD_DIM
    gsum = (head[:, None] == head[None, :]).astype(BF16)
    dtb = jnp.pad(dt_bias.astype(F32), (0, LANES - dt_bias.shape[0])).reshape(1, LANES)
    full = lambda arr: pl.BlockSpec(arr.shape, lambda i: (0,) * arr.ndim)
    tile = lambda cols: pl.BlockSpec((tm, cols), lambda i: (i, 0))
    rope = pl.BlockSpec((tm, LANES), lambda i: (i % tiles_per_seq, 0))
    consts = (g_mix.reshape(1, d), w_qk, w_v, w_z, w_xbc, w_dt, gains, gsum)
    tail = (conv_w, conv_b.reshape(1, -1), dtb)
    outs = ((d, BF16), (2 * width, F32), (w_v.shape[1], BF16), (w_z.shape[1], BF16), (w_xbc.shape[1], BF16),
            (LANES, F32))
    return pl.pallas_call(
        functools.partial(_inproj_kernel, tiles_per_seq=tiles_per_seq),
        grid=(n // tm,),
        in_specs=[tile(d)] + [full(a) for a in consts] + [rope, rope] + [full(a) for a in tail],
        out_specs=[tile(c) for c, _ in outs],
        out_shape=[jax.ShapeDtypeStruct((n, c), dt) for c, dt in outs],
        scratch_shapes=[pltpu.VMEM((tm + 8, w_xbc.shape[1]), F32)],
        compiler_params=_params("arbitrary"),
        name="in_proj",
    )(x2, *consts, cos_t, sin_t, *tail)


def _moba_prep_kernel(k_ref, v_ref, km_ref, ka_ref, vt_ref):
    n = pl.program_id(1)
    ones = jnp.ones((MOBA_VT_ROWS - MOBA_HEAD_DIM, MOBA_BLOCK), BF16)
    k = k_ref[0]
    km_ref[0, 0] = jnp.mean(k, axis=0, keepdims=True)
    lane = lax.broadcasted_iota(jnp.int32, (MOBA_BLOCK, LANES), 1)
    own = lane < MOBA_HEAD_DIM
    onehot = jnp.where(lane == MOBA_HEAD_DIM + n, 1.0, 0.0)
    for pr in range(MOBA_HEADS // 2):
        pair = slice(pr * LANES, (pr + 1) * LANES)
        kp = k[:, pair]
        ka_ref[0, 2 * pr, 0] = jnp.where(own, kp, onehot).astype(BF16)
        ka_ref[0, 2 * pr + 1, 0] = jnp.where(own, pltpu.roll(kp, MOBA_HEAD_DIM, axis=1), onehot).astype(BF16)
        vp_t = v_ref[0, :, pair].astype(F32).T
        for hh in range(2):
            vt_ref[0, 2 * pr + hh, 0, :MOBA_HEAD_DIM, :] = vp_t[hh * MOBA_HEAD_DIM:(hh + 1) * MOBA_HEAD_DIM].astype(BF16)
            vt_ref[0, 2 * pr + hh, 0, MOBA_HEAD_DIM:, :] = ones


def _moba_prep(qk3, v3):
    b, s, w = v3.shape
    nb = s // MOBA_BLOCK
    return pl.pallas_call(
        _moba_prep_kernel,
        grid=(b, nb),
        in_specs=[pl.BlockSpec((1, MOBA_BLOCK, w), lambda bi, n: (bi, n, 1)),
                  pl.BlockSpec((1, MOBA_BLOCK, w), lambda bi, n: (bi, n, 0))],
        out_specs=[
            pl.BlockSpec((1, 1, 1, w), lambda bi, n: (bi, n, 0, 0)),
            pl.BlockSpec((1, MOBA_HEADS, 1, MOBA_BLOCK, LANES), lambda bi, n: (bi, 0, n, 0, 0)),
            pl.BlockSpec((1, MOBA_HEADS, 1, MOBA_VT_ROWS, MOBA_BLOCK), lambda bi, n: (bi, 0, n, 0, 0)),
        ],
        out_shape=[jax.ShapeDtypeStruct((b, nb, 1, w), F32),
                   jax.ShapeDtypeStruct((b, MOBA_HEADS, nb, MOBA_BLOCK, LANES), BF16),
                   jax.ShapeDtypeStruct((b, MOBA_HEADS, nb, MOBA_VT_ROWS, MOBA_BLOCK), BF16)],
        compiler_params=_params("parallel", "parallel"),
        name="moba_prep",
    )(qk3, v3)


def _moba_select_kernel(q_ref, km_ref, qa_ref):
    blk = MOBA_BLOCK
    nsel = MOBA_MAX_BLOCKS
    i = pl.program_id(1)
    qscale = (MOBA_HEAD_DIM ** -0.5) * 1.4426950408889634
    q_t = (q_ref[0] * qscale).T
    kmean = km_ref[0]
    lane = lax.broadcasted_iota(jnp.int32, (nsel, LANES), 1)
    blk_row = lax.broadcasted_iota(jnp.int32, (nsel, blk), 0)
    blk_rowf = blk_row.astype(F32)
    pad = jnp.zeros((LANES - MOBA_HEAD_DIM - nsel, blk), F32)
    for hd in range(MOBA_HEADS):
        pr, hh = hd // 2, hd % 2
        hmask = (lane >= hh * MOBA_HEAD_DIM) & (lane < (hh + 1) * MOBA_HEAD_DIM)
        km_h = jnp.where(hmask, kmean[:, pr * LANES:(pr + 1) * LANES], 0.0)
        gate = jnp.dot(km_h, q_t[pr * LANES:(pr + 1) * LANES], precision=HIGHEST, preferred_element_type=F32)
        g = jnp.where(blk_row < i, gate, -jnp.inf)
        bias = jnp.full((nsel, blk), MASKED, F32)
        for _ in range(MOBA_TOPK):
            m = jnp.max(g, axis=0, keepdims=True)
            hit = (g == m) & (m > -jnp.inf)
            first = jnp.min(jnp.where(hit, blk_rowf, float(nsel)), axis=0, keepdims=True)
            sel = blk_rowf == first
            bias = jnp.where(sel, 0.0, bias)
            g = jnp.where(sel, -jnp.inf, g)
        qs = q_t[hd * MOBA_HEAD_DIM:(hd + 1) * MOBA_HEAD_DIM]
        qa_ref[0, hd, 0] = jnp.concatenate([qs, bias, pad], axis=0).astype(BF16)


def _moba_select(qk3, kmean_pad):
    b, s, w = qk3.shape[0], qk3.shape[1], qk3.shape[2] // 2
    nq = s // MOBA_BLOCK
    return pl.pallas_call(
        _moba_select_kernel,
        grid=(b, nq),
        in_specs=[pl.BlockSpec((1, MOBA_BLOCK, w), lambda bi, i: (bi, i, 0)),
                  pl.BlockSpec((1, MOBA_MAX_BLOCKS, w), lambda bi, i: (bi, 0, 0))],
        out_specs=pl.BlockSpec((1, MOBA_HEADS, 1, LANES, MOBA_BLOCK), lambda bi, i: (bi, 0, i, 0, 0)),
        out_shape=jax.ShapeDtypeStruct((b, MOBA_HEADS, nq, LANES, MOBA_BLOCK), BF16),
        compiler_params=_params("parallel", "parallel"),
        name="moba_select",
    )(qk3, kmean_pad)


def _moba_kernel(qa_ref, ka_ref, vt_ref, o_ref, sa_ref, sb_ref, gma_ref, gmb_ref, m_ref, acc_ref):
    blk = MOBA_BLOCK
    i = pl.program_id(2)
    ngroups = (i + MOBA_UNROLL) // MOBA_UNROLL
    key_pos = lax.broadcasted_iota(jnp.int32, (blk, blk), 0)
    qry_pos = lax.broadcasted_iota(jnp.int32, (blk, blk), 1)
    causal = key_pos <= qry_pos
    feat_row = lax.broadcasted_iota(jnp.int32, (LANES, blk), 0) < MOBA_HEAD_DIM

    def fold(s, op):
        return op(s.reshape(blk // 8, 8, blk), axis=0)

    for hh in range(2):
        m_ref[hh] = jnp.full((8, blk), MASKED, F32)
        acc_ref[hh] = jnp.zeros((MOBA_VT_ROWS, blk), F32)

    def score_group(grp, s_ref, gm_ref, first=False):
        for hh in range(2):
            gmax = None
            for u in range(MOBA_UNROLL):
                if first and u == 0:
                    qa_own = jnp.where(feat_row, qa_ref[0, hh, 0], jnp.zeros((), BF16))
                    s = jnp.where(causal, jnp.dot(ka_ref[0, hh, i], qa_own, preferred_element_type=F32), MASKED)
                else:
                    n = grp * MOBA_UNROLL + (u - 1)
                    s = jnp.dot(ka_ref[0, hh, n], qa_ref[0, hh, 0], preferred_element_type=F32)
                s_ref[hh, u] = s
                gmax = fold(s, jnp.max) if gmax is None else jnp.maximum(gmax, fold(s, jnp.max))
            gm_ref[hh] = gmax

    def value_group(grp, s_ref, gm_ref):
        for hh in range(2):
            m_old = m_ref[hh][0:1]
            m_new = jnp.maximum(m_old, jnp.max(gm_ref[hh], axis=0, keepdims=True))
            alpha = jnp.exp2(m_old - m_new)
            acc = acc_ref[hh] * alpha
            for u in range(MOBA_UNROLL):
                n = grp * MOBA_UNROLL + (u - 1)
                if u == 0:
                    n = jnp.where(grp == 0, i, n)
                p = jnp.exp2(s_ref[hh, u] - m_new).astype(BF16)
                acc = acc + jnp.dot(vt_ref[0, hh, n], p, preferred_element_type=F32)
            m_ref[hh] = jnp.broadcast_to(m_new, (8, blk))
            acc_ref[hh] = acc

    score_group(0, sa_ref, gma_ref, first=True)

    npairs = (ngroups - 1) // 2

    def body(t, carry):
        score_group(2 * t + 1, sb_ref, gmb_ref)
        value_group(2 * t, sa_ref, gma_ref)
        score_group(2 * t + 2, sa_ref, gma_ref)
        value_group(2 * t + 1, sb_ref, gmb_ref)
        return carry

    lax.fori_loop(0, npairs, body, 0)
    left = ngroups - 2 * npairs

    @pl.when(left == 2)
    def _():
        score_group(2 * npairs + 1, sb_ref, gmb_ref)
        value_group(2 * npairs, sa_ref, gma_ref)
        value_group(2 * npairs + 1, sb_ref, gmb_ref)

    @pl.when(left == 1)
    def _():
        value_group(2 * npairs, sa_ref, gma_ref)

    outs = [acc_ref[hh, :MOBA_HEAD_DIM, :] / acc_ref[hh, MOBA_HEAD_DIM:MOBA_HEAD_DIM + 1, :] for hh in range(2)]
    o_ref[0] = jnp.concatenate(outs, axis=0).T.astype(o_ref.dtype)


def _moba_attention(qa, ka, vt):
    b, heads, nq = qa.shape[:3]
    s = nq * MOBA_BLOCK
    pairs = heads // 2
    return pl.pallas_call(
        _moba_kernel,
        grid=(b, pairs, nq),
        in_specs=[
            pl.BlockSpec((1, 2, 1, LANES, MOBA_BLOCK), lambda bi, p, i: (bi, p, i, 0, 0)),
            pl.BlockSpec((1, 2, nq, MOBA_BLOCK, LANES), lambda bi, p, i: (bi, p, 0, 0, 0)),
            pl.BlockSpec((1, 2, nq, MOBA_VT_ROWS, MOBA_BLOCK), lambda bi, p, i: (bi, p, 0, 0, 0)),
        ],
        out_specs=pl.BlockSpec((1, MOBA_BLOCK, LANES), lambda bi, p, i: (bi, i, p)),
        out_shape=jax.ShapeDtypeStruct((b, s, heads * MOBA_HEAD_DIM), BF16),
        scratch_shapes=[
            pltpu.VMEM((2, MOBA_UNROLL, MOBA_BLOCK, MOBA_BLOCK), F32),
            pltpu.VMEM((2, MOBA_UNROLL, MOBA_BLOCK, MOBA_BLOCK), F32),
            pltpu.VMEM((2, 8, MOBA_BLOCK), F32),
            pltpu.VMEM((2, 8, MOBA_BLOCK), F32),
            pltpu.VMEM((2, 8, MOBA_BLOCK), F32),
            pltpu.VMEM((2, MOBA_VT_ROWS, MOBA_BLOCK), F32),
        ],
        compiler_params=_params("parallel", "parallel", "arbitrary"),
        name="moba_attention",
    )(qa, ka, vt)


def _ssd_kernel(xbc_ref, zs_ref, dt_ref, alog_ref, dskip_ref, gn_ref, ex_ref, o_ref, state_ref):
    q = SSD_CHUNK
    d_in = SSD_HEADS * SSD_HEAD_DIM
    bc_w = SSD_GROUPS * SSD_STATE
    gw = d_in // SSD_GROUPS
    c = pl.program_id(1)

    @pl.when(c == 0)
    def _():
        state_ref[...] = jnp.zeros_like(state_ref)

    xs = xbc_ref[0, :, :d_in].astype(F32)
    bm = xbc_ref[0, :, d_in:d_in + bc_w]
    cm = xbc_ref[0, :, d_in + bc_w:]
    dt = dt_ref[0]
    a = dt * (-jnp.exp(alog_ref[...]))
    row = lax.broadcasted_iota(jnp.int32, (q, q), 0)
    col = lax.broadcasted_iota(jnp.int32, (q, q), 1)
    tril = col <= row
    a_cum = jnp.dot(tril.astype(F32), a, precision=HIGHEST, preferred_element_type=F32)
    a_cum_t = a_cum.T
    ex = ex_ref[...]
    dt_x = _split3_dot(dt, ex)
    acum_x = _split3_dot(a_cum, ex)
    alast_x = acum_x[q - 1:q, :]
    x_dt = xs * dt_x
    xd = (x_dt * jnp.exp(alast_x - acum_x)).astype(BF16)
    x_dt_b = x_dt.astype(BF16)
    e_acum = jnp.exp(acum_x)
    e_alast = jnp.exp(alast_x)
    lane2 = lax.broadcasted_iota(jnp.int32, (q, 2 * SSD_HEAD_DIM), 1)
    heads_per_group = SSD_HEADS // SSD_GROUPS

    for g in range(SSD_GROUPS):
        bg = bm[:, g * SSD_STATE:(g + 1) * SSD_STATE]
        cg = cm[:, g * SSD_STATE:(g + 1) * SSD_STATE]
        cb = _nt_dot(cg, bg)
        h_in = state_ref[g]
        y_g = jnp.dot(cg, h_in.astype(BF16), preferred_element_type=F32) * e_acum[:, g * gw:(g + 1) * gw]
        parts = []
        for pr in range(heads_per_group // 2):
            xp = x_dt_b[:, g * gw + pr * 2 * SSD_HEAD_DIM:g * gw + (pr + 1) * 2 * SSD_HEAD_DIM]
            ys = []
            for hh in range(2):
                hd = g * heads_per_group + pr * 2 + hh
                seg = a_cum[:, hd:hd + 1] - a_cum_t[hd:hd + 1, :]
                lmat = jnp.exp(jnp.where(tril, seg, -jnp.inf))
                ys.append(jnp.dot((cb * lmat).astype(BF16), xp, preferred_element_type=F32))
            parts.append(jnp.where(lane2 < SSD_HEAD_DIM, ys[0], ys[1]))
        y_g = y_g + jnp.concatenate(parts, axis=1)
        st = jnp.dot(bg.astype(F32).T.astype(BF16), xd[:, g * gw:(g + 1) * gw], preferred_element_type=F32)
        state_ref[g] = h_in * e_alast[:, g * gw:(g + 1) * gw] + st

        sl = slice(g * gw, (g + 1) * gw)
        y_g = y_g + dskip_ref[:, sl] * xs[:, sl]
        y_g = y_g * zs_ref[0, :, sl].astype(F32)
        r = lax.rsqrt(jnp.mean(y_g * y_g, axis=-1, keepdims=True) + EPS)
        o_ref[0, :, sl] = ((y_g * r) * gn_ref[:, sl]).astype(o_ref.dtype)


def _ssd(xbc3, zs3, dt3, a_log, d_skip, ssd_norm):
    b, s, xw = xbc3.shape
    d_in = SSD_HEADS * SSD_HEAD_DIM
    nc = s // SSD_CHUNK
    pad = LANES - SSD_HEADS
    alog = jnp.pad(a_log.astype(F32), (0, pad)).reshape(1, LANES)
    dskip = jnp.repeat(d_skip.astype(F32), SSD_HEAD_DIM).reshape(1, d_in)
    expand = (jnp.arange(LANES)[:, None] == (jnp.arange(d_in) // SSD_HEAD_DIM)[None, :]).astype(BF16)
    const = lambda shape: pl.BlockSpec(shape, lambda bi, ci: (0,) * len(shape))
    return pl.pallas_call(
        _ssd_kernel,
        grid=(b, nc),
        in_specs=[
            pl.BlockSpec((1, SSD_CHUNK, xw), lambda bi, ci: (bi, ci, 0)),
            pl.BlockSpec((1, SSD_CHUNK, d_in), lambda bi, ci: (bi, ci, 0)),
            pl.BlockSpec((1, SSD_CHUNK, LANES), lambda bi, ci: (bi, ci, 0)),
            const((1, LANES)), const((1, d_in)), const((1, d_in)), const((LANES, d_in)),
        ],
        out_specs=pl.BlockSpec((1, SSD_CHUNK, d_in), lambda bi, ci: (bi, ci, 0)),
        out_shape=jax.ShapeDtypeStruct((b, s, d_in), BF16),
        scratch_shapes=[pltpu.VMEM((SSD_GROUPS, SSD_STATE, d_in // SSD_GROUPS), F32)],
        compiler_params=_params("parallel", "arbitrary"),
        name="ssd_scan",
    )(xbc3, zs3, dt3, alog, dskip, ssd_norm.reshape(1, d_in), expand)


def _memkv_kernel(mem_ref, g_ref, w_ref, kg_ref, km_ref, vm_ref):
    m = mem_ref[0]
    r = lax.rsqrt(jnp.mean(m * m, axis=-1, keepdims=True) + EPS)
    mn = ((m * r) * g_ref[...]).astype(BF16)
    kv = jnp.dot(mn, w_ref[...], preferred_element_type=F32)
    mw = MEM_HEADS * MEM_HEAD_DIM
    for hd in range(MEM_HEADS):
        sl = slice(hd * MEM_HEAD_DIM, (hd + 1) * MEM_HEAD_DIM)
        kh = kv[:, sl]
        rk = lax.rsqrt(jnp.mean(kh * kh, axis=-1, keepdims=True) + EPS)
        km_ref[0, :, sl] = ((kh * rk) * kg_ref[...]).astype(km_ref.dtype)
    vm_ref[0] = kv[:, mw:].astype(vm_ref.dtype)


def _memkv(mem, g_mem, w_kv, k_gain):
    b, m, d = mem.shape
    mw = MEM_HEADS * MEM_HEAD_DIM
    return pl.pallas_call(
        _memkv_kernel,
        grid=(b,),
        in_specs=[
            pl.BlockSpec((1, m, d), lambda bi: (bi, 0, 0)),
            pl.BlockSpec((1, d), lambda bi: (0, 0)),
            pl.BlockSpec((d, 2 * mw), lambda bi: (0, 0)),
            pl.BlockSpec((1, MEM_HEAD_DIM), lambda bi: (0, 0)),
        ],
        out_specs=[pl.BlockSpec((1, m, mw), lambda bi: (bi, 0, 0)), pl.BlockSpec((1, m, mw), lambda bi: (bi, 0, 0))],
        out_shape=[jax.ShapeDtypeStruct((b, m, mw), BF16), jax.ShapeDtypeStruct((b, m, mw), BF16)],
        compiler_params=_params("parallel"),
        name="mem_kv",
    )(mem, g_mem.reshape(1, d), w_kv, k_gain.reshape(1, MEM_HEAD_DIM))


def _memattn_kernel(h_ref, w_ref, qg_ref, km_ref, vm_ref, o_ref):
    qm = jnp.dot(h_ref[...], w_ref[...], preferred_element_type=F32)
    scale = MEM_HEAD_DIM ** -0.5
    for hd in range(MEM_HEADS):
        sl = slice(hd * MEM_HEAD_DIM, (hd + 1) * MEM_HEAD_DIM)
        qh = qm[:, sl]
        r = lax.rsqrt(jnp.mean(qh * qh, axis=-1, keepdims=True) + EPS)
        qn = ((qh * r) * qg_ref[...]).astype(BF16)
        s = _nt_dot(qn, km_ref[0, :, sl]) * scale
        p = jnp.exp(s - jnp.max(s, axis=-1, keepdims=True))
        l = jnp.sum(p, axis=-1, keepdims=True)
        o = jnp.dot(p.astype(BF16), vm_ref[0, :, sl], preferred_element_type=F32)
        o_ref[:, sl] = (o / l).astype(o_ref.dtype)


def _memattn(h, w_qm, q_gain, km, vm, seq, tm=512):
    n, d = h.shape
    b, m, mw = km.shape
    tiles_per_seq = seq // tm
    return pl.pallas_call(
        _memattn_kernel,
        grid=(n // tm,),
        in_specs=[
            pl.BlockSpec((tm, d), lambda i: (i, 0)),
            pl.BlockSpec((d, mw), lambda i: (0, 0)),
            pl.BlockSpec((1, MEM_HEAD_DIM), lambda i: (0, 0)),
            pl.BlockSpec((1, m, mw), lambda i: (i // tiles_per_seq, 0, 0)),
            pl.BlockSpec((1, m, mw), lambda i: (i // tiles_per_seq, 0, 0)),
        ],
        out_specs=pl.BlockSpec((tm, mw), lambda i: (i, 0)),
        out_shape=jax.ShapeDtypeStruct((n, mw), BF16),
        compiler_params=_params("parallel"),
        name="mem_attention",
    )(h, w_qm, q_gain.reshape(1, MEM_HEAD_DIM), km, vm)


def _merge_kernel(x_ref, h_ref, oa_ref, os_ref, om_ref, wg_ref, wa_ref, ws_ref, wm_ref, wo_ref, gf_ref, wr_ref,
                  x1_ref, h2_ref, route_ref):
    d = x_ref.shape[1]
    gates = _sigmoid(jnp.dot(h_ref[...], wg_ref[...], preferred_element_type=F32))
    merged = gates[:, :d] * jnp.dot(oa_ref[...], wa_ref[...], preferred_element_type=F32)
    merged = merged + gates[:, d:2 * d] * jnp.dot(os_ref[...], ws_ref[...], preferred_element_type=F32)
    merged = merged + gates[:, 2 * d:] * jnp.dot(om_ref[...], wm_ref[...], preferred_element_type=F32)
    x1 = x_ref[...] + jnp.dot(merged.astype(BF16), wo_ref[...], preferred_element_type=F32)
    x1_ref[...] = x1
    r = lax.rsqrt(jnp.mean(x1 * x1, axis=-1, keepdims=True) + EPS)
    h2 = (x1 * r) * gf_ref[...]
    for c in range(d // LANES):
        h2_ref[:, c, :] = h2[:, c * LANES:(c + 1) * LANES]

    h2_hi = h2.astype(BF16)
    h2_lo = (h2 - h2_hi.astype(F32)).astype(BF16)
    lg = (jnp.dot(h2_hi, wr_ref[0], preferred_element_type=F32) + jnp.dot(h2_hi, wr_ref[1], preferred_element_type=F32)
          + jnp.dot(h2_lo, wr_ref[0], preferred_element_type=F32))
    lanef = lax.broadcasted_iota(jnp.int32, lg.shape, 1).astype(F32)
    big = float(LANES)
    gmask = lanef < MOE_GROUPS
    gl = jnp.where(gmask, lg, -jnp.inf)
    gmax = jnp.max(gl, axis=-1, keepdims=True)
    p_g = 1.0 / jnp.sum(jnp.exp(gl - gmax), axis=-1, keepdims=True)
    g_sel = jnp.min(jnp.where(gl == gmax, lanef, big), axis=-1, keepdims=True)
    lo = MOE_GROUPS + MOE_EXPERTS_PER_GROUP * g_sel
    el = jnp.where((lanef >= lo) & (lanef < lo + MOE_EXPERTS_PER_GROUP), lg, -jnp.inf)
    m1 = jnp.max(el, axis=-1, keepdims=True)
    i1 = jnp.min(jnp.where(el == m1, lanef, big), axis=-1, keepdims=True)
    el2 = jnp.where(lanef == i1, -jnp.inf, el)
    m2 = jnp.max(el2, axis=-1, keepdims=True)
    i2 = jnp.min(jnp.where(el2 == m2, lanef, big), axis=-1, keepdims=True)
    e2 = jnp.exp(m2 - m1)
    w1 = 1.0 / (1.0 + e2)
    w2 = e2 / (1.0 + e2)
    route = jnp.where(lanef == 0, i1 - MOE_GROUPS, 0.0)
    route = jnp.where(lanef == 1, i2 - MOE_GROUPS, route)
    route = jnp.where(lanef == 2, p_g * w1, route)
    route = jnp.where(lanef == 3, p_g * w2, route)
    route_ref[...] = route


def _merge(x2, h, o_a, o_s, o_m, w_gates, w_a, w_s, w_m, w_out, g_ffn, w_router, tm=512):
    n, d = x2.shape
    full = lambda arr: pl.BlockSpec(arr.shape, lambda i: (0,) * arr.ndim)
    tile = lambda arr: pl.BlockSpec((tm, arr.shape[1]), lambda i: (i, 0))
    gf = g_ffn.reshape(1, d)
    return pl.pallas_call(
        _merge_kernel,
        grid=(n // tm,),
        in_specs=[tile(x2), tile(h), tile(o_a), tile(o_s), tile(o_m), full(w_gates), full(w_a), full(w_s), full(w_m),
                  full(w_out), full(gf), full(w_router)],
        out_specs=[pl.BlockSpec((tm, d), lambda i: (i, 0)),
                   pl.BlockSpec((tm, d // LANES, LANES), lambda i: (i, 0, 0)),
                   pl.BlockSpec((tm, LANES), lambda i: (i, 0))],
        out_shape=[jax.ShapeDtypeStruct((n, d), F32), jax.ShapeDtypeStruct((n, d // LANES, LANES), F32),
                   jax.ShapeDtypeStruct((n, LANES), F32)],
        compiler_params=_params("parallel"),
        name="merge_router",
    )(x2, h, o_a, o_s, o_m, w_gates, w_a, w_s, w_m, w_out, gf, w_router)


def _moe_kernel(be_ref, j0_ref, nv_ref, tok_ref, dst_ref, h2_hbm, wg_ref, wu_ref, wd_ref, y2_hbm, xbuf, ybuf, gsem,
                ssem):
    rows = xbuf.shape[1]
    n_tok = h2_hbm.shape[0]
    nblocks = pl.num_programs(0)
    i = pl.program_id(0)
    slot = lax.rem(i, 2)
    other = 1 - slot
    dump0 = MOE_TOPK * n_tok

    def block_rows(blk):
        b = jnp.clip(blk, 0, nblocks - 1)
        return j0_ref[b], jnp.where(blk < 0, -1, nv_ref[b] - 1)

    n_assign = tok_ref.shape[0]

    def gather_row(span, sl, r, prio):
        j0, _ = span
        tok = tok_ref[jnp.minimum(j0 + r, n_assign - 1)]
        pltpu.make_async_copy(h2_hbm.at[pl.ds(tok, 1)], xbuf.at[sl, pl.ds(r, 1)], gsem.at[sl]).start(priority=prio)

    def scatter_row(span, sl, r, prio):
        j0, last = span
        dst = jnp.where(r <= last, dst_ref[jnp.minimum(j0 + r, n_assign - 1)], dump0 + sl * rows + r)
        pltpu.make_async_copy(ybuf.at[sl, pl.ds(r, 1)], y2_hbm.at[pl.ds(dst, 1)], ssem.at[sl]).start(priority=prio)

    def looped(fn, blk, sl):
        span = block_rows(blk)

        def body(c, carry):
            for k in range(MOE_DMA_UNROLL):
                fn(span, sl, c * MOE_DMA_UNROLL + k, k % 2)
            return carry
        lax.fori_loop(0, rows // MOE_DMA_UNROLL, body, 0)

    def gather_wait(sl):
        pltpu.make_async_copy(h2_hbm.at[pl.ds(0, rows)], xbuf.at[sl], gsem.at[sl]).wait()

    def scatter_wait(sl):
        pltpu.make_async_copy(ybuf.at[sl], y2_hbm.at[pl.ds(0, rows)], ssem.at[sl]).wait()

    @pl.when(i == 0)
    def _():
        looped(gather_row, 0, 0)
        ybuf[...] = jnp.zeros_like(ybuf)
        pltpu.make_async_copy(ybuf.at[0], y2_hbm.at[pl.ds(dump0, rows)], ssem.at[0]).start()

    gather_wait(slot)
    scatter_wait(slot)
    chunks = xbuf.shape[2]
    nxt, prv = block_rows(i + 1), block_rows(i - 1)
    for r in range(rows):
        scatter_row(prv, other, r, r % 2)
    x = jnp.concatenate([xbuf[slot, :, c, :] for c in range(chunks)], axis=1).astype(BF16)
    for r in range(rows):
        gather_row(nxt, other, r, r % 2)
    gate = jnp.dot(x, wg_ref[0], preferred_element_type=F32)
    up = jnp.dot(x, wu_ref[0], preferred_element_type=F32)
    hid = (gate * _sigmoid(gate)) * up
    y = jnp.dot(hid.astype(BF16), wd_ref[0], preferred_element_type=F32)
    for c in range(chunks):
        ybuf[slot, :, c, :] = y[:, c * LANES:(c + 1) * LANES]

    @pl.when(i == nblocks - 1)
    def _():
        gather_wait(other)
        scatter_wait(other)
        looped(scatter_row, i, slot)
        scatter_wait(slot)


def _moe_experts(h2, tables, w_gate, w_up, w_down):
    n = h2.shape[0]
    e, d, f = w_gate.shape
    nblocks = tables[0].shape[0]
    wmap = lambda i, be, j0, nv, tok, dst: (be[i], 0, 0)
    grid_spec = pltpu.PrefetchScalarGridSpec(
        num_scalar_prefetch=5,
        grid=(nblocks,),
        in_specs=[
            pl.BlockSpec(memory_space=pl.ANY),
            pl.BlockSpec((1, d, f), wmap),
            pl.BlockSpec((1, d, f), wmap),
            pl.BlockSpec((1, f, d), wmap),
        ],
        out_specs=pl.BlockSpec(memory_space=pl.ANY),
        scratch_shapes=[
            pltpu.VMEM((2, MOE_ROWS, d // LANES, LANES), F32),
            pltpu.VMEM((2, MOE_ROWS, d // LANES, LANES), F32),
            pltpu.SemaphoreType.DMA((2,)),
            pltpu.SemaphoreType.DMA((2,)),
        ],
    )
    return pl.pallas_call(
        _moe_kernel,
        grid_spec=grid_spec,
        out_shape=jax.ShapeDtypeStruct((n * MOE_TOPK + 2 * MOE_ROWS, d // LANES, LANES), F32),
        compiler_params=_params("arbitrary"),
        name="moe_experts",
    )(*tables, h2, w_gate, w_up, w_down)


def _combine_kernel(x1_ref, y0_ref, y1_ref, route_ref, o_ref):
    route = route_ref[...]
    w0 = route[:, 2:3]
    w1 = route[:, 3:4]
    for c in range(y0_ref.shape[1]):
        sl = slice(c * LANES, (c + 1) * LANES)
        o_ref[:, sl] = x1_ref[:, sl] + (w0 * y0_ref[:, c, :] + w1 * y1_ref[:, c, :])


def _combine(x1, y2, route, tm=512):
    n, d = x1.shape
    tiles = n // tm
    ytile = (tm, d // LANES, LANES)
    return pl.pallas_call(
        _combine_kernel,
        grid=(tiles,),
        in_specs=[pl.BlockSpec((tm, d), lambda i: (i, 0)), pl.BlockSpec(ytile, lambda i: (i, 0, 0)),
                  pl.BlockSpec(ytile, lambda i: (tiles + i, 0, 0)), pl.BlockSpec((tm, LANES), lambda i: (i, 0))],
        out_specs=pl.BlockSpec((tm, d), lambda i: (i, 0)),
        out_shape=jax.ShapeDtypeStruct((n, d), F32),
        compiler_params=_params("parallel"),
        name="moe_combine",
    )(x1, y2, y2, route)


def _dispatch_tables(route, n_tok):
    n_assign = n_tok * MOE_TOPK
    e_flat = route[:, :MOE_TOPK].astype(jnp.int32).reshape(n_assign)
    order = jnp.argsort(e_flat).astype(jnp.int32)
    counts = jnp.sum(e_flat[:, None] == jnp.arange(MOE_EXPERTS, dtype=jnp.int32)[None, :], axis=0, dtype=jnp.int32)
    blocks_per_expert = (counts + MOE_ROWS - 1) // MOE_ROWS
    blk_end = jnp.cumsum(blocks_per_expert)
    raw_start = jnp.cumsum(counts) - counts
    nblocks = n_assign // MOE_ROWS + MOE_EXPERTS
    blk = jnp.arange(nblocks, dtype=jnp.int32)
    blk_expert = jnp.minimum(jnp.sum(blk_end[None, :] <= blk[:, None], axis=1), MOE_EXPERTS - 1).astype(jnp.int32)
    within = blk - (blk_end - blocks_per_expert)[blk_expert]
    blk_nvalid = jnp.clip(counts[blk_expert] - within * MOE_ROWS, 0, MOE_ROWS).astype(jnp.int32)
    blk_j0 = jnp.where(blk_nvalid > 0, raw_start[blk_expert] + within * MOE_ROWS, 0).astype(jnp.int32)
    tok_sorted = order // MOE_TOPK
    dst_sorted = (order % MOE_TOPK) * n_tok + tok_sorted
    return blk_expert, blk_j0, blk_nvalid, tok_sorted, dst_sorted


def kernel(x, mem, g_mix, w_in, moba_q_norm, moba_k_norm, conv_w, conv_b, dt_bias, a_log, d_skip, ssd_norm, g_mem,
           w_mem_kv, mem_q_norm, mem_k_norm, w_o_moba, w_o_ssd, w_o_mem, w_out, g_ffn, w_router_group,
           w_router_expert, w_gate, w_up, w_down):
    bsz, seq, d = x.shape
    assert seq % MOBA_BLOCK == 0 and seq % SSD_CHUNK == 0
    n_tok = bsz * seq
    moba_w = MOBA_HEADS * MOBA_HEAD_DIM
    d_in = SSD_HEADS * SSD_HEAD_DIM
    xbc_w = d_in + 2 * SSD_GROUPS * SSD_STATE
    mem_w = MEM_HEADS * MEM_HEAD_DIM
    sizes = (moba_w, moba_w, moba_w, d_in, xbc_w, SSD_HEADS, mem_w, 3 * d)
    offs = [0]
    for sz in sizes:
        offs.append(offs[-1] + sz)
    w_in_b = w_in.astype(BF16)
    w_qk = w_in_b[:, offs[0]:offs[2]]
    w_v = w_in_b[:, offs[2]:offs[3]]
    w_z = w_in_b[:, offs[3]:offs[4]]
    w_xbc = w_in_b[:, offs[4]:offs[5]]
    w_dt = jnp.pad(w_in_b[:, offs[5]:offs[6]], ((0, 0), (0, LANES - SSD_HEADS)))
    w_qm = w_in_b[:, offs[6]:offs[7]]
    w_gates = w_in_b[:, offs[7]:offs[8]]

    x2 = x.reshape(n_tok, d)

    half = MOBA_HEAD_DIM // 2
    inv = ROPE_THETA ** (-jnp.arange(half, dtype=F32) / half)
    ang = jnp.arange(seq, dtype=F32)[:, None] * inv[None, :]
    cos_t = jnp.tile(jnp.cos(ang), (1, LANES // half))
    sin_t = jnp.tile(jnp.concatenate([-jnp.sin(ang), jnp.sin(ang)], axis=1), (1, LANES // MOBA_HEAD_DIM))
    gains = jnp.stack([jnp.tile(moba_q_norm, MOBA_HEADS), jnp.tile(moba_k_norm, MOBA_HEADS)]).reshape(2, 1, moba_w)

    h, qk, v, zs, xbc, dt = _inproj(x2, g_mix, w_qk, w_v, w_z, w_xbc, w_dt, gains, cos_t, sin_t, conv_w, conv_b,
                                    dt_bias, seq)

    qk3 = qk.reshape(bsz, seq, 2 * moba_w)
    v3 = v.reshape(bsz, seq, moba_w)
    kmean, ka, vt = _moba_prep(qk3, v3)
    nb = seq // MOBA_BLOCK
    assert nb <= MOBA_MAX_BLOCKS and nb % MOBA_UNROLL == 0
    kmean_pad = jnp.pad(kmean.reshape(bsz, nb, moba_w), ((0, 0), (0, MOBA_MAX_BLOCKS - nb), (0, 0)))
    qa = _moba_select(qk3, kmean_pad)
    o_a = _moba_attention(qa, ka, vt).reshape(n_tok, moba_w)

    o_s = _ssd(xbc.reshape(bsz, seq, xbc_w), zs.reshape(bsz, seq, d_in), dt.reshape(bsz, seq, LANES), a_log, d_skip,
               ssd_norm).reshape(n_tok, d_in)

    km, vm = _memkv(mem, g_mem, w_mem_kv.astype(BF16), mem_k_norm)
    o_m = _memattn(h, w_qm, mem_q_norm, km, vm, seq)

    w_router = jnp.pad(jnp.concatenate([w_router_group, w_router_expert], axis=1),
                       ((0, 0), (0, LANES - MOE_GROUPS - MOE_EXPERTS)))
    w_router_hi = w_router.astype(BF16)
    w_router = jnp.stack([w_router_hi, (w_router - w_router_hi.astype(F32)).astype(BF16)])
    x1, h2, route = _merge(x2, h, o_a, o_s, o_m, w_gates, w_o_moba.astype(BF16), w_o_ssd.astype(BF16),
                           w_o_mem.astype(BF16), w_out.astype(BF16), g_ffn, w_router)

    tables = _dispatch_tables(route, n_tok)
    y2 = _moe_experts(h2, tables, w_gate.astype(BF16), w_up.astype(BF16), w_down.astype(BF16))
    out = _combine(x1, y2, route)
    return out.reshape(bsz, seq, d)
```

```python
import functools

import jax
import jax.numpy as jnp
from jax import lax
from jax.experimental import pallas as pl
from jax.experimental.pallas import tpu as pltpu

F32 = jnp.float32
BF16 = jnp.bfloat16
HIGHEST = lax.Precision.HIGHEST

EPS = 1e-6
ROPE_THETA = 10000.0
MOBA_HEADS = 8
MOBA_HEAD_DIM = 64
MOBA_BLOCK = 256
MOBA_TOPK = 3
SSD_HEAD_DIM = 64
SSD_HEADS = 16
SSD_GROUPS = 4
SSD_STATE = 128
SSD_CONV = 4
SSD_CHUNK = 256
MEM_HEADS = 4
MEM_HEAD_DIM = 128
MOE_GROUPS = 4
MOE_EXPERTS_PER_GROUP = 8
MOE_EXPERTS = MOE_GROUPS * MOE_EXPERTS_PER_GROUP
MOE_TOPK = 2

LANES = 128
MASKED = -1e30
MOBA_MAX_BLOCKS = 32
MOBA_UNROLL = 4
MOBA_VT_ROWS = MOBA_HEAD_DIM + 16
MOE_ROWS = 256
MOE_BUFFERS = 3
MOE_DMA_UNROLL = 8
VMEM_LIMIT = 56 * 1024 * 1024


def _params(*sem):
    return pltpu.CompilerParams(dimension_semantics=sem, vmem_limit_bytes=VMEM_LIMIT)


def _sigmoid(x):
    return 1.0 / (1.0 + jnp.exp(-x))


def _split3_dot(x, sel):
    hi = x.astype(BF16)
    rest = x - hi.astype(F32)
    mid = rest.astype(BF16)
    lo = (rest - mid.astype(F32)).astype(BF16)
    return (jnp.dot(hi, sel, preferred_element_type=F32) + jnp.dot(mid, sel, preferred_element_type=F32)
            + jnp.dot(lo, sel, preferred_element_type=F32))


def _nt_dot(a, b, precision=None):
    return lax.dot_general(a, b, (((1,), (1,)), ((), ())), precision=precision, preferred_element_type=F32)


def _softplus(x):
    return jnp.maximum(x, 0.0) + jnp.log1p(jnp.exp(-jnp.abs(x)))


def _inproj_kernel(x_ref, gmix_ref, wqk_ref, wv_ref, wz_ref, wxbc_ref, wdt_ref, gain_ref, gsum_ref, cos_ref, sin_ref,
                   cw_ref, cb_ref, dtb_ref, h_ref, qk_ref, v_ref, zs_ref, xbc_ref, dt_ref, ext_ref, *,
                   tiles_per_seq):
    tm = x_ref.shape[0]
    x = x_ref[...]
    r = lax.rsqrt(jnp.mean(x * x, axis=-1, keepdims=True) + EPS)
    h = ((x * r) * gmix_ref[...]).astype(BF16)
    h_ref[...] = h

    width = gsum_ref.shape[0]
    half = MOBA_HEAD_DIM // 2
    lane = lax.broadcasted_iota(jnp.int32, (tm, width), 1)
    first = (lane & (MOBA_HEAD_DIM - 1)) < half
    reps = width // cos_ref.shape[1]
    cos = jnp.concatenate([cos_ref[...]] * reps, axis=1)
    sin = jnp.concatenate([sin_ref[...]] * reps, axis=1)
    for j in range(2):
        u = jnp.dot(h, wqk_ref[:, j * width:(j + 1) * width], preferred_element_type=F32)
        sq = u * u
        hi = sq.astype(BF16)
        lo = (sq - hi.astype(F32)).astype(BF16)
        ss = (jnp.dot(hi, gsum_ref[...], preferred_element_type=F32)
              + jnp.dot(lo, gsum_ref[...], preferred_element_type=F32))
        un = (u * lax.rsqrt(ss * (1.0 / MOBA_HEAD_DIM) + EPS)) * gain_ref[j]
        partner = jnp.where(first, pltpu.roll(un, width - half, axis=1), pltpu.roll(un, half, axis=1))
        qk_ref[:, j * width:(j + 1) * width] = un * cos + partner * sin

    v_ref[...] = jnp.dot(h, wv_ref[...], preferred_element_type=F32).astype(v_ref.dtype)
    z = jnp.dot(h, wz_ref[...], preferred_element_type=F32)
    zs_ref[...] = (z * _sigmoid(z)).astype(zs_ref.dtype)
    dt_ref[...] = _softplus(jnp.dot(h, wdt_ref[...], preferred_element_type=F32) + dtb_ref[...])

    @pl.when(pl.program_id(0) % tiles_per_seq == 0)
    def _():
        ext_ref[0:8, :] = jnp.zeros((8, ext_ref.shape[1]), F32)

    cchunk = 512
    for c0 in range(0, wxbc_ref.shape[1], cchunk):
        cs = slice(c0, c0 + cchunk)
        u = jnp.dot(h, wxbc_ref[:, cs], preferred_element_type=F32)
        ext_ref[8:8 + tm, cs] = u
        acc = cb_ref[:, cs] + cw_ref[SSD_CONV - 1:SSD_CONV, cs] * u
        for kk in range(SSD_CONV - 1):
            off = 8 - (SSD_CONV - 1 - kk)
            acc = acc + cw_ref[kk:kk + 1, cs] * ext_ref[off:off + tm, cs]
        ext_ref[0:8, cs] = u[tm - 8:tm, :]
        xbc_ref[:, cs] = (acc * _sigmoid(acc)).astype(xbc_ref.dtype)


def _inproj(x2, g_mix, w_qk, w_v, w_z, w_xbc, w_dt, gains, cos_t, sin_t, conv_w, conv_b, dt_bias, seq, tm=512):
    n, d = x2.shape
    width = w_qk.shape[1] // 2
    tiles_per_seq = seq // tm
    head = jnp.arange(width) // MOBA_HEAD_DIM
    gsum = (head[:, None] == head[None, :]).astype(BF16)
    dtb = jnp.pad(dt_bias.astype(F32), (0, LANES - dt_bias.shape[0])).reshape(1, LANES)
    full = lambda arr: pl.BlockSpec(arr.shape, lambda i: (0,) * arr.ndim)
    tile = lambda cols: pl.BlockSpec((tm, cols), lambda i: (i, 0))
    rope = pl.BlockSpec((tm, LANES), lambda i: (i % tiles_per_seq, 0))
    consts = (g_mix.reshape(1, d), w_qk, w_v, w_z, w_xbc, w_dt, gains, gsum)
    tail = (conv_w, conv_b.reshape(1, -1), dtb)
    outs = ((d, BF16), (2 * width, F32), (w_v.shape[1], BF16), (w_z.shape[1], BF16), (w_xbc.shape[1], BF16),
            (LANES, F32))
    return pl.pallas_call(
        functools.partial(_inproj_kernel, tiles_per_seq=tiles_per_seq),
        grid=(n // tm,),
        in_specs=[tile(d)] + [full(a) for a in consts] + [rope, rope] + [full(a) for a in tail],
        out_specs=[tile(c) for c, _ in outs],
        out_shape=[jax.ShapeDtypeStruct((n, c), dt) for c, dt in outs],
        scratch_shapes=[pltpu.VMEM((tm + 8, w_xbc.shape[1]), F32)],
        compiler_params=_params("arbitrary"),
        name="in_proj",
    )(x2, *consts, cos_t, sin_t, *tail)


def _moba_prep_kernel(k_ref, v_ref, km_ref, ka_ref, vt_ref):
    n = pl.program_id(1)
    ones = jnp.ones((MOBA_VT_ROWS - MOBA_HEAD_DIM, MOBA_BLOCK), BF16)
    k = k_ref[0]
    km_ref[0, 0] = jnp.mean(k, axis=0, keepdims=True)
    lane = lax.broadcasted_iota(jnp.int32, (MOBA_BLOCK, LANES), 1)
    own = lane < MOBA_HEAD_DIM
    onehot = jnp.where(lane == MOBA_HEAD_DIM + n, 1.0, 0.0)
    for pr in range(MOBA_HEADS // 2):
        pair = slice(pr * LANES, (pr + 1) * LANES)
        kp = k[:, pair]
        ka_ref[0, 2 * pr, 0] = jnp.where(own, kp, onehot).astype(BF16)
        ka_ref[0, 2 * pr + 1, 0] = jnp.where(own, pltpu.roll(kp, MOBA_HEAD_DIM, axis=1), onehot).astype(BF16)
        vp_t = v_ref[0, :, pair].astype(F32).T
        for hh in range(2):
            vt_ref[0, 2 * pr + hh, 0, :MOBA_HEAD_DIM, :] = vp_t[hh * MOBA_HEAD_DIM:(hh + 1) * MOBA_HEAD_DIM].astype(BF16)
            vt_ref[0, 2 * pr + hh, 0, MOBA_HEAD_DIM:, :] = ones


def _moba_prep(qk3, v3):
    b, s, w = v3.shape
    nb = s // MOBA_BLOCK
    return pl.pallas_call(
        _moba_prep_kernel,
        grid=(b, nb),
        in_specs=[pl.BlockSpec((1, MOBA_BLOCK, w), lambda bi, n: (bi, n, 1)),
                  pl.BlockSpec((1, MOBA_BLOCK, w), lambda bi, n: (bi, n, 0))],
        out_specs=[
            pl.BlockSpec((1, 1, 1, w), lambda bi, n: (bi, n, 0, 0)),
            pl.BlockSpec((1, MOBA_HEADS, 1, MOBA_BLOCK, LANES), lambda bi, n: (bi, 0, n, 0, 0)),
            pl.BlockSpec((1, MOBA_HEADS, 1, MOBA_VT_ROWS, MOBA_BLOCK), lambda bi, n: (bi, 0, n, 0, 0)),
        ],
        out_shape=[jax.ShapeDtypeStruct((b, nb, 1, w), F32),
                   jax.ShapeDtypeStruct((b, MOBA_HEADS, nb, MOBA_BLOCK, LANES), BF16),
                   jax.ShapeDtypeStruct((b, MOBA_HEADS, nb, MOBA_VT_ROWS, MOBA_BLOCK), BF16)],
        compiler_params=_params("parallel", "parallel"),
        name="moba_prep",
    )(qk3, v3)


def _moba_select_kernel(q_ref, km_ref, qa_ref):
    blk = MOBA_BLOCK
    nsel = MOBA_MAX_BLOCKS
    i = pl.program_id(1)
    qscale = (MOBA_HEAD_DIM ** -0.5) * 1.4426950408889634
    q_t = (q_ref[0] * qscale).T
    kmean = km_ref[0]
    lane = lax.broadcasted_iota(jnp.int32, (nsel, LANES), 1)
    blk_row = lax.broadcasted_iota(jnp.int32, (nsel, blk), 0)
    blk_rowf = blk_row.astype(F32)
    pad = jnp.zeros((LANES - MOBA_HEAD_DIM - nsel, blk), F32)
    for hd in range(MOBA_HEADS):
        pr, hh = hd // 2, hd % 2
        hmask = (lane >= hh * MOBA_HEAD_DIM) & (lane < (hh + 1) * MOBA_HEAD_DIM)
        km_h = jnp.where(hmask, kmean[:, pr * LANES:(pr + 1) * LANES], 0.0)
        gate = jnp.dot(km_h, q_t[pr * LANES:(pr + 1) * LANES], precision=HIGHEST, preferred_element_type=F32)
        g = jnp.where(blk_row < i, gate, -jnp.inf)
        bias = jnp.full((nsel, blk), MASKED, F32)
        for _ in range(MOBA_TOPK):
            m = jnp.max(g, axis=0, keepdims=True)
            hit = (g == m) & (m > -jnp.inf)
            first = jnp.min(jnp.where(hit, blk_rowf, float(nsel)), axis=0, keepdims=True)
            sel = blk_rowf == first
            bias = jnp.where(sel, 0.0, bias)
            g = jnp.where(sel, -jnp.inf, g)
        qs = q_t[hd * MOBA_HEAD_DIM:(hd + 1) * MOBA_HEAD_DIM]
        qa_ref[0, hd, 0] = jnp.concatenate([qs, bias, pad], axis=0).astype(BF16)


def _moba_select(qk3, kmean_pad):
    b, s, w = qk3.shape[0], qk3.shape[1], qk3.shape[2] // 2
    nq = s // MOBA_BLOCK
    return pl.pallas_call(
        _moba_select_kernel,
        grid=(b, nq),
        in_specs=[pl.BlockSpec((1, MOBA_BLOCK, w), lambda bi, i: (bi, i, 0)),
                  pl.BlockSpec((1, MOBA_MAX_BLOCKS, w), lambda bi, i: (bi, 0, 0))],
        out_specs=pl.BlockSpec((1, MOBA_HEADS, 1, LANES, MOBA_BLOCK), lambda bi, i: (bi, 0, i, 0, 0)),
        out_shape=jax.ShapeDtypeStruct((b, MOBA_HEADS, nq, LANES, MOBA_BLOCK), BF16),
        compiler_params=_params("parallel", "parallel"),
        name="moba_select",
    )(qk3, kmean_pad)


def _moba_kernel(qa_ref, ka_ref, vt_ref, o_ref, sa_ref, sb_ref, gma_ref, gmb_ref, m_ref, acc_ref):
    blk = MOBA_BLOCK
    i = pl.program_id(2)
    ngroups = (i + MOBA_UNROLL) // MOBA_UNROLL
    key_pos = lax.broadcasted_iota(jnp.int32, (blk, blk), 0)
    qry_pos = lax.broadcasted_iota(jnp.int32, (blk, blk), 1)
    causal = key_pos <= qry_pos
    feat_row = lax.broadcasted_iota(jnp.int32, (LANES, blk), 0) < MOBA_HEAD_DIM

    def fold(s, op):
        return op(s.reshape(blk // 8, 8, blk), axis=0)

    for hh in range(2):
        m_ref[hh] = jnp.full((8, blk), MASKED, F32)
        acc_ref[hh] = jnp.zeros((MOBA_VT_ROWS, blk), F32)

    def score_group(grp, s_ref, gm_ref, first=False):
        for hh in range(2):
            gmax = None
            for u in range(MOBA_UNROLL):
                if first and u == 0:
                    qa_own = jnp.where(feat_row, qa_ref[0, hh, 0], jnp.zeros((), BF16))
                    s = jnp.where(causal, jnp.dot(ka_ref[0, hh, i], qa_own, preferred_element_type=F32), MASKED)
                else:
                    n = grp * MOBA_UNROLL + (u - 1)
                    s = jnp.dot(ka_ref[0, hh, n], qa_ref[0, hh, 0], preferred_element_type=F32)
                s_ref[hh, u] = s
                gmax = fold(s, jnp.max) if gmax is None else jnp.maximum(gmax, fold(s, jnp.max))
            gm_ref[hh] = gmax

    def value_group(grp, s_ref, gm_ref):
        for hh in range(2):
            m_old = m_ref[hh][0:1]
            m_new = jnp.maximum(m_old, jnp.max(gm_ref[hh], axis=0, keepdims=True))
            alpha = jnp.exp2(m_old - m_new)
            acc = acc_ref[hh] * alpha
            for u in range(MOBA_UNROLL):
                n = grp * MOBA_UNROLL + (u - 1)
                if u == 0:
                    n = jnp.where(grp == 0, i, n)
                p = jnp.exp2(s_ref[hh, u] - m_new).astype(BF16)
                acc = acc + jnp.dot(vt_ref[0, hh, n], p, preferred_element_type=F32)
            m_ref[hh] = jnp.broadcast_to(m_new, (8, blk))
            acc_ref[hh] = acc

    score_group(0, sa_ref, gma_ref, first=True)

    npairs = (ngroups - 1) // 2

    def body(t, carry):
        score_group(2 * t + 1, sb_ref, gmb_ref)
        value_group(2 * t, sa_ref, gma_ref)
        score_group(2 * t + 2, sa_ref, gma_ref)
        value_group(2 * t + 1, sb_ref, gmb_ref)
        return carry

    lax.fori_loop(0, npairs, body, 0)
    left = ngroups - 2 * npairs

    @pl.when(left == 2)
    def _():
        score_group(2 * npairs + 1, sb_ref, gmb_ref)
        value_group(2 * npairs, sa_ref, gma_ref)
        value_group(2 * npairs + 1, sb_ref, gmb_ref)

    @pl.when(left == 1)
    def _():
        value_group(2 * npairs, sa_ref, gma_ref)

    outs = [acc_ref[hh, :MOBA_HEAD_DIM, :] / acc_ref[hh, MOBA_HEAD_DIM:MOBA_HEAD_DIM + 1, :] for hh in range(2)]
    o_ref[0] = jnp.concatenate(outs, axis=0).T.astype(o_ref.dtype)


def _moba_attention(qa, ka, vt):
    b, heads, nq = qa.shape[:3]
    s = nq * MOBA_BLOCK
    pairs = heads // 2
    return pl.pallas_call(
        _moba_kernel,
        grid=(b, pairs, nq),
        in_specs=[
            pl.BlockSpec((1, 2, 1, LANES, MOBA_BLOCK), lambda bi, p, i: (bi, p, i, 0, 0)),
            pl.BlockSpec((1, 2, nq, MOBA_BLOCK, LANES), lambda bi, p, i: (bi, p, 0, 0, 0)),
            pl.BlockSpec((1, 2, nq, MOBA_VT_ROWS, MOBA_BLOCK), lambda bi, p, i: (bi, p, 0, 0, 0)),
        ],
        out_specs=pl.BlockSpec((1, MOBA_BLOCK, LANES), lambda bi, p, i: (bi, i, p)),
        out_shape=jax.ShapeDtypeStruct((b, s, heads * MOBA_HEAD_DIM), BF16),
        scratch_shapes=[
            pltpu.VMEM((2, MOBA_UNROLL, MOBA_BLOCK, MOBA_BLOCK), F32),
            pltpu.VMEM((2, MOBA_UNROLL, MOBA_BLOCK, MOBA_BLOCK), F32),
            pltpu.VMEM((2, 8, MOBA_BLOCK), F32),
            pltpu.VMEM((2, 8, MOBA_BLOCK), F32),
            pltpu.VMEM((2, 8, MOBA_BLOCK), F32),
            pltpu.VMEM((2, MOBA_VT_ROWS, MOBA_BLOCK), F32),
        ],
        compiler_params=_params("parallel", "parallel", "arbitrary"),
        name="moba_attention",
    )(qa, ka, vt)


def _ssd_kernel(xbc_ref, zs_ref, dt_ref, alog_ref, dskip_ref, gn_ref, ex_ref, o_ref, state_ref):
    q = SSD_CHUNK
    d_in = SSD_HEADS * SSD_HEAD_DIM
    bc_w = SSD_GROUPS * SSD_STATE
    gw = d_in // SSD_GROUPS
    c = pl.program_id(1)

    @pl.when(c == 0)
    def _():
        state_ref[...] = jnp.zeros_like(state_ref)

    xs = xbc_ref[0, :, :d_in].astype(F32)
    bm = xbc_ref[0, :, d_in:d_in + bc_w]
    cm = xbc_ref[0, :, d_in + bc_w:]
    dt = dt_ref[0]
    a = dt * (-jnp.exp(alog_ref[...]))
    row = lax.broadcasted_iota(jnp.int32, (q, q), 0)
    col = lax.broadcasted_iota(jnp.int32, (q, q), 1)
    tril = col <= row
    a_cum = jnp.dot(tril.astype(F32), a, precision=HIGHEST, preferred_element_type=F32)
    a_cum_t = a_cum.T
    ex = ex_ref[...]
    dt_x = _split3_dot(dt, ex)
    acum_x = _split3_dot(a_cum, ex)
    alast_x = acum_x[q - 1:q, :]
    x_dt = xs * dt_x
    xd = (x_dt * jnp.exp(alast_x - acum_x)).astype(BF16)
    x_dt_b = x_dt.astype(BF16)
    e_acum = jnp.exp(acum_x)
    e_alast = jnp.exp(alast_x)
    lane2 = lax.broadcasted_iota(jnp.int32, (q, 2 * SSD_HEAD_DIM), 1)
    heads_per_group = SSD_HEADS // SSD_GROUPS

    for g in range(SSD_GROUPS):
        bg = bm[:, g * SSD_STATE:(g + 1) * SSD_STATE]
        cg = cm[:, g * SSD_STATE:(g + 1) * SSD_STATE]
        cb = _nt_dot(cg, bg)
        h_in = state_ref[g]
        y_g = jnp.dot(cg, h_in.astype(BF16), preferred_element_type=F32) * e_acum[:, g * gw:(g + 1) * gw]
        parts = []
        for pr in range(heads_per_group // 2):
            xp = x_dt_b[:, g * gw + pr * 2 * SSD_HEAD_DIM:g * gw + (pr + 1) * 2 * SSD_HEAD_DIM]
            ys = []
            for hh in range(2):
                hd = g * heads_per_group + pr * 2 + hh
                seg = a_cum[:, hd:hd + 1] - a_cum_t[hd:hd + 1, :]
                lmat = jnp.exp(jnp.where(tril, seg, -jnp.inf))
                ys.append(jnp.dot((cb * lmat).astype(BF16), xp, preferred_element_type=F32))
            parts.append(jnp.where(lane2 < SSD_HEAD_DIM, ys[0], ys[1]))
        y_g = y_g + jnp.concatenate(parts, axis=1)
        st = jnp.dot(bg.astype(F32).T.astype(BF16), xd[:, g * gw:(g + 1) * gw], preferred_element_type=F32)
        state_ref[g] = h_in * e_alast[:, g * gw:(g + 1) * gw] + st

        sl = slice(g * gw, (g + 1) * gw)
        y_g = y_g + dskip_ref[:, sl] * xs[:, sl]
        y_g = y_g * zs_ref[0, :, sl].astype(F32)
        r = lax.rsqrt(jnp.mean(y_g * y_g, axis=-1, keepdims=True) + EPS)
        o_ref[0, :, sl] = ((y_g * r) * gn_ref[:, sl]).astype(o_ref.dtype)


def _ssd(xbc3, zs3, dt3, a_log, d_skip, ssd_norm):
    b, s, xw = xbc3.shape
    d_in = SSD_HEADS * SSD_HEAD_DIM
    nc = s // SSD_CHUNK
    pad = LANES - SSD_HEADS
    alog = jnp.pad(a_log.astype(F32), (0, pad)).reshape(1, LANES)
    dskip = jnp.repeat(d_skip.astype(F32), SSD_HEAD_DIM).reshape(1, d_in)
    expand = (jnp.arange(LANES)[:, None] == (jnp.arange(d_in) // SSD_HEAD_DIM)[None, :]).astype(BF16)
    const = lambda shape: pl.BlockSpec(shape, lambda bi, ci: (0,) * len(shape))
    return pl.pallas_call(
        _ssd_kernel,
        grid=(b, nc),
        in_specs=[
            pl.BlockSpec((1, SSD_CHUNK, xw), lambda bi, ci: (bi, ci, 0)),
            pl.BlockSpec((1, SSD_CHUNK, d_in), lambda bi, ci: (bi, ci, 0)),
            pl.BlockSpec((1, SSD_CHUNK, LANES), lambda bi, ci: (bi, ci, 0)),
            const((1, LANES)), const((1, d_in)), const((1, d_in)), const((LANES, d_in)),
        ],
        out_specs=pl.BlockSpec((1, SSD_CHUNK, d_in), lambda bi, ci: (bi, ci, 0)),
        out_shape=jax.ShapeDtypeStruct((b, s, d_in), BF16),
        scratch_shapes=[pltpu.VMEM((SSD_GROUPS, SSD_STATE, d_in // SSD_GROUPS), F32)],
        compiler_params=_params("parallel", "arbitrary"),
        name="ssd_scan",
    )(xbc3, zs3, dt3, alog, dskip, ssd_norm.reshape(1, d_in), expand)


def _memkv_kernel(mem_ref, g_ref, w_ref, kg_ref, km_ref, vm_ref):
    m = mem_ref[0]
    r = lax.rsqrt(jnp.mean(m * m, axis=-1, keepdims=True) + EPS)
    mn = ((m * r) * g_ref[...]).astype(BF16)
    kv = jnp.dot(mn, w_ref[...], preferred_element_type=F32)
    mw = MEM_HEADS * MEM_HEAD_DIM
    for hd in range(MEM_HEADS):
        sl = slice(hd * MEM_HEAD_DIM, (hd + 1) * MEM_HEAD_DIM)
        kh = kv[:, sl]
        rk = lax.rsqrt(jnp.mean(kh * kh, axis=-1, keepdims=True) + EPS)
        km_ref[0, :, sl] = ((kh * rk) * kg_ref[...]).astype(km_ref.dtype)
    vm_ref[0] = kv[:, mw:].astype(vm_ref.dtype)


def _memkv(mem, g_mem, w_kv, k_gain):
    b, m, d = mem.shape
    mw = MEM_HEADS * MEM_HEAD_DIM
    return pl.pallas_call(
        _memkv_kernel,
        grid=(b,),
        in_specs=[
            pl.BlockSpec((1, m, d), lambda bi: (bi, 0, 0)),
            pl.BlockSpec((1, d), lambda bi: (0, 0)),
            pl.BlockSpec((d, 2 * mw), lambda bi: (0, 0)),
            pl.BlockSpec((1, MEM_HEAD_DIM), lambda bi: (0, 0)),
        ],
        out_specs=[pl.BlockSpec((1, m, mw), lambda bi: (bi, 0, 0)), pl.BlockSpec((1, m, mw), lambda bi: (bi, 0, 0))],
        out_shape=[jax.ShapeDtypeStruct((b, m, mw), BF16), jax.ShapeDtypeStruct((b, m, mw), BF16)],
        compiler_params=_params("parallel"),
        name="mem_kv",
    )(mem, g_mem.reshape(1, d), w_kv, k_gain.reshape(1, MEM_HEAD_DIM))


def _memattn_kernel(h_ref, w_ref, qg_ref, km_ref, vm_ref, o_ref):
    qm = jnp.dot(h_ref[...], w_ref[...], preferred_element_type=F32)
    scale = MEM_HEAD_DIM ** -0.5
    for hd in range(MEM_HEADS):
        sl = slice(hd * MEM_HEAD_DIM, (hd + 1) * MEM_HEAD_DIM)
        qh = qm[:, sl]
        r = lax.rsqrt(jnp.mean(qh * qh, axis=-1, keepdims=True) + EPS)
        qn = ((qh * r) * qg_ref[...]).astype(BF16)
        s = _nt_dot(qn, km_ref[0, :, sl]) * scale
        p = jnp.exp(s - jnp.max(s, axis=-1, keepdims=True))
        l = jnp.sum(p, axis=-1, keepdims=True)
        o = jnp.dot(p.astype(BF16), vm_ref[0, :, sl], preferred_element_type=F32)
        o_ref[:, sl] = (o / l).astype(o_ref.dtype)


def _memattn(h, w_qm, q_gain, km, vm, seq, tm=512):
    n, d = h.shape
    b, m, mw = km.shape
    tiles_per_seq = seq // tm
    return pl.pallas_call(
        _memattn_kernel,
        grid=(n // tm,),
        in_specs=[
            pl.BlockSpec((tm, d), lambda i: (i, 0)),
            pl.BlockSpec((d, mw), lambda i: (0, 0)),
            pl.BlockSpec((1, MEM_HEAD_DIM), lambda i: (0, 0)),
            pl.BlockSpec((1, m, mw), lambda i: (i // tiles_per_seq, 0, 0)),
            pl.BlockSpec((1, m, mw), lambda i: (i // tiles_per_seq, 0, 0)),
        ],
        out_specs=pl.BlockSpec((tm, mw), lambda i: (i, 0)),
        out_shape=jax.ShapeDtypeStruct((n, mw), BF16),
        compiler_params=_params("parallel"),
        name="mem_attention",
    )(h, w_qm, q_gain.reshape(1, MEM_HEAD_DIM), km, vm)


def _merge_kernel(x_ref, h_ref, oa_ref, os_ref, om_ref, wg_ref, wa_ref, ws_ref, wm_ref, wo_ref, gf_ref, wr_ref,
                  x1_ref, h2_ref, route_ref):
    d = x_ref.shape[1]
    gates = _sigmoid(jnp.dot(h_ref[...], wg_ref[...], preferred_element_type=F32))
    merged = gates[:, :d] * jnp.dot(oa_ref[...], wa_ref[...], preferred_element_type=F32)
    merged = merged + gates[:, d:2 * d] * jnp.dot(os_ref[...], ws_ref[...], preferred_element_type=F32)
    merged = merged + gates[:, 2 * d:] * jnp.dot(om_ref[...], wm_ref[...], preferred_element_type=F32)
    x1 = x_ref[...] + jnp.dot(merged.astype(BF16), wo_ref[...], preferred_element_type=F32)
    x1_ref[...] = x1
    r = lax.rsqrt(jnp.mean(x1 * x1, axis=-1, keepdims=True) + EPS)
    h2 = (x1 * r) * gf_ref[...]
    for c in range(d // LANES):
        h2_ref[:, c, :] = h2[:, c * LANES:(c + 1) * LANES]

    h2_hi = h2.astype(BF16)
    h2_lo = (h2 - h2_hi.astype(F32)).astype(BF16)
    lg = (jnp.dot(h2_hi, wr_ref[0], preferred_element_type=F32) + jnp.dot(h2_hi, wr_ref[1], preferred_element_type=F32)
          + jnp.dot(h2_lo, wr_ref[0], preferred_element_type=F32))
    lanef = lax.broadcasted_iota(jnp.int32, lg.shape, 1).astype(F32)
    big = float(LANES)
    gmask = lanef < MOE_GROUPS
    gl = jnp.where(gmask, lg, -jnp.inf)
    gmax = jnp.max(gl, axis=-1, keepdims=True)
    p_g = 1.0 / jnp.sum(jnp.exp(gl - gmax), axis=-1, keepdims=True)
    g_sel = jnp.min(jnp.where(gl == gmax, lanef, big), axis=-1, keepdims=True)
    lo = MOE_GROUPS + MOE_EXPERTS_PER_GROUP * g_sel
    el = jnp.where((lanef >= lo) & (lanef < lo + MOE_EXPERTS_PER_GROUP), lg, -jnp.inf)
    m1 = jnp.max(el, axis=-1, keepdims=True)
    i1 = jnp.min(jnp.where(el == m1, lanef, big), axis=-1, keepdims=True)
    el2 = jnp.where(lanef == i1, -jnp.inf, el)
    m2 = jnp.max(el2, axis=-1, keepdims=True)
    i2 = jnp.min(jnp.where(el2 == m2, lanef, big), axis=-1, keepdims=True)
    e2 = jnp.exp(m2 - m1)
    w1 = 1.0 / (1.0 + e2)
    w2 = e2 / (1.0 + e2)
    route = jnp.where(lanef == 0, i1 - MOE_GROUPS, 0.0)
    route = jnp.where(lanef == 1, i2 - MOE_GROUPS, route)
    route = jnp.where(lanef == 2, p_g * w1, route)
    route = jnp.where(lanef == 3, p_g * w2, route)
    route_ref[...] = route


def _merge(x2, h, o_a, o_s, o_m, w_gates, w_a, w_s, w_m, w_out, g_ffn, w_router, tm=512):
    n, d = x2.shape
    full = lambda arr: pl.BlockSpec(arr.shape, lambda i: (0,) * arr.ndim)
    tile = lambda arr: pl.BlockSpec((tm, arr.shape[1]), lambda i: (i, 0))
    gf = g_ffn.reshape(1, d)
    return pl.pallas_call(
        _merge_kernel,
        grid=(n // tm,),
        in_specs=[tile(x2), tile(h), tile(o_a), tile(o_s), tile(o_m), full(w_gates), full(w_a), full(w_s), full(w_m),
                  full(w_out), full(gf), full(w_router)],
        out_specs=[pl.BlockSpec((tm, d), lambda i: (i, 0)),
                   pl.BlockSpec((tm, d // LANES, LANES), lambda i: (i, 0, 0)),
                   pl.BlockSpec((tm, LANES), lambda i: (i, 0))],
        out_shape=[jax.ShapeDtypeStruct((n, d), F32), jax.ShapeDtypeStruct((n, d // LANES, LANES), F32),
                   jax.ShapeDtypeStruct((n, LANES), F32)],
        compiler_params=_params("parallel"),
        name="merge_router",
    )(x2, h, o_a, o_s, o_m, w_gates, w_a, w_s, w_m, w_out, gf, w_router)


def _moe_kernel(be_ref, j0_ref, nv_ref, tok_ref, dst_ref, h2_hbm, wg_ref, wu_ref, wd_ref, y2_hbm, xbuf, ybuf, gsem,
                ssem):
    rows = xbuf.shape[1]
    n_tok = h2_hbm.shape[0]
    nblocks = pl.num_programs(0)
    i = pl.program_id(0)
    nbuf = xbuf.shape[0]
    slot = lax.rem(i, nbuf)
    slot_next = lax.rem(i + 1, nbuf)
    slot_prev = lax.rem(i + 2, nbuf)
    dump0 = MOE_TOPK * n_tok

    def block_rows(blk):
        b = jnp.clip(blk, 0, nblocks - 1)
        return j0_ref[b], jnp.where(blk < 0, -1, nv_ref[b] - 1)

    n_assign = tok_ref.shape[0]

    def gather_row(span, sl, r, prio):
        j0, _ = span
        tok = tok_ref[jnp.minimum(j0 + r, n_assign - 1)]
        pltpu.make_async_copy(h2_hbm.at[pl.ds(tok, 1)], xbuf.at[sl, pl.ds(r, 1)], gsem.at[sl]).start(priority=prio)

    def scatter_row(span, sl, r, prio):
        j0, last = span
        dst = jnp.where(r <= last, dst_ref[jnp.minimum(j0 + r, n_assign - 1)], dump0 + sl * rows + r)
        pltpu.make_async_copy(ybuf.at[sl, pl.ds(r, 1)], y2_hbm.at[pl.ds(dst, 1)], ssem.at[sl]).start(priority=prio)

    def looped(fn, blk, sl):
        span = block_rows(blk)

        def body(c, carry):
            for k in range(MOE_DMA_UNROLL):
                fn(span, sl, c * MOE_DMA_UNROLL + k, k % 2)
            return carry
        lax.fori_loop(0, rows // MOE_DMA_UNROLL, body, 0)

    def gather_wait(sl):
        pltpu.make_async_copy(h2_hbm.at[pl.ds(0, rows)], xbuf.at[sl], gsem.at[sl]).wait()

    def scatter_wait(sl):
        pltpu.make_async_copy(ybuf.at[sl], y2_hbm.at[pl.ds(0, rows)], ssem.at[sl]).wait()

    @pl.when(i == 0)
    def _():
        looped(gather_row, 0, 0)
        looped(gather_row, 1, 1)
        ybuf[...] = jnp.zeros_like(ybuf)
        for sl in range(2):
            pltpu.make_async_copy(ybuf.at[sl], y2_hbm.at[pl.ds(dump0 + sl * rows, rows)], ssem.at[sl]).start()

    gather_wait(slot)
    scatter_wait(slot)
    chunks = xbuf.shape[2]
    nxt, prv = block_rows(i + 2), block_rows(i - 1)
    for r in range(rows):
        scatter_row(prv, slot_prev, r, r % 2)
    x4 = jnp.swapaxes(xbuf[slot].reshape(rows // 8, 8, chunks, LANES), 1, 2)
    x = jnp.concatenate([x4[:, c].reshape(rows, LANES) for c in range(chunks)], axis=1).astype(BF16)
    for r in range(rows):
        gather_row(nxt, slot_prev, r, r % 2)
    gate = jnp.dot(x, wg_ref[0], preferred_element_type=F32)
    up = jnp.dot(x, wu_ref[0], preferred_element_type=F32)
    hid = (gate * _sigmoid(gate)) * up
    y = jnp.dot(hid.astype(BF16), wd_ref[0], preferred_element_type=F32)
    y4 = jnp.stack([y[:, c * LANES:(c + 1) * LANES].reshape(rows // 8, 8, LANES) for c in range(chunks)], axis=1)
    ybuf[slot] = jnp.swapaxes(y4, 1, 2).reshape(rows, chunks, LANES)

    @pl.when(i == nblocks - 1)
    def _():
        gather_wait(slot_next)
        gather_wait(slot_prev)
        scatter_wait(slot_next)
        scatter_wait(slot_prev)
        looped(scatter_row, i, slot)
        scatter_wait(slot)


def _moe_experts(h2, tables, w_gate, w_up, w_down):
    n = h2.shape[0]
    e, d, f = w_gate.shape
    nblocks = tables[0].shape[0]
    wmap = lambda i, be, j0, nv, tok, dst: (be[i], 0, 0)
    grid_spec = pltpu.PrefetchScalarGridSpec(
        num_scalar_prefetch=5,
        grid=(nblocks,),
        in_specs=[
            pl.BlockSpec(memory_space=pl.ANY),
            pl.BlockSpec((1, d, f), wmap),
            pl.BlockSpec((1, d, f), wmap),
            pl.BlockSpec((1, f, d), wmap),
        ],
        out_specs=pl.BlockSpec(memory_space=pl.ANY),
        scratch_shapes=[
            pltpu.VMEM((MOE_BUFFERS, MOE_ROWS, d // LANES, LANES), F32),
            pltpu.VMEM((MOE_BUFFERS, MOE_ROWS, d // LANES, LANES), F32),
            pltpu.SemaphoreType.DMA((MOE_BUFFERS,)),
            pltpu.SemaphoreType.DMA((MOE_BUFFERS,)),
        ],
    )
    return pl.pallas_call(
        _moe_kernel,
        grid_spec=grid_spec,
        out_shape=jax.ShapeDtypeStruct((n * MOE_TOPK + MOE_BUFFERS * MOE_ROWS, d // LANES, LANES), F32),
        compiler_params=_params("arbitrary"),
        name="moe_experts",
    )(*tables, h2, w_gate, w_up, w_down)


def _combine_kernel(x1_ref, y0_ref, y1_ref, route_ref, o_ref):
    route = route_ref[...]
    w0 = route[:, 2:3]
    w1 = route[:, 3:4]
    for c in range(y0_ref.shape[1]):
        sl = slice(c * LANES, (c + 1) * LANES)
        o_ref[:, sl] = x1_ref[:, sl] + (w0 * y0_ref[:, c, :] + w1 * y1_ref[:, c, :])


def _combine(x1, y2, route, tm=512):
    n, d = x1.shape
    tiles = n // tm
    ytile = (tm, d // LANES, LANES)
    return pl.pallas_call(
        _combine_kernel,
        grid=(tiles,),
        in_specs=[pl.BlockSpec((tm, d), lambda i: (i, 0)), pl.BlockSpec(ytile, lambda i: (i, 0, 0)),
                  pl.BlockSpec(ytile, lambda i: (tiles + i, 0, 0)), pl.BlockSpec((tm, LANES), lambda i: (i, 0))],
        out_specs=pl.BlockSpec((tm, d), lambda i: (i, 0)),
        out_shape=jax.ShapeDtypeStruct((n, d), F32),
        compiler_params=_params("parallel"),
        name="moe_combine",
    )(x1, y2, y2, route)


def _dispatch_tables(route, n_tok):
    n_assign = n_tok * MOE_TOPK
    e_flat = route[:, :MOE_TOPK].astype(jnp.int32).reshape(n_assign)
    order = jnp.argsort(e_flat).astype(jnp.int32)
    counts = jnp.sum(e_flat[:, None] == jnp.arange(MOE_EXPERTS, dtype=jnp.int32)[None, :], axis=0, dtype=jnp.int32)
    blocks_per_expert = (counts + MOE_ROWS - 1) // MOE_ROWS
    blk_end = jnp.cumsum(blocks_per_expert)
    raw_start = jnp.cumsum(counts) - counts
    nblocks = n_assign // MOE_ROWS + MOE_EXPERTS
    blk = jnp.arange(nblocks, dtype=jnp.int32)
    blk_expert = jnp.minimum(jnp.sum(blk_end[None, :] <= blk[:, None], axis=1), MOE_EXPERTS - 1).astype(jnp.int32)
    within = blk - (blk_end - blocks_per_expert)[blk_expert]
    blk_nvalid = jnp.clip(counts[blk_expert] - within * MOE_ROWS, 0, MOE_ROWS).astype(jnp.int32)
    blk_j0 = jnp.where(blk_nvalid > 0, raw_start[blk_expert] + within * MOE_ROWS, 0).astype(jnp.int32)
    tok_sorted = order // MOE_TOPK
    dst_sorted = (order % MOE_TOPK) * n_tok + tok_sorted
    return blk_expert, blk_j0, blk_nvalid, tok_sorted, dst_sorted


def kernel(x, mem, g_mix, w_in, moba_q_norm, moba_k_norm, conv_w, conv_b, dt_bias, a_log, d_skip, ssd_norm, g_mem,
           w_mem_kv, mem_q_norm, mem_k_norm, w_o_moba, w_o_ssd, w_o_mem, w_out, g_ffn, w_router_group,
           w_router_expert, w_gate, w_up, w_down):
    bsz, seq, d = x.shape
    assert seq % MOBA_BLOCK == 0 and seq % SSD_CHUNK == 0
    n_tok = bsz * seq
    moba_w = MOBA_HEADS * MOBA_HEAD_DIM
    d_in = SSD_HEADS * SSD_HEAD_DIM
    xbc_w = d_in + 2 * SSD_GROUPS * SSD_STATE
    mem_w = MEM_HEADS * MEM_HEAD_DIM
    sizes = (moba_w, moba_w, moba_w, d_in, xbc_w, SSD_HEADS, mem_w, 3 * d)
    offs = [0]
    for sz in sizes:
        offs.append(offs[-1] + sz)
    w_in_b = w_in.astype(BF16)
    w_qk = w_in_b[:, offs[0]:offs[2]]
    w_v = w_in_b[:, offs[2]:offs[3]]
    w_z = w_in_b[:, offs[3]:offs[4]]
    w_xbc = w_in_b[:, offs[4]:offs[5]]
    w_dt = jnp.pad(w_in_b[:, offs[5]:offs[6]], ((0, 0), (0, LANES - SSD_HEADS)))
    w_qm = w_in_b[:, offs[6]:offs[7]]
    w_gates = w_in_b[:, offs[7]:offs[8]]

    x2 = x.reshape(n_tok, d)

    half = MOBA_HEAD_DIM // 2
    inv = ROPE_THETA ** (-jnp.arange(half, dtype=F32) / half)
    ang = jnp.arange(seq, dtype=F32)[:, None] * inv[None, :]
    cos_t = jnp.tile(jnp.cos(ang), (1, LANES // half))
    sin_t = jnp.tile(jnp.concatenate([-jnp.sin(ang), jnp.sin(ang)], axis=1), (1, LANES // MOBA_HEAD_DIM))
    gains = jnp.stack([jnp.tile(moba_q_norm, MOBA_HEADS), jnp.tile(moba_k_norm, MOBA_HEADS)]).reshape(2, 1, moba_w)

    h, qk, v, zs, xbc, dt = _inproj(x2, g_mix, w_qk, w_v, w_z, w_xbc, w_dt, gains, cos_t, sin_t, conv_w, conv_b,
                                    dt_bias, seq)

    qk3 = qk.reshape(bsz, seq, 2 * moba_w)
    v3 = v.reshape(bsz, seq, moba_w)
    kmean, ka, vt = _moba_prep(qk3, v3)
    nb = seq // MOBA_BLOCK
    assert nb <= MOBA_MAX_BLOCKS and nb % MOBA_UNROLL == 0
    kmean_pad = jnp.pad(kmean.reshape(bsz, nb, moba_w), ((0, 0), (0, MOBA_MAX_BLOCKS - nb), (0, 0)))
    qa = _moba_select(qk3, kmean_pad)
    o_a = _moba_attention(qa, ka, vt).reshape(n_tok, moba_w)

    o_s = _ssd(xbc.reshape(bsz, seq, xbc_w), zs.reshape(bsz, seq, d_in), dt.reshape(bsz, seq, LANES), a_log, d_skip,
               ssd_norm).reshape(n_tok, d_in)

    km, vm = _memkv(mem, g_mem, w_mem_kv.astype(BF16), mem_k_norm)
    o_m = _memattn(h, w_qm, mem_q_norm, km, vm, seq)

    w_router = jnp.pad(jnp.concatenate([w_router_group, w_router_expert], axis=1),
                       ((0, 0), (0, LANES - MOE_GROUPS - MOE_EXPERTS)))
    w_router_hi = w_router.astype(BF16)
    w_router = jnp.stack([w_router_hi, (w_router - w_router_hi.astype(F32)).astype(BF16)])
    x1, h2, route = _merge(x2, h, o_a, o_s, o_m, w_gates, w_o_moba.astype(BF16), w_o_ssd.astype(BF16),
                           w_o_mem.astype(BF16), w_out.astype(BF16), g_ffn, w_router)

    tables = _dispatch_tables(route, n_tok)
    y2 = _moe_experts(h2, tables, w_gate.astype(BF16), w_up.astype(BF16), w_down.astype(BF16))
    out = _combine(x1, y2, route)
    return out.reshape(bsz, seq, d)
```

```python
import functools

import jax
import jax.numpy as jnp
from jax import lax
from jax.experimental import pallas as pl
from jax.experimental.pallas import tpu as pltpu

F32 = jnp.float32
BF16 = jnp.bfloat16
HIGHEST = lax.Precision.HIGHEST

EPS = 1e-6
ROPE_THETA = 10000.0
MOBA_HEADS = 8
MOBA_HEAD_DIM = 64
MOBA_BLOCK = 256
MOBA_TOPK = 3
SSD_HEAD_DIM = 64
SSD_HEADS = 16
SSD_GROUPS = 4
SSD_STATE = 128
SSD_CONV = 4
SSD_CHUNK = 256
MEM_HEADS = 4
MEM_HEAD_DIM = 128
MOE_GROUPS = 4
MOE_EXPERTS_PER_GROUP = 8
MOE_EXPERTS = MOE_GROUPS * MOE_EXPERTS_PER_GROUP
MOE_TOPK = 2

LANES = 128
MASKED = -1e30
MOBA_MAX_BLOCKS = 32
MOBA_UNROLL = 4
MOBA_VT_ROWS = MOBA_HEAD_DIM + 16
MOE_ROWS = 256
INPROJ_CONV_CHUNK = 512
MOE_BUFFERS = 3
MOE_DMA_UNROLL = 8
VMEM_LIMIT = 56 * 1024 * 1024


def _params(*sem):
    return pltpu.CompilerParams(dimension_semantics=sem, vmem_limit_bytes=VMEM_LIMIT)


def _sigmoid(x):
    return 1.0 / (1.0 + jnp.exp(-x))


def _split3_dot(x, sel):
    hi = x.astype(BF16)
    rest = x - hi.astype(F32)
    mid = rest.astype(BF16)
    lo = (rest - mid.astype(F32)).astype(BF16)
    return (jnp.dot(hi, sel, preferred_element_type=F32) + jnp.dot(mid, sel, preferred_element_type=F32)
            + jnp.dot(lo, sel, preferred_element_type=F32))


def _rows_to_token_tiles(x):
    rows, width = x.shape
    chunks = width // LANES
    x4 = jnp.stack([x[:, k * LANES:(k + 1) * LANES].reshape(rows // 8, 8, LANES) for k in range(chunks)], axis=1)
    return jnp.swapaxes(x4, 1, 2).reshape(rows, chunks, LANES)


def _token_tiles_to_rows(x3):
    rows, chunks, _ = x3.shape
    x4 = jnp.swapaxes(x3.reshape(rows // 8, 8, chunks, LANES), 1, 2)
    return jnp.concatenate([x4[:, k].reshape(rows, LANES) for k in range(chunks)], axis=1)


def _nt_dot(a, b, precision=None):
    return lax.dot_general(a, b, (((1,), (1,)), ((), ())), precision=precision, preferred_element_type=F32)


def _softplus(x):
    return jnp.maximum(x, 0.0) + jnp.log1p(jnp.exp(-jnp.abs(x)))


def _inproj_kernel(x_ref, gmix_ref, wqk_ref, wv_ref, wz_ref, wxbc_ref, wdt_ref, gain_ref, gsum_ref, cos_ref, sin_ref,
                   cw_ref, cb_ref, dtb_ref, h_ref, qk_ref, v_ref, zs_ref, xbc_ref, dt_ref, *ext_refs,
                   tiles_per_seq):
    tm = x_ref.shape[0]
    x = x_ref[...]
    r = lax.rsqrt(jnp.mean(x * x, axis=-1, keepdims=True) + EPS)
    h = ((x * r) * gmix_ref[...]).astype(BF16)
    h_ref[...] = h

    width = gsum_ref.shape[0]
    half = MOBA_HEAD_DIM // 2
    lane = lax.broadcasted_iota(jnp.int32, (tm, width), 1)
    first = (lane & (MOBA_HEAD_DIM - 1)) < half
    reps = width // cos_ref.shape[1]
    cos = jnp.concatenate([cos_ref[...]] * reps, axis=1)
    sin = jnp.concatenate([sin_ref[...]] * reps, axis=1)
    for j in range(2):
        u = jnp.dot(h, wqk_ref[:, j * width:(j + 1) * width], preferred_element_type=F32)
        sq = u * u
        hi = sq.astype(BF16)
        lo = (sq - hi.astype(F32)).astype(BF16)
        ss = (jnp.dot(hi, gsum_ref[...], preferred_element_type=F32)
              + jnp.dot(lo, gsum_ref[...], preferred_element_type=F32))
        un = (u * lax.rsqrt(ss * (1.0 / MOBA_HEAD_DIM) + EPS)) * gain_ref[j]
        partner = jnp.where(first, pltpu.roll(un, width - half, axis=1), pltpu.roll(un, half, axis=1))
        qk_ref[:, j * width:(j + 1) * width] = un * cos + partner * sin

    v_ref[...] = jnp.dot(h, wv_ref[...], preferred_element_type=F32).astype(v_ref.dtype)
    z = jnp.dot(h, wz_ref[...], preferred_element_type=F32)
    zs_ref[...] = (z * _sigmoid(z)).astype(zs_ref.dtype)
    dt_ref[...] = _softplus(jnp.dot(h, wdt_ref[...], preferred_element_type=F32) + dtb_ref[...])

    @pl.when(pl.program_id(0) % tiles_per_seq == 0)
    def _():
        for ext_ref in ext_refs:
            ext_ref[0:8, :] = jnp.zeros((8, ext_ref.shape[1]), F32)

    cchunk = ext_refs[0].shape[1]
    for ci, ext_ref in enumerate(ext_refs):
        cs = slice(ci * cchunk, (ci + 1) * cchunk)
        u = jnp.dot(h, wxbc_ref[:, cs], preferred_element_type=F32)
        ext_ref[8:8 + tm, :] = u
        acc = cb_ref[:, cs] + cw_ref[SSD_CONV - 1:SSD_CONV, cs] * u
        for kk in range(SSD_CONV - 1):
            off = 8 - (SSD_CONV - 1 - kk)
            acc = acc + cw_ref[kk:kk + 1, cs] * ext_ref[off:off + tm, :]
        ext_ref[0:8, :] = u[tm - 8:tm, :]
        xbc_ref[:, cs] = (acc * _sigmoid(acc)).astype(xbc_ref.dtype)


def _inproj(x2, g_mix, w_qk, w_v, w_z, w_xbc, w_dt, gains, cos_t, sin_t, conv_w, conv_b, dt_bias, seq, tm=512):
    n, d = x2.shape
    width = w_qk.shape[1] // 2
    tiles_per_seq = seq // tm
    head = jnp.arange(width) // MOBA_HEAD_DIM
    gsum = (head[:, None] == head[None, :]).astype(BF16)
    dtb = jnp.pad(dt_bias.astype(F32), (0, LANES - dt_bias.shape[0])).reshape(1, LANES)
    full = lambda arr: pl.BlockSpec(arr.shape, lambda i: (0,) * arr.ndim)
    tile = lambda cols: pl.BlockSpec((tm, cols), lambda i: (i, 0))
    rope = pl.BlockSpec((tm, LANES), lambda i: (i % tiles_per_seq, 0))
    consts = (g_mix.reshape(1, d), w_qk, w_v, w_z, w_xbc, w_dt, gains, gsum)
    tail = (conv_w, conv_b.reshape(1, -1), dtb)
    outs = ((d, BF16), (2 * width, F32), (w_v.shape[1], BF16), (w_z.shape[1], BF16), (w_xbc.shape[1], BF16),
            (LANES, F32))
    return pl.pallas_call(
        functools.partial(_inproj_kernel, tiles_per_seq=tiles_per_seq),
        grid=(n // tm,),
        in_specs=[tile(d)] + [full(a) for a in consts] + [rope, rope] + [full(a) for a in tail],
        out_specs=[tile(c) for c, _ in outs],
        out_shape=[jax.ShapeDtypeStruct((n, c), dt) for c, dt in outs],
        scratch_shapes=[pltpu.VMEM((tm + 8, INPROJ_CONV_CHUNK), F32)] * (w_xbc.shape[1] // INPROJ_CONV_CHUNK),
        compiler_params=_params("arbitrary"),
        name="in_proj",
    )(x2, *consts, cos_t, sin_t, *tail)


def _moba_prep_kernel(k_ref, v_ref, km_ref, ka_ref, vt_ref):
    n = pl.program_id(1)
    ones = jnp.ones((MOBA_VT_ROWS - MOBA_HEAD_DIM, MOBA_BLOCK), BF16)
    k = k_ref[0]
    km_ref[0, 0] = jnp.mean(k, axis=0, keepdims=True)
    lane = lax.broadcasted_iota(jnp.int32, (MOBA_BLOCK, LANES), 1)
    own = lane < MOBA_HEAD_DIM
    onehot = jnp.where(lane == MOBA_HEAD_DIM + n, 1.0, 0.0)
    for pr in range(MOBA_HEADS // 2):
        pair = slice(pr * LANES, (pr + 1) * LANES)
        kp = k[:, pair]
        ka_ref[0, 2 * pr, 0] = jnp.where(own, kp, onehot).astype(BF16)
        ka_ref[0, 2 * pr + 1, 0] = jnp.where(own, pltpu.roll(kp, MOBA_HEAD_DIM, axis=1), onehot).astype(BF16)
        vp_t = v_ref[0, :, pair].astype(F32).T
        for hh in range(2):
            vt_ref[0, 2 * pr + hh, 0, :MOBA_HEAD_DIM, :] = vp_t[hh * MOBA_HEAD_DIM:(hh + 1) * MOBA_HEAD_DIM].astype(BF16)
            vt_ref[0, 2 * pr + hh, 0, MOBA_HEAD_DIM:, :] = ones


def _moba_prep(qk3, v3):
    b, s, w = v3.shape
    nb = s // MOBA_BLOCK
    return pl.pallas_call(
        _moba_prep_kernel,
        grid=(b, nb),
        in_specs=[pl.BlockSpec((1, MOBA_BLOCK, w), lambda bi, n: (bi, n, 1)),
                  pl.BlockSpec((1, MOBA_BLOCK, w), lambda bi, n: (bi, n, 0))],
        out_specs=[
            pl.BlockSpec((1, 1, 1, w), lambda bi, n: (bi, n, 0, 0)),
            pl.BlockSpec((1, MOBA_HEADS, 1, MOBA_BLOCK, LANES), lambda bi, n: (bi, 0, n, 0, 0)),
            pl.BlockSpec((1, MOBA_HEADS, 1, MOBA_VT_ROWS, MOBA_BLOCK), lambda bi, n: (bi, 0, n, 0, 0)),
        ],
        out_shape=[jax.ShapeDtypeStruct((b, nb, 1, w), F32),
                   jax.ShapeDtypeStruct((b, MOBA_HEADS, nb, MOBA_BLOCK, LANES), BF16),
                   jax.ShapeDtypeStruct((b, MOBA_HEADS, nb, MOBA_VT_ROWS, MOBA_BLOCK), BF16)],
        compiler_params=_params("parallel", "parallel"),
        name="moba_prep",
    )(qk3, v3)


def _moba_select_kernel(q_ref, km_ref, qa_ref):
    blk = MOBA_BLOCK
    nsel = MOBA_MAX_BLOCKS
    i = pl.program_id(1)
    qscale = (MOBA_HEAD_DIM ** -0.5) * 1.4426950408889634
    q_t = (q_ref[0] * qscale).T
    kmean = km_ref[0]
    lane = lax.broadcasted_iota(jnp.int32, (nsel, LANES), 1)
    blk_row = lax.broadcasted_iota(jnp.int32, (nsel, blk), 0)
    blk_rowf = blk_row.astype(F32)
    pad = jnp.zeros((LANES - MOBA_HEAD_DIM - nsel, blk), F32)
    for hd in range(MOBA_HEADS):
        pr, hh = hd // 2, hd % 2
        hmask = (lane >= hh * MOBA_HEAD_DIM) & (lane < (hh + 1) * MOBA_HEAD_DIM)
        km_h = jnp.where(hmask, kmean[:, pr * LANES:(pr + 1) * LANES], 0.0)
        gate = jnp.dot(km_h, q_t[pr * LANES:(pr + 1) * LANES], precision=HIGHEST, preferred_element_type=F32)
        g = jnp.where(blk_row < i, gate, -jnp.inf)
        bias = jnp.full((nsel, blk), MASKED, F32)
        for _ in range(MOBA_TOPK):
            m = jnp.max(g, axis=0, keepdims=True)
            hit = (g == m) & (m > -jnp.inf)
            first = jnp.min(jnp.where(hit, blk_rowf, float(nsel)), axis=0, keepdims=True)
            sel = blk_rowf == first
            bias = jnp.where(sel, 0.0, bias)
            g = jnp.where(sel, -jnp.inf, g)
        qs = q_t[hd * MOBA_HEAD_DIM:(hd + 1) * MOBA_HEAD_DIM]
        qa_ref[0, hd, 0] = jnp.concatenate([qs, bias, pad], axis=0).astype(BF16)


def _moba_select(qk3, kmean_pad):
    b, s, w = qk3.shape[0], qk3.shape[1], qk3.shape[2] // 2
    nq = s // MOBA_BLOCK
    return pl.pallas_call(
        _moba_select_kernel,
        grid=(b, nq),
        in_specs=[pl.BlockSpec((1, MOBA_BLOCK, w), lambda bi, i: (bi, i, 0)),
                  pl.BlockSpec((1, MOBA_MAX_BLOCKS, w), lambda bi, i: (bi, 0, 0))],
        out_specs=pl.BlockSpec((1, MOBA_HEADS, 1, LANES, MOBA_BLOCK), lambda bi, i: (bi, 0, i, 0, 0)),
        out_shape=jax.ShapeDtypeStruct((b, MOBA_HEADS, nq, LANES, MOBA_BLOCK), BF16),
        compiler_params=_params("parallel", "parallel"),
        name="moba_select",
    )(qk3, kmean_pad)


def _moba_kernel(qa_ref, qan_ref, ka_ref, vt_ref, o_ref, sa_ref, sb_ref, sc_ref, gma_ref, gmb_ref, gmc_ref, m_ref,
                 acc_ref):
    blk = MOBA_BLOCK
    nblk = ka_ref.shape[2]
    i = pl.program_id(2)
    ngroups = (i + MOBA_UNROLL) // MOBA_UNROLL
    key_pos = lax.broadcasted_iota(jnp.int32, (blk, blk), 0)
    qry_pos = lax.broadcasted_iota(jnp.int32, (blk, blk), 1)
    causal = key_pos <= qry_pos
    feat_row = lax.broadcasted_iota(jnp.int32, (LANES, blk), 0) < MOBA_HEAD_DIM

    def fold(s, op):
        return op(s.reshape(blk // 8, 8, blk), axis=0)

    for hh in range(2):
        m_ref[hh] = jnp.full((8, blk), MASKED, F32)
        acc_ref[hh] = jnp.zeros((MOBA_VT_ROWS, blk), F32)

    def score_tiles(tiles, s_ref, gm_ref):
        for hh in range(2):
            gmax = None
            for u, (n, q_of, is_own) in enumerate(tiles):
                s = jnp.dot(ka_ref[0, hh, n], q_of(hh), preferred_element_type=F32)
                if is_own:
                    s = jnp.where(causal, s, MASKED)
                s_ref[hh, u] = s
                gmax = fold(s, jnp.max) if gmax is None else jnp.maximum(gmax, fold(s, jnp.max))
            gm_ref[hh] = gmax

    def score_first(src_ref, own, s_ref, gm_ref):
        q_own = lambda hh: jnp.where(feat_row, src_ref[0, hh, 0], jnp.zeros((), BF16))
        q_past = lambda hh: src_ref[0, hh, 0]
        tiles = [(own, q_own, True)] + [(u - 1, q_past, False) for u in range(1, MOBA_UNROLL)]
        score_tiles(tiles, s_ref, gm_ref)

    def score_group(grp, s_ref, gm_ref):
        q_past = lambda hh: qa_ref[0, hh, 0]
        tiles = [(jnp.minimum(grp * MOBA_UNROLL + (u - 1), nblk - 1), q_past, False) for u in range(MOBA_UNROLL)]
        score_tiles(tiles, s_ref, gm_ref)

    def value_group(grp, s_ref, gm_ref):
        for hh in range(2):
            m_old = m_ref[hh][0:1]
            m_new = jnp.maximum(m_old, jnp.max(gm_ref[hh], axis=0, keepdims=True))
            alpha = jnp.exp2(m_old - m_new)
            acc = acc_ref[hh] * alpha
            for u in range(MOBA_UNROLL):
                n = jnp.minimum(grp * MOBA_UNROLL + (u - 1), nblk - 1)
                if u == 0:
                    n = jnp.where(grp == 0, i, n)
                p = jnp.exp2(s_ref[hh, u] - m_new).astype(BF16)
                acc = acc + jnp.dot(vt_ref[0, hh, n], p, preferred_element_type=F32)
            m_ref[hh] = jnp.broadcast_to(m_new, (8, blk))
            acc_ref[hh] = acc

    @pl.when(i == 0)
    def _():
        score_first(qa_ref, i, sc_ref, gmc_ref)

    score_group(1, sa_ref, gma_ref)
    value_group(0, sc_ref, gmc_ref)

    rest = ngroups - 1
    npairs = jnp.maximum(rest - 1, 0) // 2

    def body(t, carry):
        score_group(2 * t + 2, sb_ref, gmb_ref)
        value_group(2 * t + 1, sa_ref, gma_ref)
        score_group(2 * t + 3, sa_ref, gma_ref)
        value_group(2 * t + 2, sb_ref, gmb_ref)
        return carry

    lax.fori_loop(0, npairs, body, 0)
    left = rest - 2 * npairs
    own_next = jnp.minimum(i + 1, nblk - 1)

    @pl.when(left == 2)
    def _():
        score_group(2 * npairs + 2, sb_ref, gmb_ref)
        value_group(2 * npairs + 1, sa_ref, gma_ref)
        score_first(qan_ref, own_next, sc_ref, gmc_ref)
        value_group(2 * npairs + 2, sb_ref, gmb_ref)

    @pl.when(left == 1)
    def _():
        score_first(qan_ref, own_next, sc_ref, gmc_ref)
        value_group(2 * npairs + 1, sa_ref, gma_ref)

    @pl.when(left == 0)
    def _():
        score_first(qan_ref, own_next, sc_ref, gmc_ref)

    outs = [acc_ref[hh, :MOBA_HEAD_DIM, :] / acc_ref[hh, MOBA_HEAD_DIM:MOBA_HEAD_DIM + 1, :] for hh in range(2)]
    o_ref[0] = jnp.concatenate(outs, axis=0).T.astype(o_ref.dtype)


def _moba_attention(qa, ka, vt):
    b, heads, nq = qa.shape[:3]
    s = nq * MOBA_BLOCK
    pairs = heads // 2
    return pl.pallas_call(
        _moba_kernel,
        grid=(b, pairs, nq),
        in_specs=[
            pl.BlockSpec((1, 2, 1, LANES, MOBA_BLOCK), lambda bi, p, i: (bi, p, i, 0, 0)),
            pl.BlockSpec((1, 2, 1, LANES, MOBA_BLOCK), lambda bi, p, i: (bi, p, jnp.minimum(i + 1, nq - 1), 0, 0)),
            pl.BlockSpec((1, 2, nq, MOBA_BLOCK, LANES), lambda bi, p, i: (bi, p, 0, 0, 0)),
            pl.BlockSpec((1, 2, nq, MOBA_VT_ROWS, MOBA_BLOCK), lambda bi, p, i: (bi, p, 0, 0, 0)),
        ],
        out_specs=pl.BlockSpec((1, MOBA_BLOCK, LANES), lambda bi, p, i: (bi, i, p)),
        out_shape=jax.ShapeDtypeStruct((b, s, heads * MOBA_HEAD_DIM), BF16),
        scratch_shapes=[
            pltpu.VMEM((2, MOBA_UNROLL, MOBA_BLOCK, MOBA_BLOCK), F32),
            pltpu.VMEM((2, MOBA_UNROLL, MOBA_BLOCK, MOBA_BLOCK), F32),
            pltpu.VMEM((2, MOBA_UNROLL, MOBA_BLOCK, MOBA_BLOCK), F32),
            pltpu.VMEM((2, 8, MOBA_BLOCK), F32),
            pltpu.VMEM((2, 8, MOBA_BLOCK), F32),
            pltpu.VMEM((2, 8, MOBA_BLOCK), F32),
            pltpu.VMEM((2, 8, MOBA_BLOCK), F32),
            pltpu.VMEM((2, MOBA_VT_ROWS, MOBA_BLOCK), F32),
        ],
        compiler_params=_params("parallel", "parallel", "arbitrary"),
        name="moba_attention",
    )(qa, qa, ka, vt)


def _ssd_kernel(xbc_ref, zs_ref, dt_ref, alog_ref, dskip_ref, gn_ref, ex_ref, o_ref, state_ref):
    q = SSD_CHUNK
    d_in = SSD_HEADS * SSD_HEAD_DIM
    bc_w = SSD_GROUPS * SSD_STATE
    gw = d_in // SSD_GROUPS
    c = pl.program_id(1)

    @pl.when(c == 0)
    def _():
        state_ref[...] = jnp.zeros_like(state_ref)

    xs = xbc_ref[0, :, :d_in].astype(F32)
    bm = xbc_ref[0, :, d_in:d_in + bc_w]
    cm = xbc_ref[0, :, d_in + bc_w:]
    dt = dt_ref[0]
    a = dt * (-jnp.exp(alog_ref[...]))
    row = lax.broadcasted_iota(jnp.int32, (q, q), 0)
    col = lax.broadcasted_iota(jnp.int32, (q, q), 1)
    tril = col <= row
    a_cum = jnp.dot(tril.astype(F32), a, precision=HIGHEST, preferred_element_type=F32)
    a_cum_t = a_cum.T
    ex = ex_ref[...]
    dt_x = _split3_dot(dt, ex)
    acum_x = _split3_dot(a_cum, ex)
    alast_x = acum_x[q - 1:q, :]
    x_dt = xs * dt_x
    xd = (x_dt * jnp.exp(alast_x - acum_x)).astype(BF16)
    x_dt_b = x_dt.astype(BF16)
    e_acum = jnp.exp(acum_x)
    e_alast = jnp.exp(alast_x)
    lane2 = lax.broadcasted_iota(jnp.int32, (q, 2 * SSD_HEAD_DIM), 1)
    heads_per_group = SSD_HEADS // SSD_GROUPS

    for g in range(SSD_GROUPS):
        bg = bm[:, g * SSD_STATE:(g + 1) * SSD_STATE]
        cg = cm[:, g * SSD_STATE:(g + 1) * SSD_STATE]
        cb = _nt_dot(cg, bg)
        h_in = state_ref[g]
        y_g = jnp.dot(cg, h_in.astype(BF16), preferred_element_type=F32) * e_acum[:, g * gw:(g + 1) * gw]
        parts = []
        for pr in range(heads_per_group // 2):
            xp = x_dt_b[:, g * gw + pr * 2 * SSD_HEAD_DIM:g * gw + (pr + 1) * 2 * SSD_HEAD_DIM]
            ys = []
            for hh in range(2):
                hd = g * heads_per_group + pr * 2 + hh
                seg = a_cum[:, hd:hd + 1] - a_cum_t[hd:hd + 1, :]
                lmat = jnp.exp(jnp.where(tril, seg, -jnp.inf))
                ys.append(jnp.dot((cb * lmat).astype(BF16), xp, preferred_element_type=F32))
            parts.append(jnp.where(lane2 < SSD_HEAD_DIM, ys[0], ys[1]))
        y_g = y_g + jnp.concatenate(parts, axis=1)
        st = jnp.dot(bg.astype(F32).T.astype(BF16), xd[:, g * gw:(g + 1) * gw], preferred_element_type=F32)
        state_ref[g] = h_in * e_alast[:, g * gw:(g + 1) * gw] + st

        sl = slice(g * gw, (g + 1) * gw)
        y_g = y_g + dskip_ref[:, sl] * xs[:, sl]
        y_g = y_g * zs_ref[0, :, sl].astype(F32)
        r = lax.rsqrt(jnp.mean(y_g * y_g, axis=-1, keepdims=True) + EPS)
        o_ref[0, :, sl] = ((y_g * r) * gn_ref[:, sl]).astype(o_ref.dtype)


def _ssd(xbc3, zs3, dt3, a_log, d_skip, ssd_norm):
    b, s, xw = xbc3.shape
    d_in = SSD_HEADS * SSD_HEAD_DIM
    nc = s // SSD_CHUNK
    pad = LANES - SSD_HEADS
    alog = jnp.pad(a_log.astype(F32), (0, pad)).reshape(1, LANES)
    dskip = jnp.repeat(d_skip.astype(F32), SSD_HEAD_DIM).reshape(1, d_in)
    expand = (jnp.arange(LANES)[:, None] == (jnp.arange(d_in) // SSD_HEAD_DIM)[None, :]).astype(BF16)
    const = lambda shape: pl.BlockSpec(shape, lambda bi, ci: (0,) * len(shape))
    return pl.pallas_call(
        _ssd_kernel,
        grid=(b, nc),
        in_specs=[
            pl.BlockSpec((1, SSD_CHUNK, xw), lambda bi, ci: (bi, ci, 0)),
            pl.BlockSpec((1, SSD_CHUNK, d_in), lambda bi, ci: (bi, ci, 0)),
            pl.BlockSpec((1, SSD_CHUNK, LANES), lambda bi, ci: (bi, ci, 0)),
            const((1, LANES)), const((1, d_in)), const((1, d_in)), const((LANES, d_in)),
        ],
        out_specs=pl.BlockSpec((1, SSD_CHUNK, d_in), lambda bi, ci: (bi, ci, 0)),
        out_shape=jax.ShapeDtypeStruct((b, s, d_in), BF16),
        scratch_shapes=[pltpu.VMEM((SSD_GROUPS, SSD_STATE, d_in // SSD_GROUPS), F32)],
        compiler_params=_params("parallel", "arbitrary"),
        name="ssd_scan",
    )(xbc3, zs3, dt3, alog, dskip, ssd_norm.reshape(1, d_in), expand)


def _memkv_kernel(mem_ref, g_ref, w_ref, kg_ref, km_ref, vm_ref):
    m = mem_ref[0]
    r = lax.rsqrt(jnp.mean(m * m, axis=-1, keepdims=True) + EPS)
    mn = ((m * r) * g_ref[...]).astype(BF16)
    kv = jnp.dot(mn, w_ref[...], preferred_element_type=F32)
    mw = MEM_HEADS * MEM_HEAD_DIM
    for hd in range(MEM_HEADS):
        sl = slice(hd * MEM_HEAD_DIM, (hd + 1) * MEM_HEAD_DIM)
        kh = kv[:, sl]
        rk = lax.rsqrt(jnp.mean(kh * kh, axis=-1, keepdims=True) + EPS)
        km_ref[0, :, sl] = ((kh * rk) * kg_ref[...]).astype(km_ref.dtype)
    vm_ref[0] = kv[:, mw:].astype(vm_ref.dtype)


def _memkv(mem, g_mem, w_kv, k_gain):
    b, m, d = mem.shape
    mw = MEM_HEADS * MEM_HEAD_DIM
    return pl.pallas_call(
        _memkv_kernel,
        grid=(b,),
        in_specs=[
            pl.BlockSpec((1, m, d), lambda bi: (bi, 0, 0)),
            pl.BlockSpec((1, d), lambda bi: (0, 0)),
            pl.BlockSpec((d, 2 * mw), lambda bi: (0, 0)),
            pl.BlockSpec((1, MEM_HEAD_DIM), lambda bi: (0, 0)),
        ],
        out_specs=[pl.BlockSpec((1, m, mw), lambda bi: (bi, 0, 0)), pl.BlockSpec((1, m, mw), lambda bi: (bi, 0, 0))],
        out_shape=[jax.ShapeDtypeStruct((b, m, mw), BF16), jax.ShapeDtypeStruct((b, m, mw), BF16)],
        compiler_params=_params("parallel"),
        name="mem_kv",
    )(mem, g_mem.reshape(1, d), w_kv, k_gain.reshape(1, MEM_HEAD_DIM))


def _memattn_kernel(h_ref, w_ref, qg_ref, km_ref, vm_ref, o_ref):
    qm = jnp.dot(h_ref[...], w_ref[...], preferred_element_type=F32)
    scale = MEM_HEAD_DIM ** -0.5
    for hd in range(MEM_HEADS):
        sl = slice(hd * MEM_HEAD_DIM, (hd + 1) * MEM_HEAD_DIM)
        qh = qm[:, sl]
        r = lax.rsqrt(jnp.mean(qh * qh, axis=-1, keepdims=True) + EPS)
        qn = ((qh * r) * qg_ref[...]).astype(BF16)
        s = _nt_dot(qn, km_ref[0, :, sl]) * scale
        p = jnp.exp(s - jnp.max(s, axis=-1, keepdims=True))
        l = jnp.sum(p, axis=-1, keepdims=True)
        o = jnp.dot(p.astype(BF16), vm_ref[0, :, sl], preferred_element_type=F32)
        o_ref[:, sl] = (o / l).astype(o_ref.dtype)


def _memattn(h, w_qm, q_gain, km, vm, seq, tm=512):
    n, d = h.shape
    b, m, mw = km.shape
    tiles_per_seq = seq // tm
    return pl.pallas_call(
        _memattn_kernel,
        grid=(n // tm,),
        in_specs=[
            pl.BlockSpec((tm, d), lambda i: (i, 0)),
            pl.BlockSpec((d, mw), lambda i: (0, 0)),
            pl.BlockSpec((1, MEM_HEAD_DIM), lambda i: (0, 0)),
            pl.BlockSpec((1, m, mw), lambda i: (i // tiles_per_seq, 0, 0)),
            pl.BlockSpec((1, m, mw), lambda i: (i // tiles_per_seq, 0, 0)),
        ],
        out_specs=pl.BlockSpec((tm, mw), lambda i: (i, 0)),
        out_shape=jax.ShapeDtypeStruct((n, mw), BF16),
        compiler_params=_params("parallel"),
        name="mem_attention",
    )(h, w_qm, q_gain.reshape(1, MEM_HEAD_DIM), km, vm)


def _merge_kernel(x_ref, h_ref, oa_ref, os_ref, om_ref, wg_ref, wa_ref, ws_ref, wm_ref, wo_ref, gf_ref, wr_ref,
                  x1_ref, h2_ref, route_ref):
    d = x_ref.shape[1]
    gates = _sigmoid(jnp.dot(h_ref[...], wg_ref[...], preferred_element_type=F32))
    merged = gates[:, :d] * jnp.dot(oa_ref[...], wa_ref[...], preferred_element_type=F32)
    merged = merged + gates[:, d:2 * d] * jnp.dot(os_ref[...], ws_ref[...], preferred_element_type=F32)
    merged = merged + gates[:, 2 * d:] * jnp.dot(om_ref[...], wm_ref[...], preferred_element_type=F32)
    x1 = x_ref[...] + jnp.dot(merged.astype(BF16), wo_ref[...], preferred_element_type=F32)
    x1_ref[...] = x1
    r = lax.rsqrt(jnp.mean(x1 * x1, axis=-1, keepdims=True) + EPS)
    h2 = (x1 * r) * gf_ref[...]
    h2_ref[...] = _rows_to_token_tiles(h2)

    h2_hi = h2.astype(BF16)
    h2_lo = (h2 - h2_hi.astype(F32)).astype(BF16)
    lg = (jnp.dot(h2_hi, wr_ref[0], preferred_element_type=F32) + jnp.dot(h2_hi, wr_ref[1], preferred_element_type=F32)
          + jnp.dot(h2_lo, wr_ref[0], preferred_element_type=F32))
    lanef = lax.broadcasted_iota(jnp.int32, lg.shape, 1).astype(F32)
    big = float(LANES)
    gmask = lanef < MOE_GROUPS
    gl = jnp.where(gmask, lg, -jnp.inf)
    gmax = jnp.max(gl, axis=-1, keepdims=True)
    p_g = 1.0 / jnp.sum(jnp.exp(gl - gmax), axis=-1, keepdims=True)
    g_sel = jnp.min(jnp.where(gl == gmax, lanef, big), axis=-1, keepdims=True)
    lo = MOE_GROUPS + MOE_EXPERTS_PER_GROUP * g_sel
    el = jnp.where((lanef >= lo) & (lanef < lo + MOE_EXPERTS_PER_GROUP), lg, -jnp.inf)
    m1 = jnp.max(el, axis=-1, keepdims=True)
    i1 = jnp.min(jnp.where(el == m1, lanef, big), axis=-1, keepdims=True)
    el2 = jnp.where(lanef == i1, -jnp.inf, el)
    m2 = jnp.max(el2, axis=-1, keepdims=True)
    i2 = jnp.min(jnp.where(el2 == m2, lanef, big), axis=-1, keepdims=True)
    e2 = jnp.exp(m2 - m1)
    w1 = 1.0 / (1.0 + e2)
    w2 = e2 / (1.0 + e2)
    route = jnp.where(lanef == 0, i1 - MOE_GROUPS, 0.0)
    route = jnp.where(lanef == 1, i2 - MOE_GROUPS, route)
    route = jnp.where(lanef == 2, p_g * w1, route)
    route = jnp.where(lanef == 3, p_g * w2, route)
    route_ref[...] = route


def _merge(x2, h, o_a, o_s, o_m, w_gates, w_a, w_s, w_m, w_out, g_ffn, w_router, tm=512):
    n, d = x2.shape
    full = lambda arr: pl.BlockSpec(arr.shape, lambda i: (0,) * arr.ndim)
    tile = lambda arr: pl.BlockSpec((tm, arr.shape[1]), lambda i: (i, 0))
    gf = g_ffn.reshape(1, d)
    return pl.pallas_call(
        _merge_kernel,
        grid=(n // tm,),
        in_specs=[tile(x2), tile(h), tile(o_a), tile(o_s), tile(o_m), full(w_gates), full(w_a), full(w_s), full(w_m),
                  full(w_out), full(gf), full(w_router)],
        out_specs=[pl.BlockSpec((tm, d), lambda i: (i, 0)),
                   pl.BlockSpec((tm, d // LANES, LANES), lambda i: (i, 0, 0)),
                   pl.BlockSpec((tm, LANES), lambda i: (i, 0))],
        out_shape=[jax.ShapeDtypeStruct((n, d), F32), jax.ShapeDtypeStruct((n, d // LANES, LANES), F32),
                   jax.ShapeDtypeStruct((n, LANES), F32)],
        compiler_params=_params("parallel"),
        name="merge_router",
    )(x2, h, o_a, o_s, o_m, w_gates, w_a, w_s, w_m, w_out, gf, w_router)


def _moe_kernel(be_ref, j0_ref, nv_ref, tok_ref, dst_ref, h2_hbm, wg_ref, wu_ref, wd_ref, y2_hbm, xbuf, ybuf, gsem,
                ssem):
    rows = xbuf.shape[1]
    n_tok = h2_hbm.shape[0]
    nblocks = pl.num_programs(0)
    i = pl.program_id(0)
    nbuf = xbuf.shape[0]
    slot = lax.rem(i, nbuf)
    slot_next = lax.rem(i + 1, nbuf)
    slot_prev = lax.rem(i + 2, nbuf)
    dump0 = MOE_TOPK * n_tok

    def block_rows(blk):
        b = jnp.clip(blk, 0, nblocks - 1)
        return j0_ref[b], jnp.where(blk < 0, -1, nv_ref[b] - 1)

    n_assign = tok_ref.shape[0]

    def gather_row(span, sl, r, prio):
        j0, _ = span
        tok = tok_ref[jnp.minimum(j0 + r, n_assign - 1)]
        pltpu.make_async_copy(h2_hbm.at[pl.ds(tok, 1)], xbuf.at[sl, pl.ds(r, 1)], gsem.at[sl]).start(priority=prio)

    def scatter_row(span, sl, r, prio):
        j0, last = span
        dst = jnp.where(r <= last, dst_ref[jnp.minimum(j0 + r, n_assign - 1)], dump0 + sl * rows + r)
        pltpu.make_async_copy(ybuf.at[sl, pl.ds(r, 1)], y2_hbm.at[pl.ds(dst, 1)], ssem.at[sl]).start(priority=prio)

    def looped(fn, blk, sl):
        span = block_rows(blk)

        def body(c, carry):
            for k in range(MOE_DMA_UNROLL):
                fn(span, sl, c * MOE_DMA_UNROLL + k, k % 2)
            return carry
        lax.fori_loop(0, rows // MOE_DMA_UNROLL, body, 0)

    def gather_wait(sl):
        pltpu.make_async_copy(h2_hbm.at[pl.ds(0, rows)], xbuf.at[sl], gsem.at[sl]).wait()

    def scatter_wait(sl):
        pltpu.make_async_copy(ybuf.at[sl], y2_hbm.at[pl.ds(0, rows)], ssem.at[sl]).wait()

    @pl.when(i == 0)
    def _():
        looped(gather_row, 0, 0)
        looped(gather_row, 1, 1)
        ybuf[...] = jnp.zeros_like(ybuf)
        for sl in range(2):
            pltpu.make_async_copy(ybuf.at[sl], y2_hbm.at[pl.ds(dump0 + sl * rows, rows)], ssem.at[sl]).start()

    gather_wait(slot)
    scatter_wait(slot)
    chunks = xbuf.shape[2]
    nxt, prv = block_rows(i + 2), block_rows(i - 1)
    for r in range(rows):
        scatter_row(prv, slot_prev, r, r % 2)
    x = _token_tiles_to_rows(xbuf[slot]).astype(BF16)
    for r in range(rows):
        gather_row(nxt, slot_prev, r, r % 2)
    gate = jnp.dot(x, wg_ref[0], preferred_element_type=F32)
    up = jnp.dot(x, wu_ref[0], preferred_element_type=F32)
    hid = (gate * _sigmoid(gate)) * up
    y = jnp.dot(hid.astype(BF16), wd_ref[0], preferred_element_type=F32)
    ybuf[slot] = _rows_to_token_tiles(y)

    @pl.when(i == nblocks - 1)
    def _():
        gather_wait(slot_next)
        gather_wait(slot_prev)
        scatter_wait(slot_next)
        scatter_wait(slot_prev)
        looped(scatter_row, i, slot)
        scatter_wait(slot)


def _moe_experts(h2, tables, w_gate, w_up, w_down):
    n = h2.shape[0]
    e, d, f = w_gate.shape
    nblocks = tables[0].shape[0]
    wmap = lambda i, be, j0, nv, tok, dst: (be[i], 0, 0)
    grid_spec = pltpu.PrefetchScalarGridSpec(
        num_scalar_prefetch=5,
        grid=(nblocks,),
        in_specs=[
            pl.BlockSpec(memory_space=pl.ANY),
            pl.BlockSpec((1, d, f), wmap),
            pl.BlockSpec((1, d, f), wmap),
            pl.BlockSpec((1, f, d), wmap),
        ],
        out_specs=pl.BlockSpec(memory_space=pl.ANY),
        scratch_shapes=[
            pltpu.VMEM((MOE_BUFFERS, MOE_ROWS, d // LANES, LANES), F32),
            pltpu.VMEM((MOE_BUFFERS, MOE_ROWS, d // LANES, LANES), F32),
            pltpu.SemaphoreType.DMA((MOE_BUFFERS,)),
            pltpu.SemaphoreType.DMA((MOE_BUFFERS,)),
        ],
    )
    return pl.pallas_call(
        _moe_kernel,
        grid_spec=grid_spec,
        out_shape=jax.ShapeDtypeStruct((n * MOE_TOPK + MOE_BUFFERS * MOE_ROWS, d // LANES, LANES), F32),
        compiler_params=_params("arbitrary"),
        name="moe_experts",
    )(*tables, h2, w_gate, w_up, w_down)


def _combine_kernel(x1_ref, y0_ref, y1_ref, route_ref, o_ref):
    route = route_ref[...]
    w0 = route[:, 2:3]
    w1 = route[:, 3:4]
    o_ref[...] = x1_ref[...] + (w0 * _token_tiles_to_rows(y0_ref[...]) + w1 * _token_tiles_to_rows(y1_ref[...]))


def _combine(x1, y2, route, tm=512):
    n, d = x1.shape
    tiles = n // tm
    ytile = (tm, d // LANES, LANES)
    return pl.pallas_call(
        _combine_kernel,
        grid=(tiles,),
        in_specs=[pl.BlockSpec((tm, d), lambda i: (i, 0)), pl.BlockSpec(ytile, lambda i: (i, 0, 0)),
                  pl.BlockSpec(ytile, lambda i: (tiles + i, 0, 0)), pl.BlockSpec((tm, LANES), lambda i: (i, 0))],
        out_specs=pl.BlockSpec((tm, d), lambda i: (i, 0)),
        out_shape=jax.ShapeDtypeStruct((n, d), F32),
        compiler_params=_params("parallel"),
        name="moe_combine",
    )(x1, y2, y2, route)


def _dispatch_tables(route, n_tok):
    n_assign = n_tok * MOE_TOPK
    e_flat = route[:, :MOE_TOPK].astype(jnp.int32).reshape(n_assign)
    order = jnp.argsort(e_flat).astype(jnp.int32)
    counts = jnp.sum(e_flat[:, None] == jnp.arange(MOE_EXPERTS, dtype=jnp.int32)[None, :], axis=0, dtype=jnp.int32)
    blocks_per_expert = (counts + MOE_ROWS - 1) // MOE_ROWS
    blk_end = jnp.cumsum(blocks_per_expert)
    raw_start = jnp.cumsum(counts) - counts
    nblocks = n_assign // MOE_ROWS + MOE_EXPERTS
    blk = jnp.arange(nblocks, dtype=jnp.int32)
    blk_expert = jnp.minimum(jnp.sum(blk_end[None, :] <= blk[:, None], axis=1), MOE_EXPERTS - 1).astype(jnp.int32)
    within = blk - (blk_end - blocks_per_expert)[blk_expert]
    blk_nvalid = jnp.clip(counts[blk_expert] - within * MOE_ROWS, 0, MOE_ROWS).astype(jnp.int32)
    blk_j0 = jnp.where(blk_nvalid > 0, raw_start[blk_expert] + within * MOE_ROWS, 0).astype(jnp.int32)
    tok_sorted = order // MOE_TOPK
    dst_sorted = (order % MOE_TOPK) * n_tok + tok_sorted
    return blk_expert, blk_j0, blk_nvalid, tok_sorted, dst_sorted


def kernel(x, mem, g_mix, w_in, moba_q_norm, moba_k_norm, conv_w, conv_b, dt_bias, a_log, d_skip, ssd_norm, g_mem,
           w_mem_kv, mem_q_norm, mem_k_norm, w_o_moba, w_o_ssd, w_o_mem, w_out, g_ffn, w_router_group,
           w_router_expert, w_gate, w_up, w_down):
    bsz, seq, d = x.shape
    assert seq % MOBA_BLOCK == 0 and seq % SSD_CHUNK == 0
    n_tok = bsz * seq
    moba_w = MOBA_HEADS * MOBA_HEAD_DIM
    d_in = SSD_HEADS * SSD_HEAD_DIM
    xbc_w = d_in + 2 * SSD_GROUPS * SSD_STATE
    mem_w = MEM_HEADS * MEM_HEAD_DIM
    sizes = (moba_w, moba_w, moba_w, d_in, xbc_w, SSD_HEADS, mem_w, 3 * d)
    offs = [0]
    for sz in sizes:
        offs.append(offs[-1] + sz)
    w_in_b = w_in.astype(BF16)
    w_qk = w_in_b[:, offs[0]:offs[2]]
    w_v = w_in_b[:, offs[2]:offs[3]]
    w_z = w_in_b[:, offs[3]:offs[4]]
    w_xbc = w_in_b[:, offs[4]:offs[5]]
    w_dt = jnp.pad(w_in_b[:, offs[5]:offs[6]], ((0, 0), (0, LANES - SSD_HEADS)))
    w_qm = w_in_b[:, offs[6]:offs[7]]
    w_gates = w_in_b[:, offs[7]:offs[8]]

    x2 = x.reshape(n_tok, d)

    half = MOBA_HEAD_DIM // 2
    inv = ROPE_THETA ** (-jnp.arange(half, dtype=F32) / half)
    ang = jnp.arange(seq, dtype=F32)[:, None] * inv[None, :]
    cos_t = jnp.tile(jnp.cos(ang), (1, LANES // half))
    sin_t = jnp.tile(jnp.concatenate([-jnp.sin(ang), jnp.sin(ang)], axis=1), (1, LANES // MOBA_HEAD_DIM))
    gains = jnp.stack([jnp.tile(moba_q_norm, MOBA_HEADS), jnp.tile(moba_k_norm, MOBA_HEADS)]).reshape(2, 1, moba_w)

    h, qk, v, zs, xbc, dt = _inproj(x2, g_mix, w_qk, w_v, w_z, w_xbc, w_dt, gains, cos_t, sin_t, conv_w, conv_b,
                                    dt_bias, seq)

    qk3 = qk.reshape(bsz, seq, 2 * moba_w)
    v3 = v.reshape(bsz, seq, moba_w)
    kmean, ka, vt = _moba_prep(qk3, v3)
    nb = seq // MOBA_BLOCK
    assert nb <= MOBA_MAX_BLOCKS and nb % MOBA_UNROLL == 0
    kmean_pad = jnp.pad(kmean.reshape(bsz, nb, moba_w), ((0, 0), (0, MOBA_MAX_BLOCKS - nb), (0, 0)))
    qa = _moba_select(qk3, kmean_pad)
    o_a = _moba_attention(qa, ka, vt).reshape(n_tok, moba_w)

    o_s = _ssd(xbc.reshape(bsz, seq, xbc_w), zs.reshape(bsz, seq, d_in), dt.reshape(bsz, seq, LANES), a_log, d_skip,
               ssd_norm).reshape(n_tok, d_in)

    km, vm = _memkv(mem, g_mem, w_mem_kv.astype(BF16), mem_k_norm)
    o_m = _memattn(h, w_qm, mem_q_norm, km, vm, seq)

    w_router = jnp.pad(jnp.concatenate([w_router_group, w_router_expert], axis=1),
                       ((0, 0), (0, LANES - MOE_GROUPS - MOE_EXPERTS)))
    w_router_hi = w_router.astype(BF16)
    w_router = jnp.stack([w_router_hi, (w_router - w_router_hi.astype(F32)).astype(BF16)])
    x1, h2, route = _merge(x2, h, o_a, o_s, o_m, w_gates, w_o_moba.astype(BF16), w_o_ssd.astype(BF16),
                           w_o_mem.astype(BF16), w_out.astype(BF16), g_ffn, w_router)

    tables = _dispatch_tables(route, n_tok)
    y2 = _moe_experts(h2, tables, w_gate.astype(BF16), w_up.astype(BF16), w_down.astype(BF16))
    out = _combine(x1, y2, route)
    return out.reshape(bsz, seq, d)
```

```python
import functools

import jax
import jax.numpy as jnp
from jax import lax
from jax.experimental import pallas as pl
from jax.experimental.pallas import tpu as pltpu

F32 = jnp.float32
BF16 = jnp.bfloat16
HIGHEST = lax.Precision.HIGHEST

EPS = 1e-6
ROPE_THETA = 10000.0
MOBA_HEADS = 8
MOBA_HEAD_DIM = 64
MOBA_BLOCK = 256
MOBA_TOPK = 3
SSD_HEAD_DIM = 64
SSD_HEADS = 16
SSD_GROUPS = 4
SSD_STATE = 128
SSD_CONV = 4
SSD_CHUNK = 256
MEM_HEADS = 4
MEM_HEAD_DIM = 128
MOE_GROUPS = 4
MOE_EXPERTS_PER_GROUP = 8
MOE_EXPERTS = MOE_GROUPS * MOE_EXPERTS_PER_GROUP
MOE_TOPK = 2

LANES = 128
MASKED = -1e30
MOBA_MAX_BLOCKS = 32
MOBA_UNROLL = 4
MOBA_TILES_PER_STEP = 2
MOBA_VT_ROWS = MOBA_HEAD_DIM + 16
MOE_ROWS = 256
INPROJ_CONV_CHUNK = 512
MOE_BUFFERS = 3
MOE_DMA_UNROLL = 8
VMEM_LIMIT = 56 * 1024 * 1024


def _params(*sem):
    return pltpu.CompilerParams(dimension_semantics=sem, vmem_limit_bytes=VMEM_LIMIT)


def _sigmoid(x):
    return 1.0 / (1.0 + jnp.exp(-x))


def _split3_dot(x, sel):
    hi = x.astype(BF16)
    rest = x - hi.astype(F32)
    mid = rest.astype(BF16)
    lo = (rest - mid.astype(F32)).astype(BF16)
    return (jnp.dot(hi, sel, preferred_element_type=F32) + jnp.dot(mid, sel, preferred_element_type=F32)
            + jnp.dot(lo, sel, preferred_element_type=F32))


def _rows_to_token_tiles(x):
    rows, width = x.shape
    chunks = width // LANES
    x4 = jnp.stack([x[:, k * LANES:(k + 1) * LANES].reshape(rows // 8, 8, LANES) for k in range(chunks)], axis=1)
    return jnp.swapaxes(x4, 1, 2).reshape(rows, chunks, LANES)


def _token_tiles_to_rows(x3):
    rows, chunks, _ = x3.shape
    x4 = jnp.swapaxes(x3.reshape(rows // 8, 8, chunks, LANES), 1, 2)
    return jnp.concatenate([x4[:, k].reshape(rows, LANES) for k in range(chunks)], axis=1)


def _nt_dot(a, b, precision=None):
    return lax.dot_general(a, b, (((1,), (1,)), ((), ())), precision=precision, preferred_element_type=F32)


def _softplus(x):
    return jnp.maximum(x, 0.0) + jnp.log1p(jnp.exp(-jnp.abs(x)))


def _inproj_kernel(x_ref, gmix_ref, wqk_ref, wv_ref, wz_ref, wxbc_ref, wdt_ref, gain_ref, gsum_ref, cos_ref, sin_ref,
                   cw_ref, cb_ref, dtb_ref, h_ref, qk_ref, v_ref, zs_ref, xbc_ref, dt_ref, *ext_refs,
                   tiles_per_seq):
    tm = x_ref.shape[0]
    x = x_ref[...]
    r = lax.rsqrt(jnp.mean(x * x, axis=-1, keepdims=True) + EPS)
    h = ((x * r) * gmix_ref[...]).astype(BF16)
    h_ref[...] = h

    width = gsum_ref.shape[0]
    half = MOBA_HEAD_DIM // 2
    lane = lax.broadcasted_iota(jnp.int32, (tm, width), 1)
    first = (lane & (MOBA_HEAD_DIM - 1)) < half
    reps = width // cos_ref.shape[1]
    cos = jnp.concatenate([cos_ref[...]] * reps, axis=1)
    sin = jnp.concatenate([sin_ref[...]] * reps, axis=1)
    for j in range(2):
        u = jnp.dot(h, wqk_ref[:, j * width:(j + 1) * width], preferred_element_type=F32)
        sq = u * u
        hi = sq.astype(BF16)
        lo = (sq - hi.astype(F32)).astype(BF16)
        ss = (jnp.dot(hi, gsum_ref[...], preferred_element_type=F32)
              + jnp.dot(lo, gsum_ref[...], preferred_element_type=F32))
        un = (u * lax.rsqrt(ss * (1.0 / MOBA_HEAD_DIM) + EPS)) * gain_ref[j]
        partner = jnp.where(first, pltpu.roll(un, width - half, axis=1), pltpu.roll(un, half, axis=1))
        qk_ref[:, j * width:(j + 1) * width] = un * cos + partner * sin

    v_ref[...] = jnp.dot(h, wv_ref[...], preferred_element_type=F32).astype(v_ref.dtype)
    z = jnp.dot(h, wz_ref[...], preferred_element_type=F32)
    zs_ref[...] = (z * _sigmoid(z)).astype(zs_ref.dtype)
    dt_ref[...] = _softplus(jnp.dot(h, wdt_ref[...], preferred_element_type=F32) + dtb_ref[...])

    @pl.when(pl.program_id(0) % tiles_per_seq == 0)
    def _():
        for ext_ref in ext_refs:
            ext_ref[0:8, :] = jnp.zeros((8, ext_ref.shape[1]), F32)

    cchunk = ext_refs[0].shape[1]
    for ci, ext_ref in enumerate(ext_refs):
        cs = slice(ci * cchunk, (ci + 1) * cchunk)
        u = jnp.dot(h, wxbc_ref[:, cs], preferred_element_type=F32)
        ext_ref[8:8 + tm, :] = u
        acc = cb_ref[:, cs] + cw_ref[SSD_CONV - 1:SSD_CONV, cs] * u
        for kk in range(SSD_CONV - 1):
            off = 8 - (SSD_CONV - 1 - kk)
            acc = acc + cw_ref[kk:kk + 1, cs] * ext_ref[off:off + tm, :]
        ext_ref[0:8, :] = u[tm - 8:tm, :]
        xbc_ref[:, cs] = (acc * _sigmoid(acc)).astype(xbc_ref.dtype)


def _inproj(x2, g_mix, w_qk, w_v, w_z, w_xbc, w_dt, gains, cos_t, sin_t, conv_w, conv_b, dt_bias, seq, tm=512):
    n, d = x2.shape
    width = w_qk.shape[1] // 2
    tiles_per_seq = seq // tm
    head = jnp.arange(width) // MOBA_HEAD_DIM
    gsum = (head[:, None] == head[None, :]).astype(BF16)
    dtb = jnp.pad(dt_bias.astype(F32), (0, LANES - dt_bias.shape[0])).reshape(1, LANES)
    full = lambda arr: pl.BlockSpec(arr.shape, lambda i: (0,) * arr.ndim)
    tile = lambda cols: pl.BlockSpec((tm, cols), lambda i: (i, 0))
    rope = pl.BlockSpec((tm, LANES), lambda i: (i % tiles_per_seq, 0))
    consts = (g_mix.reshape(1, d), w_qk, w_v, w_z, w_xbc, w_dt, gains, gsum)
    tail = (conv_w, conv_b.reshape(1, -1), dtb)
    outs = ((d, BF16), (2 * width, F32), (w_v.shape[1], BF16), (w_z.shape[1], BF16), (w_xbc.shape[1], BF16),
            (LANES, F32))
    return pl.pallas_call(
        functools.partial(_inproj_kernel, tiles_per_seq=tiles_per_seq),
        grid=(n // tm,),
        in_specs=[tile(d)] + [full(a) for a in consts] + [rope, rope] + [full(a) for a in tail],
        out_specs=[tile(c) for c, _ in outs],
        out_shape=[jax.ShapeDtypeStruct((n, c), dt) for c, dt in outs],
        scratch_shapes=[pltpu.VMEM((tm + 8, INPROJ_CONV_CHUNK), F32)] * (w_xbc.shape[1] // INPROJ_CONV_CHUNK),
        compiler_params=_params("arbitrary"),
        name="in_proj",
    )(x2, *consts, cos_t, sin_t, *tail)


def _moba_prep_kernel(k_ref, v_ref, km_ref, ka_ref, vt_ref):
    n = pl.program_id(1)
    ones = jnp.ones((MOBA_VT_ROWS - MOBA_HEAD_DIM, MOBA_BLOCK), BF16)
    k = k_ref[0]
    km_ref[0, 0] = jnp.mean(k, axis=0, keepdims=True)
    lane = lax.broadcasted_iota(jnp.int32, (MOBA_BLOCK, LANES), 1)
    own = lane < MOBA_HEAD_DIM
    onehot = jnp.where(lane == MOBA_HEAD_DIM + n, 1.0, 0.0)
    for pr in range(MOBA_HEADS // 2):
        pair = slice(pr * LANES, (pr + 1) * LANES)
        kp = k[:, pair]
        ka_ref[0, 2 * pr, 0] = jnp.where(own, kp, onehot).astype(BF16)
        ka_ref[0, 2 * pr + 1, 0] = jnp.where(own, pltpu.roll(kp, MOBA_HEAD_DIM, axis=1), onehot).astype(BF16)
        vp_t = v_ref[0, :, pair].astype(F32).T
        for hh in range(2):
            vt_ref[0, 2 * pr + hh, 0, :MOBA_HEAD_DIM, :] = vp_t[hh * MOBA_HEAD_DIM:(hh + 1) * MOBA_HEAD_DIM].astype(BF16)
            vt_ref[0, 2 * pr + hh, 0, MOBA_HEAD_DIM:, :] = ones


def _moba_prep(qk3, v3):
    b, s, w = v3.shape
    nb = s // MOBA_BLOCK
    return pl.pallas_call(
        _moba_prep_kernel,
        grid=(b, nb),
        in_specs=[pl.BlockSpec((1, MOBA_BLOCK, w), lambda bi, n: (bi, n, 1)),
                  pl.BlockSpec((1, MOBA_BLOCK, w), lambda bi, n: (bi, n, 0))],
        out_specs=[
            pl.BlockSpec((1, 1, 1, w), lambda bi, n: (bi, n, 0, 0)),
            pl.BlockSpec((1, MOBA_HEADS, 1, MOBA_BLOCK, LANES), lambda bi, n: (bi, 0, n, 0, 0)),
            pl.BlockSpec((1, MOBA_HEADS, 1, MOBA_VT_ROWS, MOBA_BLOCK), lambda bi, n: (bi, 0, n, 0, 0)),
        ],
        out_shape=[jax.ShapeDtypeStruct((b, nb, 1, w), F32),
                   jax.ShapeDtypeStruct((b, MOBA_HEADS, nb, MOBA_BLOCK, LANES), BF16),
                   jax.ShapeDtypeStruct((b, MOBA_HEADS, nb, MOBA_VT_ROWS, MOBA_BLOCK), BF16)],
        compiler_params=_params("parallel", "parallel"),
        name="moba_prep",
    )(qk3, v3)


def _moba_select_kernel(q_ref, km_ref, qa_ref):
    blk = MOBA_BLOCK
    nsel = MOBA_MAX_BLOCKS
    i = pl.program_id(1)
    qscale = (MOBA_HEAD_DIM ** -0.5) * 1.4426950408889634
    q_t = (q_ref[0] * qscale).T
    kmean = km_ref[0]
    lane = lax.broadcasted_iota(jnp.int32, (nsel, LANES), 1)
    blk_row = lax.broadcasted_iota(jnp.int32, (nsel, blk), 0)
    blk_rowf = blk_row.astype(F32)
    pad = jnp.zeros((LANES - MOBA_HEAD_DIM - nsel, blk), F32)
    for hd in range(MOBA_HEADS):
        pr, hh = hd // 2, hd % 2
        hmask = (lane >= hh * MOBA_HEAD_DIM) & (lane < (hh + 1) * MOBA_HEAD_DIM)
        km_h = jnp.where(hmask, kmean[:, pr * LANES:(pr + 1) * LANES], 0.0)
        gate = jnp.dot(km_h, q_t[pr * LANES:(pr + 1) * LANES], precision=HIGHEST, preferred_element_type=F32)
        g = jnp.where(blk_row < i, gate, -jnp.inf)
        bias = jnp.full((nsel, blk), MASKED, F32)
        for _ in range(MOBA_TOPK):
            m = jnp.max(g, axis=0, keepdims=True)
            hit = (g == m) & (m > -jnp.inf)
            first = jnp.min(jnp.where(hit, blk_rowf, float(nsel)), axis=0, keepdims=True)
            sel = blk_rowf == first
            bias = jnp.where(sel, 0.0, bias)
            g = jnp.where(sel, -jnp.inf, g)
        qs = q_t[hd * MOBA_HEAD_DIM:(hd + 1) * MOBA_HEAD_DIM]
        qa_ref[0, hd, 0] = jnp.concatenate([qs, bias, pad], axis=0).astype(BF16)


def _moba_select(qk3, kmean_pad):
    b, s, w = qk3.shape[0], qk3.shape[1], qk3.shape[2] // 2
    nq = s // MOBA_BLOCK
    return pl.pallas_call(
        _moba_select_kernel,
        grid=(b, nq),
        in_specs=[pl.BlockSpec((1, MOBA_BLOCK, w), lambda bi, i: (bi, i, 0)),
                  pl.BlockSpec((1, MOBA_MAX_BLOCKS, w), lambda bi, i: (bi, 0, 0))],
        out_specs=pl.BlockSpec((1, MOBA_HEADS, 1, LANES, MOBA_BLOCK), lambda bi, i: (bi, 0, i, 0, 0)),
        out_shape=jax.ShapeDtypeStruct((b, MOBA_HEADS, nq, LANES, MOBA_BLOCK), BF16),
        compiler_params=_params("parallel", "parallel"),
        name="moba_select",
    )(qk3, kmean_pad)


def _moba_kernel(qa_ref, qan_ref, ka_ref, vt_ref, o_ref, sa_ref, sb_ref, sc_ref, gma_ref, gmb_ref, gmc_ref, m_ref,
                 acc_ref):
    blk = MOBA_BLOCK
    nblk = ka_ref.shape[2]
    key_pos = lax.broadcasted_iota(jnp.int32, (blk, blk), 0)
    qry_pos = lax.broadcasted_iota(jnp.int32, (blk, blk), 1)
    causal = key_pos <= qry_pos
    feat_row = lax.broadcasted_iota(jnp.int32, (LANES, blk), 0) < MOBA_HEAD_DIM

    def fold(s, op):
        return op(s.reshape(blk // 8, 8, blk), axis=0)

    def score_tiles(tiles, s_ref, gm_ref):
        for hh in range(2):
            gmax = None
            for u, (n, q_of, is_own) in enumerate(tiles):
                s = jnp.dot(ka_ref[0, hh, n], q_of(hh), preferred_element_type=F32)
                if is_own:
                    s = jnp.where(causal, s, MASKED)
                s_ref[hh, u] = s
                gmax = fold(s, jnp.max) if gmax is None else jnp.maximum(gmax, fold(s, jnp.max))
            gm_ref[hh] = gmax

    def score_first(q_past, own, s_ref, gm_ref):
        q_own = lambda hh: jnp.where(feat_row, q_past(hh), jnp.zeros((), BF16))
        tiles = [(own, q_own, True)] + [(u - 1, q_past, False) for u in range(1, MOBA_UNROLL)]
        score_tiles(tiles, s_ref, gm_ref)

    def score_group(q_past, grp, s_ref, gm_ref):
        tiles = [(jnp.minimum(grp * MOBA_UNROLL + (u - 1), nblk - 1), q_past, False) for u in range(MOBA_UNROLL)]
        score_tiles(tiles, s_ref, gm_ref)

    def value_group(i, grp, s_ref, gm_ref):
        for hh in range(2):
            m_old = m_ref[hh][0:1]
            m_new = jnp.maximum(m_old, jnp.max(gm_ref[hh], axis=0, keepdims=True))
            alpha = jnp.exp2(m_old - m_new)
            acc = acc_ref[hh] * alpha
            for u in range(MOBA_UNROLL):
                n = jnp.minimum(grp * MOBA_UNROLL + (u - 1), nblk - 1)
                if u == 0:
                    n = jnp.where(grp == 0, i, n)
                p = jnp.exp2(s_ref[hh, u] - m_new).astype(BF16)
                acc = acc + jnp.dot(vt_ref[0, hh, n], p, preferred_element_type=F32)
            m_ref[hh] = jnp.broadcast_to(m_new, (8, blk))
            acc_ref[hh] = acc

    first_tile = pl.program_id(2) * MOBA_TILES_PER_STEP

    @pl.when(first_tile == 0)
    def _():
        score_first(lambda hh: qa_ref[0, hh, 0], 0, sc_ref, gmc_ref)

    for sub in range(MOBA_TILES_PER_STEP):
        i = first_tile + sub
        ngroups = (i + MOBA_UNROLL) // MOBA_UNROLL
        q_cur = lambda hh, sub=sub: qa_ref[0, hh, sub]
        if sub + 1 < MOBA_TILES_PER_STEP:
            q_nxt = lambda hh, sub=sub: qa_ref[0, hh, sub + 1]
        else:
            q_nxt = lambda hh: qan_ref[0, hh, 0]
        for hh in range(2):
            m_ref[hh] = jnp.full((8, blk), MASKED, F32)
            acc_ref[hh] = jnp.zeros((MOBA_VT_ROWS, blk), F32)

        score_group(q_cur, 1, sa_ref, gma_ref)
        value_group(i, 0, sc_ref, gmc_ref)

        rest = ngroups - 1
        npairs = jnp.maximum(rest - 1, 0) // 2

        def body(t, carry, i=i, q_cur=q_cur):
            score_group(q_cur, 2 * t + 2, sb_ref, gmb_ref)
            value_group(i, 2 * t + 1, sa_ref, gma_ref)
            score_group(q_cur, 2 * t + 3, sa_ref, gma_ref)
            value_group(i, 2 * t + 2, sb_ref, gmb_ref)
            return carry

        lax.fori_loop(0, npairs, body, 0)
        left = rest - 2 * npairs
        own_next = jnp.minimum(i + 1, nblk - 1)

        @pl.when(left == 2)
        def _(i=i, q_cur=q_cur, q_nxt=q_nxt, npairs=npairs, own_next=own_next):
            score_group(q_cur, 2 * npairs + 2, sb_ref, gmb_ref)
            value_group(i, 2 * npairs + 1, sa_ref, gma_ref)
            score_first(q_nxt, own_next, sc_ref, gmc_ref)
            value_group(i, 2 * npairs + 2, sb_ref, gmb_ref)

        @pl.when(left == 1)
        def _(i=i, q_nxt=q_nxt, npairs=npairs, own_next=own_next):
            score_first(q_nxt, own_next, sc_ref, gmc_ref)
            value_group(i, 2 * npairs + 1, sa_ref, gma_ref)

        @pl.when(left == 0)
        def _(q_nxt=q_nxt, own_next=own_next):
            score_first(q_nxt, own_next, sc_ref, gmc_ref)

        outs = [acc_ref[hh, :MOBA_HEAD_DIM, :] / acc_ref[hh, MOBA_HEAD_DIM:MOBA_HEAD_DIM + 1, :] for hh in range(2)]
        o_ref[0, sub * blk:(sub + 1) * blk, :] = jnp.concatenate(outs, axis=0).T.astype(o_ref.dtype)


def _moba_attention(qa, ka, vt):
    b, heads, nq = qa.shape[:3]
    s = nq * MOBA_BLOCK
    pairs = heads // 2
    tps = MOBA_TILES_PER_STEP
    assert nq % tps == 0
    return pl.pallas_call(
        _moba_kernel,
        grid=(b, pairs, nq // tps),
        in_specs=[
            pl.BlockSpec((1, 2, tps, LANES, MOBA_BLOCK), lambda bi, p, j: (bi, p, j, 0, 0)),
            pl.BlockSpec((1, 2, 1, LANES, MOBA_BLOCK),
                         lambda bi, p, j: (bi, p, jnp.minimum((j + 1) * tps, nq - 1), 0, 0)),
            pl.BlockSpec((1, 2, nq, MOBA_BLOCK, LANES), lambda bi, p, j: (bi, p, 0, 0, 0)),
            pl.BlockSpec((1, 2, nq, MOBA_VT_ROWS, MOBA_BLOCK), lambda bi, p, j: (bi, p, 0, 0, 0)),
        ],
        out_specs=pl.BlockSpec((1, tps * MOBA_BLOCK, LANES), lambda bi, p, j: (bi, j, p)),
        out_shape=jax.ShapeDtypeStruct((b, s, heads * MOBA_HEAD_DIM), BF16),
        scratch_shapes=[
            pltpu.VMEM((2, MOBA_UNROLL, MOBA_BLOCK, MOBA_BLOCK), F32),
            pltpu.VMEM((2, MOBA_UNROLL, MOBA_BLOCK, MOBA_BLOCK), F32),
            pltpu.VMEM((2, MOBA_UNROLL, MOBA_BLOCK, MOBA_BLOCK), F32),
            pltpu.VMEM((2, 8, MOBA_BLOCK), F32),
            pltpu.VMEM((2, 8, MOBA_BLOCK), F32),
            pltpu.VMEM((2, 8, MOBA_BLOCK), F32),
            pltpu.VMEM((2, 8, MOBA_BLOCK), F32),
            pltpu.VMEM((2, MOBA_VT_ROWS, MOBA_BLOCK), F32),
        ],
        compiler_params=_params("parallel", "parallel", "arbitrary"),
        name="moba_attention",
    )(qa, qa, ka, vt)


def _ssd_kernel(xbc_ref, zs_ref, dt_ref, alog_ref, dskip_ref, gn_ref, ex_ref, o_ref, state_ref):
    q = SSD_CHUNK
    d_in = SSD_HEADS * SSD_HEAD_DIM
    bc_w = SSD_GROUPS * SSD_STATE
    gw = d_in // SSD_GROUPS
    c = pl.program_id(1)

    @pl.when(c == 0)
    def _():
        state_ref[...] = jnp.zeros_like(state_ref)

    xs = xbc_ref[0, :, :d_in].astype(F32)
    bm = xbc_ref[0, :, d_in:d_in + bc_w]
    cm = xbc_ref[0, :, d_in + bc_w:]
    dt = dt_ref[0]
    a = dt * (-jnp.exp(alog_ref[...]))
    row = lax.broadcasted_iota(jnp.int32, (q, q), 0)
    col = lax.broadcasted_iota(jnp.int32, (q, q), 1)
    tril = col <= row
    a_cum = jnp.dot(tril.astype(F32), a, precision=HIGHEST, preferred_element_type=F32)
    a_cum_t = a_cum.T
    ex = ex_ref[...]
    dt_x = _split3_dot(dt, ex)
    acum_x = _split3_dot(a_cum, ex)
    alast_x = acum_x[q - 1:q, :]
    x_dt = xs * dt_x
    xd = (x_dt * jnp.exp(alast_x - acum_x)).astype(BF16)
    x_dt_b = x_dt.astype(BF16)
    e_acum = jnp.exp(acum_x)
    e_alast = jnp.exp(alast_x)
    lane2 = lax.broadcasted_iota(jnp.int32, (q, 2 * SSD_HEAD_DIM), 1)
    heads_per_group = SSD_HEADS // SSD_GROUPS

    for g in range(SSD_GROUPS):
        bg = bm[:, g * SSD_STATE:(g + 1) * SSD_STATE]
        cg = cm[:, g * SSD_STATE:(g + 1) * SSD_STATE]
        cb = _nt_dot(cg, bg)
        h_in = state_ref[g]
        y_g = jnp.dot(cg, h_in.astype(BF16), preferred_element_type=F32) * e_acum[:, g * gw:(g + 1) * gw]
        parts = []
        for pr in range(heads_per_group // 2):
            xp = x_dt_b[:, g * gw + pr * 2 * SSD_HEAD_DIM:g * gw + (pr + 1) * 2 * SSD_HEAD_DIM]
            ys = []
            for hh in range(2):
                hd = g * heads_per_group + pr * 2 + hh
                seg = a_cum[:, hd:hd + 1] - a_cum_t[hd:hd + 1, :]
                lmat = jnp.exp(jnp.where(tril, seg, -jnp.inf))
                ys.append(jnp.dot((cb * lmat).astype(BF16), xp, preferred_element_type=F32))
            parts.append(jnp.where(lane2 < SSD_HEAD_DIM, ys[0], ys[1]))
        y_g = y_g + jnp.concatenate(parts, axis=1)
        st = jnp.dot(bg.astype(F32).T.astype(BF16), xd[:, g * gw:(g + 1) * gw], preferred_element_type=F32)
        state_ref[g] = h_in * e_alast[:, g * gw:(g + 1) * gw] + st

        sl = slice(g * gw, (g + 1) * gw)
        y_g = y_g + dskip_ref[:, sl] * xs[:, sl]
        y_g = y_g * zs_ref[0, :, sl].astype(F32)
        r = lax.rsqrt(jnp.mean(y_g * y_g, axis=-1, keepdims=True) + EPS)
        o_ref[0, :, sl] = ((y_g * r) * gn_ref[:, sl]).astype(o_ref.dtype)


def _ssd(xbc3, zs3, dt3, a_log, d_skip, ssd_norm):
    b, s, xw = xbc3.shape
    d_in = SSD_HEADS * SSD_HEAD_DIM
    nc = s // SSD_CHUNK
    pad = LANES - SSD_HEADS
    alog = jnp.pad(a_log.astype(F32), (0, pad)).reshape(1, LANES)
    dskip = jnp.repeat(d_skip.astype(F32), SSD_HEAD_DIM).reshape(1, d_in)
    expand = (jnp.arange(LANES)[:, None] == (jnp.arange(d_in) // SSD_HEAD_DIM)[None, :]).astype(BF16)
    const = lambda shape: pl.BlockSpec(shape, lambda bi, ci: (0,) * len(shape))
    return pl.pallas_call(
        _ssd_kernel,
        grid=(b, nc),
        in_specs=[
            pl.BlockSpec((1, SSD_CHUNK, xw), lambda bi, ci: (bi, ci, 0)),
            pl.BlockSpec((1, SSD_CHUNK, d_in), lambda bi, ci: (bi, ci, 0)),
            pl.BlockSpec((1, SSD_CHUNK, LANES), lambda bi, ci: (bi, ci, 0)),
            const((1, LANES)), const((1, d_in)), const((1, d_in)), const((LANES, d_in)),
        ],
        out_specs=pl.BlockSpec((1, SSD_CHUNK, d_in), lambda bi, ci: (bi, ci, 0)),
        out_shape=jax.ShapeDtypeStruct((b, s, d_in), BF16),
        scratch_shapes=[pltpu.VMEM((SSD_GROUPS, SSD_STATE, d_in // SSD_GROUPS), F32)],
        compiler_params=_params("parallel", "arbitrary"),
        name="ssd_scan",
    )(xbc3, zs3, dt3, alog, dskip, ssd_norm.reshape(1, d_in), expand)


def _memkv_kernel(mem_ref, g_ref, w_ref, kg_ref, km_ref, vm_ref):
    m = mem_ref[0]
    r = lax.rsqrt(jnp.mean(m * m, axis=-1, keepdims=True) + EPS)
    mn = ((m * r) * g_ref[...]).astype(BF16)
    kv = jnp.dot(mn, w_ref[...], preferred_element_type=F32)
    mw = MEM_HEADS * MEM_HEAD_DIM
    for hd in range(MEM_HEADS):
        sl = slice(hd * MEM_HEAD_DIM, (hd + 1) * MEM_HEAD_DIM)
        kh = kv[:, sl]
        rk = lax.rsqrt(jnp.mean(kh * kh, axis=-1, keepdims=True) + EPS)
        km_ref[0, :, sl] = ((kh * rk) * kg_ref[...]).astype(km_ref.dtype)
    vm_ref[0] = kv[:, mw:].astype(vm_ref.dtype)


def _memkv(mem, g_mem, w_kv, k_gain):
    b, m, d = mem.shape
    mw = MEM_HEADS * MEM_HEAD_DIM
    return pl.pallas_call(
        _memkv_kernel,
        grid=(b,),
        in_specs=[
            pl.BlockSpec((1, m, d), lambda bi: (bi, 0, 0)),
            pl.BlockSpec((1, d), lambda bi: (0, 0)),
            pl.BlockSpec((d, 2 * mw), lambda bi: (0, 0)),
            pl.BlockSpec((1, MEM_HEAD_DIM), lambda bi: (0, 0)),
        ],
        out_specs=[pl.BlockSpec((1, m, mw), lambda bi: (bi, 0, 0)), pl.BlockSpec((1, m, mw), lambda bi: (bi, 0, 0))],
        out_shape=[jax.ShapeDtypeStruct((b, m, mw), BF16), jax.ShapeDtypeStruct((b, m, mw), BF16)],
        compiler_params=_params("parallel"),
        name="mem_kv",
    )(mem, g_mem.reshape(1, d), w_kv, k_gain.reshape(1, MEM_HEAD_DIM))


def _memattn_kernel(h_ref, w_ref, qg_ref, km_ref, vm_ref, o_ref):
    qm = jnp.dot(h_ref[...], w_ref[...], preferred_element_type=F32)
    scale = MEM_HEAD_DIM ** -0.5
    for hd in range(MEM_HEADS):
        sl = slice(hd * MEM_HEAD_DIM, (hd + 1) * MEM_HEAD_DIM)
        qh = qm[:, sl]
        r = lax.rsqrt(jnp.mean(qh * qh, axis=-1, keepdims=True) + EPS)
        qn = ((qh * r) * qg_ref[...]).astype(BF16)
        s = _nt_dot(qn, km_ref[0, :, sl]) * scale
        p = jnp.exp(s - jnp.max(s, axis=-1, keepdims=True))
        l = jnp.sum(p, axis=-1, keepdims=True)
        o = jnp.dot(p.astype(BF16), vm_ref[0, :, sl], preferred_element_type=F32)
        o_ref[:, sl] = (o / l).astype(o_ref.dtype)


def _memattn(h, w_qm, q_gain, km, vm, seq, tm=512):
    n, d = h.shape
    b, m, mw = km.shape
    tiles_per_seq = seq // tm
    return pl.pallas_call(
        _memattn_kernel,
        grid=(n // tm,),
        in_specs=[
            pl.BlockSpec((tm, d), lambda i: (i, 0)),
            pl.BlockSpec((d, mw), lambda i: (0, 0)),
            pl.BlockSpec((1, MEM_HEAD_DIM), lambda i: (0, 0)),
            pl.BlockSpec((1, m, mw), lambda i: (i // tiles_per_seq, 0, 0)),
            pl.BlockSpec((1, m, mw), lambda i: (i // tiles_per_seq, 0, 0)),
        ],
        out_specs=pl.BlockSpec((tm, mw), lambda i: (i, 0)),
        out_shape=jax.ShapeDtypeStruct((n, mw), BF16),
        compiler_params=_params("parallel"),
        name="mem_attention",
    )(h, w_qm, q_gain.reshape(1, MEM_HEAD_DIM), km, vm)


def _merge_kernel(x_ref, h_ref, oa_ref, os_ref, om_ref, wg_ref, wa_ref, ws_ref, wm_ref, wo_ref, gf_ref, wr_ref,
                  x1_ref, h2_ref, route_ref):
    d = x_ref.shape[1]
    gates = _sigmoid(jnp.dot(h_ref[...], wg_ref[...], preferred_element_type=F32))
    merged = gates[:, :d] * jnp.dot(oa_ref[...], wa_ref[...], preferred_element_type=F32)
    merged = merged + gates[:, d:2 * d] * jnp.dot(os_ref[...], ws_ref[...], preferred_element_type=F32)
    merged = merged + gates[:, 2 * d:] * jnp.dot(om_ref[...], wm_ref[...], preferred_element_type=F32)
    x1 = x_ref[...] + jnp.dot(merged.astype(BF16), wo_ref[...], preferred_element_type=F32)
    x1_ref[...] = x1
    r = lax.rsqrt(jnp.mean(x1 * x1, axis=-1, keepdims=True) + EPS)
    h2 = (x1 * r) * gf_ref[...]
    h2_ref[...] = _rows_to_token_tiles(h2)

    h2_hi = h2.astype(BF16)
    h2_lo = (h2 - h2_hi.astype(F32)).astype(BF16)
    lg = (jnp.dot(h2_hi, wr_ref[0], preferred_element_type=F32) + jnp.dot(h2_hi, wr_ref[1], preferred_element_type=F32)
          + jnp.dot(h2_lo, wr_ref[0], preferred_element_type=F32))
    lanef = lax.broadcasted_iota(jnp.int32, lg.shape, 1).astype(F32)
    big = float(LANES)
    gmask = lanef < MOE_GROUPS
    gl = jnp.where(gmask, lg, -jnp.inf)
    gmax = jnp.max(gl, axis=-1, keepdims=True)
    p_g = 1.0 / jnp.sum(jnp.exp(gl - gmax), axis=-1, keepdims=True)
    g_sel = jnp.min(jnp.where(gl == gmax, lanef, big), axis=-1, keepdims=True)
    lo = MOE_GROUPS + MOE_EXPERTS_PER_GROUP * g_sel
    el = jnp.where((lanef >= lo) & (lanef < lo + MOE_EXPERTS_PER_GROUP), lg, -jnp.inf)
    m1 = jnp.max(el, axis=-1, keepdims=True)
    i1 = jnp.min(jnp.where(el == m1, lanef, big), axis=-1, keepdims=True)
    el2 = jnp.where(lanef == i1, -jnp.inf, el)
    m2 = jnp.max(el2, axis=-1, keepdims=True)
    i2 = jnp.min(jnp.where(el2 == m2, lanef, big), axis=-1, keepdims=True)
    e2 = jnp.exp(m2 - m1)
    w1 = 1.0 / (1.0 + e2)
    w2 = e2 / (1.0 + e2)
    route = jnp.where(lanef == 0, i1 - MOE_GROUPS, 0.0)
    route = jnp.where(lanef == 1, i2 - MOE_GROUPS, route)
    route = jnp.where(lanef == 2, p_g * w1, route)
    route = jnp.where(lanef == 3, p_g * w2, route)
    route_ref[...] = route


def _merge(x2, h, o_a, o_s, o_m, w_gates, w_a, w_s, w_m, w_out, g_ffn, w_router, tm=512):
    n, d = x2.shape
    full = lambda arr: pl.BlockSpec(arr.shape, lambda i: (0,) * arr.ndim)
    tile = lambda arr: pl.BlockSpec((tm, arr.shape[1]), lambda i: (i, 0))
    gf = g_ffn.reshape(1, d)
    return pl.pallas_call(
        _merge_kernel,
        grid=(n // tm,),
        in_specs=[tile(x2), tile(h), tile(o_a), tile(o_s), tile(o_m), full(w_gates), full(w_a), full(w_s), full(w_m),
                  full(w_out), full(gf), full(w_router)],
        out_specs=[pl.BlockSpec((tm, d), lambda i: (i, 0)),
                   pl.BlockSpec((tm, d // LANES, LANES), lambda i: (i, 0, 0)),
                   pl.BlockSpec((tm, LANES), lambda i: (i, 0))],
        out_shape=[jax.ShapeDtypeStruct((n, d), F32), jax.ShapeDtypeStruct((n, d // LANES, LANES), F32),
                   jax.ShapeDtypeStruct((n, LANES), F32)],
        compiler_params=_params("parallel"),
        name="merge_router",
    )(x2, h, o_a, o_s, o_m, w_gates, w_a, w_s, w_m, w_out, gf, w_router)


def _moe_kernel(be_ref, j0_ref, nv_ref, tok_ref, dst_ref, h2_hbm, wg_ref, wu_ref, wd_ref, y2_hbm, xbuf, ybuf, gsem,
                ssem):
    rows = xbuf.shape[1]
    n_tok = h2_hbm.shape[0]
    nblocks = pl.num_programs(0)
    i = pl.program_id(0)
    nbuf = xbuf.shape[0]
    slot = lax.rem(i, nbuf)
    slot_next = lax.rem(i + 1, nbuf)
    slot_prev = lax.rem(i + 2, nbuf)
    dump0 = MOE_TOPK * n_tok

    def block_rows(blk):
        b = jnp.clip(blk, 0, nblocks - 1)
        return j0_ref[b], jnp.where(blk < 0, -1, nv_ref[b] - 1)

    n_assign = tok_ref.shape[0]

    def gather_row(span, sl, r, prio):
        j0, _ = span
        tok = tok_ref[jnp.minimum(j0 + r, n_assign - 1)]
        pltpu.make_async_copy(h2_hbm.at[pl.ds(tok, 1)], xbuf.at[sl, pl.ds(r, 1)], gsem.at[sl]).start(priority=prio)

    def scatter_row(span, sl, r, prio):
        j0, last = span
        dst = jnp.where(r <= last, dst_ref[jnp.minimum(j0 + r, n_assign - 1)], dump0 + sl * rows + r)
        pltpu.make_async_copy(ybuf.at[sl, pl.ds(r, 1)], y2_hbm.at[pl.ds(dst, 1)], ssem.at[sl]).start(priority=prio)

    def looped(fn, blk, sl):
        span = block_rows(blk)

        def body(c, carry):
            for k in range(MOE_DMA_UNROLL):
                fn(span, sl, c * MOE_DMA_UNROLL + k, k % 2)
            return carry
        lax.fori_loop(0, rows // MOE_DMA_UNROLL, body, 0)

    def gather_wait(sl):
        pltpu.make_async_copy(h2_hbm.at[pl.ds(0, rows)], xbuf.at[sl], gsem.at[sl]).wait()

    def scatter_wait(sl):
        pltpu.make_async_copy(ybuf.at[sl], y2_hbm.at[pl.ds(0, rows)], ssem.at[sl]).wait()

    @pl.when(i == 0)
    def _():
        looped(gather_row, 0, 0)
        looped(gather_row, 1, 1)
        ybuf[...] = jnp.zeros_like(ybuf)
        for sl in range(2):
            pltpu.make_async_copy(ybuf.at[sl], y2_hbm.at[pl.ds(dump0 + sl * rows, rows)], ssem.at[sl]).start()

    gather_wait(slot)
    scatter_wait(slot)
    chunks = xbuf.shape[2]
    nxt, prv = block_rows(i + 2), block_rows(i - 1)
    for r in range(rows):
        scatter_row(prv, slot_prev, r, r % 2)
    x = _token_tiles_to_rows(xbuf[slot]).astype(BF16)
    for r in range(rows):
        gather_row(nxt, slot_prev, r, r % 2)
    gate = jnp.dot(x, wg_ref[0], preferred_element_type=F32)
    up = jnp.dot(x, wu_ref[0], preferred_element_type=F32)
    hid = (gate * _sigmoid(gate)) * up
    y = jnp.dot(hid.astype(BF16), wd_ref[0], preferred_element_type=F32)
    ybuf[slot] = _rows_to_token_tiles(y)

    @pl.when(i == nblocks - 1)
    def _():
        gather_wait(slot_next)
        gather_wait(slot_prev)
        scatter_wait(slot_next)
        scatter_wait(slot_prev)
        looped(scatter_row, i, slot)
        scatter_wait(slot)


def _moe_experts(h2, tables, w_gate, w_up, w_down):
    n = h2.shape[0]
    e, d, f = w_gate.shape
    nblocks = tables[0].shape[0]
    wmap = lambda i, be, j0, nv, tok, dst: (be[i], 0, 0)
    grid_spec = pltpu.PrefetchScalarGridSpec(
        num_scalar_prefetch=5,
        grid=(nblocks,),
        in_specs=[
            pl.BlockSpec(memory_space=pl.ANY),
            pl.BlockSpec((1, d, f), wmap),
            pl.BlockSpec((1, d, f), wmap),
            pl.BlockSpec((1, f, d), wmap),
        ],
        out_specs=pl.BlockSpec(memory_space=pl.ANY),
        scratch_shapes=[
            pltpu.VMEM((MOE_BUFFERS, MOE_ROWS, d // LANES, LANES), F32),
            pltpu.VMEM((MOE_BUFFERS, MOE_ROWS, d // LANES, LANES), F32),
            pltpu.SemaphoreType.DMA((MOE_BUFFERS,)),
            pltpu.SemaphoreType.DMA((MOE_BUFFERS,)),
        ],
    )
    return pl.pallas_call(
        _moe_kernel,
        grid_spec=grid_spec,
        out_shape=jax.ShapeDtypeStruct((n * MOE_TOPK + MOE_BUFFERS * MOE_ROWS, d // LANES, LANES), F32),
        compiler_params=_params("arbitrary"),
        name="moe_experts",
    )(*tables, h2, w_gate, w_up, w_down)


def _combine_kernel(x1_ref, y0_ref, y1_ref, route_ref, o_ref):
    route = route_ref[...]
    w0 = route[:, 2:3]
    w1 = route[:, 3:4]
    o_ref[...] = x1_ref[...] + (w0 * _token_tiles_to_rows(y0_ref[...]) + w1 * _token_tiles_to_rows(y1_ref[...]))


def _combine(x1, y2, route, tm=512):
    n, d = x1.shape
    tiles = n // tm
    ytile = (tm, d // LANES, LANES)
    return pl.pallas_call(
        _combine_kernel,
        grid=(tiles,),
        in_specs=[pl.BlockSpec((tm, d), lambda i: (i, 0)), pl.BlockSpec(ytile, lambda i: (i, 0, 0)),
                  pl.BlockSpec(ytile, lambda i: (tiles + i, 0, 0)), pl.BlockSpec((tm, LANES), lambda i: (i, 0))],
        out_specs=pl.BlockSpec((tm, d), lambda i: (i, 0)),
        out_shape=jax.ShapeDtypeStruct((n, d), F32),
        compiler_params=_params("parallel"),
        name="moe_combine",
    )(x1, y2, y2, route)


def _dispatch_tables(route, n_tok):
    n_assign = n_tok * MOE_TOPK
    e_flat = route[:, :MOE_TOPK].astype(jnp.int32).reshape(n_assign)
    order = jnp.argsort(e_flat).astype(jnp.int32)
    counts = jnp.sum(e_flat[:, None] == jnp.arange(MOE_EXPERTS, dtype=jnp.int32)[None, :], axis=0, dtype=jnp.int32)
    blocks_per_expert = (counts + MOE_ROWS - 1) // MOE_ROWS
    blk_end = jnp.cumsum(blocks_per_expert)
    raw_start = jnp.cumsum(counts) - counts
    nblocks = n_assign // MOE_ROWS + MOE_EXPERTS
    blk = jnp.arange(nblocks, dtype=jnp.int32)
    blk_expert = jnp.minimum(jnp.sum(blk_end[None, :] <= blk[:, None], axis=1), MOE_EXPERTS - 1).astype(jnp.int32)
    within = blk - (blk_end - blocks_per_expert)[blk_expert]
    blk_nvalid = jnp.clip(counts[blk_expert] - within * MOE_ROWS, 0, MOE_ROWS).astype(jnp.int32)
    blk_j0 = jnp.where(blk_nvalid > 0, raw_start[blk_expert] + within * MOE_ROWS, 0).astype(jnp.int32)
    tok_sorted = order // MOE_TOPK
    dst_sorted = (order % MOE_TOPK) * n_tok + tok_sorted
    return blk_expert, blk_j0, blk_nvalid, tok_sorted, dst_sorted


def kernel(x, mem, g_mix, w_in, moba_q_norm, moba_k_norm, conv_w, conv_b, dt_bias, a_log, d_skip, ssd_norm, g_mem,
           w_mem_kv, mem_q_norm, mem_k_norm, w_o_moba, w_o_ssd, w_o_mem, w_out, g_ffn, w_router_group,
           w_router_expert, w_gate, w_up, w_down):
    bsz, seq, d = x.shape
    assert seq % MOBA_BLOCK == 0 and seq % SSD_CHUNK == 0
    n_tok = bsz * seq
    moba_w = MOBA_HEADS * MOBA_HEAD_DIM
    d_in = SSD_HEADS * SSD_HEAD_DIM
    xbc_w = d_in + 2 * SSD_GROUPS * SSD_STATE
    mem_w = MEM_HEADS * MEM_HEAD_DIM
    sizes = (moba_w, moba_w, moba_w, d_in, xbc_w, SSD_HEADS, mem_w, 3 * d)
    offs = [0]
    for sz in sizes:
        offs.append(offs[-1] + sz)
    w_in_b = w_in.astype(BF16)
    w_qk = w_in_b[:, offs[0]:offs[2]]
    w_v = w_in_b[:, offs[2]:offs[3]]
    w_z = w_in_b[:, offs[3]:offs[4]]
    w_xbc = w_in_b[:, offs[4]:offs[5]]
    w_dt = jnp.pad(w_in_b[:, offs[5]:offs[6]], ((0, 0), (0, LANES - SSD_HEADS)))
    w_qm = w_in_b[:, offs[6]:offs[7]]
    w_gates = w_in_b[:, offs[7]:offs[8]]

    x2 = x.reshape(n_tok, d)

    half = MOBA_HEAD_DIM // 2
    inv = ROPE_THETA ** (-jnp.arange(half, dtype=F32) / half)
    ang = jnp.arange(seq, dtype=F32)[:, None] * inv[None, :]
    cos_t = jnp.tile(jnp.cos(ang), (1, LANES // half))
    sin_t = jnp.tile(jnp.concatenate([-jnp.sin(ang), jnp.sin(ang)], axis=1), (1, LANES // MOBA_HEAD_DIM))
    gains = jnp.stack([jnp.tile(moba_q_norm, MOBA_HEADS), jnp.tile(moba_k_norm, MOBA_HEADS)]).reshape(2, 1, moba_w)

    h, qk, v, zs, xbc, dt = _inproj(x2, g_mix, w_qk, w_v, w_z, w_xbc, w_dt, gains, cos_t, sin_t, conv_w, conv_b,
                                    dt_bias, seq)

    qk3 = qk.reshape(bsz, seq, 2 * moba_w)
    v3 = v.reshape(bsz, seq, moba_w)
    kmean, ka, vt = _moba_prep(qk3, v3)
    nb = seq // MOBA_BLOCK
    assert nb <= MOBA_MAX_BLOCKS and nb % MOBA_UNROLL == 0
    kmean_pad = jnp.pad(kmean.reshape(bsz, nb, moba_w), ((0, 0), (0, MOBA_MAX_BLOCKS - nb), (0, 0)))
    qa = _moba_select(qk3, kmean_pad)
    o_a = _moba_attention(qa, ka, vt).reshape(n_tok, moba_w)

    o_s = _ssd(xbc.reshape(bsz, seq, xbc_w), zs.reshape(bsz, seq, d_in), dt.reshape(bsz, seq, LANES), a_log, d_skip,
               ssd_norm).reshape(n_tok, d_in)

    km, vm = _memkv(mem, g_mem, w_mem_kv.astype(BF16), mem_k_norm)
    o_m = _memattn(h, w_qm, mem_q_norm, km, vm, seq)

    w_router = jnp.pad(jnp.concatenate([w_router_group, w_router_expert], axis=1),
                       ((0, 0), (0, LANES - MOE_GROUPS - MOE_EXPERTS)))
    w_router_hi = w_router.astype(BF16)
    w_router = jnp.stack([w_router_hi, (w_router - w_router_hi.astype(F32)).astype(BF16)])
    x1, h2, route = _merge(x2, h, o_a, o_s, o_m, w_gates, w_o_moba.astype(BF16), w_o_ssd.astype(BF16),
                           w_o_mem.astype(BF16), w_out.astype(BF16), g_ffn, w_router)

    tables = _dispatch_tables(route, n_tok)
    y2 = _moe_experts(h2, tables, w_gate.astype(BF16), w_up.astype(BF16), w_down.astype(BF16))
    out = _combine(x1, y2, route)
    return out.reshape(bsz, seq, d)
```

```python
import functools

import jax
import jax.numpy as jnp
from jax import lax
from jax.experimental import pallas as pl
from jax.experimental.pallas import tpu as pltpu

F32 = jnp.float32
BF16 = jnp.bfloat16
HIGHEST = lax.Precision.HIGHEST

EPS = 1e-6
ROPE_THETA = 10000.0
MOBA_HEADS = 8
MOBA_HEAD_DIM = 64
MOBA_BLOCK = 256
MOBA_TOPK = 3
SSD_HEAD_DIM = 64
SSD_HEADS = 16
SSD_GROUPS = 4
SSD_STATE = 128
SSD_CONV = 4
SSD_CHUNK = 256
MEM_HEADS = 4
MEM_HEAD_DIM = 128
MOE_GROUPS = 4
MOE_EXPERTS_PER_GROUP = 8
MOE_EXPERTS = MOE_GROUPS * MOE_EXPERTS_PER_GROUP
MOE_TOPK = 2

LANES = 128
MASKED = -1e30
MOBA_MAX_BLOCKS = 32
MOBA_UNROLL = 4
MOBA_TILES_PER_STEP = 2
MOBA_VT_ROWS = MOBA_HEAD_DIM + 16
MOE_ROWS = 256
INPROJ_CONV_CHUNK = 512
MOE_BUFFERS = 3
MOE_DMA_UNROLL = 8
VMEM_LIMIT = 56 * 1024 * 1024


def _params(*sem):
    return pltpu.CompilerParams(dimension_semantics=sem, vmem_limit_bytes=VMEM_LIMIT)


def _sigmoid(x):
    return 1.0 / (1.0 + jnp.exp(-x))


def _split3_dot(x, sel):
    hi = x.astype(BF16)
    rest = x - hi.astype(F32)
    mid = rest.astype(BF16)
    lo = (rest - mid.astype(F32)).astype(BF16)
    return (jnp.dot(hi, sel, preferred_element_type=F32) + jnp.dot(mid, sel, preferred_element_type=F32)
            + jnp.dot(lo, sel, preferred_element_type=F32))


def _rows_to_token_tiles(x):
    rows, width = x.shape
    chunks = width // LANES
    x4 = jnp.stack([x[:, k * LANES:(k + 1) * LANES].reshape(rows // 8, 8, LANES) for k in range(chunks)], axis=1)
    return jnp.swapaxes(x4, 1, 2).reshape(rows, chunks, LANES)


def _token_tiles_to_rows(x3):
    rows, chunks, _ = x3.shape
    x4 = jnp.swapaxes(x3.reshape(rows // 8, 8, chunks, LANES), 1, 2)
    return jnp.concatenate([x4[:, k].reshape(rows, LANES) for k in range(chunks)], axis=1)


def _nt_dot(a, b, precision=None):
    return lax.dot_general(a, b, (((1,), (1,)), ((), ())), precision=precision, preferred_element_type=F32)


def _softplus(x):
    return jnp.maximum(x, 0.0) + jnp.log1p(jnp.exp(-jnp.abs(x)))


def _inproj_kernel(x_ref, gmix_ref, wqk_ref, wv_ref, wz_ref, wxbc_ref, wdt_ref, gain_ref, gsum_ref, cos_ref, sin_ref,
                   cw_ref, cb_ref, dtb_ref, h_ref, qk_ref, v_ref, zs_ref, xbc_ref, dt_ref, *ext_refs,
                   tiles_per_seq):
    tm = x_ref.shape[0]
    x = x_ref[...]
    r = lax.rsqrt(jnp.mean(x * x, axis=-1, keepdims=True) + EPS)
    h = ((x * r) * gmix_ref[...]).astype(BF16)
    h_ref[...] = h

    width = gsum_ref.shape[0]
    half = MOBA_HEAD_DIM // 2
    lane = lax.broadcasted_iota(jnp.int32, (tm, width), 1)
    first = (lane & (MOBA_HEAD_DIM - 1)) < half
    reps = width // cos_ref.shape[1]
    cos = jnp.concatenate([cos_ref[...]] * reps, axis=1)
    sin = jnp.concatenate([sin_ref[...]] * reps, axis=1)
    for j in range(2):
        u = jnp.dot(h, wqk_ref[:, j * width:(j + 1) * width], preferred_element_type=F32)
        sq = u * u
        hi = sq.astype(BF16)
        lo = (sq - hi.astype(F32)).astype(BF16)
        ss = (jnp.dot(hi, gsum_ref[...], preferred_element_type=F32)
              + jnp.dot(lo, gsum_ref[...], preferred_element_type=F32))
        un = (u * lax.rsqrt(ss * (1.0 / MOBA_HEAD_DIM) + EPS)) * gain_ref[j]
        partner = jnp.where(first, pltpu.roll(un, width - half, axis=1), pltpu.roll(un, half, axis=1))
        qk_ref[:, j * width:(j + 1) * width] = un * cos + partner * sin

    v_ref[...] = jnp.dot(h, wv_ref[...], preferred_element_type=F32).astype(v_ref.dtype)
    z = jnp.dot(h, wz_ref[...], preferred_element_type=F32)
    zs_ref[...] = (z * _sigmoid(z)).astype(zs_ref.dtype)
    dt_ref[...] = _softplus(jnp.dot(h, wdt_ref[...], preferred_element_type=F32) + dtb_ref[...])

    @pl.when(pl.program_id(0) % tiles_per_seq == 0)
    def _():
        for ext_ref in ext_refs:
            ext_ref[0:8, :] = jnp.zeros((8, ext_ref.shape[1]), F32)

    cchunk = ext_refs[0].shape[1]
    for ci, ext_ref in enumerate(ext_refs):
        cs = slice(ci * cchunk, (ci + 1) * cchunk)
        u = jnp.dot(h, wxbc_ref[:, cs], preferred_element_type=F32)
        ext_ref[8:8 + tm, :] = u
        acc = cb_ref[:, cs] + cw_ref[SSD_CONV - 1:SSD_CONV, cs] * u
        for kk in range(SSD_CONV - 1):
            off = 8 - (SSD_CONV - 1 - kk)
            acc = acc + cw_ref[kk:kk + 1, cs] * ext_ref[off:off + tm, :]
        ext_ref[0:8, :] = u[tm - 8:tm, :]
        xbc_ref[:, cs] = (acc * _sigmoid(acc)).astype(xbc_ref.dtype)


def _inproj(x2, g_mix, w_qk, w_v, w_z, w_xbc, w_dt, gains, cos_t, sin_t, conv_w, conv_b, dt_bias, seq, tm=512):
    n, d = x2.shape
    width = w_qk.shape[1] // 2
    tiles_per_seq = seq // tm
    head = jnp.arange(width) // MOBA_HEAD_DIM
    gsum = (head[:, None] == head[None, :]).astype(BF16)
    dtb = jnp.pad(dt_bias.astype(F32), (0, LANES - dt_bias.shape[0])).reshape(1, LANES)
    full = lambda arr: pl.BlockSpec(arr.shape, lambda i: (0,) * arr.ndim)
    tile = lambda cols: pl.BlockSpec((tm, cols), lambda i: (i, 0))
    rope = pl.BlockSpec((tm, LANES), lambda i: (i % tiles_per_seq, 0))
    consts = (g_mix.reshape(1, d), w_qk, w_v, w_z, w_xbc, w_dt, gains, gsum)
    tail = (conv_w, conv_b.reshape(1, -1), dtb)
    outs = ((d, BF16), (2 * width, F32), (w_v.shape[1], BF16), (w_z.shape[1], BF16), (w_xbc.shape[1], BF16),
            (LANES, F32))
    return pl.pallas_call(
        functools.partial(_inproj_kernel, tiles_per_seq=tiles_per_seq),
        grid=(n // tm,),
        in_specs=[tile(d)] + [full(a) for a in consts] + [rope, rope] + [full(a) for a in tail],
        out_specs=[tile(c) for c, _ in outs],
        out_shape=[jax.ShapeDtypeStruct((n, c), dt) for c, dt in outs],
        scratch_shapes=[pltpu.VMEM((tm + 8, INPROJ_CONV_CHUNK), F32)] * (w_xbc.shape[1] // INPROJ_CONV_CHUNK),
        compiler_params=_params("arbitrary"),
        name="in_proj",
    )(x2, *consts, cos_t, sin_t, *tail)


def _moba_prep_kernel(q_ref, k_ref, v_ref, ka_ref, vt_ref, qa_ref, km_ref):
    n = pl.program_id(1)

    @pl.when(n == 0)
    def _():
        km_ref[...] = jnp.zeros_like(km_ref)

    _moba_select(q_ref, km_ref, qa_ref, n)

    ones = jnp.ones((MOBA_VT_ROWS - MOBA_HEAD_DIM, MOBA_BLOCK), BF16)
    k = k_ref[0]
    km_ref[pl.ds(n, 1), :] = jnp.mean(k, axis=0, keepdims=True)
    lane = lax.broadcasted_iota(jnp.int32, (MOBA_BLOCK, LANES), 1)
    own = lane < MOBA_HEAD_DIM
    onehot = jnp.where(lane == MOBA_HEAD_DIM + n, 1.0, 0.0)
    for pr in range(MOBA_HEADS // 2):
        pair = slice(pr * LANES, (pr + 1) * LANES)
        kp = k[:, pair]
        ka_ref[0, 2 * pr, 0] = jnp.where(own, kp, onehot).astype(BF16)
        ka_ref[0, 2 * pr + 1, 0] = jnp.where(own, pltpu.roll(kp, MOBA_HEAD_DIM, axis=1), onehot).astype(BF16)
        vp_t = v_ref[0, :, pair].astype(F32).T
        for hh in range(2):
            vt_ref[0, 2 * pr + hh, 0, :MOBA_HEAD_DIM, :] = vp_t[hh * MOBA_HEAD_DIM:(hh + 1) * MOBA_HEAD_DIM].astype(BF16)
            vt_ref[0, 2 * pr + hh, 0, MOBA_HEAD_DIM:, :] = ones


def _moba_prep(qk3, v3):
    b, s, w = v3.shape
    nb = s // MOBA_BLOCK
    return pl.pallas_call(
        _moba_prep_kernel,
        grid=(b, nb),
        in_specs=[pl.BlockSpec((1, MOBA_BLOCK, w), lambda bi, n: (bi, n, 0)),
                  pl.BlockSpec((1, MOBA_BLOCK, w), lambda bi, n: (bi, n, 1)),
                  pl.BlockSpec((1, MOBA_BLOCK, w), lambda bi, n: (bi, n, 0))],
        out_specs=[
            pl.BlockSpec((1, MOBA_HEADS, 1, MOBA_BLOCK, LANES), lambda bi, n: (bi, 0, n, 0, 0)),
            pl.BlockSpec((1, MOBA_HEADS, 1, MOBA_VT_ROWS, MOBA_BLOCK), lambda bi, n: (bi, 0, n, 0, 0)),
            pl.BlockSpec((1, MOBA_HEADS, 1, LANES, MOBA_BLOCK), lambda bi, n: (bi, 0, n, 0, 0)),
        ],
        out_shape=[jax.ShapeDtypeStruct((b, MOBA_HEADS, nb, MOBA_BLOCK, LANES), BF16),
                   jax.ShapeDtypeStruct((b, MOBA_HEADS, nb, MOBA_VT_ROWS, MOBA_BLOCK), BF16),
                   jax.ShapeDtypeStruct((b, MOBA_HEADS, nb, LANES, MOBA_BLOCK), BF16)],
        scratch_shapes=[pltpu.VMEM((MOBA_MAX_BLOCKS, w), F32)],
        compiler_params=_params("parallel", "arbitrary"),
        name="moba_prep",
    )(qk3, qk3, v3)


def _moba_select(q_ref, km_ref, qa_ref, i):
    blk = MOBA_BLOCK
    nsel = MOBA_MAX_BLOCKS
    qscale = (MOBA_HEAD_DIM ** -0.5) * 1.4426950408889634
    q_t = (q_ref[0] * qscale).T
    kmean = km_ref[...]
    lane = lax.broadcasted_iota(jnp.int32, (nsel, LANES), 1)
    blk_row = lax.broadcasted_iota(jnp.int32, (nsel, blk), 0)
    blk_rowf = blk_row.astype(F32)
    pad = jnp.zeros((LANES - MOBA_HEAD_DIM - nsel, blk), F32)
    for hd in range(MOBA_HEADS):
        pr, hh = hd // 2, hd % 2
        hmask = (lane >= hh * MOBA_HEAD_DIM) & (lane < (hh + 1) * MOBA_HEAD_DIM)
        km_h = jnp.where(hmask, kmean[:, pr * LANES:(pr + 1) * LANES], 0.0)
        gate = jnp.dot(km_h, q_t[pr * LANES:(pr + 1) * LANES], precision=HIGHEST, preferred_element_type=F32)
        g = jnp.where(blk_row < i, gate, -jnp.inf)
        bias = jnp.full((nsel, blk), MASKED, F32)
        for _ in range(MOBA_TOPK):
            m = jnp.max(g, axis=0, keepdims=True)
            hit = (g == m) & (m > -jnp.inf)
            first = jnp.min(jnp.where(hit, blk_rowf, float(nsel)), axis=0, keepdims=True)
            sel = blk_rowf == first
            bias = jnp.where(sel, 0.0, bias)
            g = jnp.where(sel, -jnp.inf, g)
        qs = q_t[hd * MOBA_HEAD_DIM:(hd + 1) * MOBA_HEAD_DIM]
        qa_ref[0, hd, 0] = jnp.concatenate([qs, bias, pad], axis=0).astype(BF16)


def _moba_kernel(qa_ref, qan_ref, ka_ref, vt_ref, o_ref, sa_ref, sb_ref, sc_ref, gma_ref, gmb_ref, gmc_ref, m_ref,
                 acc_ref):
    blk = MOBA_BLOCK
    nblk = ka_ref.shape[2]
    key_pos = lax.broadcasted_iota(jnp.int32, (blk, blk), 0)
    qry_pos = lax.broadcasted_iota(jnp.int32, (blk, blk), 1)
    causal = key_pos <= qry_pos
    feat_row = lax.broadcasted_iota(jnp.int32, (LANES, blk), 0) < MOBA_HEAD_DIM

    def fold(s, op):
        return op(s.reshape(blk // 8, 8, blk), axis=0)

    def score_tiles(tiles, s_ref, gm_ref):
        for hh in range(2):
            gmax = None
            for u, (n, q_of, is_own) in enumerate(tiles):
                s = jnp.dot(ka_ref[0, hh, n], q_of(hh), preferred_element_type=F32)
                if is_own:
                    s = jnp.where(causal, s, MASKED)
                s_ref[hh, u] = s
                gmax = fold(s, jnp.max) if gmax is None else jnp.maximum(gmax, fold(s, jnp.max))
            gm_ref[hh] = gmax

    def score_first(q_past, own, s_ref, gm_ref):
        q_own = lambda hh: jnp.where(feat_row, q_past(hh), jnp.zeros((), BF16))
        tiles = [(own, q_own, True)] + [(u - 1, q_past, False) for u in range(1, MOBA_UNROLL)]
        score_tiles(tiles, s_ref, gm_ref)

    def score_group(q_past, grp, s_ref, gm_ref):
        tiles = [(jnp.minimum(grp * MOBA_UNROLL + (u - 1), nblk - 1), q_past, False) for u in range(MOBA_UNROLL)]
        score_tiles(tiles, s_ref, gm_ref)

    def value_group(i, grp, s_ref, gm_ref):
        for hh in range(2):
            m_old = m_ref[hh][0:1]
            m_new = jnp.maximum(m_old, jnp.max(gm_ref[hh], axis=0, keepdims=True))
            alpha = jnp.exp2(m_old - m_new)
            acc = acc_ref[hh] * alpha
            for u in range(MOBA_UNROLL):
                n = jnp.minimum(grp * MOBA_UNROLL + (u - 1), nblk - 1)
                if u == 0:
                    n = jnp.where(grp == 0, i, n)
                p = jnp.exp2(s_ref[hh, u] - m_new).astype(BF16)
                acc = acc + jnp.dot(vt_ref[0, hh, n], p, preferred_element_type=F32)
            m_ref[hh] = jnp.broadcast_to(m_new, (8, blk))
            acc_ref[hh] = acc

    first_tile = pl.program_id(2) * MOBA_TILES_PER_STEP

    @pl.when(first_tile == 0)
    def _():
        score_first(lambda hh: qa_ref[0, hh, 0], 0, sc_ref, gmc_ref)

    for sub in range(MOBA_TILES_PER_STEP):
        i = first_tile + sub
        ngroups = (i + MOBA_UNROLL) // MOBA_UNROLL
        q_cur = lambda hh, sub=sub: qa_ref[0, hh, sub]
        if sub + 1 < MOBA_TILES_PER_STEP:
            q_nxt = lambda hh, sub=sub: qa_ref[0, hh, sub + 1]
        else:
            q_nxt = lambda hh: qan_ref[0, hh, 0]
        for hh in range(2):
            m_ref[hh] = jnp.full((8, blk), MASKED, F32)
            acc_ref[hh] = jnp.zeros((MOBA_VT_ROWS, blk), F32)

        score_group(q_cur, 1, sa_ref, gma_ref)
        value_group(i, 0, sc_ref, gmc_ref)

        rest = ngroups - 1
        npairs = jnp.maximum(rest - 1, 0) // 2

        def body(t, carry, i=i, q_cur=q_cur):
            score_group(q_cur, 2 * t + 2, sb_ref, gmb_ref)
            value_group(i, 2 * t + 1, sa_ref, gma_ref)
            score_group(q_cur, 2 * t + 3, sa_ref, gma_ref)
            value_group(i, 2 * t + 2, sb_ref, gmb_ref)
            return carry

        lax.fori_loop(0, npairs, body, 0)
        left = rest - 2 * npairs
        own_next = jnp.minimum(i + 1, nblk - 1)

        @pl.when(left == 2)
        def _(i=i, q_cur=q_cur, q_nxt=q_nxt, npairs=npairs, own_next=own_next):
            score_group(q_cur, 2 * npairs + 2, sb_ref, gmb_ref)
            value_group(i, 2 * npairs + 1, sa_ref, gma_ref)
            score_first(q_nxt, own_next, sc_ref, gmc_ref)
            value_group(i, 2 * npairs + 2, sb_ref, gmb_ref)

        @pl.when(left == 1)
        def _(i=i, q_nxt=q_nxt, npairs=npairs, own_next=own_next):
            score_first(q_nxt, own_next, sc_ref, gmc_ref)
            value_group(i, 2 * npairs + 1, sa_ref, gma_ref)

        @pl.when(left == 0)
        def _(q_nxt=q_nxt, own_next=own_next):
            score_first(q_nxt, own_next, sc_ref, gmc_ref)

        outs = [acc_ref[hh, :MOBA_HEAD_DIM, :] / acc_ref[hh, MOBA_HEAD_DIM:MOBA_HEAD_DIM + 1, :] for hh in range(2)]
        o_ref[0, sub * blk:(sub + 1) * blk, :] = jnp.concatenate(outs, axis=0).T.astype(o_ref.dtype)


def _moba_attention(qa, ka, vt):
    b, heads, nq = qa.shape[:3]
    s = nq * MOBA_BLOCK
    pairs = heads // 2
    tps = MOBA_TILES_PER_STEP
    assert nq % tps == 0
    return pl.pallas_call(
        _moba_kernel,
        grid=(b, pairs, nq // tps),
        in_specs=[
            pl.BlockSpec((1, 2, tps, LANES, MOBA_BLOCK), lambda bi, p, j: (bi, p, j, 0, 0)),
            pl.BlockSpec((1, 2, 1, LANES, MOBA_BLOCK),
                         lambda bi, p, j: (bi, p, jnp.minimum((j + 1) * tps, nq - 1), 0, 0)),
            pl.BlockSpec((1, 2, nq, MOBA_BLOCK, LANES), lambda bi, p, j: (bi, p, 0, 0, 0)),
            pl.BlockSpec((1, 2, nq, MOBA_VT_ROWS, MOBA_BLOCK), lambda bi, p, j: (bi, p, 0, 0, 0)),
        ],
        out_specs=pl.BlockSpec((1, tps * MOBA_BLOCK, LANES), lambda bi, p, j: (bi, j, p)),
        out_shape=jax.ShapeDtypeStruct((b, s, heads * MOBA_HEAD_DIM), BF16),
        scratch_shapes=[
            pltpu.VMEM((2, MOBA_UNROLL, MOBA_BLOCK, MOBA_BLOCK), F32),
            pltpu.VMEM((2, MOBA_UNROLL, MOBA_BLOCK, MOBA_BLOCK), F32),
            pltpu.VMEM((2, MOBA_UNROLL, MOBA_BLOCK, MOBA_BLOCK), F32),
            pltpu.VMEM((2, 8, MOBA_BLOCK), F32),
            pltpu.VMEM((2, 8, MOBA_BLOCK), F32),
            pltpu.VMEM((2, 8, MOBA_BLOCK), F32),
            pltpu.VMEM((2, 8, MOBA_BLOCK), F32),
            pltpu.VMEM((2, MOBA_VT_ROWS, MOBA_BLOCK), F32),
        ],
        compiler_params=_params("parallel", "parallel", "arbitrary"),
        name="moba_attention",
    )(qa, qa, ka, vt)


def _ssd_kernel(xbc_ref, zs_ref, dt_ref, alog_ref, dskip_ref, gn_ref, ex_ref, o_ref, state_ref):
    q = SSD_CHUNK
    d_in = SSD_HEADS * SSD_HEAD_DIM
    bc_w = SSD_GROUPS * SSD_STATE
    gw = d_in // SSD_GROUPS
    c = pl.program_id(1)

    @pl.when(c == 0)
    def _():
        state_ref[...] = jnp.zeros_like(state_ref)

    xs = xbc_ref[0, :, :d_in].astype(F32)
    bm = xbc_ref[0, :, d_in:d_in + bc_w]
    cm = xbc_ref[0, :, d_in + bc_w:]
    dt = dt_ref[0]
    a = dt * (-jnp.exp(alog_ref[...]))
    row = lax.broadcasted_iota(jnp.int32, (q, q), 0)
    col = lax.broadcasted_iota(jnp.int32, (q, q), 1)
    tril = col <= row
    a_cum = jnp.dot(tril.astype(F32), a, precision=HIGHEST, preferred_element_type=F32)
    a_cum_t = a_cum.T
    ex = ex_ref[...]
    dt_x = _split3_dot(dt, ex)
    acum_x = _split3_dot(a_cum, ex)
    alast_x = acum_x[q - 1:q, :]
    x_dt = xs * dt_x
    xd = (x_dt * jnp.exp(alast_x - acum_x)).astype(BF16)
    x_dt_b = x_dt.astype(BF16)
    e_acum = jnp.exp(acum_x)
    e_alast = jnp.exp(alast_x)
    lane2 = lax.broadcasted_iota(jnp.int32, (q, 2 * SSD_HEAD_DIM), 1)
    heads_per_group = SSD_HEADS // SSD_GROUPS

    for g in range(SSD_GROUPS):
        bg = bm[:, g * SSD_STATE:(g + 1) * SSD_STATE]
        cg = cm[:, g * SSD_STATE:(g + 1) * SSD_STATE]
        cb = _nt_dot(cg, bg)
        h_in = state_ref[g]
        y_g = jnp.dot(cg, h_in.astype(BF16), preferred_element_type=F32) * e_acum[:, g * gw:(g + 1) * gw]
        parts = []
        for pr in range(heads_per_group // 2):
            xp = x_dt_b[:, g * gw + pr * 2 * SSD_HEAD_DIM:g * gw + (pr + 1) * 2 * SSD_HEAD_DIM]
            ys = []
            for hh in range(2):
                hd = g * heads_per_group + pr * 2 + hh
                seg = a_cum[:, hd:hd + 1] - a_cum_t[hd:hd + 1, :]
                lmat = jnp.exp(jnp.where(tril, seg, -jnp.inf))
                ys.append(jnp.dot((cb * lmat).astype(BF16), xp, preferred_element_type=F32))
            parts.append(jnp.where(lane2 < SSD_HEAD_DIM, ys[0], ys[1]))
        y_g = y_g + jnp.concatenate(parts, axis=1)
        st = jnp.dot(bg.astype(F32).T.astype(BF16), xd[:, g * gw:(g + 1) * gw], preferred_element_type=F32)
        state_ref[g] = h_in * e_alast[:, g * gw:(g + 1) * gw] + st

        sl = slice(g * gw, (g + 1) * gw)
        y_g = y_g + dskip_ref[:, sl] * xs[:, sl]
        y_g = y_g * zs_ref[0, :, sl].astype(F32)
        r = lax.rsqrt(jnp.mean(y_g * y_g, axis=-1, keepdims=True) + EPS)
        o_ref[0, :, sl] = ((y_g * r) * gn_ref[:, sl]).astype(o_ref.dtype)


def _ssd(xbc3, zs3, dt3, a_log, d_skip, ssd_norm):
    b, s, xw = xbc3.shape
    d_in = SSD_HEADS * SSD_HEAD_DIM
    nc = s // SSD_CHUNK
    pad = LANES - SSD_HEADS
    alog = jnp.pad(a_log.astype(F32), (0, pad)).reshape(1, LANES)
    dskip = jnp.repeat(d_skip.astype(F32), SSD_HEAD_DIM).reshape(1, d_in)
    expand = (jnp.arange(LANES)[:, None] == (jnp.arange(d_in) // SSD_HEAD_DIM)[None, :]).astype(BF16)
    const = lambda shape: pl.BlockSpec(shape, lambda bi, ci: (0,) * len(shape))
    return pl.pallas_call(
        _ssd_kernel,
        grid=(b, nc),
        in_specs=[
            pl.BlockSpec((1, SSD_CHUNK, xw), lambda bi, ci: (bi, ci, 0)),
            pl.BlockSpec((1, SSD_CHUNK, d_in), lambda bi, ci: (bi, ci, 0)),
            pl.BlockSpec((1, SSD_CHUNK, LANES), lambda bi, ci: (bi, ci, 0)),
            const((1, LANES)), const((1, d_in)), const((1, d_in)), const((LANES, d_in)),
        ],
        out_specs=pl.BlockSpec((1, SSD_CHUNK, d_in), lambda bi, ci: (bi, ci, 0)),
        out_shape=jax.ShapeDtypeStruct((b, s, d_in), BF16),
        scratch_shapes=[pltpu.VMEM((SSD_GROUPS, SSD_STATE, d_in // SSD_GROUPS), F32)],
        compiler_params=_params("parallel", "arbitrary"),
        name="ssd_scan",
    )(xbc3, zs3, dt3, alog, dskip, ssd_norm.reshape(1, d_in), expand)


def _memkv_kernel(mem_ref, g_ref, w_ref, kg_ref, km_ref, vm_ref):
    m = mem_ref[0]
    r = lax.rsqrt(jnp.mean(m * m, axis=-1, keepdims=True) + EPS)
    mn = ((m * r) * g_ref[...]).astype(BF16)
    kv = jnp.dot(mn, w_ref[...], preferred_element_type=F32)
    mw = MEM_HEADS * MEM_HEAD_DIM
    for hd in range(MEM_HEADS):
        sl = slice(hd * MEM_HEAD_DIM, (hd + 1) * MEM_HEAD_DIM)
        kh = kv[:, sl]
        rk = lax.rsqrt(jnp.mean(kh * kh, axis=-1, keepdims=True) + EPS)
        km_ref[0, :, sl] = ((kh * rk) * kg_ref[...]).astype(km_ref.dtype)
    vm_ref[0] = kv[:, mw:].astype(vm_ref.dtype)


def _memkv(mem, g_mem, w_kv, k_gain):
    b, m, d = mem.shape
    mw = MEM_HEADS * MEM_HEAD_DIM
    return pl.pallas_call(
        _memkv_kernel,
        grid=(b,),
        in_specs=[
            pl.BlockSpec((1, m, d), lambda bi: (bi, 0, 0)),
            pl.BlockSpec((1, d), lambda bi: (0, 0)),
            pl.BlockSpec((d, 2 * mw), lambda bi: (0, 0)),
            pl.BlockSpec((1, MEM_HEAD_DIM), lambda bi: (0, 0)),
        ],
        out_specs=[pl.BlockSpec((1, m, mw), lambda bi: (bi, 0, 0)), pl.BlockSpec((1, m, mw), lambda bi: (bi, 0, 0))],
        out_shape=[jax.ShapeDtypeStruct((b, m, mw), BF16), jax.ShapeDtypeStruct((b, m, mw), BF16)],
        compiler_params=_params("parallel"),
        name="mem_kv",
    )(mem, g_mem.reshape(1, d), w_kv, k_gain.reshape(1, MEM_HEAD_DIM))


def _memattn_kernel(h_ref, w_ref, qg_ref, km_ref, vm_ref, o_ref):
    qm = jnp.dot(h_ref[...], w_ref[...], preferred_element_type=F32)
    scale = MEM_HEAD_DIM ** -0.5
    for hd in range(MEM_HEADS):
        sl = slice(hd * MEM_HEAD_DIM, (hd + 1) * MEM_HEAD_DIM)
        qh = qm[:, sl]
        r = lax.rsqrt(jnp.mean(qh * qh, axis=-1, keepdims=True) + EPS)
        qn = ((qh * r) * qg_ref[...]).astype(BF16)
        s = _nt_dot(qn, km_ref[0, :, sl]) * scale
        p = jnp.exp(s - jnp.max(s, axis=-1, keepdims=True))
        l = jnp.sum(p, axis=-1, keepdims=True)
        o = jnp.dot(p.astype(BF16), vm_ref[0, :, sl], preferred_element_type=F32)
        o_ref[:, sl] = (o / l).astype(o_ref.dtype)


def _memattn(h, w_qm, q_gain, km, vm, seq, tm=512):
    n, d = h.shape
    b, m, mw = km.shape
    tiles_per_seq = seq // tm
    return pl.pallas_call(
        _memattn_kernel,
        grid=(n // tm,),
        in_specs=[
            pl.BlockSpec((tm, d), lambda i: (i, 0)),
            pl.BlockSpec((d, mw), lambda i: (0, 0)),
            pl.BlockSpec((1, MEM_HEAD_DIM), lambda i: (0, 0)),
            pl.BlockSpec((1, m, mw), lambda i: (i // tiles_per_seq, 0, 0)),
            pl.BlockSpec((1, m, mw), lambda i: (i // tiles_per_seq, 0, 0)),
        ],
        out_specs=pl.BlockSpec((tm, mw), lambda i: (i, 0)),
        out_shape=jax.ShapeDtypeStruct((n, mw), BF16),
        compiler_params=_params("parallel"),
        name="mem_attention",
    )(h, w_qm, q_gain.reshape(1, MEM_HEAD_DIM), km, vm)


def _merge_kernel(x_ref, h_ref, oa_ref, os_ref, om_ref, wg_ref, wa_ref, ws_ref, wm_ref, wo_ref, gf_ref, wr_ref,
                  x1_ref, h2_ref, route_ref):
    d = x_ref.shape[1]
    gates = _sigmoid(jnp.dot(h_ref[...], wg_ref[...], preferred_element_type=F32))
    merged = gates[:, :d] * jnp.dot(oa_ref[...], wa_ref[...], preferred_element_type=F32)
    merged = merged + gates[:, d:2 * d] * jnp.dot(os_ref[...], ws_ref[...], preferred_element_type=F32)
    merged = merged + gates[:, 2 * d:] * jnp.dot(om_ref[...], wm_ref[...], preferred_element_type=F32)
    x1 = x_ref[...] + jnp.dot(merged.astype(BF16), wo_ref[...], preferred_element_type=F32)
    x1_ref[...] = x1
    r = lax.rsqrt(jnp.mean(x1 * x1, axis=-1, keepdims=True) + EPS)
    h2 = (x1 * r) * gf_ref[...]
    h2_ref[...] = _rows_to_token_tiles(h2)

    h2_hi = h2.astype(BF16)
    h2_lo = (h2 - h2_hi.astype(F32)).astype(BF16)
    lg = (jnp.dot(h2_hi, wr_ref[0], preferred_element_type=F32) + jnp.dot(h2_hi, wr_ref[1], preferred_element_type=F32)
          + jnp.dot(h2_lo, wr_ref[0], preferred_element_type=F32))
    lanef = lax.broadcasted_iota(jnp.int32, lg.shape, 1).astype(F32)
    big = float(LANES)
    gmask = lanef < MOE_GROUPS
    gl = jnp.where(gmask, lg, -jnp.inf)
    gmax = jnp.max(gl, axis=-1, keepdims=True)
    p_g = 1.0 / jnp.sum(jnp.exp(gl - gmax), axis=-1, keepdims=True)
    g_sel = jnp.min(jnp.where(gl == gmax, lanef, big), axis=-1, keepdims=True)
    lo = MOE_GROUPS + MOE_EXPERTS_PER_GROUP * g_sel
    el = jnp.where((lanef >= lo) & (lanef < lo + MOE_EXPERTS_PER_GROUP), lg, -jnp.inf)
    m1 = jnp.max(el, axis=-1, keepdims=True)
    i1 = jnp.min(jnp.where(el == m1, lanef, big), axis=-1, keepdims=True)
    el2 = jnp.where(lanef == i1, -jnp.inf, el)
    m2 = jnp.max(el2, axis=-1, keepdims=True)
    i2 = jnp.min(jnp.where(el2 == m2, lanef, big), axis=-1, keepdims=True)
    e2 = jnp.exp(m2 - m1)
    w1 = 1.0 / (1.0 + e2)
    w2 = e2 / (1.0 + e2)
    route = jnp.where(lanef == 0, i1 - MOE_GROUPS, 0.0)
    route = jnp.where(lanef == 1, i2 - MOE_GROUPS, route)
    route = jnp.where(lanef == 2, p_g * w1, route)
    route = jnp.where(lanef == 3, p_g * w2, route)
    route_ref[...] = route


def _merge(x2, h, o_a, o_s, o_m, w_gates, w_a, w_s, w_m, w_out, g_ffn, w_router, tm=512):
    n, d = x2.shape
    full = lambda arr: pl.BlockSpec(arr.shape, lambda i: (0,) * arr.ndim)
    tile = lambda arr: pl.BlockSpec((tm, arr.shape[1]), lambda i: (i, 0))
    gf = g_ffn.reshape(1, d)
    return pl.pallas_call(
        _merge_kernel,
        grid=(n // tm,),
        in_specs=[tile(x2), tile(h), tile(o_a), tile(o_s), tile(o_m), full(w_gates), full(w_a), full(w_s), full(w_m),
                  full(w_out), full(gf), full(w_router)],
        out_specs=[pl.BlockSpec((tm, d), lambda i: (i, 0)),
                   pl.BlockSpec((tm, d // LANES, LANES), lambda i: (i, 0, 0)),
                   pl.BlockSpec((tm, LANES), lambda i: (i, 0))],
        out_shape=[jax.ShapeDtypeStruct((n, d), F32), jax.ShapeDtypeStruct((n, d // LANES, LANES), F32),
                   jax.ShapeDtypeStruct((n, LANES), F32)],
        compiler_params=_params("parallel"),
        name="merge_router",
    )(x2, h, o_a, o_s, o_m, w_gates, w_a, w_s, w_m, w_out, gf, w_router)


def _moe_kernel(be_ref, j0_ref, nv_ref, tok_ref, dst_ref, h2_hbm, wg_ref, wu_ref, wd_ref, y2_hbm, xbuf, ybuf, gsem,
                ssem):
    rows = xbuf.shape[1]
    n_tok = h2_hbm.shape[0]
    nblocks = pl.num_programs(0)
    i = pl.program_id(0)
    nbuf = xbuf.shape[0]
    slot = lax.rem(i, nbuf)
    slot_next = lax.rem(i + 1, nbuf)
    slot_prev = lax.rem(i + 2, nbuf)
    dump0 = MOE_TOPK * n_tok

    def block_rows(blk):
        b = jnp.clip(blk, 0, nblocks - 1)
        return j0_ref[b], jnp.where(blk < 0, -1, nv_ref[b] - 1)

    n_assign = tok_ref.shape[0]

    def gather_row(span, sl, r, prio):
        j0, _ = span
        tok = tok_ref[jnp.minimum(j0 + r, n_assign - 1)]
        pltpu.make_async_copy(h2_hbm.at[pl.ds(tok, 1)], xbuf.at[sl, pl.ds(r, 1)], gsem.at[sl]).start(priority=prio)

    def scatter_row(span, sl, r, prio):
        j0, last = span
        dst = jnp.where(r <= last, dst_ref[jnp.minimum(j0 + r, n_assign - 1)], dump0 + sl * rows + r)
        pltpu.make_async_copy(ybuf.at[sl, pl.ds(r, 1)], y2_hbm.at[pl.ds(dst, 1)], ssem.at[sl]).start(priority=prio)

    def looped(fn, blk, sl):
        span = block_rows(blk)

        def body(c, carry):
            for k in range(MOE_DMA_UNROLL):
                fn(span, sl, c * MOE_DMA_UNROLL + k, k % 2)
            return carry
        lax.fori_loop(0, rows // MOE_DMA_UNROLL, body, 0)

    def gather_wait(sl):
        pltpu.make_async_copy(h2_hbm.at[pl.ds(0, rows)], xbuf.at[sl], gsem.at[sl]).wait()

    def scatter_wait(sl):
        pltpu.make_async_copy(ybuf.at[sl], y2_hbm.at[pl.ds(0, rows)], ssem.at[sl]).wait()

    @pl.when(i == 0)
    def _():
        looped(gather_row, 0, 0)
        looped(gather_row, 1, 1)
        ybuf[...] = jnp.zeros_like(ybuf)
        for sl in range(2):
            pltpu.make_async_copy(ybuf.at[sl], y2_hbm.at[pl.ds(dump0 + sl * rows, rows)], ssem.at[sl]).start()

    gather_wait(slot)
    scatter_wait(slot)
    nxt, prv = block_rows(i + 2), block_rows(i - 1)
    for r in range(rows):
        scatter_row(prv, slot_prev, r, r % 2)
    x = _token_tiles_to_rows(xbuf[slot]).astype(BF16)
    for r in range(rows):
        gather_row(nxt, slot_prev, r, r % 2)
    gate = jnp.dot(x, wg_ref[0], preferred_element_type=F32)
    up = jnp.dot(x, wu_ref[0], preferred_element_type=F32)
    hid = (gate * _sigmoid(gate)) * up
    y = jnp.dot(hid.astype(BF16), wd_ref[0], preferred_element_type=F32)
    ybuf[slot] = _rows_to_token_tiles(y)

    @pl.when(i == nblocks - 1)
    def _():
        gather_wait(slot_next)
        gather_wait(slot_prev)
        scatter_wait(slot_next)
        scatter_wait(slot_prev)
        looped(scatter_row, i, slot)
        scatter_wait(slot)


def _moe_experts(h2, tables, w_gate, w_up, w_down):
    n = h2.shape[0]
    e, d, f = w_gate.shape
    nblocks = tables[0].shape[0]
    wmap = lambda i, be, j0, nv, tok, dst: (be[i], 0, 0)
    grid_spec = pltpu.PrefetchScalarGridSpec(
        num_scalar_prefetch=5,
        grid=(nblocks,),
        in_specs=[
            pl.BlockSpec(memory_space=pl.ANY),
            pl.BlockSpec((1, d, f), wmap),
            pl.BlockSpec((1, d, f), wmap),
            pl.BlockSpec((1, f, d), wmap),
        ],
        out_specs=pl.BlockSpec(memory_space=pl.ANY),
        scratch_shapes=[
            pltpu.VMEM((MOE_BUFFERS, MOE_ROWS, d // LANES, LANES), F32),
            pltpu.VMEM((MOE_BUFFERS, MOE_ROWS, d // LANES, LANES), F32),
            pltpu.SemaphoreType.DMA((MOE_BUFFERS,)),
            pltpu.SemaphoreType.DMA((MOE_BUFFERS,)),
        ],
    )
    return pl.pallas_call(
        _moe_kernel,
        grid_spec=grid_spec,
        out_shape=jax.ShapeDtypeStruct((n * MOE_TOPK + MOE_BUFFERS * MOE_ROWS, d // LANES, LANES), F32),
        compiler_params=_params("arbitrary"),
        name="moe_experts",
    )(*tables, h2, w_gate, w_up, w_down)


def _combine_kernel(x1_ref, y0_ref, y1_ref, route_ref, o_ref):
    route = route_ref[...]
    w0 = route[:, 2:3]
    w1 = route[:, 3:4]
    o_ref[...] = x1_ref[...] + (w0 * _token_tiles_to_rows(y0_ref[...]) + w1 * _token_tiles_to_rows(y1_ref[...]))


def _combine(x1, y2, route, tm=512):
    n, d = x1.shape
    tiles = n // tm
    ytile = (tm, d // LANES, LANES)
    return pl.pallas_call(
        _combine_kernel,
        grid=(tiles,),
        in_specs=[pl.BlockSpec((tm, d), lambda i: (i, 0)), pl.BlockSpec(ytile, lambda i: (i, 0, 0)),
                  pl.BlockSpec(ytile, lambda i: (tiles + i, 0, 0)), pl.BlockSpec((tm, LANES), lambda i: (i, 0))],
        out_specs=pl.BlockSpec((tm, d), lambda i: (i, 0)),
        out_shape=jax.ShapeDtypeStruct((n, d), F32),
        compiler_params=_params("parallel"),
        name="moe_combine",
    )(x1, y2, y2, route)


def _dispatch_tables(route, n_tok):
    n_assign = n_tok * MOE_TOPK
    e_flat = route[:, :MOE_TOPK].astype(jnp.int32).reshape(n_assign)
    order = jnp.argsort(e_flat).astype(jnp.int32)
    counts = jnp.sum(e_flat[:, None] == jnp.arange(MOE_EXPERTS, dtype=jnp.int32)[None, :], axis=0, dtype=jnp.int32)
    blocks_per_expert = (counts + MOE_ROWS - 1) // MOE_ROWS
    blk_end = jnp.cumsum(blocks_per_expert)
    raw_start = jnp.cumsum(counts) - counts
    nblocks = n_assign // MOE_ROWS + MOE_EXPERTS
    blk = jnp.arange(nblocks, dtype=jnp.int32)
    blk_expert = jnp.minimum(jnp.sum(blk_end[None, :] <= blk[:, None], axis=1), MOE_EXPERTS - 1).astype(jnp.int32)
    within = blk - (blk_end - blocks_per_expert)[blk_expert]
    blk_nvalid = jnp.clip(counts[blk_expert] - within * MOE_ROWS, 0, MOE_ROWS).astype(jnp.int32)
    blk_j0 = jnp.where(blk_nvalid > 0, raw_start[blk_expert] + within * MOE_ROWS, 0).astype(jnp.int32)
    tok_sorted = order // MOE_TOPK
    dst_sorted = (order % MOE_TOPK) * n_tok + tok_sorted
    return blk_expert, blk_j0, blk_nvalid, tok_sorted, dst_sorted


def kernel(x, mem, g_mix, w_in, moba_q_norm, moba_k_norm, conv_w, conv_b, dt_bias, a_log, d_skip, ssd_norm, g_mem,
           w_mem_kv, mem_q_norm, mem_k_norm, w_o_moba, w_o_ssd, w_o_mem, w_out, g_ffn, w_router_group,
           w_router_expert, w_gate, w_up, w_down):
    bsz, seq, d = x.shape
    assert seq % MOBA_BLOCK == 0 and seq % SSD_CHUNK == 0
    n_tok = bsz * seq
    moba_w = MOBA_HEADS * MOBA_HEAD_DIM
    d_in = SSD_HEADS * SSD_HEAD_DIM
    xbc_w = d_in + 2 * SSD_GROUPS * SSD_STATE
    mem_w = MEM_HEADS * MEM_HEAD_DIM
    sizes = (moba_w, moba_w, moba_w, d_in, xbc_w, SSD_HEADS, mem_w, 3 * d)
    offs = [0]
    for sz in sizes:
        offs.append(offs[-1] + sz)
    w_in_b = w_in.astype(BF16)
    w_qk = w_in_b[:, offs[0]:offs[2]]
    w_v = w_in_b[:, offs[2]:offs[3]]
    w_z = w_in_b[:, offs[3]:offs[4]]
    w_xbc = w_in_b[:, offs[4]:offs[5]]
    w_dt = jnp.pad(w_in_b[:, offs[5]:offs[6]], ((0, 0), (0, LANES - SSD_HEADS)))
    w_qm = w_in_b[:, offs[6]:offs[7]]
    w_gates = w_in_b[:, offs[7]:offs[8]]

    x2 = x.reshape(n_tok, d)

    half = MOBA_HEAD_DIM // 2
    inv = ROPE_THETA ** (-jnp.arange(half, dtype=F32) / half)
    ang = jnp.arange(seq, dtype=F32)[:, None] * inv[None, :]
    cos_t = jnp.tile(jnp.cos(ang), (1, LANES // half))
    sin_t = jnp.tile(jnp.concatenate([-jnp.sin(ang), jnp.sin(ang)], axis=1), (1, LANES // MOBA_HEAD_DIM))
    gains = jnp.stack([jnp.tile(moba_q_norm, MOBA_HEADS), jnp.tile(moba_k_norm, MOBA_HEADS)]).reshape(2, 1, moba_w)

    h, qk, v, zs, xbc, dt = _inproj(x2, g_mix, w_qk, w_v, w_z, w_xbc, w_dt, gains, cos_t, sin_t, conv_w, conv_b,
                                    dt_bias, seq)

    qk3 = qk.reshape(bsz, seq, 2 * moba_w)
    v3 = v.reshape(bsz, seq, moba_w)
    nb = seq // MOBA_BLOCK
    assert nb <= MOBA_MAX_BLOCKS and nb % MOBA_UNROLL == 0
    ka, vt, qa = _moba_prep(qk3, v3)
    o_a = _moba_attention(qa, ka, vt).reshape(n_tok, moba_w)

    o_s = _ssd(xbc.reshape(bsz, seq, xbc_w), zs.reshape(bsz, seq, d_in), dt.reshape(bsz, seq, LANES), a_log, d_skip,
               ssd_norm).reshape(n_tok, d_in)

    km, vm = _memkv(mem, g_mem, w_mem_kv.astype(BF16), mem_k_norm)
    o_m = _memattn(h, w_qm, mem_q_norm, km, vm, seq)

    w_router = jnp.pad(jnp.concatenate([w_router_group, w_router_expert], axis=1),
                       ((0, 0), (0, LANES - MOE_GROUPS - MOE_EXPERTS)))
    w_router_hi = w_router.astype(BF16)
    w_router = jnp.stack([w_router_hi, (w_router - w_router_hi.astype(F32)).astype(BF16)])
    x1, h2, route = _merge(x2, h, o_a, o_s, o_m, w_gates, w_o_moba.astype(BF16), w_o_ssd.astype(BF16),
                           w_o_mem.astype(BF16), w_out.astype(BF16), g_ffn, w_router)

    tables = _dispatch_tables(route, n_tok)
    y2 = _moe_experts(h2, tables, w_gate.astype(BF16), w_up.astype(BF16), w_down.astype(BF16))
    out = _combine(x1, y2, route)
    return out.reshape(bsz, seq, d)
```

```python
import functools

import jax
import jax.numpy as jnp
from jax import lax
from jax.experimental import pallas as pl
from jax.experimental.pallas import tpu as pltpu

F32 = jnp.float32
BF16 = jnp.bfloat16
HIGHEST = lax.Precision.HIGHEST

EPS = 1e-6
ROPE_THETA = 10000.0
MOBA_HEADS = 8
MOBA_HEAD_DIM = 64
MOBA_BLOCK = 256
MOBA_TOPK = 3
SSD_HEAD_DIM = 64
SSD_HEADS = 16
SSD_GROUPS = 4
SSD_STATE = 128
SSD_CONV = 4
SSD_CHUNK = 256
MEM_HEADS = 4
MEM_HEAD_DIM = 128
MOE_GROUPS = 4
MOE_EXPERTS_PER_GROUP = 8
MOE_EXPERTS = MOE_GROUPS * MOE_EXPERTS_PER_GROUP
MOE_TOPK = 2

LANES = 128
MASKED = -1e30
MOBA_MAX_BLOCKS = 32
MOBA_UNROLL = 4
MOBA_TILES_PER_STEP = 2
MOBA_VT_ROWS = MOBA_HEAD_DIM + 16
MOE_ROWS = 256
INPROJ_CONV_CHUNK = 512
MOE_BUFFERS = 3
MOE_DMA_UNROLL = 8
VMEM_LIMIT = 56 * 1024 * 1024


def _params(*sem):
    return pltpu.CompilerParams(dimension_semantics=sem, vmem_limit_bytes=VMEM_LIMIT)


def _sigmoid(x):
    return 1.0 / (1.0 + jnp.exp(-x))


def _split3_dot(x, sel):
    hi = x.astype(BF16)
    rest = x - hi.astype(F32)
    mid = rest.astype(BF16)
    lo = (rest - mid.astype(F32)).astype(BF16)
    return (jnp.dot(hi, sel, preferred_element_type=F32) + jnp.dot(mid, sel, preferred_element_type=F32)
            + jnp.dot(lo, sel, preferred_element_type=F32))


def _rows_to_token_tiles(x):
    rows, width = x.shape
    chunks = width // LANES
    x4 = jnp.stack([x[:, k * LANES:(k + 1) * LANES].reshape(rows // 8, 8, LANES) for k in range(chunks)], axis=1)
    return jnp.swapaxes(x4, 1, 2).reshape(rows, chunks, LANES)


def _token_tiles_to_rows(x3):
    rows, chunks, _ = x3.shape
    x4 = jnp.swapaxes(x3.reshape(rows // 8, 8, chunks, LANES), 1, 2)
    return jnp.concatenate([x4[:, k].reshape(rows, LANES) for k in range(chunks)], axis=1)


def _nt_dot(a, b, precision=None):
    return lax.dot_general(a, b, (((1,), (1,)), ((), ())), precision=precision, preferred_element_type=F32)


def _softplus(x):
    return jnp.maximum(x, 0.0) + jnp.log1p(jnp.exp(-jnp.abs(x)))


def _inproj_kernel(x_ref, gmix_ref, wqk_ref, wv_ref, wz_ref, wxbc_ref, wdt_ref, gain_ref, gsum_ref, cos_ref, sin_ref,
                   cw_ref, cb_ref, dtb_ref, h_ref, qk_ref, v_ref, zs_ref, xbc_ref, dt_ref, *ext_refs,
                   tiles_per_seq):
    tm = x_ref.shape[0]
    x = x_ref[...]
    r = lax.rsqrt(jnp.mean(x * x, axis=-1, keepdims=True) + EPS)
    h = ((x * r) * gmix_ref[...]).astype(BF16)
    h_ref[...] = h

    width = gsum_ref.shape[0]
    half = MOBA_HEAD_DIM // 2
    lane = lax.broadcasted_iota(jnp.int32, (tm, width), 1)
    first = (lane & (MOBA_HEAD_DIM - 1)) < half
    reps = width // cos_ref.shape[1]
    cos = jnp.concatenate([cos_ref[...]] * reps, axis=1)
    sin = jnp.concatenate([sin_ref[...]] * reps, axis=1)
    for j in range(2):
        u = jnp.dot(h, wqk_ref[:, j * width:(j + 1) * width], preferred_element_type=F32)
        sq = u * u
        hi = sq.astype(BF16)
        lo = (sq - hi.astype(F32)).astype(BF16)
        ss = (jnp.dot(hi, gsum_ref[...], preferred_element_type=F32)
              + jnp.dot(lo, gsum_ref[...], preferred_element_type=F32))
        un = (u * lax.rsqrt(ss * (1.0 / MOBA_HEAD_DIM) + EPS)) * gain_ref[j]
        partner = jnp.where(first, pltpu.roll(un, width - half, axis=1), pltpu.roll(un, half, axis=1))
        qk_ref[:, j * width:(j + 1) * width] = un * cos + partner * sin

    v_ref[...] = jnp.dot(h, wv_ref[...], preferred_element_type=F32).astype(v_ref.dtype)
    z = jnp.dot(h, wz_ref[...], preferred_element_type=F32)
    zs_ref[...] = (z * _sigmoid(z)).astype(zs_ref.dtype)
    dt_ref[...] = _softplus(jnp.dot(h, wdt_ref[...], preferred_element_type=F32) + dtb_ref[...])

    @pl.when(pl.program_id(0) % tiles_per_seq == 0)
    def _():
        for ext_ref in ext_refs:
            ext_ref[0:8, :] = jnp.zeros((8, ext_ref.shape[1]), F32)

    cchunk = ext_refs[0].shape[1]
    for ci, ext_ref in enumerate(ext_refs):
        cs = slice(ci * cchunk, (ci + 1) * cchunk)
        u = jnp.dot(h, wxbc_ref[:, cs], preferred_element_type=F32)
        ext_ref[8:8 + tm, :] = u
        acc = cb_ref[:, cs] + cw_ref[SSD_CONV - 1:SSD_CONV, cs] * u
        for kk in range(SSD_CONV - 1):
            off = 8 - (SSD_CONV - 1 - kk)
            acc = acc + cw_ref[kk:kk + 1, cs] * ext_ref[off:off + tm, :]
        ext_ref[0:8, :] = u[tm - 8:tm, :]
        xbc_ref[:, cs] = (acc * _sigmoid(acc)).astype(xbc_ref.dtype)


def _inproj(x2, g_mix, w_qk, w_v, w_z, w_xbc, w_dt, gains, cos_t, sin_t, conv_w, conv_b, dt_bias, seq, tm=512):
    n, d = x2.shape
    width = w_qk.shape[1] // 2
    tiles_per_seq = seq // tm
    head = jnp.arange(width) // MOBA_HEAD_DIM
    gsum = (head[:, None] == head[None, :]).astype(BF16)
    dtb = jnp.pad(dt_bias.astype(F32), (0, LANES - dt_bias.shape[0])).reshape(1, LANES)
    full = lambda arr: pl.BlockSpec(arr.shape, lambda i: (0,) * arr.ndim)
    tile = lambda cols: pl.BlockSpec((tm, cols), lambda i: (i, 0))
    rope = pl.BlockSpec((tm, LANES), lambda i: (i % tiles_per_seq, 0))
    consts = (g_mix.reshape(1, d), w_qk, w_v, w_z, w_xbc, w_dt, gains, gsum)
    tail = (conv_w, conv_b.reshape(1, -1), dtb)
    outs = ((d, BF16), (2 * width, F32), (w_v.shape[1], BF16), (w_z.shape[1], BF16), (w_xbc.shape[1], BF16),
            (LANES, F32))
    return pl.pallas_call(
        functools.partial(_inproj_kernel, tiles_per_seq=tiles_per_seq),
        grid=(n // tm,),
        in_specs=[tile(d)] + [full(a) for a in consts] + [rope, rope] + [full(a) for a in tail],
        out_specs=[tile(c) for c, _ in outs],
        out_shape=[jax.ShapeDtypeStruct((n, c), dt) for c, dt in outs],
        scratch_shapes=[pltpu.VMEM((tm + 8, INPROJ_CONV_CHUNK), F32)] * (w_xbc.shape[1] // INPROJ_CONV_CHUNK),
        compiler_params=_params("arbitrary"),
        name="in_proj",
    )(x2, *consts, cos_t, sin_t, *tail)


def _moba_prep_kernel(q_ref, k_ref, v_ref, ka_ref, vt_ref, qa_ref, km_ref):
    n = pl.program_id(1)

    @pl.when(n == 0)
    def _():
        km_ref[...] = jnp.zeros_like(km_ref)

    _moba_select(q_ref, km_ref, qa_ref, n)

    ones = jnp.ones((MOBA_VT_ROWS - MOBA_HEAD_DIM, MOBA_BLOCK), BF16)
    k = k_ref[0]
    km_ref[pl.ds(n, 1), :] = jnp.mean(k, axis=0, keepdims=True)
    lane = lax.broadcasted_iota(jnp.int32, (MOBA_BLOCK, LANES), 1)
    own = lane < MOBA_HEAD_DIM
    onehot = jnp.where(lane == MOBA_HEAD_DIM + n, 1.0, 0.0)
    for pr in range(MOBA_HEADS // 2):
        pair = slice(pr * LANES, (pr + 1) * LANES)
        kp = k[:, pair]
        ka_ref[0, 2 * pr, 0] = jnp.where(own, kp, onehot).astype(BF16)
        ka_ref[0, 2 * pr + 1, 0] = jnp.where(own, pltpu.roll(kp, MOBA_HEAD_DIM, axis=1), onehot).astype(BF16)
        vp_t = v_ref[0, :, pair].astype(F32).T
        for hh in range(2):
            vt_ref[0, 2 * pr + hh, 0, :MOBA_HEAD_DIM, :] = vp_t[hh * MOBA_HEAD_DIM:(hh + 1) * MOBA_HEAD_DIM].astype(BF16)
            vt_ref[0, 2 * pr + hh, 0, MOBA_HEAD_DIM:, :] = ones


def _moba_prep(qk3, v3):
    b, s, w = v3.shape
    nb = s // MOBA_BLOCK
    return pl.pallas_call(
        _moba_prep_kernel,
        grid=(b, nb),
        in_specs=[pl.BlockSpec((1, MOBA_BLOCK, w), lambda bi, n: (bi, n, 0)),
                  pl.BlockSpec((1, MOBA_BLOCK, w), lambda bi, n: (bi, n, 1)),
                  pl.BlockSpec((1, MOBA_BLOCK, w), lambda bi, n: (bi, n, 0))],
        out_specs=[
            pl.BlockSpec((1, MOBA_HEADS, 1, MOBA_BLOCK, LANES), lambda bi, n: (bi, 0, n, 0, 0)),
            pl.BlockSpec((1, MOBA_HEADS, 1, MOBA_VT_ROWS, MOBA_BLOCK), lambda bi, n: (bi, 0, n, 0, 0)),
            pl.BlockSpec((1, MOBA_HEADS, 1, LANES, MOBA_BLOCK), lambda bi, n: (bi, 0, n, 0, 0)),
        ],
        out_shape=[jax.ShapeDtypeStruct((b, MOBA_HEADS, nb, MOBA_BLOCK, LANES), BF16),
                   jax.ShapeDtypeStruct((b, MOBA_HEADS, nb, MOBA_VT_ROWS, MOBA_BLOCK), BF16),
                   jax.ShapeDtypeStruct((b, MOBA_HEADS, nb, LANES, MOBA_BLOCK), BF16)],
        scratch_shapes=[pltpu.VMEM((MOBA_MAX_BLOCKS, w), F32)],
        compiler_params=_params("parallel", "arbitrary"),
        name="moba_prep",
    )(qk3, qk3, v3)


def _moba_select(q_ref, km_ref, qa_ref, i):
    blk = MOBA_BLOCK
    nsel = MOBA_MAX_BLOCKS
    qscale = (MOBA_HEAD_DIM ** -0.5) * 1.4426950408889634
    q_t = (q_ref[0] * qscale).T
    kmean = km_ref[...]
    lane = lax.broadcasted_iota(jnp.int32, (nsel, LANES), 1)
    blk_row = lax.broadcasted_iota(jnp.int32, (nsel, blk), 0)
    blk_rowf = blk_row.astype(F32)
    pad = jnp.zeros((LANES - MOBA_HEAD_DIM - nsel, blk), F32)
    for hd in range(MOBA_HEADS):
        pr, hh = hd // 2, hd % 2
        hmask = (lane >= hh * MOBA_HEAD_DIM) & (lane < (hh + 1) * MOBA_HEAD_DIM)
        km_h = jnp.where(hmask, kmean[:, pr * LANES:(pr + 1) * LANES], 0.0)
        gate = jnp.dot(km_h, q_t[pr * LANES:(pr + 1) * LANES], precision=HIGHEST, preferred_element_type=F32)
        g = jnp.where(blk_row < i, gate, -jnp.inf)
        bias = jnp.full((nsel, blk), MASKED, F32)
        for _ in range(MOBA_TOPK):
            m = jnp.max(g, axis=0, keepdims=True)
            hit = (g == m) & (m > -jnp.inf)
            first = jnp.min(jnp.where(hit, blk_rowf, float(nsel)), axis=0, keepdims=True)
            sel = blk_rowf == first
            bias = jnp.where(sel, 0.0, bias)
            g = jnp.where(sel, -jnp.inf, g)
        qs = q_t[hd * MOBA_HEAD_DIM:(hd + 1) * MOBA_HEAD_DIM]
        qa_ref[0, hd, 0] = jnp.concatenate([qs, bias, pad], axis=0).astype(BF16)


def _moba_kernel(qa_ref, qan_ref, ka_ref, vt_ref, o_ref, sa_ref, sb_ref, sc_ref, gma_ref, gmb_ref, gmc_ref, m_ref,
                 acc_ref):
    blk = MOBA_BLOCK
    nblk = ka_ref.shape[2]
    key_pos = lax.broadcasted_iota(jnp.int32, (blk, blk), 0)
    qry_pos = lax.broadcasted_iota(jnp.int32, (blk, blk), 1)
    causal = key_pos <= qry_pos
    feat_row = lax.broadcasted_iota(jnp.int32, (LANES, blk), 0) < MOBA_HEAD_DIM

    def fold(s, op):
        return op(s.reshape(blk // 8, 8, blk), axis=0)

    def score_tiles(tiles, s_ref, gm_ref):
        for hh in range(2):
            gmax = None
            for u, (n, q_of, is_own) in enumerate(tiles):
                s = jnp.dot(ka_ref[0, hh, n], q_of(hh), preferred_element_type=F32)
                if is_own:
                    s = jnp.where(causal, s, MASKED)
                s_ref[hh, u] = s
                gmax = fold(s, jnp.max) if gmax is None else jnp.maximum(gmax, fold(s, jnp.max))
            gm_ref[hh] = gmax

    def score_first(q_past, own, s_ref, gm_ref):
        q_own = lambda hh: jnp.where(feat_row, q_past(hh), jnp.zeros((), BF16))
        tiles = [(own, q_own, True)] + [(u - 1, q_past, False) for u in range(1, MOBA_UNROLL)]
        score_tiles(tiles, s_ref, gm_ref)

    def score_group(q_past, grp, s_ref, gm_ref):
        tiles = [(jnp.minimum(grp * MOBA_UNROLL + (u - 1), nblk - 1), q_past, False) for u in range(MOBA_UNROLL)]
        score_tiles(tiles, s_ref, gm_ref)

    def value_group(i, grp, s_ref, gm_ref):
        for hh in range(2):
            m_old = m_ref[hh][0:1]
            m_new = jnp.maximum(m_old, jnp.max(gm_ref[hh], axis=0, keepdims=True))
            alpha = jnp.exp2(m_old - m_new)
            acc = acc_ref[hh] * alpha
            for u in range(MOBA_UNROLL):
                n = jnp.minimum(grp * MOBA_UNROLL + (u - 1), nblk - 1)
                if u == 0:
                    n = jnp.where(grp == 0, i, n)
                p = jnp.exp2(s_ref[hh, u] - m_new).astype(BF16)
                acc = acc + jnp.dot(vt_ref[0, hh, n], p, preferred_element_type=F32)
            m_ref[hh] = jnp.broadcast_to(m_new, (8, blk))
            acc_ref[hh] = acc

    first_tile = pl.program_id(2) * MOBA_TILES_PER_STEP

    @pl.when(first_tile == 0)
    def _():
        score_first(lambda hh: qa_ref[0, hh, 0], 0, sc_ref, gmc_ref)

    for sub in range(MOBA_TILES_PER_STEP):
        i = first_tile + sub
        ngroups = (i + MOBA_UNROLL) // MOBA_UNROLL
        q_cur = lambda hh, sub=sub: qa_ref[0, hh, sub]
        if sub + 1 < MOBA_TILES_PER_STEP:
            q_nxt = lambda hh, sub=sub: qa_ref[0, hh, sub + 1]
        else:
            q_nxt = lambda hh: qan_ref[0, hh, 0]
        for hh in range(2):
            m_ref[hh] = jnp.full((8, blk), MASKED, F32)
            acc_ref[hh] = jnp.zeros((MOBA_VT_ROWS, blk), F32)

        score_group(q_cur, 1, sa_ref, gma_ref)
        value_group(i, 0, sc_ref, gmc_ref)

        rest = ngroups - 1
        npairs = jnp.maximum(rest - 1, 0) // 2

        def body(t, carry, i=i, q_cur=q_cur):
            score_group(q_cur, 2 * t + 2, sb_ref, gmb_ref)
            value_group(i, 2 * t + 1, sa_ref, gma_ref)
            score_group(q_cur, 2 * t + 3, sa_ref, gma_ref)
            value_group(i, 2 * t + 2, sb_ref, gmb_ref)
            return carry

        lax.fori_loop(0, npairs, body, 0)
        left = rest - 2 * npairs
        own_next = jnp.minimum(i + 1, nblk - 1)

        @pl.when(left == 2)
        def _(i=i, q_cur=q_cur, q_nxt=q_nxt, npairs=npairs, own_next=own_next):
            score_group(q_cur, 2 * npairs + 2, sb_ref, gmb_ref)
            value_group(i, 2 * npairs + 1, sa_ref, gma_ref)
            score_first(q_nxt, own_next, sc_ref, gmc_ref)
            value_group(i, 2 * npairs + 2, sb_ref, gmb_ref)

        @pl.when(left == 1)
        def _(i=i, q_nxt=q_nxt, npairs=npairs, own_next=own_next):
            score_first(q_nxt, own_next, sc_ref, gmc_ref)
            value_group(i, 2 * npairs + 1, sa_ref, gma_ref)

        @pl.when(left == 0)
        def _(q_nxt=q_nxt, own_next=own_next):
            score_first(q_nxt, own_next, sc_ref, gmc_ref)

        outs = [acc_ref[hh, :MOBA_HEAD_DIM, :] / acc_ref[hh, MOBA_HEAD_DIM:MOBA_HEAD_DIM + 1, :] for hh in range(2)]
        o_ref[0, sub * blk:(sub + 1) * blk, :] = jnp.concatenate(outs, axis=0).T.astype(o_ref.dtype)


def _moba_attention(qa, ka, vt):
    b, heads, nq = qa.shape[:3]
    s = nq * MOBA_BLOCK
    pairs = heads // 2
    tps = MOBA_TILES_PER_STEP
    assert nq % tps == 0
    return pl.pallas_call(
        _moba_kernel,
        grid=(b, pairs, nq // tps),
        in_specs=[
            pl.BlockSpec((1, 2, tps, LANES, MOBA_BLOCK), lambda bi, p, j: (bi, p, j, 0, 0)),
            pl.BlockSpec((1, 2, 1, LANES, MOBA_BLOCK),
                         lambda bi, p, j: (bi, p, jnp.minimum((j + 1) * tps, nq - 1), 0, 0)),
            pl.BlockSpec((1, 2, nq, MOBA_BLOCK, LANES), lambda bi, p, j: (bi, p, 0, 0, 0)),
            pl.BlockSpec((1, 2, nq, MOBA_VT_ROWS, MOBA_BLOCK), lambda bi, p, j: (bi, p, 0, 0, 0)),
        ],
        out_specs=pl.BlockSpec((1, tps * MOBA_BLOCK, LANES), lambda bi, p, j: (bi, j, p)),
        out_shape=jax.ShapeDtypeStruct((b, s, heads * MOBA_HEAD_DIM), BF16),
        scratch_shapes=[
            pltpu.VMEM((2, MOBA_UNROLL, MOBA_BLOCK, MOBA_BLOCK), F32),
            pltpu.VMEM((2, MOBA_UNROLL, MOBA_BLOCK, MOBA_BLOCK), F32),
            pltpu.VMEM((2, MOBA_UNROLL, MOBA_BLOCK, MOBA_BLOCK), F32),
            pltpu.VMEM((2, 8, MOBA_BLOCK), F32),
            pltpu.VMEM((2, 8, MOBA_BLOCK), F32),
            pltpu.VMEM((2, 8, MOBA_BLOCK), F32),
            pltpu.VMEM((2, 8, MOBA_BLOCK), F32),
            pltpu.VMEM((2, MOBA_VT_ROWS, MOBA_BLOCK), F32),
        ],
        compiler_params=_params("parallel", "parallel", "arbitrary"),
        name="moba_attention",
    )(qa, qa, ka, vt)


def _ssd_kernel(xbc_ref, zs_ref, dt_ref, alog_ref, dskip_ref, gn_ref, ex_ref, o_ref, state_ref):
    q = SSD_CHUNK
    d_in = SSD_HEADS * SSD_HEAD_DIM
    bc_w = SSD_GROUPS * SSD_STATE
    gw = d_in // SSD_GROUPS
    c = pl.program_id(1)

    @pl.when(c == 0)
    def _():
        state_ref[...] = jnp.zeros_like(state_ref)

    xs = xbc_ref[0, :, :d_in].astype(F32)
    bm = xbc_ref[0, :, d_in:d_in + bc_w]
    cm = xbc_ref[0, :, d_in + bc_w:]
    dt = dt_ref[0]
    a = dt * (-jnp.exp(alog_ref[...]))
    row = lax.broadcasted_iota(jnp.int32, (q, q), 0)
    col = lax.broadcasted_iota(jnp.int32, (q, q), 1)
    tril = col <= row
    a_cum = jnp.dot(tril.astype(F32), a, precision=HIGHEST, preferred_element_type=F32)
    a_cum_t = a_cum.T
    ex = ex_ref[...]
    dt_x = _split3_dot(dt, ex)
    acum_x = _split3_dot(a_cum, ex)
    alast_x = acum_x[q - 1:q, :]
    x_dt = xs * dt_x
    xd = (x_dt * jnp.exp(alast_x - acum_x)).astype(BF16)
    x_dt_b = x_dt.astype(BF16)
    e_acum = jnp.exp(acum_x)
    e_alast = jnp.exp(alast_x)
    lane2 = lax.broadcasted_iota(jnp.int32, (q, 2 * SSD_HEAD_DIM), 1)
    heads_per_group = SSD_HEADS // SSD_GROUPS

    for g in range(SSD_GROUPS):
        bg = bm[:, g * SSD_STATE:(g + 1) * SSD_STATE]
        cg = cm[:, g * SSD_STATE:(g + 1) * SSD_STATE]
        cb = _nt_dot(cg, bg)
        h_in = state_ref[g]
        y_g = jnp.dot(cg, h_in.astype(BF16), preferred_element_type=F32) * e_acum[:, g * gw:(g + 1) * gw]
        parts = []
        for pr in range(heads_per_group // 2):
            xp = x_dt_b[:, g * gw + pr * 2 * SSD_HEAD_DIM:g * gw + (pr + 1) * 2 * SSD_HEAD_DIM]
            ys = []
            for hh in range(2):
                hd = g * heads_per_group + pr * 2 + hh
                seg = a_cum[:, hd:hd + 1] - a_cum_t[hd:hd + 1, :]
                lmat = jnp.exp(jnp.where(tril, seg, -jnp.inf))
                ys.append(jnp.dot((cb * lmat).astype(BF16), xp, preferred_element_type=F32))
            parts.append(jnp.where(lane2 < SSD_HEAD_DIM, ys[0], ys[1]))
        y_g = y_g + jnp.concatenate(parts, axis=1)
        st = jnp.dot(bg.astype(F32).T.astype(BF16), xd[:, g * gw:(g + 1) * gw], preferred_element_type=F32)
        state_ref[g] = h_in * e_alast[:, g * gw:(g + 1) * gw] + st

        sl = slice(g * gw, (g + 1) * gw)
        y_g = y_g + dskip_ref[:, sl] * xs[:, sl]
        y_g = y_g * zs_ref[0, :, sl].astype(F32)
        r = lax.rsqrt(jnp.mean(y_g * y_g, axis=-1, keepdims=True) + EPS)
        o_ref[0, :, sl] = ((y_g * r) * gn_ref[:, sl]).astype(o_ref.dtype)


def _ssd(xbc3, zs3, dt3, a_log, d_skip, ssd_norm):
    b, s, xw = xbc3.shape
    d_in = SSD_HEADS * SSD_HEAD_DIM
    nc = s // SSD_CHUNK
    pad = LANES - SSD_HEADS
    alog = jnp.pad(a_log.astype(F32), (0, pad)).reshape(1, LANES)
    dskip = jnp.repeat(d_skip.astype(F32), SSD_HEAD_DIM).reshape(1, d_in)
    expand = (jnp.arange(LANES)[:, None] == (jnp.arange(d_in) // SSD_HEAD_DIM)[None, :]).astype(BF16)
    const = lambda shape: pl.BlockSpec(shape, lambda bi, ci: (0,) * len(shape))
    return pl.pallas_call(
        _ssd_kernel,
        grid=(b, nc),
        in_specs=[
            pl.BlockSpec((1, SSD_CHUNK, xw), lambda bi, ci: (bi, ci, 0)),
            pl.BlockSpec((1, SSD_CHUNK, d_in), lambda bi, ci: (bi, ci, 0)),
            pl.BlockSpec((1, SSD_CHUNK, LANES), lambda bi, ci: (bi, ci, 0)),
            const((1, LANES)), const((1, d_in)), const((1, d_in)), const((LANES, d_in)),
        ],
        out_specs=pl.BlockSpec((1, SSD_CHUNK, d_in), lambda bi, ci: (bi, ci, 0)),
        out_shape=jax.ShapeDtypeStruct((b, s, d_in), BF16),
        scratch_shapes=[pltpu.VMEM((SSD_GROUPS, SSD_STATE, d_in // SSD_GROUPS), F32)],
        compiler_params=_params("parallel", "arbitrary"),
        name="ssd_scan",
    )(xbc3, zs3, dt3, alog, dskip, ssd_norm.reshape(1, d_in), expand)


def _memkv_kernel(mem_ref, g_ref, w_ref, kg_ref, km_ref, vm_ref):
    m = mem_ref[0]
    r = lax.rsqrt(jnp.mean(m * m, axis=-1, keepdims=True) + EPS)
    mn = ((m * r) * g_ref[...]).astype(BF16)
    kv = jnp.dot(mn, w_ref[...], preferred_element_type=F32)
    mw = MEM_HEADS * MEM_HEAD_DIM
    for hd in range(MEM_HEADS):
        sl = slice(hd * MEM_HEAD_DIM, (hd + 1) * MEM_HEAD_DIM)
        kh = kv[:, sl]
        rk = lax.rsqrt(jnp.mean(kh * kh, axis=-1, keepdims=True) + EPS)
        km_ref[0, :, sl] = ((kh * rk) * kg_ref[...]).astype(km_ref.dtype)
    vm_ref[0] = kv[:, mw:].astype(vm_ref.dtype)


def _memkv(mem, g_mem, w_kv, k_gain):
    b, m, d = mem.shape
    mw = MEM_HEADS * MEM_HEAD_DIM
    return pl.pallas_call(
        _memkv_kernel,
        grid=(b,),
        in_specs=[
            pl.BlockSpec((1, m, d), lambda bi: (bi, 0, 0)),
            pl.BlockSpec((1, d), lambda bi: (0, 0)),
            pl.BlockSpec((d, 2 * mw), lambda bi: (0, 0)),
            pl.BlockSpec((1, MEM_HEAD_DIM), lambda bi: (0, 0)),
        ],
        out_specs=[pl.BlockSpec((1, m, mw), lambda bi: (bi, 0, 0)), pl.BlockSpec((1, m, mw), lambda bi: (bi, 0, 0))],
        out_shape=[jax.ShapeDtypeStruct((b, m, mw), BF16), jax.ShapeDtypeStruct((b, m, mw), BF16)],
        compiler_params=_params("parallel"),
        name="mem_kv",
    )(mem, g_mem.reshape(1, d), w_kv, k_gain.reshape(1, MEM_HEAD_DIM))


def _memattn_kernel(h_ref, w_ref, qg_ref, km_ref, vm_ref, o_ref):
    qm = jnp.dot(h_ref[...], w_ref[...], preferred_element_type=F32)
    scale = MEM_HEAD_DIM ** -0.5
    for hd in range(MEM_HEADS):
        sl = slice(hd * MEM_HEAD_DIM, (hd + 1) * MEM_HEAD_DIM)
        qh = qm[:, sl]
        r = lax.rsqrt(jnp.mean(qh * qh, axis=-1, keepdims=True) + EPS)
        qn = ((qh * r) * qg_ref[...]).astype(BF16)
        s = _nt_dot(qn, km_ref[0, :, sl]) * scale
        p = jnp.exp(s - jnp.max(s, axis=-1, keepdims=True))
        l = jnp.sum(p, axis=-1, keepdims=True)
        o = jnp.dot(p.astype(BF16), vm_ref[0, :, sl], preferred_element_type=F32)
        o_ref[:, sl] = (o / l).astype(o_ref.dtype)


def _memattn(h, w_qm, q_gain, km, vm, seq, tm=512):
    n, d = h.shape
    b, m, mw = km.shape
    tiles_per_seq = seq // tm
    return pl.pallas_call(
        _memattn_kernel,
        grid=(n // tm,),
        in_specs=[
            pl.BlockSpec((tm, d), lambda i: (i, 0)),
            pl.BlockSpec((d, mw), lambda i: (0, 0)),
            pl.BlockSpec((1, MEM_HEAD_DIM), lambda i: (0, 0)),
            pl.BlockSpec((1, m, mw), lambda i: (i // tiles_per_seq, 0, 0)),
            pl.BlockSpec((1, m, mw), lambda i: (i // tiles_per_seq, 0, 0)),
        ],
        out_specs=pl.BlockSpec((tm, mw), lambda i: (i, 0)),
        out_shape=jax.ShapeDtypeStruct((n, mw), BF16),
        compiler_params=_params("parallel"),
        name="mem_attention",
    )(h, w_qm, q_gain.reshape(1, MEM_HEAD_DIM), km, vm)


def _merge_kernel(x_ref, h_ref, oa_ref, os_ref, om_ref, wg_ref, wa_ref, ws_ref, wm_ref, wo_ref, gf_ref, wr_ref,
                  x1_ref, h2_ref, route_ref, cnt_ref):
    d = x_ref.shape[1]
    gates = _sigmoid(jnp.dot(h_ref[...], wg_ref[...], preferred_element_type=F32))
    merged = gates[:, :d] * jnp.dot(oa_ref[...], wa_ref[...], preferred_element_type=F32)
    merged = merged + gates[:, d:2 * d] * jnp.dot(os_ref[...], ws_ref[...], preferred_element_type=F32)
    merged = merged + gates[:, 2 * d:] * jnp.dot(om_ref[...], wm_ref[...], preferred_element_type=F32)
    x1 = x_ref[...] + jnp.dot(merged.astype(BF16), wo_ref[...], preferred_element_type=F32)
    x1_ref[...] = x1
    r = lax.rsqrt(jnp.mean(x1 * x1, axis=-1, keepdims=True) + EPS)
    h2 = (x1 * r) * gf_ref[...]
    h2_ref[...] = _rows_to_token_tiles(h2)

    h2_hi = h2.astype(BF16)
    h2_lo = (h2 - h2_hi.astype(F32)).astype(BF16)
    lg = (jnp.dot(h2_hi, wr_ref[0], preferred_element_type=F32) + jnp.dot(h2_hi, wr_ref[1], preferred_element_type=F32)
          + jnp.dot(h2_lo, wr_ref[0], preferred_element_type=F32))
    lanef = lax.broadcasted_iota(jnp.int32, lg.shape, 1).astype(F32)
    big = float(LANES)
    gmask = lanef < MOE_GROUPS
    gl = jnp.where(gmask, lg, -jnp.inf)
    gmax = jnp.max(gl, axis=-1, keepdims=True)
    p_g = 1.0 / jnp.sum(jnp.exp(gl - gmax), axis=-1, keepdims=True)
    g_sel = jnp.min(jnp.where(gl == gmax, lanef, big), axis=-1, keepdims=True)
    lo = MOE_GROUPS + MOE_EXPERTS_PER_GROUP * g_sel
    el = jnp.where((lanef >= lo) & (lanef < lo + MOE_EXPERTS_PER_GROUP), lg, -jnp.inf)
    m1 = jnp.max(el, axis=-1, keepdims=True)
    i1 = jnp.min(jnp.where(el == m1, lanef, big), axis=-1, keepdims=True)
    el2 = jnp.where(lanef == i1, -jnp.inf, el)
    m2 = jnp.max(el2, axis=-1, keepdims=True)
    i2 = jnp.min(jnp.where(el2 == m2, lanef, big), axis=-1, keepdims=True)
    e2 = jnp.exp(m2 - m1)
    w1 = 1.0 / (1.0 + e2)
    w2 = e2 / (1.0 + e2)
    route = jnp.where(lanef == 0, i1 - MOE_GROUPS, 0.0)
    route = jnp.where(lanef == 1, i2 - MOE_GROUPS, route)
    route = jnp.where(lanef == 2, p_g * w1, route)
    route = jnp.where(lanef == 3, p_g * w2, route)
    route_ref[...] = route

    @pl.when(pl.program_id(0) == 0)
    def _():
        cnt_ref[...] = jnp.zeros_like(cnt_ref)

    hits = jnp.where(lanef == i1 - MOE_GROUPS, 1.0, 0.0) + jnp.where(lanef == i2 - MOE_GROUPS, 1.0, 0.0)
    cnt_ref[...] += jnp.sum(hits.reshape(hits.shape[0] // 8, 8, LANES), axis=0)


def _merge(x2, h, o_a, o_s, o_m, w_gates, w_a, w_s, w_m, w_out, g_ffn, w_router, tm=512):
    n, d = x2.shape
    full = lambda arr: pl.BlockSpec(arr.shape, lambda i: (0,) * arr.ndim)
    tile = lambda arr: pl.BlockSpec((tm, arr.shape[1]), lambda i: (i, 0))
    gf = g_ffn.reshape(1, d)
    return pl.pallas_call(
        _merge_kernel,
        grid=(n // tm,),
        in_specs=[tile(x2), tile(h), tile(o_a), tile(o_s), tile(o_m), full(w_gates), full(w_a), full(w_s), full(w_m),
                  full(w_out), full(gf), full(w_router)],
        out_specs=[pl.BlockSpec((tm, d), lambda i: (i, 0)),
                   pl.BlockSpec((tm, d // LANES, LANES), lambda i: (i, 0, 0)),
                   pl.BlockSpec((tm, LANES), lambda i: (i, 0)),
                   pl.BlockSpec((8, LANES), lambda i: (0, 0))],
        out_shape=[jax.ShapeDtypeStruct((n, d), F32), jax.ShapeDtypeStruct((n, d // LANES, LANES), F32),
                   jax.ShapeDtypeStruct((n, LANES), F32), jax.ShapeDtypeStruct((8, LANES), F32)],
        compiler_params=_params("arbitrary"),
        name="merge_router",
    )(x2, h, o_a, o_s, o_m, w_gates, w_a, w_s, w_m, w_out, gf, w_router)


def _moe_kernel(be_ref, j0_ref, nv_ref, tok_ref, dst_ref, h2_hbm, wg_ref, wu_ref, wd_ref, y2_hbm, xbuf, ybuf, gsem,
                ssem, wgb, wub, wdb):
    rows = xbuf.shape[1]
    n_tok = h2_hbm.shape[0]
    nblocks = pl.num_programs(0)
    i = pl.program_id(0)
    nbuf = xbuf.shape[0]
    slot = lax.rem(i, nbuf)
    slot_next = lax.rem(i + 1, nbuf)
    slot_prev = lax.rem(i + 2, nbuf)
    dump0 = MOE_TOPK * n_tok

    def block_rows(blk):
        b = jnp.clip(blk, 0, nblocks - 1)
        return j0_ref[b], jnp.where(blk < 0, -1, nv_ref[b] - 1)

    n_assign = tok_ref.shape[0]

    def gather_row(span, sl, r, prio):
        j0, _ = span
        tok = tok_ref[jnp.minimum(j0 + r, n_assign - 1)]
        pltpu.make_async_copy(h2_hbm.at[pl.ds(tok, 1)], xbuf.at[sl, pl.ds(r, 1)], gsem.at[sl]).start(priority=prio)

    def scatter_row(span, sl, r, prio):
        j0, last = span
        dst = jnp.where(r <= last, dst_ref[jnp.minimum(j0 + r, n_assign - 1)], dump0 + sl * rows + r)
        pltpu.make_async_copy(ybuf.at[sl, pl.ds(r, 1)], y2_hbm.at[pl.ds(dst, 1)], ssem.at[sl]).start(priority=prio)

    def looped(fn, blk, sl):
        span = block_rows(blk)

        def body(c, carry):
            for k in range(MOE_DMA_UNROLL):
                fn(span, sl, c * MOE_DMA_UNROLL + k, k % 2)
            return carry
        lax.fori_loop(0, rows // MOE_DMA_UNROLL, body, 0)

    def gather_wait(sl):
        pltpu.make_async_copy(h2_hbm.at[pl.ds(0, rows)], xbuf.at[sl], gsem.at[sl]).wait()

    def scatter_wait(sl):
        pltpu.make_async_copy(ybuf.at[sl], y2_hbm.at[pl.ds(0, rows)], ssem.at[sl]).wait()

    @pl.when(i == 0)
    def _():
        looped(gather_row, 0, 0)
        looped(gather_row, 1, 1)
        ybuf[...] = jnp.zeros_like(ybuf)
        for sl in range(2):
            pltpu.make_async_copy(ybuf.at[sl], y2_hbm.at[pl.ds(dump0 + sl * rows, rows)], ssem.at[sl]).start()

    @pl.when((i == 0) | (be_ref[i] != be_ref[jnp.maximum(i - 1, 0)]))
    def _():
        wgb[...] = wg_ref[0].astype(BF16)
        wub[...] = wu_ref[0].astype(BF16)
        wdb[...] = wd_ref[0].astype(BF16)

    gather_wait(slot)
    scatter_wait(slot)
    nxt, prv = block_rows(i + 2), block_rows(i - 1)
    for r in range(rows):
        scatter_row(prv, slot_prev, r, r % 2)
    x = _token_tiles_to_rows(xbuf[slot]).astype(BF16)
    for r in range(rows):
        gather_row(nxt, slot_prev, r, r % 2)
    gate = jnp.dot(x, wgb[...], preferred_element_type=F32)
    up = jnp.dot(x, wub[...], preferred_element_type=F32)
    hid = (gate * _sigmoid(gate)) * up
    y = jnp.dot(hid.astype(BF16), wdb[...], preferred_element_type=F32)
    ybuf[slot] = _rows_to_token_tiles(y)

    @pl.when(i == nblocks - 1)
    def _():
        gather_wait(slot_next)
        gather_wait(slot_prev)
        scatter_wait(slot_next)
        scatter_wait(slot_prev)
        looped(scatter_row, i, slot)
        scatter_wait(slot)


def _moe_experts(h2, tables, w_gate, w_up, w_down):
    n = h2.shape[0]
    e, d, f = w_gate.shape
    nblocks = tables[0].shape[0]
    wmap = lambda i, be, j0, nv, tok, dst: (be[i], 0, 0)
    grid_spec = pltpu.PrefetchScalarGridSpec(
        num_scalar_prefetch=5,
        grid=(nblocks,),
        in_specs=[
            pl.BlockSpec(memory_space=pl.ANY),
            pl.BlockSpec((1, d, f), wmap),
            pl.BlockSpec((1, d, f), wmap),
            pl.BlockSpec((1, f, d), wmap),
        ],
        out_specs=pl.BlockSpec(memory_space=pl.ANY),
        scratch_shapes=[
            pltpu.VMEM((MOE_BUFFERS, MOE_ROWS, d // LANES, LANES), F32),
            pltpu.VMEM((MOE_BUFFERS, MOE_ROWS, d // LANES, LANES), F32),
            pltpu.SemaphoreType.DMA((MOE_BUFFERS,)),
            pltpu.SemaphoreType.DMA((MOE_BUFFERS,)),
            pltpu.VMEM((d, f), BF16), pltpu.VMEM((d, f), BF16), pltpu.VMEM((f, d), BF16),
        ],
    )
    return pl.pallas_call(
        _moe_kernel,
        grid_spec=grid_spec,
        out_shape=jax.ShapeDtypeStruct((n * MOE_TOPK + MOE_BUFFERS * MOE_ROWS, d // LANES, LANES), F32),
        compiler_params=_params("arbitrary"),
        name="moe_experts",
    )(*tables, h2, w_gate, w_up, w_down)


def _combine_kernel(x1_ref, y0_ref, y1_ref, route_ref, o_ref):
    route = route_ref[...]
    w0 = route[:, 2:3]
    w1 = route[:, 3:4]
    o_ref[...] = x1_ref[...] + (w0 * _token_tiles_to_rows(y0_ref[...]) + w1 * _token_tiles_to_rows(y1_ref[...]))


def _combine(x1, y2, route, tm=512):
    n, d = x1.shape
    tiles = n // tm
    ytile = (tm, d // LANES, LANES)
    return pl.pallas_call(
        _combine_kernel,
        grid=(tiles,),
        in_specs=[pl.BlockSpec((tm, d), lambda i: (i, 0)), pl.BlockSpec(ytile, lambda i: (i, 0, 0)),
                  pl.BlockSpec(ytile, lambda i: (tiles + i, 0, 0)), pl.BlockSpec((tm, LANES), lambda i: (i, 0))],
        out_specs=pl.BlockSpec((tm, d), lambda i: (i, 0)),
        out_shape=jax.ShapeDtypeStruct((n, d), F32),
        compiler_params=_params("parallel"),
        name="moe_combine",
    )(x1, y2, y2, route)


def _dispatch_tables(route, expert_counts, n_tok):
    n_assign = n_tok * MOE_TOPK
    e_flat = route[:, :MOE_TOPK].astype(jnp.int32).reshape(n_assign)
    order = jnp.argsort(e_flat).astype(jnp.int32)
    counts = jnp.sum(expert_counts, axis=0)[:MOE_EXPERTS].astype(jnp.int32)
    blocks_per_expert = (counts + MOE_ROWS - 1) // MOE_ROWS
    blk_end = jnp.cumsum(blocks_per_expert)
    raw_start = jnp.cumsum(counts) - counts
    nblocks = n_assign // MOE_ROWS + MOE_EXPERTS
    blk = jnp.arange(nblocks, dtype=jnp.int32)
    blk_expert = jnp.minimum(jnp.sum(blk_end[None, :] <= blk[:, None], axis=1), MOE_EXPERTS - 1).astype(jnp.int32)
    within = blk - (blk_end - blocks_per_expert)[blk_expert]
    blk_nvalid = jnp.clip(counts[blk_expert] - within * MOE_ROWS, 0, MOE_ROWS).astype(jnp.int32)
    blk_j0 = jnp.where(blk_nvalid > 0, raw_start[blk_expert] + within * MOE_ROWS, 0).astype(jnp.int32)
    tok_sorted = order // MOE_TOPK
    dst_sorted = (order % MOE_TOPK) * n_tok + tok_sorted
    return blk_expert, blk_j0, blk_nvalid, tok_sorted, dst_sorted


def kernel(x, mem, g_mix, w_in, moba_q_norm, moba_k_norm, conv_w, conv_b, dt_bias, a_log, d_skip, ssd_norm, g_mem,
           w_mem_kv, mem_q_norm, mem_k_norm, w_o_moba, w_o_ssd, w_o_mem, w_out, g_ffn, w_router_group,
           w_router_expert, w_gate, w_up, w_down):
    bsz, seq, d = x.shape
    assert seq % MOBA_BLOCK == 0 and seq % SSD_CHUNK == 0
    n_tok = bsz * seq
    moba_w = MOBA_HEADS * MOBA_HEAD_DIM
    d_in = SSD_HEADS * SSD_HEAD_DIM
    xbc_w = d_in + 2 * SSD_GROUPS * SSD_STATE
    mem_w = MEM_HEADS * MEM_HEAD_DIM
    sizes = (moba_w, moba_w, moba_w, d_in, xbc_w, SSD_HEADS, mem_w, 3 * d)
    offs = [0]
    for sz in sizes:
        offs.append(offs[-1] + sz)
    w_in_b = w_in.astype(BF16)
    w_qk = w_in_b[:, offs[0]:offs[2]]
    w_v = w_in_b[:, offs[2]:offs[3]]
    w_z = w_in_b[:, offs[3]:offs[4]]
    w_xbc = w_in_b[:, offs[4]:offs[5]]
    w_dt = jnp.pad(w_in_b[:, offs[5]:offs[6]], ((0, 0), (0, LANES - SSD_HEADS)))
    w_qm = w_in_b[:, offs[6]:offs[7]]
    w_gates = w_in_b[:, offs[7]:offs[8]]

    x2 = x.reshape(n_tok, d)

    half = MOBA_HEAD_DIM // 2
    inv = ROPE_THETA ** (-jnp.arange(half, dtype=F32) / half)
    ang = jnp.arange(seq, dtype=F32)[:, None] * inv[None, :]
    cos_t = jnp.tile(jnp.cos(ang), (1, LANES // half))
    sin_t = jnp.tile(jnp.concatenate([-jnp.sin(ang), jnp.sin(ang)], axis=1), (1, LANES // MOBA_HEAD_DIM))
    gains = jnp.stack([jnp.tile(moba_q_norm, MOBA_HEADS), jnp.tile(moba_k_norm, MOBA_HEADS)]).reshape(2, 1, moba_w)

    h, qk, v, zs, xbc, dt = _inproj(x2, g_mix, w_qk, w_v, w_z, w_xbc, w_dt, gains, cos_t, sin_t, conv_w, conv_b,
                                    dt_bias, seq)

    qk3 = qk.reshape(bsz, seq, 2 * moba_w)
    v3 = v.reshape(bsz, seq, moba_w)
    nb = seq // MOBA_BLOCK
    assert nb <= MOBA_MAX_BLOCKS and nb % MOBA_UNROLL == 0
    ka, vt, qa = _moba_prep(qk3, v3)
    o_a = _moba_attention(qa, ka, vt).reshape(n_tok, moba_w)

    o_s = _ssd(xbc.reshape(bsz, seq, xbc_w), zs.reshape(bsz, seq, d_in), dt.reshape(bsz, seq, LANES), a_log, d_skip,
               ssd_norm).reshape(n_tok, d_in)

    km, vm = _memkv(mem, g_mem, w_mem_kv.astype(BF16), mem_k_norm)
    o_m = _memattn(h, w_qm, mem_q_norm, km, vm, seq)

    w_router = jnp.pad(jnp.concatenate([w_router_group, w_router_expert], axis=1),
                       ((0, 0), (0, LANES - MOE_GROUPS - MOE_EXPERTS)))
    w_router_hi = w_router.astype(BF16)
    w_router = jnp.stack([w_router_hi, (w_router - w_router_hi.astype(F32)).astype(BF16)])
    x1, h2, route, expert_counts = _merge(x2, h, o_a, o_s, o_m, w_gates, w_o_moba.astype(BF16),
                                          w_o_ssd.astype(BF16), w_o_mem.astype(BF16), w_out.astype(BF16), g_ffn,
                                          w_router)

    tables = _dispatch_tables(route, expert_counts, n_tok)
    y2 = _moe_experts(h2, tables, w_gate, w_up, w_down)
    out = _combine(x1, y2, route)
    return out.reshape(bsz, seq, d)
```

```python
import functools

import jax
import jax.numpy as jnp
from jax import lax
from jax.experimental import pallas as pl
from jax.experimental.pallas import tpu as pltpu

F32 = jnp.float32
BF16 = jnp.bfloat16
HIGHEST = lax.Precision.HIGHEST

EPS = 1e-6
ROPE_THETA = 10000.0
MOBA_HEADS = 8
MOBA_HEAD_DIM = 64
MOBA_BLOCK = 256
MOBA_TOPK = 3
SSD_HEAD_DIM = 64
SSD_HEADS = 16
SSD_GROUPS = 4
SSD_STATE = 128
SSD_CONV = 4
SSD_CHUNK = 256
MEM_HEADS = 4
MEM_HEAD_DIM = 128
MOE_GROUPS = 4
MOE_EXPERTS_PER_GROUP = 8
MOE_EXPERTS = MOE_GROUPS * MOE_EXPERTS_PER_GROUP
MOE_TOPK = 2

LANES = 128
MASKED = -1e30
MOBA_MAX_BLOCKS = 32
MOBA_UNROLL = 4
MOBA_TILES_PER_STEP = 4
MOBA_VT_ROWS = MOBA_HEAD_DIM + 16
MOE_ROWS = 256
INPROJ_CONV_CHUNK = 512
MOE_BUFFERS = 3
MOE_DMA_UNROLL = 8
VMEM_LIMIT = 56 * 1024 * 1024


def _params(*sem):
    return pltpu.CompilerParams(dimension_semantics=sem, vmem_limit_bytes=VMEM_LIMIT)


def _sigmoid(x):
    return 1.0 / (1.0 + jnp.exp(-x))


def _split3_dot(x, sel):
    hi = x.astype(BF16)
    rest = x - hi.astype(F32)
    mid = rest.astype(BF16)
    lo = (rest - mid.astype(F32)).astype(BF16)
    return (jnp.dot(hi, sel, preferred_element_type=F32) + jnp.dot(mid, sel, preferred_element_type=F32)
            + jnp.dot(lo, sel, preferred_element_type=F32))


def _rows_to_token_tiles(x):
    rows, width = x.shape
    chunks = width // LANES
    x4 = jnp.stack([x[:, k * LANES:(k + 1) * LANES].reshape(rows // 8, 8, LANES) for k in range(chunks)], axis=1)
    return jnp.swapaxes(x4, 1, 2).reshape(rows, chunks, LANES)


def _token_tiles_to_rows(x3):
    rows, chunks, _ = x3.shape
    x4 = jnp.swapaxes(x3.reshape(rows // 8, 8, chunks, LANES), 1, 2)
    return jnp.concatenate([x4[:, k].reshape(rows, LANES) for k in range(chunks)], axis=1)


def _nt_dot(a, b, precision=None):
    return lax.dot_general(a, b, (((1,), (1,)), ((), ())), precision=precision, preferred_element_type=F32)


def _softplus(x):
    return jnp.maximum(x, 0.0) + jnp.log1p(jnp.exp(-jnp.abs(x)))


def _inproj_kernel(x_ref, gmix_ref, wqk_ref, wv_ref, wz_ref, wxbc_ref, wdt_ref, gain_ref, gsum_ref, cos_ref, sin_ref,
                   cw_ref, cb_ref, dtb_ref, h_ref, qk_ref, v_ref, zs_ref, xbc_ref, dt_ref, *ext_refs,
                   tiles_per_seq):
    tm = x_ref.shape[0]
    x = x_ref[...]
    r = lax.rsqrt(jnp.mean(x * x, axis=-1, keepdims=True) + EPS)
    h = ((x * r) * gmix_ref[...]).astype(BF16)
    h_ref[...] = h

    width = gsum_ref.shape[0]
    half = MOBA_HEAD_DIM // 2
    lane = lax.broadcasted_iota(jnp.int32, (tm, width), 1)
    first = (lane & (MOBA_HEAD_DIM - 1)) < half
    reps = width // cos_ref.shape[1]
    cos = jnp.concatenate([cos_ref[...]] * reps, axis=1)
    sin = jnp.concatenate([sin_ref[...]] * reps, axis=1)
    for j in range(2):
        u = jnp.dot(h, wqk_ref[:, j * width:(j + 1) * width], preferred_element_type=F32)
        sq = u * u
        hi = sq.astype(BF16)
        lo = (sq - hi.astype(F32)).astype(BF16)
        ss = (jnp.dot(hi, gsum_ref[...], preferred_element_type=F32)
              + jnp.dot(lo, gsum_ref[...], preferred_element_type=F32))
        un = (u * lax.rsqrt(ss * (1.0 / MOBA_HEAD_DIM) + EPS)) * gain_ref[j]
        partner = jnp.where(first, pltpu.roll(un, width - half, axis=1), pltpu.roll(un, half, axis=1))
        qk_ref[:, j * width:(j + 1) * width] = un * cos + partner * sin

    v_ref[...] = jnp.dot(h, wv_ref[...], preferred_element_type=F32).astype(v_ref.dtype)
    z = jnp.dot(h, wz_ref[...], preferred_element_type=F32)
    zs_ref[...] = (z * _sigmoid(z)).astype(zs_ref.dtype)
    dt_ref[...] = _softplus(jnp.dot(h, wdt_ref[...], preferred_element_type=F32) + dtb_ref[...])

    @pl.when(pl.program_id(0) % tiles_per_seq == 0)
    def _():
        for ext_ref in ext_refs:
            ext_ref[0:8, :] = jnp.zeros((8, ext_ref.shape[1]), F32)

    cchunk = ext_refs[0].shape[1]
    for ci, ext_ref in enumerate(ext_refs):
        cs = slice(ci * cchunk, (ci + 1) * cchunk)
        u = jnp.dot(h, wxbc_ref[:, cs], preferred_element_type=F32)
        ext_ref[8:8 + tm, :] = u
        acc = cb_ref[:, cs] + cw_ref[SSD_CONV - 1:SSD_CONV, cs] * u
        for kk in range(SSD_CONV - 1):
            off = 8 - (SSD_CONV - 1 - kk)
            acc = acc + cw_ref[kk:kk + 1, cs] * ext_ref[off:off + tm, :]
        ext_ref[0:8, :] = u[tm - 8:tm, :]
        xbc_ref[:, cs] = (acc * _sigmoid(acc)).astype(xbc_ref.dtype)


def _inproj(x2, g_mix, w_qk, w_v, w_z, w_xbc, w_dt, gains, cos_t, sin_t, conv_w, conv_b, dt_bias, seq, tm=512):
    n, d = x2.shape
    width = w_qk.shape[1] // 2
    tiles_per_seq = seq // tm
    head = jnp.arange(width) // MOBA_HEAD_DIM
    gsum = (head[:, None] == head[None, :]).astype(BF16)
    dtb = jnp.pad(dt_bias.astype(F32), (0, LANES - dt_bias.shape[0])).reshape(1, LANES)
    full = lambda arr: pl.BlockSpec(arr.shape, lambda i: (0,) * arr.ndim)
    tile = lambda cols: pl.BlockSpec((tm, cols), lambda i: (i, 0))
    rope = pl.BlockSpec((tm, LANES), lambda i: (i % tiles_per_seq, 0))
    consts = (g_mix.reshape(1, d), w_qk, w_v, w_z, w_xbc, w_dt, gains, gsum)
    tail = (conv_w, conv_b.reshape(1, -1), dtb)
    outs = ((d, BF16), (2 * width, F32), (w_v.shape[1], BF16), (w_z.shape[1], BF16), (w_xbc.shape[1], BF16),
            (LANES, F32))
    return pl.pallas_call(
        functools.partial(_inproj_kernel, tiles_per_seq=tiles_per_seq),
        grid=(n // tm,),
        in_specs=[tile(d)] + [full(a) for a in consts] + [rope, rope] + [full(a) for a in tail],
        out_specs=[tile(c) for c, _ in outs],
        out_shape=[jax.ShapeDtypeStruct((n, c), dt) for c, dt in outs],
        scratch_shapes=[pltpu.VMEM((tm + 8, INPROJ_CONV_CHUNK), F32)] * (w_xbc.shape[1] // INPROJ_CONV_CHUNK),
        compiler_params=_params("arbitrary"),
        name="in_proj",
    )(x2, *consts, cos_t, sin_t, *tail)


def _moba_prep_kernel(q_ref, k_ref, v_ref, ka_ref, vt_ref, qa_ref, km_ref):
    n = pl.program_id(1)

    @pl.when(n == 0)
    def _():
        km_ref[...] = jnp.zeros_like(km_ref)

    _moba_select(q_ref, km_ref, qa_ref, n)

    ones = jnp.ones((MOBA_VT_ROWS - MOBA_HEAD_DIM, MOBA_BLOCK), BF16)
    k = k_ref[0]
    km_ref[pl.ds(n, 1), :] = jnp.mean(k, axis=0, keepdims=True)
    lane = lax.broadcasted_iota(jnp.int32, (MOBA_BLOCK, LANES), 1)
    own = lane < MOBA_HEAD_DIM
    onehot = jnp.where(lane == MOBA_HEAD_DIM + n, 1.0, 0.0)
    for pr in range(MOBA_HEADS // 2):
        pair = slice(pr * LANES, (pr + 1) * LANES)
        kp = k[:, pair]
        ka_ref[0, 2 * pr, 0] = jnp.where(own, kp, onehot).astype(BF16)
        ka_ref[0, 2 * pr + 1, 0] = jnp.where(own, pltpu.roll(kp, MOBA_HEAD_DIM, axis=1), onehot).astype(BF16)
        vp_t = v_ref[0, :, pair].astype(F32).T
        for hh in range(2):
            vt_ref[0, 2 * pr + hh, 0, :MOBA_HEAD_DIM, :] = vp_t[hh * MOBA_HEAD_DIM:(hh + 1) * MOBA_HEAD_DIM].astype(BF16)
            vt_ref[0, 2 * pr + hh, 0, MOBA_HEAD_DIM:, :] = ones


def _moba_prep(qk3, v3):
    b, s, w = v3.shape
    nb = s // MOBA_BLOCK
    return pl.pallas_call(
        _moba_prep_kernel,
        grid=(b, nb),
        in_specs=[pl.BlockSpec((1, MOBA_BLOCK, w), lambda bi, n: (bi, n, 0)),
                  pl.BlockSpec((1, MOBA_BLOCK, w), lambda bi, n: (bi, n, 1)),
                  pl.BlockSpec((1, MOBA_BLOCK, w), lambda bi, n: (bi, n, 0))],
        out_specs=[
            pl.BlockSpec((1, MOBA_HEADS, 1, MOBA_BLOCK, LANES), lambda bi, n: (bi, 0, n, 0, 0)),
            pl.BlockSpec((1, MOBA_HEADS, 1, MOBA_VT_ROWS, MOBA_BLOCK), lambda bi, n: (bi, 0, n, 0, 0)),
            pl.BlockSpec((1, MOBA_HEADS, 1, LANES, MOBA_BLOCK), lambda bi, n: (bi, 0, n, 0, 0)),
        ],
        out_shape=[jax.ShapeDtypeStruct((b, MOBA_HEADS, nb, MOBA_BLOCK, LANES), BF16),
                   jax.ShapeDtypeStruct((b, MOBA_HEADS, nb, MOBA_VT_ROWS, MOBA_BLOCK), BF16),
                   jax.ShapeDtypeStruct((b, MOBA_HEADS, nb, LANES, MOBA_BLOCK), BF16)],
        scratch_shapes=[pltpu.VMEM((MOBA_MAX_BLOCKS, w), F32)],
        compiler_params=_params("parallel", "arbitrary"),
        name="moba_prep",
    )(qk3, qk3, v3)


def _moba_select(q_ref, km_ref, qa_ref, i):
    blk = MOBA_BLOCK
    nsel = MOBA_MAX_BLOCKS
    qscale = (MOBA_HEAD_DIM ** -0.5) * 1.4426950408889634
    q_t = (q_ref[0] * qscale).T
    kmean = km_ref[...]
    lane = lax.broadcasted_iota(jnp.int32, (nsel, LANES), 1)
    blk_row = lax.broadcasted_iota(jnp.int32, (nsel, blk), 0)
    blk_rowf = blk_row.astype(F32)
    pad = jnp.zeros((LANES - MOBA_HEAD_DIM - nsel, blk), F32)
    for hd in range(MOBA_HEADS):
        pr, hh = hd // 2, hd % 2
        hmask = (lane >= hh * MOBA_HEAD_DIM) & (lane < (hh + 1) * MOBA_HEAD_DIM)
        km_h = jnp.where(hmask, kmean[:, pr * LANES:(pr + 1) * LANES], 0.0)
        gate = jnp.dot(km_h, q_t[pr * LANES:(pr + 1) * LANES], precision=HIGHEST, preferred_element_type=F32)
        g = jnp.where(blk_row < i, gate, -jnp.inf)
        bias = jnp.full((nsel, blk), MASKED, F32)
        for _ in range(MOBA_TOPK):
            m = jnp.max(g, axis=0, keepdims=True)
            hit = (g == m) & (m > -jnp.inf)
            first = jnp.min(jnp.where(hit, blk_rowf, float(nsel)), axis=0, keepdims=True)
            sel = blk_rowf == first
            bias = jnp.where(sel, 0.0, bias)
            g = jnp.where(sel, -jnp.inf, g)
        qs = q_t[hd * MOBA_HEAD_DIM:(hd + 1) * MOBA_HEAD_DIM]
        qa_ref[0, hd, 0] = jnp.concatenate([qs, bias, pad], axis=0).astype(BF16)


def _moba_kernel(qa_ref, qan_ref, ka_ref, vt_ref, o_ref, sa_ref, sb_ref, sc_ref, gma_ref, gmb_ref, gmc_ref, m_ref,
                 acc_ref):
    blk = MOBA_BLOCK
    nblk = ka_ref.shape[2]
    key_pos = lax.broadcasted_iota(jnp.int32, (blk, blk), 0)
    qry_pos = lax.broadcasted_iota(jnp.int32, (blk, blk), 1)
    causal = key_pos <= qry_pos
    feat_row = lax.broadcasted_iota(jnp.int32, (LANES, blk), 0) < MOBA_HEAD_DIM

    def fold(s, op):
        return op(s.reshape(blk // 8, 8, blk), axis=0)

    def score_tiles(tiles, s_ref, gm_ref):
        for hh in range(2):
            gmax = None
            for u, (n, q_of, is_own) in enumerate(tiles):
                s = jnp.dot(ka_ref[0, hh, n], q_of(hh), preferred_element_type=F32)
                if is_own:
                    s = jnp.where(causal, s, MASKED)
                s_ref[hh, u] = s
                gmax = fold(s, jnp.max) if gmax is None else jnp.maximum(gmax, fold(s, jnp.max))
            gm_ref[hh] = gmax

    def score_first(q_past, own, s_ref, gm_ref):
        q_own = lambda hh: jnp.where(feat_row, q_past(hh), jnp.zeros((), BF16))
        tiles = [(own, q_own, True)] + [(u - 1, q_past, False) for u in range(1, MOBA_UNROLL)]
        score_tiles(tiles, s_ref, gm_ref)

    def score_group(q_past, grp, s_ref, gm_ref):
        tiles = [(jnp.minimum(grp * MOBA_UNROLL + (u - 1), nblk - 1), q_past, False) for u in range(MOBA_UNROLL)]
        score_tiles(tiles, s_ref, gm_ref)

    def value_group(i, grp, s_ref, gm_ref):
        for hh in range(2):
            m_old = m_ref[hh][0:1]
            m_new = jnp.maximum(m_old, jnp.max(gm_ref[hh], axis=0, keepdims=True))
            alpha = jnp.exp2(m_old - m_new)
            acc = acc_ref[hh] * alpha
            for u in range(MOBA_UNROLL):
                n = jnp.minimum(grp * MOBA_UNROLL + (u - 1), nblk - 1)
                if u == 0:
                    n = jnp.where(grp == 0, i, n)
                p = jnp.exp2(s_ref[hh, u] - m_new).astype(BF16)
                acc = acc + jnp.dot(vt_ref[0, hh, n], p, preferred_element_type=F32)
            m_ref[hh] = jnp.broadcast_to(m_new, (8, blk))
            acc_ref[hh] = acc

    first_tile = pl.program_id(2) * MOBA_TILES_PER_STEP

    @pl.when(first_tile == 0)
    def _():
        score_first(lambda hh: qa_ref[0, hh, 0], 0, sc_ref, gmc_ref)

    for sub in range(MOBA_TILES_PER_STEP):
        i = first_tile + sub
        ngroups = (i + MOBA_UNROLL) // MOBA_UNROLL
        q_cur = lambda hh, sub=sub: qa_ref[0, hh, sub]
        if sub + 1 < MOBA_TILES_PER_STEP:
            q_nxt = lambda hh, sub=sub: qa_ref[0, hh, sub + 1]
        else:
            q_nxt = lambda hh: qan_ref[0, hh, 0]
        for hh in range(2):
            m_ref[hh] = jnp.full((8, blk), MASKED, F32)
            acc_ref[hh] = jnp.zeros((MOBA_VT_ROWS, blk), F32)

        score_group(q_cur, 1, sa_ref, gma_ref)
        value_group(i, 0, sc_ref, gmc_ref)

        rest = ngroups - 1
        npairs = jnp.maximum(rest - 1, 0) // 2

        def body(t, carry, i=i, q_cur=q_cur):
            score_group(q_cur, 2 * t + 2, sb_ref, gmb_ref)
            value_group(i, 2 * t + 1, sa_ref, gma_ref)
            score_group(q_cur, 2 * t + 3, sa_ref, gma_ref)
            value_group(i, 2 * t + 2, sb_ref, gmb_ref)
            return carry

        lax.fori_loop(0, npairs, body, 0)
        left = rest - 2 * npairs
        own_next = jnp.minimum(i + 1, nblk - 1)

        @pl.when(left == 2)
        def _(i=i, q_cur=q_cur, q_nxt=q_nxt, npairs=npairs, own_next=own_next):
            score_group(q_cur, 2 * npairs + 2, sb_ref, gmb_ref)
            value_group(i, 2 * npairs + 1, sa_ref, gma_ref)
            score_first(q_nxt, own_next, sc_ref, gmc_ref)
            value_group(i, 2 * npairs + 2, sb_ref, gmb_ref)

        @pl.when(left == 1)
        def _(i=i, q_nxt=q_nxt, npairs=npairs, own_next=own_next):
            score_first(q_nxt, own_next, sc_ref, gmc_ref)
            value_group(i, 2 * npairs + 1, sa_ref, gma_ref)

        @pl.when(left == 0)
        def _(q_nxt=q_nxt, own_next=own_next):
            score_first(q_nxt, own_next, sc_ref, gmc_ref)

        outs = [acc_ref[hh, :MOBA_HEAD_DIM, :] / acc_ref[hh, MOBA_HEAD_DIM:MOBA_HEAD_DIM + 1, :] for hh in range(2)]
        o_ref[0, sub * blk:(sub + 1) * blk, :] = jnp.concatenate(outs, axis=0).T.astype(o_ref.dtype)


def _moba_attention(qa, ka, vt):
    b, heads, nq = qa.shape[:3]
    s = nq * MOBA_BLOCK
    pairs = heads // 2
    tps = MOBA_TILES_PER_STEP
    assert nq % tps == 0
    return pl.pallas_call(
        _moba_kernel,
        grid=(b, pairs, nq // tps),
        in_specs=[
            pl.BlockSpec((1, 2, tps, LANES, MOBA_BLOCK), lambda bi, p, j: (bi, p, j, 0, 0)),
            pl.BlockSpec((1, 2, 1, LANES, MOBA_BLOCK),
                         lambda bi, p, j: (bi, p, jnp.minimum((j + 1) * tps, nq - 1), 0, 0)),
            pl.BlockSpec((1, 2, nq, MOBA_BLOCK, LANES), lambda bi, p, j: (bi, p, 0, 0, 0)),
            pl.BlockSpec((1, 2, nq, MOBA_VT_ROWS, MOBA_BLOCK), lambda bi, p, j: (bi, p, 0, 0, 0)),
        ],
        out_specs=pl.BlockSpec((1, tps * MOBA_BLOCK, LANES), lambda bi, p, j: (bi, j, p)),
        out_shape=jax.ShapeDtypeStruct((b, s, heads * MOBA_HEAD_DIM), BF16),
        scratch_shapes=[
            pltpu.VMEM((2, MOBA_UNROLL, MOBA_BLOCK, MOBA_BLOCK), F32),
            pltpu.VMEM((2, MOBA_UNROLL, MOBA_BLOCK, MOBA_BLOCK), F32),
            pltpu.VMEM((2, MOBA_UNROLL, MOBA_BLOCK, MOBA_BLOCK), F32),
            pltpu.VMEM((2, 8, MOBA_BLOCK), F32),
            pltpu.VMEM((2, 8, MOBA_BLOCK), F32),
            pltpu.VMEM((2, 8, MOBA_BLOCK), F32),
            pltpu.VMEM((2, 8, MOBA_BLOCK), F32),
            pltpu.VMEM((2, MOBA_VT_ROWS, MOBA_BLOCK), F32),
        ],
        compiler_params=_params("parallel", "parallel", "arbitrary"),
        name="moba_attention",
    )(qa, qa, ka, vt)


def _ssd_kernel(xbc_ref, zs_ref, dt_ref, alog_ref, dskip_ref, gn_ref, ex_ref, o_ref, state_ref):
    q = SSD_CHUNK
    d_in = SSD_HEADS * SSD_HEAD_DIM
    bc_w = SSD_GROUPS * SSD_STATE
    gw = d_in // SSD_GROUPS
    c = pl.program_id(1)

    @pl.when(c == 0)
    def _():
        state_ref[...] = jnp.zeros_like(state_ref)

    xs = xbc_ref[0, :, :d_in].astype(F32)
    bm = xbc_ref[0, :, d_in:d_in + bc_w]
    cm = xbc_ref[0, :, d_in + bc_w:]
    dt = dt_ref[0]
    a = dt * (-jnp.exp(alog_ref[...]))
    row = lax.broadcasted_iota(jnp.int32, (q, q), 0)
    col = lax.broadcasted_iota(jnp.int32, (q, q), 1)
    tril = col <= row
    a_cum = jnp.dot(tril.astype(F32), a, precision=HIGHEST, preferred_element_type=F32)
    a_cum_t = a_cum.T
    ex = ex_ref[...]
    dt_x = _split3_dot(dt, ex)
    acum_x = _split3_dot(a_cum, ex)
    alast_x = acum_x[q - 1:q, :]
    x_dt = xs * dt_x
    xd = (x_dt * jnp.exp(alast_x - acum_x)).astype(BF16)
    x_dt_b = x_dt.astype(BF16)
    e_acum = jnp.exp(acum_x)
    e_alast = jnp.exp(alast_x)
    lane2 = lax.broadcasted_iota(jnp.int32, (q, 2 * SSD_HEAD_DIM), 1)
    heads_per_group = SSD_HEADS // SSD_GROUPS

    for g in range(SSD_GROUPS):
        bg = bm[:, g * SSD_STATE:(g + 1) * SSD_STATE]
        cg = cm[:, g * SSD_STATE:(g + 1) * SSD_STATE]
        cb = _nt_dot(cg, bg)
        h_in = state_ref[g]
        y_g = jnp.dot(cg, h_in.astype(BF16), preferred_element_type=F32) * e_acum[:, g * gw:(g + 1) * gw]
        parts = []
        for pr in range(heads_per_group // 2):
            xp = x_dt_b[:, g * gw + pr * 2 * SSD_HEAD_DIM:g * gw + (pr + 1) * 2 * SSD_HEAD_DIM]
            ys = []
            for hh in range(2):
                hd = g * heads_per_group + pr * 2 + hh
                seg = a_cum[:, hd:hd + 1] - a_cum_t[hd:hd + 1, :]
                lmat = jnp.exp(jnp.where(tril, seg, -jnp.inf))
                ys.append(jnp.dot((cb * lmat).astype(BF16), xp, preferred_element_type=F32))
            parts.append(jnp.where(lane2 < SSD_HEAD_DIM, ys[0], ys[1]))
        y_g = y_g + jnp.concatenate(parts, axis=1)
        st = jnp.dot(bg.astype(F32).T.astype(BF16), xd[:, g * gw:(g + 1) * gw], preferred_element_type=F32)
        state_ref[g] = h_in * e_alast[:, g * gw:(g + 1) * gw] + st

        sl = slice(g * gw, (g + 1) * gw)
        y_g = y_g + dskip_ref[:, sl] * xs[:, sl]
        y_g = y_g * zs_ref[0, :, sl].astype(F32)
        r = lax.rsqrt(jnp.mean(y_g * y_g, axis=-1, keepdims=True) + EPS)
        o_ref[0, :, sl] = ((y_g * r) * gn_ref[:, sl]).astype(o_ref.dtype)


def _ssd(xbc3, zs3, dt3, a_log, d_skip, ssd_norm):
    b, s, xw = xbc3.shape
    d_in = SSD_HEADS * SSD_HEAD_DIM
    nc = s // SSD_CHUNK
    pad = LANES - SSD_HEADS
    alog = jnp.pad(a_log.astype(F32), (0, pad)).reshape(1, LANES)
    dskip = jnp.repeat(d_skip.astype(F32), SSD_HEAD_DIM).reshape(1, d_in)
    expand = (jnp.arange(LANES)[:, None] == (jnp.arange(d_in) // SSD_HEAD_DIM)[None, :]).astype(BF16)
    const = lambda shape: pl.BlockSpec(shape, lambda bi, ci: (0,) * len(shape))
    return pl.pallas_call(
        _ssd_kernel,
        grid=(b, nc),
        in_specs=[
            pl.BlockSpec((1, SSD_CHUNK, xw), lambda bi, ci: (bi, ci, 0)),
            pl.BlockSpec((1, SSD_CHUNK, d_in), lambda bi, ci: (bi, ci, 0)),
            pl.BlockSpec((1, SSD_CHUNK, LANES), lambda bi, ci: (bi, ci, 0)),
            const((1, LANES)), const((1, d_in)), const((1, d_in)), const((LANES, d_in)),
        ],
        out_specs=pl.BlockSpec((1, SSD_CHUNK, d_in), lambda bi, ci: (bi, ci, 0)),
        out_shape=jax.ShapeDtypeStruct((b, s, d_in), BF16),
        scratch_shapes=[pltpu.VMEM((SSD_GROUPS, SSD_STATE, d_in // SSD_GROUPS), F32)],
        compiler_params=_params("parallel", "arbitrary"),
        name="ssd_scan",
    )(xbc3, zs3, dt3, alog, dskip, ssd_norm.reshape(1, d_in), expand)


def _memkv_kernel(mem_ref, g_ref, w_ref, kg_ref, km_ref, vm_ref):
    m = mem_ref[0]
    r = lax.rsqrt(jnp.mean(m * m, axis=-1, keepdims=True) + EPS)
    mn = ((m * r) * g_ref[...]).astype(BF16)
    kv = jnp.dot(mn, w_ref[...], preferred_element_type=F32)
    mw = MEM_HEADS * MEM_HEAD_DIM
    for hd in range(MEM_HEADS):
        sl = slice(hd * MEM_HEAD_DIM, (hd + 1) * MEM_HEAD_DIM)
        kh = kv[:, sl]
        rk = lax.rsqrt(jnp.mean(kh * kh, axis=-1, keepdims=True) + EPS)
        km_ref[0, :, sl] = ((kh * rk) * kg_ref[...]).astype(km_ref.dtype)
    vm_ref[0] = kv[:, mw:].astype(vm_ref.dtype)


def _memkv(mem, g_mem, w_kv, k_gain):
    b, m, d = mem.shape
    mw = MEM_HEADS * MEM_HEAD_DIM
    return pl.pallas_call(
        _memkv_kernel,
        grid=(b,),
        in_specs=[
            pl.BlockSpec((1, m, d), lambda bi: (bi, 0, 0)),
            pl.BlockSpec((1, d), lambda bi: (0, 0)),
            pl.BlockSpec((d, 2 * mw), lambda bi: (0, 0)),
            pl.BlockSpec((1, MEM_HEAD_DIM), lambda bi: (0, 0)),
        ],
        out_specs=[pl.BlockSpec((1, m, mw), lambda bi: (bi, 0, 0)), pl.BlockSpec((1, m, mw), lambda bi: (bi, 0, 0))],
        out_shape=[jax.ShapeDtypeStruct((b, m, mw), BF16), jax.ShapeDtypeStruct((b, m, mw), BF16)],
        compiler_params=_params("parallel"),
        name="mem_kv",
    )(mem, g_mem.reshape(1, d), w_kv, k_gain.reshape(1, MEM_HEAD_DIM))


def _memattn_kernel(h_ref, w_ref, qg_ref, km_ref, vm_ref, o_ref):
    qm = jnp.dot(h_ref[...], w_ref[...], preferred_element_type=F32)
    scale = MEM_HEAD_DIM ** -0.5
    for hd in range(MEM_HEADS):
        sl = slice(hd * MEM_HEAD_DIM, (hd + 1) * MEM_HEAD_DIM)
        qh = qm[:, sl]
        r = lax.rsqrt(jnp.mean(qh * qh, axis=-1, keepdims=True) + EPS)
        qn = ((qh * r) * qg_ref[...]).astype(BF16)
        s = _nt_dot(qn, km_ref[0, :, sl]) * scale
        p = jnp.exp(s - jnp.max(s, axis=-1, keepdims=True))
        l = jnp.sum(p, axis=-1, keepdims=True)
        o = jnp.dot(p.astype(BF16), vm_ref[0, :, sl], preferred_element_type=F32)
        o_ref[:, sl] = (o / l).astype(o_ref.dtype)


def _memattn(h, w_qm, q_gain, km, vm, seq, tm=1024):
    n, d = h.shape
    b, m, mw = km.shape
    tiles_per_seq = seq // tm
    return pl.pallas_call(
        _memattn_kernel,
        grid=(n // tm,),
        in_specs=[
            pl.BlockSpec((tm, d), lambda i: (i, 0)),
            pl.BlockSpec((d, mw), lambda i: (0, 0)),
            pl.BlockSpec((1, MEM_HEAD_DIM), lambda i: (0, 0)),
            pl.BlockSpec((1, m, mw), lambda i: (i // tiles_per_seq, 0, 0)),
            pl.BlockSpec((1, m, mw), lambda i: (i // tiles_per_seq, 0, 0)),
        ],
        out_specs=pl.BlockSpec((tm, mw), lambda i: (i, 0)),
        out_shape=jax.ShapeDtypeStruct((n, mw), BF16),
        compiler_params=_params("parallel"),
        name="mem_attention",
    )(h, w_qm, q_gain.reshape(1, MEM_HEAD_DIM), km, vm)


def _merge_kernel(x_ref, h_ref, oa_ref, os_ref, om_ref, wg_ref, wa_ref, ws_ref, wm_ref, wo_ref, gf_ref, wr_ref,
                  x1_ref, h2_ref, route_ref, cnt_ref):
    d = x_ref.shape[1]
    gates = _sigmoid(jnp.dot(h_ref[...], wg_ref[...], preferred_element_type=F32))
    merged = gates[:, :d] * jnp.dot(oa_ref[...], wa_ref[...], preferred_element_type=F32)
    merged = merged + gates[:, d:2 * d] * jnp.dot(os_ref[...], ws_ref[...], preferred_element_type=F32)
    merged = merged + gates[:, 2 * d:] * jnp.dot(om_ref[...], wm_ref[...], preferred_element_type=F32)
    x1 = x_ref[...] + jnp.dot(merged.astype(BF16), wo_ref[...], preferred_element_type=F32)
    x1_ref[...] = x1
    r = lax.rsqrt(jnp.mean(x1 * x1, axis=-1, keepdims=True) + EPS)
    h2 = (x1 * r) * gf_ref[...]
    h2_ref[...] = _rows_to_token_tiles(h2)

    h2_hi = h2.astype(BF16)
    h2_lo = (h2 - h2_hi.astype(F32)).astype(BF16)
    lg = (jnp.dot(h2_hi, wr_ref[0], preferred_element_type=F32) + jnp.dot(h2_hi, wr_ref[1], preferred_element_type=F32)
          + jnp.dot(h2_lo, wr_ref[0], preferred_element_type=F32))
    lanef = lax.broadcasted_iota(jnp.int32, lg.shape, 1).astype(F32)
    big = float(LANES)
    gmask = lanef < MOE_GROUPS
    gl = jnp.where(gmask, lg, -jnp.inf)
    gmax = jnp.max(gl, axis=-1, keepdims=True)
    p_g = 1.0 / jnp.sum(jnp.exp(gl - gmax), axis=-1, keepdims=True)
    g_sel = jnp.min(jnp.where(gl == gmax, lanef, big), axis=-1, keepdims=True)
    lo = MOE_GROUPS + MOE_EXPERTS_PER_GROUP * g_sel
    el = jnp.where((lanef >= lo) & (lanef < lo + MOE_EXPERTS_PER_GROUP), lg, -jnp.inf)
    m1 = jnp.max(el, axis=-1, keepdims=True)
    i1 = jnp.min(jnp.where(el == m1, lanef, big), axis=-1, keepdims=True)
    el2 = jnp.where(lanef == i1, -jnp.inf, el)
    m2 = jnp.max(el2, axis=-1, keepdims=True)
    i2 = jnp.min(jnp.where(el2 == m2, lanef, big), axis=-1, keepdims=True)
    e2 = jnp.exp(m2 - m1)
    w1 = 1.0 / (1.0 + e2)
    w2 = e2 / (1.0 + e2)
    route = jnp.where(lanef == 0, i1 - MOE_GROUPS, 0.0)
    route = jnp.where(lanef == 1, i2 - MOE_GROUPS, route)
    route = jnp.where(lanef == 2, p_g * w1, route)
    route = jnp.where(lanef == 3, p_g * w2, route)
    route_ref[...] = route

    @pl.when(pl.program_id(0) == 0)
    def _():
        cnt_ref[...] = jnp.zeros_like(cnt_ref)

    hits = jnp.where(lanef == i1 - MOE_GROUPS, 1.0, 0.0) + jnp.where(lanef == i2 - MOE_GROUPS, 1.0, 0.0)
    cnt_ref[...] += jnp.sum(hits.reshape(hits.shape[0] // 8, 8, LANES), axis=0)


def _merge(x2, h, o_a, o_s, o_m, w_gates, w_a, w_s, w_m, w_out, g_ffn, w_router, tm=512):
    n, d = x2.shape
    full = lambda arr: pl.BlockSpec(arr.shape, lambda i: (0,) * arr.ndim)
    tile = lambda arr: pl.BlockSpec((tm, arr.shape[1]), lambda i: (i, 0))
    gf = g_ffn.reshape(1, d)
    return pl.pallas_call(
        _merge_kernel,
        grid=(n // tm,),
        in_specs=[tile(x2), tile(h), tile(o_a), tile(o_s), tile(o_m), full(w_gates), full(w_a), full(w_s), full(w_m),
                  full(w_out), full(gf), full(w_router)],
        out_specs=[pl.BlockSpec((tm, d), lambda i: (i, 0)),
                   pl.BlockSpec((tm, d // LANES, LANES), lambda i: (i, 0, 0)),
                   pl.BlockSpec((tm, LANES), lambda i: (i, 0)),
                   pl.BlockSpec((8, LANES), lambda i: (0, 0))],
        out_shape=[jax.ShapeDtypeStruct((n, d), F32), jax.ShapeDtypeStruct((n, d // LANES, LANES), F32),
                   jax.ShapeDtypeStruct((n, LANES), F32), jax.ShapeDtypeStruct((8, LANES), F32)],
        compiler_params=_params("arbitrary"),
        name="merge_router",
    )(x2, h, o_a, o_s, o_m, w_gates, w_a, w_s, w_m, w_out, gf, w_router)


def _moe_kernel(be_ref, j0_ref, nv_ref, tok_ref, dst_ref, h2_hbm, wg_ref, wu_ref, wd_ref, y2_hbm, xbuf, ybuf, gsem,
                ssem, wgb, wub, wdb):
    rows = xbuf.shape[1]
    n_tok = h2_hbm.shape[0]
    nblocks = pl.num_programs(0)
    i = pl.program_id(0)
    nbuf = xbuf.shape[0]
    slot = lax.rem(i, nbuf)
    slot_next = lax.rem(i + 1, nbuf)
    slot_prev = lax.rem(i + 2, nbuf)
    dump0 = MOE_TOPK * n_tok

    def block_rows(blk):
        b = jnp.clip(blk, 0, nblocks - 1)
        return j0_ref[b], jnp.where(blk < 0, -1, nv_ref[b] - 1)

    n_assign = tok_ref.shape[0]

    def gather_row(span, sl, r, prio):
        j0, _ = span
        tok = tok_ref[jnp.minimum(j0 + r, n_assign - 1)]
        pltpu.make_async_copy(h2_hbm.at[pl.ds(tok, 1)], xbuf.at[sl, pl.ds(r, 1)], gsem.at[sl]).start(priority=prio)

    def scatter_row(span, sl, r, prio):
        j0, last = span
        dst = jnp.where(r <= last, dst_ref[jnp.minimum(j0 + r, n_assign - 1)], dump0 + sl * rows + r)
        pltpu.make_async_copy(ybuf.at[sl, pl.ds(r, 1)], y2_hbm.at[pl.ds(dst, 1)], ssem.at[sl]).start(priority=prio)

    def looped(fn, blk, sl):
        span = block_rows(blk)

        def body(c, carry):
            for k in range(MOE_DMA_UNROLL):
                fn(span, sl, c * MOE_DMA_UNROLL + k, k % 2)
            return carry
        lax.fori_loop(0, rows // MOE_DMA_UNROLL, body, 0)

    def gather_wait(sl):
        pltpu.make_async_copy(h2_hbm.at[pl.ds(0, rows)], xbuf.at[sl], gsem.at[sl]).wait()

    def scatter_wait(sl):
        pltpu.make_async_copy(ybuf.at[sl], y2_hbm.at[pl.ds(0, rows)], ssem.at[sl]).wait()

    @pl.when(i == 0)
    def _():
        looped(gather_row, 0, 0)
        looped(gather_row, 1, 1)
        ybuf[...] = jnp.zeros_like(ybuf)
        for sl in range(2):
            pltpu.make_async_copy(ybuf.at[sl], y2_hbm.at[pl.ds(dump0 + sl * rows, rows)], ssem.at[sl]).start()

    @pl.when((i == 0) | (be_ref[i] != be_ref[jnp.maximum(i - 1, 0)]))
    def _():
        wgb[...] = wg_ref[0].astype(BF16)
        wub[...] = wu_ref[0].astype(BF16)
        wdb[...] = wd_ref[0].astype(BF16)

    gather_wait(slot)
    scatter_wait(slot)
    nxt, prv = block_rows(i + 2), block_rows(i - 1)
    for r in range(rows):
        scatter_row(prv, slot_prev, r, r % 2)
    x = _token_tiles_to_rows(xbuf[slot]).astype(BF16)
    for r in range(rows):
        gather_row(nxt, slot_prev, r, r % 2)
    gate = jnp.dot(x, wgb[...], preferred_element_type=F32)
    up = jnp.dot(x, wub[...], preferred_element_type=F32)
    hid = (gate * _sigmoid(gate)) * up
    y = jnp.dot(hid.astype(BF16), wdb[...], preferred_element_type=F32)
    ybuf[slot] = _rows_to_token_tiles(y)

    @pl.when(i == nblocks - 1)
    def _():
        gather_wait(slot_next)
        gather_wait(slot_prev)
        scatter_wait(slot_next)
        scatter_wait(slot_prev)
        looped(scatter_row, i, slot)
        scatter_wait(slot)


def _moe_experts(h2, tables, w_gate, w_up, w_down):
    n = h2.shape[0]
    e, d, f = w_gate.shape
    nblocks = tables[0].shape[0]
    wmap = lambda i, be, j0, nv, tok, dst: (be[i], 0, 0)
    grid_spec = pltpu.PrefetchScalarGridSpec(
        num_scalar_prefetch=5,
        grid=(nblocks,),
        in_specs=[
            pl.BlockSpec(memory_space=pl.ANY),
            pl.BlockSpec((1, d, f), wmap),
            pl.BlockSpec((1, d, f), wmap),
            pl.BlockSpec((1, f, d), wmap),
        ],
        out_specs=pl.BlockSpec(memory_space=pl.ANY),
        scratch_shapes=[
            pltpu.VMEM((MOE_BUFFERS, MOE_ROWS, d // LANES, LANES), F32),
            pltpu.VMEM((MOE_BUFFERS, MOE_ROWS, d // LANES, LANES), F32),
            pltpu.SemaphoreType.DMA((MOE_BUFFERS,)),
            pltpu.SemaphoreType.DMA((MOE_BUFFERS,)),
            pltpu.VMEM((d, f), BF16), pltpu.VMEM((d, f), BF16), pltpu.VMEM((f, d), BF16),
        ],
    )
    return pl.pallas_call(
        _moe_kernel,
        grid_spec=grid_spec,
        out_shape=jax.ShapeDtypeStruct((n * MOE_TOPK + MOE_BUFFERS * MOE_ROWS, d // LANES, LANES), F32),
        compiler_params=_params("arbitrary"),
        name="moe_experts",
    )(*tables, h2, w_gate, w_up, w_down)


def _combine_kernel(x1_ref, y0_ref, y1_ref, route_ref, o_ref):
    route = route_ref[...]
    w0 = route[:, 2:3]
    w1 = route[:, 3:4]
    o_ref[...] = x1_ref[...] + (w0 * _token_tiles_to_rows(y0_ref[...]) + w1 * _token_tiles_to_rows(y1_ref[...]))


def _combine(x1, y2, route, tm=512):
    n, d = x1.shape
    tiles = n // tm
    ytile = (tm, d // LANES, LANES)
    return pl.pallas_call(
        _combine_kernel,
        grid=(tiles,),
        in_specs=[pl.BlockSpec((tm, d), lambda i: (i, 0)), pl.BlockSpec(ytile, lambda i: (i, 0, 0)),
                  pl.BlockSpec(ytile, lambda i: (tiles + i, 0, 0)), pl.BlockSpec((tm, LANES), lambda i: (i, 0))],
        out_specs=pl.BlockSpec((tm, d), lambda i: (i, 0)),
        out_shape=jax.ShapeDtypeStruct((n, d), F32),
        compiler_params=_params("parallel"),
        name="moe_combine",
    )(x1, y2, y2, route)


def _dispatch_tables(route, expert_counts, n_tok):
    n_assign = n_tok * MOE_TOPK
    e_flat = route[:, :MOE_TOPK].astype(jnp.int32).reshape(n_assign)
    order = jnp.argsort(e_flat).astype(jnp.int32)
    counts = jnp.sum(expert_counts, axis=0)[:MOE_EXPERTS].astype(jnp.int32)
    blocks_per_expert = (counts + MOE_ROWS - 1) // MOE_ROWS
    blk_end = jnp.cumsum(blocks_per_expert)
    raw_start = jnp.cumsum(counts) - counts
    nblocks = n_assign // MOE_ROWS + MOE_EXPERTS
    blk = jnp.arange(nblocks, dtype=jnp.int32)
    blk_expert = jnp.minimum(jnp.sum(blk_end[None, :] <= blk[:, None], axis=1), MOE_EXPERTS - 1).astype(jnp.int32)
    within = blk - (blk_end - blocks_per_expert)[blk_expert]
    blk_nvalid = jnp.clip(counts[blk_expert] - within * MOE_ROWS, 0, MOE_ROWS).astype(jnp.int32)
    blk_j0 = jnp.where(blk_nvalid > 0, raw_start[blk_expert] + within * MOE_ROWS, 0).astype(jnp.int32)
    tok_sorted = order // MOE_TOPK
    dst_sorted = (order % MOE_TOPK) * n_tok + tok_sorted
    return blk_expert, blk_j0, blk_nvalid, tok_sorted, dst_sorted


def kernel(x, mem, g_mix, w_in, moba_q_norm, moba_k_norm, conv_w, conv_b, dt_bias, a_log, d_skip, ssd_norm, g_mem,
           w_mem_kv, mem_q_norm, mem_k_norm, w_o_moba, w_o_ssd, w_o_mem, w_out, g_ffn, w_router_group,
           w_router_expert, w_gate, w_up, w_down):
    bsz, seq, d = x.shape
    assert seq % MOBA_BLOCK == 0 and seq % SSD_CHUNK == 0
    n_tok = bsz * seq
    moba_w = MOBA_HEADS * MOBA_HEAD_DIM
    d_in = SSD_HEADS * SSD_HEAD_DIM
    xbc_w = d_in + 2 * SSD_GROUPS * SSD_STATE
    mem_w = MEM_HEADS * MEM_HEAD_DIM
    sizes = (moba_w, moba_w, moba_w, d_in, xbc_w, SSD_HEADS, mem_w, 3 * d)
    offs = [0]
    for sz in sizes:
        offs.append(offs[-1] + sz)
    w_in_b = w_in.astype(BF16)
    w_qk = w_in_b[:, offs[0]:offs[2]]
    w_v = w_in_b[:, offs[2]:offs[3]]
    w_z = w_in_b[:, offs[3]:offs[4]]
    w_xbc = w_in_b[:, offs[4]:offs[5]]
    w_dt = jnp.pad(w_in_b[:, offs[5]:offs[6]], ((0, 0), (0, LANES - SSD_HEADS)))
    w_qm = w_in_b[:, offs[6]:offs[7]]
    w_gates = w_in_b[:, offs[7]:offs[8]]

    x2 = x.reshape(n_tok, d)

    half = MOBA_HEAD_DIM // 2
    inv = ROPE_THETA ** (-jnp.arange(half, dtype=F32) / half)
    ang = jnp.arange(seq, dtype=F32)[:, None] * inv[None, :]
    cos_t = jnp.tile(jnp.cos(ang), (1, LANES // half))
    sin_t = jnp.tile(jnp.concatenate([-jnp.sin(ang), jnp.sin(ang)], axis=1), (1, LANES // MOBA_HEAD_DIM))
    gains = jnp.stack([jnp.tile(moba_q_norm, MOBA_HEADS), jnp.tile(moba_k_norm, MOBA_HEADS)]).reshape(2, 1, moba_w)

    h, qk, v, zs, xbc, dt = _inproj(x2, g_mix, w_qk, w_v, w_z, w_xbc, w_dt, gains, cos_t, sin_t, conv_w, conv_b,
                                    dt_bias, seq)

    qk3 = qk.reshape(bsz, seq, 2 * moba_w)
    v3 = v.reshape(bsz, seq, moba_w)
    nb = seq // MOBA_BLOCK
    assert nb <= MOBA_MAX_BLOCKS and nb % MOBA_UNROLL == 0
    ka, vt, qa = _moba_prep(qk3, v3)
    o_a = _moba_attention(qa, ka, vt).reshape(n_tok, moba_w)

    o_s = _ssd(xbc.reshape(bsz, seq, xbc_w), zs.reshape(bsz, seq, d_in), dt.reshape(bsz, seq, LANES), a_log, d_skip,
               ssd_norm).reshape(n_tok, d_in)

    km, vm = _memkv(mem, g_mem, w_mem_kv.astype(BF16), mem_k_norm)
    o_m = _memattn(h, w_qm, mem_q_norm, km, vm, seq)

    w_router = jnp.pad(jnp.concatenate([w_router_group, w_router_expert], axis=1),
                       ((0, 0), (0, LANES - MOE_GROUPS - MOE_EXPERTS)))
    w_router_hi = w_router.astype(BF16)
    w_router = jnp.stack([w_router_hi, (w_router - w_router_hi.astype(F32)).astype(BF16)])
    x1, h2, route, expert_counts = _merge(x2, h, o_a, o_s, o_m, w_gates, w_o_moba.astype(BF16),
                                          w_o_ssd.astype(BF16), w_o_mem.astype(BF16), w_out.astype(BF16), g_ffn,
                                          w_router)

    tables = _dispatch_tables(route, expert_counts, n_tok)
    y2 = _moe_experts(h2, tables, w_gate, w_up, w_down)
    out = _combine(x1, y2, route)
    return out.reshape(bsz, seq, d)
```

```python
import functools

import jax
import jax.numpy as jnp
from jax import lax
from jax.experimental import pallas as pl
from jax.experimental.pallas import tpu as pltpu

F32 = jnp.float32
BF16 = jnp.bfloat16
HIGHEST = lax.Precision.HIGHEST

EPS = 1e-6
ROPE_THETA = 10000.0
MOBA_HEADS = 8
MOBA_HEAD_DIM = 64
MOBA_BLOCK = 256
MOBA_TOPK = 3
SSD_HEAD_DIM = 64
SSD_HEADS = 16
SSD_GROUPS = 4
SSD_STATE = 128
SSD_CONV = 4
SSD_CHUNK = 256
MEM_HEADS = 4
MEM_HEAD_DIM = 128
MOE_GROUPS = 4
MOE_EXPERTS_PER_GROUP = 8
MOE_EXPERTS = MOE_GROUPS * MOE_EXPERTS_PER_GROUP
MOE_TOPK = 2

LANES = 128
SUBLANES = 8
LOG2E = 1.4426950408889634
MASKED = -1e30
MOBA_MAX_BLOCKS = 32
MOBA_UNROLL = 2
MOBA_TILES_PER_STEP = 4
MOBA_VT_ROWS = MOBA_HEAD_DIM + 16
MOE_ROWS = 256
INPROJ_CONV_CHUNK = 512
MOE_BUFFERS = 3
MOE_DMA_UNROLL = 8
VMEM_LIMIT = 56 * 1024 * 1024


def _params(*sem):
    return pltpu.CompilerParams(dimension_semantics=sem, vmem_limit_bytes=VMEM_LIMIT)


def _sigmoid(x):
    return 1.0 / (1.0 + jnp.exp(-x))


def _split3_dot(x, sel):
    hi = x.astype(BF16)
    rest = x - hi.astype(F32)
    mid = rest.astype(BF16)
    lo = (rest - mid.astype(F32)).astype(BF16)
    return (jnp.dot(hi, sel, preferred_element_type=F32) + jnp.dot(mid, sel, preferred_element_type=F32)
            + jnp.dot(lo, sel, preferred_element_type=F32))


def _rows_to_token_tiles(x):
    rows, width = x.shape
    chunks = width // LANES
    x4 = jnp.stack([x[:, k * LANES:(k + 1) * LANES].reshape(rows // SUBLANES, SUBLANES, LANES)
                    for k in range(chunks)], axis=1)
    return jnp.swapaxes(x4, 1, 2).reshape(rows, chunks, LANES)


def _token_tiles_to_rows(x3):
    rows, chunks, _ = x3.shape
    x4 = jnp.swapaxes(x3.reshape(rows // SUBLANES, SUBLANES, chunks, LANES), 1, 2)
    return jnp.concatenate([x4[:, k].reshape(rows, LANES) for k in range(chunks)], axis=1)


def _nt_dot(a, b, precision=None):
    return lax.dot_general(a, b, (((1,), (1,)), ((), ())), precision=precision, preferred_element_type=F32)


def _softplus(x):
    return jnp.maximum(x, 0.0) + jnp.log1p(jnp.exp(-jnp.abs(x)))


def _inproj_kernel(x_ref, gmix_ref, wqk_ref, wv_ref, wz_ref, wxbc_ref, wdt_ref, gain_ref, gsum_ref, cos_ref, sin_ref,
                   cw_ref, cb_ref, dtb_ref, h_ref, qk_ref, v_ref, zs_ref, xbc_ref, dt_ref, *ext_refs,
                   tiles_per_seq):
    tm = x_ref.shape[0]
    x = x_ref[...]
    r = lax.rsqrt(jnp.mean(x * x, axis=-1, keepdims=True) + EPS)
    h = ((x * r) * gmix_ref[...]).astype(BF16)
    h_ref[...] = h

    width = gsum_ref.shape[0]
    half = MOBA_HEAD_DIM // 2
    lane = lax.broadcasted_iota(jnp.int32, (tm, width), 1)
    first = (lane & (MOBA_HEAD_DIM - 1)) < half
    reps = width // cos_ref.shape[1]
    cos = jnp.concatenate([cos_ref[...]] * reps, axis=1)
    sin = jnp.concatenate([sin_ref[...]] * reps, axis=1)
    for j in range(2):
        u = jnp.dot(h, wqk_ref[:, j * width:(j + 1) * width], preferred_element_type=F32)
        sq = u * u
        hi = sq.astype(BF16)
        lo = (sq - hi.astype(F32)).astype(BF16)
        ss = (jnp.dot(hi, gsum_ref[...], preferred_element_type=F32)
              + jnp.dot(lo, gsum_ref[...], preferred_element_type=F32))
        un = (u * lax.rsqrt(ss * (1.0 / MOBA_HEAD_DIM) + EPS)) * gain_ref[j]
        partner = jnp.where(first, pltpu.roll(un, width - half, axis=1), pltpu.roll(un, half, axis=1))
        qk_ref[:, j * width:(j + 1) * width] = un * cos + partner * sin

    v_ref[...] = jnp.dot(h, wv_ref[...], preferred_element_type=F32).astype(v_ref.dtype)
    z = jnp.dot(h, wz_ref[...], preferred_element_type=F32)
    zs_ref[...] = (z * _sigmoid(z)).astype(zs_ref.dtype)
    dt_ref[...] = _softplus(jnp.dot(h, wdt_ref[...], preferred_element_type=F32) + dtb_ref[...])

    halo = SUBLANES

    @pl.when(pl.program_id(0) % tiles_per_seq == 0)
    def _():
        for ext_ref in ext_refs:
            ext_ref[0:halo, :] = jnp.zeros((halo, ext_ref.shape[1]), F32)

    cchunk = ext_refs[0].shape[1]
    for ci, ext_ref in enumerate(ext_refs):
        cs = slice(ci * cchunk, (ci + 1) * cchunk)
        u = jnp.dot(h, wxbc_ref[:, cs], preferred_element_type=F32)
        ext_ref[halo:halo + tm, :] = u
        acc = cb_ref[:, cs] + cw_ref[SSD_CONV - 1:SSD_CONV, cs] * u
        for kk in range(SSD_CONV - 1):
            off = halo - (SSD_CONV - 1 - kk)
            acc = acc + cw_ref[kk:kk + 1, cs] * ext_ref[off:off + tm, :]
        ext_ref[0:halo, :] = u[tm - halo:tm, :]
        xbc_ref[:, cs] = (acc * _sigmoid(acc)).astype(xbc_ref.dtype)


def _inproj(x2, g_mix, w_qk, w_v, w_z, w_xbc, w_dt, gains, cos_t, sin_t, conv_w, conv_b, dt_bias, seq, tm=512):
    n, d = x2.shape
    width = w_qk.shape[1] // 2
    tiles_per_seq = seq // tm
    head = jnp.arange(width) // MOBA_HEAD_DIM
    gsum = (head[:, None] == head[None, :]).astype(BF16)
    dtb = jnp.pad(dt_bias.astype(F32), (0, LANES - dt_bias.shape[0])).reshape(1, LANES)
    full = lambda arr: pl.BlockSpec(arr.shape, lambda i: (0,) * arr.ndim)
    tile = lambda cols: pl.BlockSpec((tm, cols), lambda i: (i, 0))
    rope = pl.BlockSpec((tm, LANES), lambda i: (i % tiles_per_seq, 0))
    consts = (g_mix.reshape(1, d), w_qk, w_v, w_z, w_xbc, w_dt, gains, gsum)
    tail = (conv_w, conv_b.reshape(1, -1), dtb)
    outs = ((d, BF16), (2 * width, F32), (w_v.shape[1], BF16), (w_z.shape[1], BF16), (w_xbc.shape[1], BF16),
            (LANES, F32))
    return pl.pallas_call(
        functools.partial(_inproj_kernel, tiles_per_seq=tiles_per_seq),
        grid=(n // tm,),
        in_specs=[tile(d)] + [full(a) for a in consts] + [rope, rope] + [full(a) for a in tail],
        out_specs=[tile(c) for c, _ in outs],
        out_shape=[jax.ShapeDtypeStruct((n, c), dt) for c, dt in outs],
        scratch_shapes=[pltpu.VMEM((tm + SUBLANES, INPROJ_CONV_CHUNK), F32)] * (w_xbc.shape[1] // INPROJ_CONV_CHUNK),
        compiler_params=_params("arbitrary"),
        name="in_proj",
    )(x2, *consts, cos_t, sin_t, *tail)


def _moba_prep_kernel(q_ref, k_ref, v_ref, ka_ref, vt_ref, qa_ref, km_ref):
    n = pl.program_id(1)

    @pl.when(n == 0)
    def _():
        km_ref[...] = jnp.zeros_like(km_ref)

    _moba_select(q_ref, km_ref, qa_ref, n)

    ones = jnp.ones((MOBA_VT_ROWS - MOBA_HEAD_DIM, MOBA_BLOCK), BF16)
    k = k_ref[0]
    km_ref[pl.ds(n, 1), :] = jnp.mean(k, axis=0, keepdims=True)
    lane = lax.broadcasted_iota(jnp.int32, (MOBA_BLOCK, LANES), 1)
    own = lane < MOBA_HEAD_DIM
    onehot = jnp.where(lane == MOBA_HEAD_DIM + n, 1.0, 0.0)
    for pr in range(MOBA_HEADS // 2):
        pair = slice(pr * LANES, (pr + 1) * LANES)
        kp = k[:, pair]
        ka_ref[0, 2 * pr, 0] = jnp.where(own, kp, onehot).astype(BF16)
        ka_ref[0, 2 * pr + 1, 0] = jnp.where(own, pltpu.roll(kp, MOBA_HEAD_DIM, axis=1), onehot).astype(BF16)
        vp_t = v_ref[0, :, pair].astype(F32).T
        for hh in range(2):
            vt_ref[0, 2 * pr + hh, 0, :MOBA_HEAD_DIM, :] = vp_t[hh * MOBA_HEAD_DIM:(hh + 1) * MOBA_HEAD_DIM].astype(BF16)
            vt_ref[0, 2 * pr + hh, 0, MOBA_HEAD_DIM:, :] = ones


def _moba_prep(qk3, v3):
    b, s, w = v3.shape
    nb = s // MOBA_BLOCK
    return pl.pallas_call(
        _moba_prep_kernel,
        grid=(b, nb),
        in_specs=[pl.BlockSpec((1, MOBA_BLOCK, w), lambda bi, n: (bi, n, 0)),
                  pl.BlockSpec((1, MOBA_BLOCK, w), lambda bi, n: (bi, n, 1)),
                  pl.BlockSpec((1, MOBA_BLOCK, w), lambda bi, n: (bi, n, 0))],
        out_specs=[
            pl.BlockSpec((1, MOBA_HEADS, 1, MOBA_BLOCK, LANES), lambda bi, n: (bi, 0, n, 0, 0)),
            pl.BlockSpec((1, MOBA_HEADS, 1, MOBA_VT_ROWS, MOBA_BLOCK), lambda bi, n: (bi, 0, n, 0, 0)),
            pl.BlockSpec((1, MOBA_HEADS, 1, LANES, MOBA_BLOCK), lambda bi, n: (bi, 0, n, 0, 0)),
        ],
        out_shape=[jax.ShapeDtypeStruct((b, MOBA_HEADS, nb, MOBA_BLOCK, LANES), BF16),
                   jax.ShapeDtypeStruct((b, MOBA_HEADS, nb, MOBA_VT_ROWS, MOBA_BLOCK), BF16),
                   jax.ShapeDtypeStruct((b, MOBA_HEADS, nb, LANES, MOBA_BLOCK), BF16)],
        scratch_shapes=[pltpu.VMEM((MOBA_MAX_BLOCKS, w), F32)],
        compiler_params=_params("parallel", "arbitrary"),
        name="moba_prep",
    )(qk3, qk3, v3)


def _moba_select(q_ref, km_ref, qa_ref, i):
    blk = MOBA_BLOCK
    nsel = MOBA_MAX_BLOCKS
    qscale = (MOBA_HEAD_DIM ** -0.5) * LOG2E
    q_t = (q_ref[0] * qscale).T
    kmean = km_ref[...]
    lane = lax.broadcasted_iota(jnp.int32, (nsel, LANES), 1)
    blk_row = lax.broadcasted_iota(jnp.int32, (nsel, blk), 0)
    blk_rowf = blk_row.astype(F32)
    pad = jnp.zeros((LANES - MOBA_HEAD_DIM - nsel, blk), F32)
    for hd in range(MOBA_HEADS):
        pr, hh = hd // 2, hd % 2
        hmask = (lane >= hh * MOBA_HEAD_DIM) & (lane < (hh + 1) * MOBA_HEAD_DIM)
        km_h = jnp.where(hmask, kmean[:, pr * LANES:(pr + 1) * LANES], 0.0)
        gate = jnp.dot(km_h, q_t[pr * LANES:(pr + 1) * LANES], precision=HIGHEST, preferred_element_type=F32)
        g = jnp.where(blk_row < i, gate, -jnp.inf)
        bias = jnp.full((nsel, blk), MASKED, F32)
        for _ in range(MOBA_TOPK):
            m = jnp.max(g, axis=0, keepdims=True)
            hit = (g == m) & (m > -jnp.inf)
            first = jnp.min(jnp.where(hit, blk_rowf, float(nsel)), axis=0, keepdims=True)
            sel = blk_rowf == first
            bias = jnp.where(sel, 0.0, bias)
            g = jnp.where(sel, -jnp.inf, g)
        qs = q_t[hd * MOBA_HEAD_DIM:(hd + 1) * MOBA_HEAD_DIM]
        qa_ref[0, hd, 0] = jnp.concatenate([qs, bias, pad], axis=0).astype(BF16)


def _moba_kernel(qa_ref, qan_ref, ka_ref, vt_ref, o_ref, sa_ref, sb_ref, sc_ref, gma_ref, gmb_ref, gmc_ref, m_ref,
                 acc_ref):
    blk = MOBA_BLOCK
    nblk = ka_ref.shape[2]
    key_pos = lax.broadcasted_iota(jnp.int32, (blk, blk), 0)
    qry_pos = lax.broadcasted_iota(jnp.int32, (blk, blk), 1)
    causal = key_pos <= qry_pos
    feat_row = lax.broadcasted_iota(jnp.int32, (LANES, blk), 0) < MOBA_HEAD_DIM

    def fold(s, op):
        return op(s.reshape(blk // SUBLANES, SUBLANES, blk), axis=0)

    def score_tiles(tiles, s_ref, gm_ref):
        for hh in range(2):
            gmax = None
            for u, (n, q_of, is_own) in enumerate(tiles):
                s = jnp.dot(ka_ref[0, hh, n], q_of(hh), preferred_element_type=F32)
                if is_own:
                    s = jnp.where(causal, s, MASKED)
                s_ref[hh, u] = s
                gmax = fold(s, jnp.max) if gmax is None else jnp.maximum(gmax, fold(s, jnp.max))
            gm_ref[hh] = gmax

    def score_first(q_past, own, s_ref, gm_ref):
        q_own = lambda hh: jnp.where(feat_row, q_past(hh), jnp.zeros((), BF16))
        tiles = [(own, q_own, True)] + [(u - 1, q_past, False) for u in range(1, MOBA_UNROLL)]
        score_tiles(tiles, s_ref, gm_ref)

    def score_group(q_past, grp, s_ref, gm_ref):
        tiles = [(jnp.minimum(grp * MOBA_UNROLL + (u - 1), nblk - 1), q_past, False) for u in range(MOBA_UNROLL)]
        score_tiles(tiles, s_ref, gm_ref)

    def value_group(i, grp, s_ref, gm_ref):
        for hh in range(2):
            m_old = m_ref[hh][0:1]
            m_new = jnp.maximum(m_old, jnp.max(gm_ref[hh], axis=0, keepdims=True))
            alpha = jnp.exp2(m_old - m_new)
            acc = acc_ref[hh] * alpha
            for u in range(MOBA_UNROLL):
                n = jnp.minimum(grp * MOBA_UNROLL + (u - 1), nblk - 1)
                if u == 0:
                    n = jnp.where(grp == 0, i, n)
                p = jnp.exp2(s_ref[hh, u] - m_new).astype(BF16)
                acc = acc + jnp.dot(vt_ref[0, hh, n], p, preferred_element_type=F32)
            m_ref[hh] = jnp.broadcast_to(m_new, (SUBLANES, blk))
            acc_ref[hh] = acc

    first_tile = pl.program_id(2) * MOBA_TILES_PER_STEP

    @pl.when(first_tile == 0)
    def _():
        score_first(lambda hh: qa_ref[0, hh, 0], 0, sc_ref, gmc_ref)

    for sub in range(MOBA_TILES_PER_STEP):
        i = first_tile + sub
        ngroups = (i + MOBA_UNROLL) // MOBA_UNROLL
        q_cur = lambda hh, sub=sub: qa_ref[0, hh, sub]
        if sub + 1 < MOBA_TILES_PER_STEP:
            q_nxt = lambda hh, sub=sub: qa_ref[0, hh, sub + 1]
        else:
            q_nxt = lambda hh: qan_ref[0, hh, 0]
        for hh in range(2):
            m_ref[hh] = jnp.full((SUBLANES, blk), MASKED, F32)
            acc_ref[hh] = jnp.zeros((MOBA_VT_ROWS, blk), F32)

        score_group(q_cur, 1, sa_ref, gma_ref)
        value_group(i, 0, sc_ref, gmc_ref)

        rest = ngroups - 1
        npairs = jnp.maximum(rest - 1, 0) // 2

        def body(t, carry, i=i, q_cur=q_cur):
            score_group(q_cur, 2 * t + 2, sb_ref, gmb_ref)
            value_group(i, 2 * t + 1, sa_ref, gma_ref)
            score_group(q_cur, 2 * t + 3, sa_ref, gma_ref)
            value_group(i, 2 * t + 2, sb_ref, gmb_ref)
            return carry

        lax.fori_loop(0, npairs, body, 0)
        left = rest - 2 * npairs
        own_next = jnp.minimum(i + 1, nblk - 1)

        @pl.when(left == 2)
        def _(i=i, q_cur=q_cur, q_nxt=q_nxt, npairs=npairs, own_next=own_next):
            score_group(q_cur, 2 * npairs + 2, sb_ref, gmb_ref)
            value_group(i, 2 * npairs + 1, sa_ref, gma_ref)
            score_first(q_nxt, own_next, sc_ref, gmc_ref)
            value_group(i, 2 * npairs + 2, sb_ref, gmb_ref)

        @pl.when(left == 1)
        def _(i=i, q_nxt=q_nxt, npairs=npairs, own_next=own_next):
            score_first(q_nxt, own_next, sc_ref, gmc_ref)
            value_group(i, 2 * npairs + 1, sa_ref, gma_ref)

        @pl.when(left == 0)
        def _(q_nxt=q_nxt, own_next=own_next):
            score_first(q_nxt, own_next, sc_ref, gmc_ref)

        outs = [acc_ref[hh, :MOBA_HEAD_DIM, :] / acc_ref[hh, MOBA_HEAD_DIM:MOBA_HEAD_DIM + 1, :] for hh in range(2)]
        o_ref[0, sub * blk:(sub + 1) * blk, :] = jnp.concatenate(outs, axis=0).T.astype(o_ref.dtype)


def _moba_attention(qa, ka, vt):
    b, heads, nq = qa.shape[:3]
    s = nq * MOBA_BLOCK
    pairs = heads // 2
    tps = MOBA_TILES_PER_STEP
    assert nq % tps == 0
    return pl.pallas_call(
        _moba_kernel,
        grid=(b, pairs, nq // tps),
        in_specs=[
            pl.BlockSpec((1, 2, tps, LANES, MOBA_BLOCK), lambda bi, p, j: (bi, p, j, 0, 0)),
            pl.BlockSpec((1, 2, 1, LANES, MOBA_BLOCK),
                         lambda bi, p, j: (bi, p, jnp.minimum((j + 1) * tps, nq - 1), 0, 0)),
            pl.BlockSpec((1, 2, nq, MOBA_BLOCK, LANES), lambda bi, p, j: (bi, p, 0, 0, 0)),
            pl.BlockSpec((1, 2, nq, MOBA_VT_ROWS, MOBA_BLOCK), lambda bi, p, j: (bi, p, 0, 0, 0)),
        ],
        out_specs=pl.BlockSpec((1, tps * MOBA_BLOCK, LANES), lambda bi, p, j: (bi, j, p)),
        out_shape=jax.ShapeDtypeStruct((b, s, heads * MOBA_HEAD_DIM), BF16),
        scratch_shapes=[
            pltpu.VMEM((2, MOBA_UNROLL, MOBA_BLOCK, MOBA_BLOCK), F32),
            pltpu.VMEM((2, MOBA_UNROLL, MOBA_BLOCK, MOBA_BLOCK), F32),
            pltpu.VMEM((2, MOBA_UNROLL, MOBA_BLOCK, MOBA_BLOCK), F32),
            pltpu.VMEM((2, SUBLANES, MOBA_BLOCK), F32),
            pltpu.VMEM((2, SUBLANES, MOBA_BLOCK), F32),
            pltpu.VMEM((2, SUBLANES, MOBA_BLOCK), F32),
            pltpu.VMEM((2, SUBLANES, MOBA_BLOCK), F32),
            pltpu.VMEM((2, MOBA_VT_ROWS, MOBA_BLOCK), F32),
        ],
        compiler_params=_params("parallel", "parallel", "arbitrary"),
        name="moba_attention",
    )(qa, qa, ka, vt)


def _ssd_kernel(xbc_ref, zs_ref, dt_ref, alog_ref, dskip_ref, gn_ref, ex_ref, o_ref, state_ref):
    q = SSD_CHUNK
    d_in = SSD_HEADS * SSD_HEAD_DIM
    bc_w = SSD_GROUPS * SSD_STATE
    gw = d_in // SSD_GROUPS
    c = pl.program_id(1)

    @pl.when(c == 0)
    def _():
        state_ref[...] = jnp.zeros_like(state_ref)

    xs = xbc_ref[0, :, :d_in].astype(F32)
    bm = xbc_ref[0, :, d_in:d_in + bc_w]
    cm = xbc_ref[0, :, d_in + bc_w:]
    dt = dt_ref[0]
    a = dt * (-jnp.exp(alog_ref[...]))
    row = lax.broadcasted_iota(jnp.int32, (q, q), 0)
    col = lax.broadcasted_iota(jnp.int32, (q, q), 1)
    tril = col <= row
    a_cum = jnp.dot(tril.astype(F32), a, precision=HIGHEST, preferred_element_type=F32)
    a_cum_t = a_cum.T
    ex = ex_ref[...]
    dt_x = _split3_dot(dt, ex)
    acum_x = _split3_dot(a_cum, ex)
    alast_x = acum_x[q - 1:q, :]
    x_dt = xs * dt_x
    xd = (x_dt * jnp.exp(alast_x - acum_x)).astype(BF16)
    x_dt_b = x_dt.astype(BF16)
    e_acum = jnp.exp(acum_x)
    e_alast = jnp.exp(alast_x)
    lane2 = lax.broadcasted_iota(jnp.int32, (q, 2 * SSD_HEAD_DIM), 1)
    heads_per_group = SSD_HEADS // SSD_GROUPS

    for g in range(SSD_GROUPS):
        bg = bm[:, g * SSD_STATE:(g + 1) * SSD_STATE]
        cg = cm[:, g * SSD_STATE:(g + 1) * SSD_STATE]
        cb = _nt_dot(cg, bg)
        h_in = state_ref[g]
        y_g = jnp.dot(cg, h_in.astype(BF16), preferred_element_type=F32) * e_acum[:, g * gw:(g + 1) * gw]
        parts = []
        for pr in range(heads_per_group // 2):
            xp = x_dt_b[:, g * gw + pr * 2 * SSD_HEAD_DIM:g * gw + (pr + 1) * 2 * SSD_HEAD_DIM]
            ys = []
            for hh in range(2):
                hd = g * heads_per_group + pr * 2 + hh
                seg = a_cum[:, hd:hd + 1] - a_cum_t[hd:hd + 1, :]
                lmat = jnp.exp(jnp.where(tril, seg, -jnp.inf))
                ys.append(jnp.dot((cb * lmat).astype(BF16), xp, preferred_element_type=F32))
            parts.append(jnp.where(lane2 < SSD_HEAD_DIM, ys[0], ys[1]))
        y_g = y_g + jnp.concatenate(parts, axis=1)
        st = jnp.dot(bg.astype(F32).T.astype(BF16), xd[:, g * gw:(g + 1) * gw], preferred_element_type=F32)
        state_ref[g] = h_in * e_alast[:, g * gw:(g + 1) * gw] + st

        sl = slice(g * gw, (g + 1) * gw)
        y_g = y_g + dskip_ref[:, sl] * xs[:, sl]
        y_g = y_g * zs_ref[0, :, sl].astype(F32)
        r = lax.rsqrt(jnp.mean(y_g * y_g, axis=-1, keepdims=True) + EPS)
        o_ref[0, :, sl] = ((y_g * r) * gn_ref[:, sl]).astype(o_ref.dtype)


def _ssd(xbc3, zs3, dt3, a_log, d_skip, ssd_norm):
    b, s, xw = xbc3.shape
    d_in = SSD_HEADS * SSD_HEAD_DIM
    nc = s // SSD_CHUNK
    pad = LANES - SSD_HEADS
    alog = jnp.pad(a_log.astype(F32), (0, pad)).reshape(1, LANES)
    dskip = jnp.repeat(d_skip.astype(F32), SSD_HEAD_DIM).reshape(1, d_in)
    expand = (jnp.arange(LANES)[:, None] == (jnp.arange(d_in) // SSD_HEAD_DIM)[None, :]).astype(BF16)
    const = lambda shape: pl.BlockSpec(shape, lambda bi, ci: (0,) * len(shape))
    return pl.pallas_call(
        _ssd_kernel,
        grid=(b, nc),
        in_specs=[
            pl.BlockSpec((1, SSD_CHUNK, xw), lambda bi, ci: (bi, ci, 0)),
            pl.BlockSpec((1, SSD_CHUNK, d_in), lambda bi, ci: (bi, ci, 0)),
            pl.BlockSpec((1, SSD_CHUNK, LANES), lambda bi, ci: (bi, ci, 0)),
            const((1, LANES)), const((1, d_in)), const((1, d_in)), const((LANES, d_in)),
        ],
        out_specs=pl.BlockSpec((1, SSD_CHUNK, d_in), lambda bi, ci: (bi, ci, 0)),
        out_shape=jax.ShapeDtypeStruct((b, s, d_in), BF16),
        scratch_shapes=[pltpu.VMEM((SSD_GROUPS, SSD_STATE, d_in // SSD_GROUPS), F32)],
        compiler_params=_params("parallel", "arbitrary"),
        name="ssd_scan",
    )(xbc3, zs3, dt3, alog, dskip, ssd_norm.reshape(1, d_in), expand)


def _memkv_kernel(mem_ref, g_ref, w_ref, kg_ref, km_ref, vm_ref):
    m = mem_ref[0]
    r = lax.rsqrt(jnp.mean(m * m, axis=-1, keepdims=True) + EPS)
    mn = ((m * r) * g_ref[...]).astype(BF16)
    kv = jnp.dot(mn, w_ref[...], preferred_element_type=F32)
    mw = MEM_HEADS * MEM_HEAD_DIM
    for hd in range(MEM_HEADS):
        sl = slice(hd * MEM_HEAD_DIM, (hd + 1) * MEM_HEAD_DIM)
        kh = kv[:, sl]
        rk = lax.rsqrt(jnp.mean(kh * kh, axis=-1, keepdims=True) + EPS)
        km_ref[0, :, sl] = ((kh * rk) * kg_ref[...]).astype(km_ref.dtype)
    vm_ref[0] = kv[:, mw:].astype(vm_ref.dtype)


def _memkv(mem, g_mem, w_kv, k_gain):
    b, m, d = mem.shape
    mw = MEM_HEADS * MEM_HEAD_DIM
    return pl.pallas_call(
        _memkv_kernel,
        grid=(b,),
        in_specs=[
            pl.BlockSpec((1, m, d), lambda bi: (bi, 0, 0)),
            pl.BlockSpec((1, d), lambda bi: (0, 0)),
            pl.BlockSpec((d, 2 * mw), lambda bi: (0, 0)),
            pl.BlockSpec((1, MEM_HEAD_DIM), lambda bi: (0, 0)),
        ],
        out_specs=[pl.BlockSpec((1, m, mw), lambda bi: (bi, 0, 0)), pl.BlockSpec((1, m, mw), lambda bi: (bi, 0, 0))],
        out_shape=[jax.ShapeDtypeStruct((b, m, mw), BF16), jax.ShapeDtypeStruct((b, m, mw), BF16)],
        compiler_params=_params("parallel"),
        name="mem_kv",
    )(mem, g_mem.reshape(1, d), w_kv, k_gain.reshape(1, MEM_HEAD_DIM))


def _memattn_kernel(h_ref, w_ref, qg_ref, km_ref, vm_ref, o_ref):
    qm = jnp.dot(h_ref[...], w_ref[...], preferred_element_type=F32)
    scale = MEM_HEAD_DIM ** -0.5
    for hd in range(MEM_HEADS):
        sl = slice(hd * MEM_HEAD_DIM, (hd + 1) * MEM_HEAD_DIM)
        qh = qm[:, sl]
        r = lax.rsqrt(jnp.mean(qh * qh, axis=-1, keepdims=True) + EPS)
        qn = ((qh * r) * qg_ref[...]).astype(BF16)
        s = _nt_dot(qn, km_ref[0, :, sl]) * scale
        p = jnp.exp(s - jnp.max(s, axis=-1, keepdims=True))
        l = jnp.sum(p, axis=-1, keepdims=True)
        o = jnp.dot(p.astype(BF16), vm_ref[0, :, sl], preferred_element_type=F32)
        o_ref[:, sl] = (o / l).astype(o_ref.dtype)


def _memattn(h, w_qm, q_gain, km, vm, seq, tm=1024):
    n, d = h.shape
    b, m, mw = km.shape
    tiles_per_seq = seq // tm
    return pl.pallas_call(
        _memattn_kernel,
        grid=(n // tm,),
        in_specs=[
            pl.BlockSpec((tm, d), lambda i: (i, 0)),
            pl.BlockSpec((d, mw), lambda i: (0, 0)),
            pl.BlockSpec((1, MEM_HEAD_DIM), lambda i: (0, 0)),
            pl.BlockSpec((1, m, mw), lambda i: (i // tiles_per_seq, 0, 0)),
            pl.BlockSpec((1, m, mw), lambda i: (i // tiles_per_seq, 0, 0)),
        ],
        out_specs=pl.BlockSpec((tm, mw), lambda i: (i, 0)),
        out_shape=jax.ShapeDtypeStruct((n, mw), BF16),
        compiler_params=_params("parallel"),
        name="mem_attention",
    )(h, w_qm, q_gain.reshape(1, MEM_HEAD_DIM), km, vm)


def _merge_kernel(x_ref, h_ref, oa_ref, os_ref, om_ref, wg_ref, wa_ref, ws_ref, wm_ref, wo_ref, gf_ref, wr_ref,
                  x1_ref, h2_ref, route_ref, cnt_ref):
    d = x_ref.shape[1]
    gates = _sigmoid(jnp.dot(h_ref[...], wg_ref[...], preferred_element_type=F32))
    merged = gates[:, :d] * jnp.dot(oa_ref[...], wa_ref[...], preferred_element_type=F32)
    merged = merged + gates[:, d:2 * d] * jnp.dot(os_ref[...], ws_ref[...], preferred_element_type=F32)
    merged = merged + gates[:, 2 * d:] * jnp.dot(om_ref[...], wm_ref[...], preferred_element_type=F32)
    x1 = x_ref[...] + jnp.dot(merged.astype(BF16), wo_ref[...], preferred_element_type=F32)
    x1_ref[...] = x1
    r = lax.rsqrt(jnp.mean(x1 * x1, axis=-1, keepdims=True) + EPS)
    h2 = (x1 * r) * gf_ref[...]
    h2_ref[...] = _rows_to_token_tiles(h2)

    h2_hi = h2.astype(BF16)
    h2_lo = (h2 - h2_hi.astype(F32)).astype(BF16)
    lg = (jnp.dot(h2_hi, wr_ref[0], preferred_element_type=F32) + jnp.dot(h2_hi, wr_ref[1], preferred_element_type=F32)
          + jnp.dot(h2_lo, wr_ref[0], preferred_element_type=F32))
    lanef = lax.broadcasted_iota(jnp.int32, lg.shape, 1).astype(F32)
    big = float(LANES)
    gmask = lanef < MOE_GROUPS
    gl = jnp.where(gmask, lg, -jnp.inf)
    gmax = jnp.max(gl, axis=-1, keepdims=True)
    p_g = 1.0 / jnp.sum(jnp.exp(gl - gmax), axis=-1, keepdims=True)
    g_sel = jnp.min(jnp.where(gl == gmax, lanef, big), axis=-1, keepdims=True)
    lo = MOE_GROUPS + MOE_EXPERTS_PER_GROUP * g_sel
    el = jnp.where((lanef >= lo) & (lanef < lo + MOE_EXPERTS_PER_GROUP), lg, -jnp.inf)
    m1 = jnp.max(el, axis=-1, keepdims=True)
    i1 = jnp.min(jnp.where(el == m1, lanef, big), axis=-1, keepdims=True)
    el2 = jnp.where(lanef == i1, -jnp.inf, el)
    m2 = jnp.max(el2, axis=-1, keepdims=True)
    i2 = jnp.min(jnp.where(el2 == m2, lanef, big), axis=-1, keepdims=True)
    e2 = jnp.exp(m2 - m1)
    w1 = 1.0 / (1.0 + e2)
    w2 = e2 / (1.0 + e2)
    route = jnp.where(lanef == 0, i1 - MOE_GROUPS, 0.0)
    route = jnp.where(lanef == 1, i2 - MOE_GROUPS, route)
    route = jnp.where(lanef == 2, p_g * w1, route)
    route = jnp.where(lanef == 3, p_g * w2, route)
    route_ref[...] = route

    @pl.when(pl.program_id(0) == 0)
    def _():
        cnt_ref[...] = jnp.zeros_like(cnt_ref)

    hits = jnp.where(lanef == i1 - MOE_GROUPS, 1.0, 0.0) + jnp.where(lanef == i2 - MOE_GROUPS, 1.0, 0.0)
    cnt_ref[...] += jnp.sum(hits.reshape(hits.shape[0] // SUBLANES, SUBLANES, LANES), axis=0)


def _merge(x2, h, o_a, o_s, o_m, w_gates, w_a, w_s, w_m, w_out, g_ffn, w_router, tm=512):
    n, d = x2.shape
    full = lambda arr: pl.BlockSpec(arr.shape, lambda i: (0,) * arr.ndim)
    tile = lambda arr: pl.BlockSpec((tm, arr.shape[1]), lambda i: (i, 0))
    gf = g_ffn.reshape(1, d)
    return pl.pallas_call(
        _merge_kernel,
        grid=(n // tm,),
        in_specs=[tile(x2), tile(h), tile(o_a), tile(o_s), tile(o_m), full(w_gates), full(w_a), full(w_s), full(w_m),
                  full(w_out), full(gf), full(w_router)],
        out_specs=[pl.BlockSpec((tm, d), lambda i: (i, 0)),
                   pl.BlockSpec((tm, d // LANES, LANES), lambda i: (i, 0, 0)),
                   pl.BlockSpec((tm, LANES), lambda i: (i, 0)),
                   pl.BlockSpec((SUBLANES, LANES), lambda i: (0, 0))],
        out_shape=[jax.ShapeDtypeStruct((n, d), F32), jax.ShapeDtypeStruct((n, d // LANES, LANES), F32),
                   jax.ShapeDtypeStruct((n, LANES), F32), jax.ShapeDtypeStruct((SUBLANES, LANES), F32)],
        compiler_params=_params("arbitrary"),
        name="merge_router",
    )(x2, h, o_a, o_s, o_m, w_gates, w_a, w_s, w_m, w_out, gf, w_router)


def _moe_kernel(be_ref, j0_ref, nv_ref, tok_ref, dst_ref, h2_hbm, wg_ref, wu_ref, wd_ref, y2_hbm, xbuf, ybuf, gsem,
                ssem, wgb, wub, wdb):
    rows = xbuf.shape[1]
    n_tok = h2_hbm.shape[0]
    nblocks = pl.num_programs(0)
    i = pl.program_id(0)
    nbuf = xbuf.shape[0]
    slot = lax.rem(i, nbuf)
    slot_next = lax.rem(i + 1, nbuf)
    slot_prev = lax.rem(i + 2, nbuf)
    dump0 = MOE_TOPK * n_tok

    def block_rows(blk):
        b = jnp.clip(blk, 0, nblocks - 1)
        return j0_ref[b], jnp.where(blk < 0, -1, nv_ref[b] - 1)

    n_assign = tok_ref.shape[0]

    def gather_row(span, sl, r, prio):
        j0, _ = span
        tok = tok_ref[jnp.minimum(j0 + r, n_assign - 1)]
        pltpu.make_async_copy(h2_hbm.at[pl.ds(tok, 1)], xbuf.at[sl, pl.ds(r, 1)], gsem.at[sl]).start(priority=prio)

    def scatter_row(span, sl, r, prio):
        j0, last = span
        dst = jnp.where(r <= last, dst_ref[jnp.minimum(j0 + r, n_assign - 1)], dump0 + sl * rows + r)
        pltpu.make_async_copy(ybuf.at[sl, pl.ds(r, 1)], y2_hbm.at[pl.ds(dst, 1)], ssem.at[sl]).start(priority=prio)

    def looped(fn, blk, sl):
        span = block_rows(blk)

        def body(c, carry):
            for k in range(MOE_DMA_UNROLL):
                fn(span, sl, c * MOE_DMA_UNROLL + k, k % 2)
            return carry
        lax.fori_loop(0, rows // MOE_DMA_UNROLL, body, 0)

    def gather_wait(sl):
        pltpu.make_async_copy(h2_hbm.at[pl.ds(0, rows)], xbuf.at[sl], gsem.at[sl]).wait()

    def scatter_wait(sl):
        pltpu.make_async_copy(ybuf.at[sl], y2_hbm.at[pl.ds(0, rows)], ssem.at[sl]).wait()

    @pl.when(i == 0)
    def _():
        looped(gather_row, 0, 0)
        looped(gather_row, 1, 1)
        ybuf[...] = jnp.zeros_like(ybuf)
        for sl in range(2):
            pltpu.make_async_copy(ybuf.at[sl], y2_hbm.at[pl.ds(dump0 + sl * rows, rows)], ssem.at[sl]).start()

    @pl.when((i == 0) | (be_ref[i] != be_ref[jnp.maximum(i - 1, 0)]))
    def _():
        wgb[...] = wg_ref[0].astype(BF16)
        wub[...] = wu_ref[0].astype(BF16)
        wdb[...] = wd_ref[0].astype(BF16)

    gather_wait(slot)
    scatter_wait(slot)
    nxt, prv = block_rows(i + 2), block_rows(i - 1)
    for r in range(rows):
        scatter_row(prv, slot_prev, r, r % 2)
    x = _token_tiles_to_rows(xbuf[slot]).astype(BF16)
    for r in range(rows):
        gather_row(nxt, slot_prev, r, r % 2)
    gate = jnp.dot(x, wgb[...], preferred_element_type=F32)
    up = jnp.dot(x, wub[...], preferred_element_type=F32)
    hid = (gate * _sigmoid(gate)) * up
    y = jnp.dot(hid.astype(BF16), wdb[...], preferred_element_type=F32)
    ybuf[slot] = _rows_to_token_tiles(y)

    @pl.when(i == nblocks - 1)
    def _():
        gather_wait(slot_next)
        gather_wait(slot_prev)
        scatter_wait(slot_next)
        scatter_wait(slot_prev)
        looped(scatter_row, i, slot)
        scatter_wait(slot)


def _moe_experts(h2, tables, w_gate, w_up, w_down):
    n = h2.shape[0]
    e, d, f = w_gate.shape
    nblocks = tables[0].shape[0]
    wmap = lambda i, be, j0, nv, tok, dst: (be[i], 0, 0)
    grid_spec = pltpu.PrefetchScalarGridSpec(
        num_scalar_prefetch=5,
        grid=(nblocks,),
        in_specs=[
            pl.BlockSpec(memory_space=pl.ANY),
            pl.BlockSpec((1, d, f), wmap),
            pl.BlockSpec((1, d, f), wmap),
            pl.BlockSpec((1, f, d), wmap),
        ],
        out_specs=pl.BlockSpec(memory_space=pl.ANY),
        scratch_shapes=[
            pltpu.VMEM((MOE_BUFFERS, MOE_ROWS, d // LANES, LANES), F32),
            pltpu.VMEM((MOE_BUFFERS, MOE_ROWS, d // LANES, LANES), F32),
            pltpu.SemaphoreType.DMA((MOE_BUFFERS,)),
            pltpu.SemaphoreType.DMA((MOE_BUFFERS,)),
            pltpu.VMEM((d, f), BF16), pltpu.VMEM((d, f), BF16), pltpu.VMEM((f, d), BF16),
        ],
    )
    return pl.pallas_call(
        _moe_kernel,
        grid_spec=grid_spec,
        out_shape=jax.ShapeDtypeStruct((n * MOE_TOPK + MOE_BUFFERS * MOE_ROWS, d // LANES, LANES), F32),
        compiler_params=_params("arbitrary"),
        name="moe_experts",
    )(*tables, h2, w_gate, w_up, w_down)


def _combine_kernel(x1_ref, y0_ref, y1_ref, route_ref, o_ref):
    route = route_ref[...]
    w0 = route[:, 2:3]
    w1 = route[:, 3:4]
    o_ref[...] = x1_ref[...] + (w0 * _token_tiles_to_rows(y0_ref[...]) + w1 * _token_tiles_to_rows(y1_ref[...]))


def _combine(x1, y2, route, tm=512):
    n, d = x1.shape
    tiles = n // tm
    ytile = (tm, d // LANES, LANES)
    return pl.pallas_call(
        _combine_kernel,
        grid=(tiles,),
        in_specs=[pl.BlockSpec((tm, d), lambda i: (i, 0)), pl.BlockSpec(ytile, lambda i: (i, 0, 0)),
                  pl.BlockSpec(ytile, lambda i: (tiles + i, 0, 0)), pl.BlockSpec((tm, LANES), lambda i: (i, 0))],
        out_specs=pl.BlockSpec((tm, d), lambda i: (i, 0)),
        out_shape=jax.ShapeDtypeStruct((n, d), F32),
        compiler_params=_params("parallel"),
        name="moe_combine",
    )(x1, y2, y2, route)


def _dispatch_tables(route, expert_counts, n_tok):
    n_assign = n_tok * MOE_TOPK
    e_flat = route[:, :MOE_TOPK].astype(jnp.int32).reshape(n_assign)
    order = jnp.argsort(e_flat).astype(jnp.int32)
    counts = jnp.sum(expert_counts, axis=0)[:MOE_EXPERTS].astype(jnp.int32)
    blocks_per_expert = (counts + MOE_ROWS - 1) // MOE_ROWS
    blk_end = jnp.cumsum(blocks_per_expert)
    raw_start = jnp.cumsum(counts) - counts
    nblocks = n_assign // MOE_ROWS + MOE_EXPERTS
    blk = jnp.arange(nblocks, dtype=jnp.int32)
    blk_expert = jnp.minimum(jnp.sum(blk_end[None, :] <= blk[:, None], axis=1), MOE_EXPERTS - 1).astype(jnp.int32)
    within = blk - (blk_end - blocks_per_expert)[blk_expert]
    blk_nvalid = jnp.clip(counts[blk_expert] - within * MOE_ROWS, 0, MOE_ROWS).astype(jnp.int32)
    blk_j0 = jnp.where(blk_nvalid > 0, raw_start[blk_expert] + within * MOE_ROWS, 0).astype(jnp.int32)
    tok_sorted = order // MOE_TOPK
    dst_sorted = (order % MOE_TOPK) * n_tok + tok_sorted
    return blk_expert, blk_j0, blk_nvalid, tok_sorted, dst_sorted


def kernel(x, mem, g_mix, w_in, moba_q_norm, moba_k_norm, conv_w, conv_b, dt_bias, a_log, d_skip, ssd_norm, g_mem,
           w_mem_kv, mem_q_norm, mem_k_norm, w_o_moba, w_o_ssd, w_o_mem, w_out, g_ffn, w_router_group,
           w_router_expert, w_gate, w_up, w_down):
    bsz, seq, d = x.shape
    assert seq % MOBA_BLOCK == 0 and seq % SSD_CHUNK == 0
    n_tok = bsz * seq
    moba_w = MOBA_HEADS * MOBA_HEAD_DIM
    d_in = SSD_HEADS * SSD_HEAD_DIM
    xbc_w = d_in + 2 * SSD_GROUPS * SSD_STATE
    mem_w = MEM_HEADS * MEM_HEAD_DIM
    sizes = (moba_w, moba_w, moba_w, d_in, xbc_w, SSD_HEADS, mem_w, 3 * d)
    offs = [0]
    for sz in sizes:
        offs.append(offs[-1] + sz)
    w_in_b = w_in.astype(BF16)
    w_qk = w_in_b[:, offs[0]:offs[2]]
    w_v = w_in_b[:, offs[2]:offs[3]]
    w_z = w_in_b[:, offs[3]:offs[4]]
    w_xbc = w_in_b[:, offs[4]:offs[5]]
    w_dt = jnp.pad(w_in_b[:, offs[5]:offs[6]], ((0, 0), (0, LANES - SSD_HEADS)))
    w_qm = w_in_b[:, offs[6]:offs[7]]
    w_gates = w_in_b[:, offs[7]:offs[8]]

    x2 = x.reshape(n_tok, d)

    half = MOBA_HEAD_DIM // 2
    inv = ROPE_THETA ** (-jnp.arange(half, dtype=F32) / half)
    ang = jnp.arange(seq, dtype=F32)[:, None] * inv[None, :]
    cos_t = jnp.tile(jnp.cos(ang), (1, LANES // half))
    sin_t = jnp.tile(jnp.concatenate([-jnp.sin(ang), jnp.sin(ang)], axis=1), (1, LANES // MOBA_HEAD_DIM))
    gains = jnp.stack([jnp.tile(moba_q_norm, MOBA_HEADS), jnp.tile(moba_k_norm, MOBA_HEADS)]).reshape(2, 1, moba_w)

    h, qk, v, zs, xbc, dt = _inproj(x2, g_mix, w_qk, w_v, w_z, w_xbc, w_dt, gains, cos_t, sin_t, conv_w, conv_b,
                                    dt_bias, seq)

    qk3 = qk.reshape(bsz, seq, 2 * moba_w)
    v3 = v.reshape(bsz, seq, moba_w)
    nb = seq // MOBA_BLOCK
    assert nb <= MOBA_MAX_BLOCKS and nb % MOBA_UNROLL == 0
    ka, vt, qa = _moba_prep(qk3, v3)
    o_a = _moba_attention(qa, ka, vt).reshape(n_tok, moba_w)

    o_s = _ssd(xbc.reshape(bsz, seq, xbc_w), zs.reshape(bsz, seq, d_in), dt.reshape(bsz, seq, LANES), a_log, d_skip,
               ssd_norm).reshape(n_tok, d_in)

    km, vm = _memkv(mem, g_mem, w_mem_kv.astype(BF16), mem_k_norm)
    o_m = _memattn(h, w_qm, mem_q_norm, km, vm, seq)

    w_router = jnp.pad(jnp.concatenate([w_router_group, w_router_expert], axis=1),
                       ((0, 0), (0, LANES - MOE_GROUPS - MOE_EXPERTS)))
    w_router_hi = w_router.astype(BF16)
    w_router = jnp.stack([w_router_hi, (w_router - w_router_hi.astype(F32)).astype(BF16)])
    x1, h2, route, expert_counts = _merge(x2, h, o_a, o_s, o_m, w_gates, w_o_moba.astype(BF16),
                                          w_o_ssd.astype(BF16), w_o_mem.astype(BF16), w_out.astype(BF16), g_ffn,
                                          w_router)

    tables = _dispatch_tables(route, expert_counts, n_tok)
    y2 = _moe_experts(h2, tables, w_gate, w_up, w_down)
    out = _combine(x1, y2, route)
    return out.reshape(bsz, seq, d)
```

```python
import functools

import jax
import jax.numpy as jnp
from jax import lax
from jax.experimental import pallas as pl
from jax.experimental.pallas import tpu as pltpu

F32 = jnp.float32
BF16 = jnp.bfloat16
HIGHEST = lax.Precision.HIGHEST

EPS = 1e-6
ROPE_THETA = 10000.0
MOBA_HEADS = 8
MOBA_HEAD_DIM = 64
MOBA_BLOCK = 256
MOBA_TOPK = 3
SSD_HEAD_DIM = 64
SSD_HEADS = 16
SSD_GROUPS = 4
SSD_STATE = 128
SSD_CONV = 4
SSD_CHUNK = 256
MEM_HEADS = 4
MEM_HEAD_DIM = 128
MOE_GROUPS = 4
MOE_EXPERTS_PER_GROUP = 8
MOE_EXPERTS = MOE_GROUPS * MOE_EXPERTS_PER_GROUP
MOE_TOPK = 2

LANES = 128
SUBLANES = 8
LOG2E = 1.4426950408889634
MASKED = -1e30
MOBA_MAX_BLOCKS = 32
MOBA_UNROLL = 2
MOBA_TILES_PER_STEP = 8
MOBA_VT_ROWS = MOBA_HEAD_DIM + 16
MOE_ROWS = 256
INPROJ_CONV_CHUNK = 512
MOE_BUFFERS = 3
MOE_DMA_UNROLL = 8
VMEM_LIMIT = 56 * 1024 * 1024


def _params(*sem):
    return pltpu.CompilerParams(dimension_semantics=sem, vmem_limit_bytes=VMEM_LIMIT)


def _sigmoid(x):
    return 1.0 / (1.0 + jnp.exp(-x))


def _split3_dot(x, sel):
    hi = x.astype(BF16)
    rest = x - hi.astype(F32)
    mid = rest.astype(BF16)
    lo = (rest - mid.astype(F32)).astype(BF16)
    return (jnp.dot(hi, sel, preferred_element_type=F32) + jnp.dot(mid, sel, preferred_element_type=F32)
            + jnp.dot(lo, sel, preferred_element_type=F32))


def _rows_to_token_tiles(x):
    rows, width = x.shape
    chunks = width // LANES
    x4 = jnp.stack([x[:, k * LANES:(k + 1) * LANES].reshape(rows // SUBLANES, SUBLANES, LANES)
                    for k in range(chunks)], axis=1)
    return jnp.swapaxes(x4, 1, 2).reshape(rows, chunks, LANES)


def _token_tiles_to_rows(x3):
    rows, chunks, _ = x3.shape
    x4 = jnp.swapaxes(x3.reshape(rows // SUBLANES, SUBLANES, chunks, LANES), 1, 2)
    return jnp.concatenate([x4[:, k].reshape(rows, LANES) for k in range(chunks)], axis=1)


def _nt_dot(a, b, precision=None):
    return lax.dot_general(a, b, (((1,), (1,)), ((), ())), precision=precision, preferred_element_type=F32)


def _softplus(x):
    return jnp.maximum(x, 0.0) + jnp.log1p(jnp.exp(-jnp.abs(x)))


def _inproj_kernel(x_ref, gmix_ref, wqk_ref, wv_ref, wz_ref, wxbc_ref, wdt_ref, gain_ref, gsum_ref, cos_ref, sin_ref,
                   cw_ref, cb_ref, dtb_ref, h_ref, qk_ref, v_ref, zs_ref, xbc_ref, dt_ref, *ext_refs,
                   tiles_per_seq):
    tm = x_ref.shape[0]
    x = x_ref[...]
    r = lax.rsqrt(jnp.mean(x * x, axis=-1, keepdims=True) + EPS)
    h = ((x * r) * gmix_ref[...]).astype(BF16)
    h_ref[...] = h

    width = gsum_ref.shape[0]
    half = MOBA_HEAD_DIM // 2
    lane = lax.broadcasted_iota(jnp.int32, (tm, width), 1)
    first = (lane & (MOBA_HEAD_DIM - 1)) < half
    reps = width // cos_ref.shape[1]
    cos = jnp.concatenate([cos_ref[...]] * reps, axis=1)
    sin = jnp.concatenate([sin_ref[...]] * reps, axis=1)
    for j in range(2):
        u = jnp.dot(h, wqk_ref[:, j * width:(j + 1) * width], preferred_element_type=F32)
        sq = u * u
        hi = sq.astype(BF16)
        lo = (sq - hi.astype(F32)).astype(BF16)
        ss = (jnp.dot(hi, gsum_ref[...], preferred_element_type=F32)
              + jnp.dot(lo, gsum_ref[...], preferred_element_type=F32))
        un = (u * lax.rsqrt(ss * (1.0 / MOBA_HEAD_DIM) + EPS)) * gain_ref[j]
        partner = jnp.where(first, pltpu.roll(un, width - half, axis=1), pltpu.roll(un, half, axis=1))
        qk_ref[:, j * width:(j + 1) * width] = un * cos + partner * sin

    v_ref[...] = jnp.dot(h, wv_ref[...], preferred_element_type=F32).astype(v_ref.dtype)
    z = jnp.dot(h, wz_ref[...], preferred_element_type=F32)
    zs_ref[...] = (z * _sigmoid(z)).astype(zs_ref.dtype)
    dt_ref[...] = _softplus(jnp.dot(h, wdt_ref[...], preferred_element_type=F32) + dtb_ref[...])

    halo = SUBLANES

    @pl.when(pl.program_id(0) % tiles_per_seq == 0)
    def _():
        for ext_ref in ext_refs:
            ext_ref[0:halo, :] = jnp.zeros((halo, ext_ref.shape[1]), F32)

    cchunk = ext_refs[0].shape[1]
    for ci, ext_ref in enumerate(ext_refs):
        cs = slice(ci * cchunk, (ci + 1) * cchunk)
        u = jnp.dot(h, wxbc_ref[:, cs], preferred_element_type=F32)
        ext_ref[halo:halo + tm, :] = u
        acc = cb_ref[:, cs] + cw_ref[SSD_CONV - 1:SSD_CONV, cs] * u
        for kk in range(SSD_CONV - 1):
            off = halo - (SSD_CONV - 1 - kk)
            acc = acc + cw_ref[kk:kk + 1, cs] * ext_ref[off:off + tm, :]
        ext_ref[0:halo, :] = u[tm - halo:tm, :]
        xbc_ref[:, cs] = (acc * _sigmoid(acc)).astype(xbc_ref.dtype)


def _inproj(x2, g_mix, w_qk, w_v, w_z, w_xbc, w_dt, gains, cos_t, sin_t, conv_w, conv_b, dt_bias, seq, tm=512):
    n, d = x2.shape
    width = w_qk.shape[1] // 2
    tiles_per_seq = seq // tm
    head = jnp.arange(width) // MOBA_HEAD_DIM
    gsum = (head[:, None] == head[None, :]).astype(BF16)
    dtb = jnp.pad(dt_bias.astype(F32), (0, LANES - dt_bias.shape[0])).reshape(1, LANES)
    full = lambda arr: pl.BlockSpec(arr.shape, lambda i: (0,) * arr.ndim)
    tile = lambda cols: pl.BlockSpec((tm, cols), lambda i: (i, 0))
    rope = pl.BlockSpec((tm, LANES), lambda i: (i % tiles_per_seq, 0))
    consts = (g_mix.reshape(1, d), w_qk, w_v, w_z, w_xbc, w_dt, gains, gsum)
    tail = (conv_w, conv_b.reshape(1, -1), dtb)
    outs = ((d, BF16), (2 * width, F32), (w_v.shape[1], BF16), (w_z.shape[1], BF16), (w_xbc.shape[1], BF16),
            (LANES, F32))
    return pl.pallas_call(
        functools.partial(_inproj_kernel, tiles_per_seq=tiles_per_seq),
        grid=(n // tm,),
        in_specs=[tile(d)] + [full(a) for a in consts] + [rope, rope] + [full(a) for a in tail],
        out_specs=[tile(c) for c, _ in outs],
        out_shape=[jax.ShapeDtypeStruct((n, c), dt) for c, dt in outs],
        scratch_shapes=[pltpu.VMEM((tm + SUBLANES, INPROJ_CONV_CHUNK), F32)] * (w_xbc.shape[1] // INPROJ_CONV_CHUNK),
        compiler_params=_params("arbitrary"),
        name="in_proj",
    )(x2, *consts, cos_t, sin_t, *tail)


def _moba_prep_kernel(q_ref, k_ref, v_ref, ka_ref, vt_ref, qa_ref, km_ref):
    n = pl.program_id(1)

    @pl.when(n == 0)
    def _():
        km_ref[...] = jnp.zeros_like(km_ref)

    _moba_select(q_ref, km_ref, qa_ref, n)

    ones = jnp.ones((MOBA_VT_ROWS - MOBA_HEAD_DIM, MOBA_BLOCK), BF16)
    k = k_ref[0]
    km_ref[pl.ds(n, 1), :] = jnp.mean(k, axis=0, keepdims=True)
    lane = lax.broadcasted_iota(jnp.int32, (MOBA_BLOCK, LANES), 1)
    own = lane < MOBA_HEAD_DIM
    onehot = jnp.where(lane == MOBA_HEAD_DIM + n, 1.0, 0.0)
    for pr in range(MOBA_HEADS // 2):
        pair = slice(pr * LANES, (pr + 1) * LANES)
        kp = k[:, pair]
        ka_ref[0, 2 * pr, 0] = jnp.where(own, kp, onehot).astype(BF16)
        ka_ref[0, 2 * pr + 1, 0] = jnp.where(own, pltpu.roll(kp, MOBA_HEAD_DIM, axis=1), onehot).astype(BF16)
        vp_t = v_ref[0, :, pair].astype(F32).T
        for hh in range(2):
            vt_ref[0, 2 * pr + hh, 0, :MOBA_HEAD_DIM, :] = vp_t[hh * MOBA_HEAD_DIM:(hh + 1) * MOBA_HEAD_DIM].astype(BF16)
            vt_ref[0, 2 * pr + hh, 0, MOBA_HEAD_DIM:, :] = ones


def _moba_prep(qk3, v3):
    b, s, w = v3.shape
    nb = s // MOBA_BLOCK
    return pl.pallas_call(
        _moba_prep_kernel,
        grid=(b, nb),
        in_specs=[pl.BlockSpec((1, MOBA_BLOCK, w), lambda bi, n: (bi, n, 0)),
                  pl.BlockSpec((1, MOBA_BLOCK, w), lambda bi, n: (bi, n, 1)),
                  pl.BlockSpec((1, MOBA_BLOCK, w), lambda bi, n: (bi, n, 0))],
        out_specs=[
            pl.BlockSpec((1, MOBA_HEADS, 1, MOBA_BLOCK, LANES), lambda bi, n: (bi, 0, n, 0, 0)),
            pl.BlockSpec((1, MOBA_HEADS, 1, MOBA_VT_ROWS, MOBA_BLOCK), lambda bi, n: (bi, 0, n, 0, 0)),
            pl.BlockSpec((1, MOBA_HEADS, 1, LANES, MOBA_BLOCK), lambda bi, n: (bi, 0, n, 0, 0)),
        ],
        out_shape=[jax.ShapeDtypeStruct((b, MOBA_HEADS, nb, MOBA_BLOCK, LANES), BF16),
                   jax.ShapeDtypeStruct((b, MOBA_HEADS, nb, MOBA_VT_ROWS, MOBA_BLOCK), BF16),
                   jax.ShapeDtypeStruct((b, MOBA_HEADS, nb, LANES, MOBA_BLOCK), BF16)],
        scratch_shapes=[pltpu.VMEM((MOBA_MAX_BLOCKS, w), F32)],
        compiler_params=_params("parallel", "arbitrary"),
        name="moba_prep",
    )(qk3, qk3, v3)


def _moba_select(q_ref, km_ref, qa_ref, i):
    blk = MOBA_BLOCK
    nsel = MOBA_MAX_BLOCKS
    qscale = (MOBA_HEAD_DIM ** -0.5) * LOG2E
    q_t = (q_ref[0] * qscale).T
    kmean = km_ref[...]
    lane = lax.broadcasted_iota(jnp.int32, (nsel, LANES), 1)
    blk_row = lax.broadcasted_iota(jnp.int32, (nsel, blk), 0)
    blk_rowf = blk_row.astype(F32)
    pad = jnp.zeros((LANES - MOBA_HEAD_DIM - nsel, blk), F32)
    for hd in range(MOBA_HEADS):
        pr, hh = hd // 2, hd % 2
        hmask = (lane >= hh * MOBA_HEAD_DIM) & (lane < (hh + 1) * MOBA_HEAD_DIM)
        km_h = jnp.where(hmask, kmean[:, pr * LANES:(pr + 1) * LANES], 0.0)
        gate = jnp.dot(km_h, q_t[pr * LANES:(pr + 1) * LANES], precision=HIGHEST, preferred_element_type=F32)
        g = jnp.where(blk_row < i, gate, -jnp.inf)
        bias = jnp.full((nsel, blk), MASKED, F32)
        for _ in range(MOBA_TOPK):
            m = jnp.max(g, axis=0, keepdims=True)
            hit = (g == m) & (m > -jnp.inf)
            first = jnp.min(jnp.where(hit, blk_rowf, float(nsel)), axis=0, keepdims=True)
            sel = blk_rowf == first
            bias = jnp.where(sel, 0.0, bias)
            g = jnp.where(sel, -jnp.inf, g)
        qs = q_t[hd * MOBA_HEAD_DIM:(hd + 1) * MOBA_HEAD_DIM]
        qa_ref[0, hd, 0] = jnp.concatenate([qs, bias, pad], axis=0).astype(BF16)


def _moba_kernel(qa_ref, qan_ref, ka_ref, vt_ref, o_ref, sa_ref, sb_ref, sc_ref, gma_ref, gmb_ref, gmc_ref, m_ref,
                 acc_ref):
    blk = MOBA_BLOCK
    nblk = ka_ref.shape[2]
    key_pos = lax.broadcasted_iota(jnp.int32, (blk, blk), 0)
    qry_pos = lax.broadcasted_iota(jnp.int32, (blk, blk), 1)
    causal = key_pos <= qry_pos
    feat_row = lax.broadcasted_iota(jnp.int32, (LANES, blk), 0) < MOBA_HEAD_DIM

    def fold(s, op):
        return op(s.reshape(blk // SUBLANES, SUBLANES, blk), axis=0)

    def score_tiles(tiles, s_ref, gm_ref):
        for hh in range(2):
            gmax = None
            for u, (n, q_of, is_own) in enumerate(tiles):
                s = jnp.dot(ka_ref[0, hh, n], q_of(hh), preferred_element_type=F32)
                if is_own:
                    s = jnp.where(causal, s, MASKED)
                s_ref[hh, u] = s
                gmax = fold(s, jnp.max) if gmax is None else jnp.maximum(gmax, fold(s, jnp.max))
            gm_ref[hh] = gmax

    def score_first(q_past, own, s_ref, gm_ref):
        q_own = lambda hh: jnp.where(feat_row, q_past(hh), jnp.zeros((), BF16))
        tiles = [(own, q_own, True)] + [(u - 1, q_past, False) for u in range(1, MOBA_UNROLL)]
        score_tiles(tiles, s_ref, gm_ref)

    def score_group(q_past, grp, s_ref, gm_ref):
        tiles = [(jnp.minimum(grp * MOBA_UNROLL + (u - 1), nblk - 1), q_past, False) for u in range(MOBA_UNROLL)]
        score_tiles(tiles, s_ref, gm_ref)

    def value_group(i, grp, s_ref, gm_ref):
        for hh in range(2):
            m_old = m_ref[hh][0:1]
            m_new = jnp.maximum(m_old, jnp.max(gm_ref[hh], axis=0, keepdims=True))
            alpha = jnp.exp2(m_old - m_new)
            acc = acc_ref[hh] * alpha
            for u in range(MOBA_UNROLL):
                n = jnp.minimum(grp * MOBA_UNROLL + (u - 1), nblk - 1)
                if u == 0:
                    n = jnp.where(grp == 0, i, n)
                p = jnp.exp2(s_ref[hh, u] - m_new).astype(BF16)
                acc = acc + jnp.dot(vt_ref[0, hh, n], p, preferred_element_type=F32)
            m_ref[hh] = jnp.broadcast_to(m_new, (SUBLANES, blk))
            acc_ref[hh] = acc

    first_tile = pl.program_id(2) * MOBA_TILES_PER_STEP

    @pl.when(first_tile == 0)
    def _():
        score_first(lambda hh: qa_ref[0, hh, 0], 0, sc_ref, gmc_ref)

    for sub in range(MOBA_TILES_PER_STEP):
        i = first_tile + sub
        ngroups = (i + MOBA_UNROLL) // MOBA_UNROLL
        q_cur = lambda hh, sub=sub: qa_ref[0, hh, sub]
        if sub + 1 < MOBA_TILES_PER_STEP:
            q_nxt = lambda hh, sub=sub: qa_ref[0, hh, sub + 1]
        else:
            q_nxt = lambda hh: qan_ref[0, hh, 0]
        for hh in range(2):
            m_ref[hh] = jnp.full((SUBLANES, blk), MASKED, F32)
            acc_ref[hh] = jnp.zeros((MOBA_VT_ROWS, blk), F32)

        score_group(q_cur, 1, sa_ref, gma_ref)
        value_group(i, 0, sc_ref, gmc_ref)

        rest = ngroups - 1
        npairs = jnp.maximum(rest - 1, 0) // 2

        def body(t, carry, i=i, q_cur=q_cur):
            score_group(q_cur, 2 * t + 2, sb_ref, gmb_ref)
            value_group(i, 2 * t + 1, sa_ref, gma_ref)
            score_group(q_cur, 2 * t + 3, sa_ref, gma_ref)
            value_group(i, 2 * t + 2, sb_ref, gmb_ref)
            return carry

        lax.fori_loop(0, npairs, body, 0)
        left = rest - 2 * npairs
        own_next = jnp.minimum(i + 1, nblk - 1)

        @pl.when(left == 2)
        def _(i=i, q_cur=q_cur, q_nxt=q_nxt, npairs=npairs, own_next=own_next):
            score_group(q_cur, 2 * npairs + 2, sb_ref, gmb_ref)
            value_group(i, 2 * npairs + 1, sa_ref, gma_ref)
            score_first(q_nxt, own_next, sc_ref, gmc_ref)
            value_group(i, 2 * npairs + 2, sb_ref, gmb_ref)

        @pl.when(left == 1)
        def _(i=i, q_nxt=q_nxt, npairs=npairs, own_next=own_next):
            score_first(q_nxt, own_next, sc_ref, gmc_ref)
            value_group(i, 2 * npairs + 1, sa_ref, gma_ref)

        @pl.when(left == 0)
        def _(q_nxt=q_nxt, own_next=own_next):
            score_first(q_nxt, own_next, sc_ref, gmc_ref)

        outs = [acc_ref[hh, :MOBA_HEAD_DIM, :] / acc_ref[hh, MOBA_HEAD_DIM:MOBA_HEAD_DIM + 1, :] for hh in range(2)]
        o_ref[0, sub * blk:(sub + 1) * blk, :] = jnp.concatenate(outs, axis=0).T.astype(o_ref.dtype)


def _moba_attention(qa, ka, vt):
    b, heads, nq = qa.shape[:3]
    s = nq * MOBA_BLOCK
    pairs = heads // 2
    tps = MOBA_TILES_PER_STEP
    assert nq % tps == 0
    return pl.pallas_call(
        _moba_kernel,
        grid=(b, pairs, nq // tps),
        in_specs=[
            pl.BlockSpec((1, 2, tps, LANES, MOBA_BLOCK), lambda bi, p, j: (bi, p, j, 0, 0)),
            pl.BlockSpec((1, 2, 1, LANES, MOBA_BLOCK),
                         lambda bi, p, j: (bi, p, jnp.minimum((j + 1) * tps, nq - 1), 0, 0)),
            pl.BlockSpec((1, 2, nq, MOBA_BLOCK, LANES), lambda bi, p, j: (bi, p, 0, 0, 0)),
            pl.BlockSpec((1, 2, nq, MOBA_VT_ROWS, MOBA_BLOCK), lambda bi, p, j: (bi, p, 0, 0, 0)),
        ],
        out_specs=pl.BlockSpec((1, tps * MOBA_BLOCK, LANES), lambda bi, p, j: (bi, j, p)),
        out_shape=jax.ShapeDtypeStruct((b, s, heads * MOBA_HEAD_DIM), BF16),
        scratch_shapes=[
            pltpu.VMEM((2, MOBA_UNROLL, MOBA_BLOCK, MOBA_BLOCK), F32),
            pltpu.VMEM((2, MOBA_UNROLL, MOBA_BLOCK, MOBA_BLOCK), F32),
            pltpu.VMEM((2, MOBA_UNROLL, MOBA_BLOCK, MOBA_BLOCK), F32),
            pltpu.VMEM((2, SUBLANES, MOBA_BLOCK), F32),
            pltpu.VMEM((2, SUBLANES, MOBA_BLOCK), F32),
            pltpu.VMEM((2, SUBLANES, MOBA_BLOCK), F32),
            pltpu.VMEM((2, SUBLANES, MOBA_BLOCK), F32),
            pltpu.VMEM((2, MOBA_VT_ROWS, MOBA_BLOCK), F32),
        ],
        compiler_params=_params("parallel", "parallel", "arbitrary"),
        name="moba_attention",
    )(qa, qa, ka, vt)


def _ssd_kernel(xbc_ref, zs_ref, dt_ref, alog_ref, dskip_ref, gn_ref, ex_ref, o_ref, state_ref):
    q = SSD_CHUNK
    d_in = SSD_HEADS * SSD_HEAD_DIM
    bc_w = SSD_GROUPS * SSD_STATE
    gw = d_in // SSD_GROUPS
    c = pl.program_id(1)

    @pl.when(c == 0)
    def _():
        state_ref[...] = jnp.zeros_like(state_ref)

    xs = xbc_ref[0, :, :d_in].astype(F32)
    bm = xbc_ref[0, :, d_in:d_in + bc_w]
    cm = xbc_ref[0, :, d_in + bc_w:]
    dt = dt_ref[0]
    a = dt * (-jnp.exp(alog_ref[...]))
    row = lax.broadcasted_iota(jnp.int32, (q, q), 0)
    col = lax.broadcasted_iota(jnp.int32, (q, q), 1)
    tril = col <= row
    a_cum = jnp.dot(tril.astype(F32), a, precision=HIGHEST, preferred_element_type=F32)
    a_cum_t = a_cum.T
    ex = ex_ref[...]
    dt_x = _split3_dot(dt, ex)
    acum_x = _split3_dot(a_cum, ex)
    alast_x = acum_x[q - 1:q, :]
    x_dt = xs * dt_x
    xd = (x_dt * jnp.exp(alast_x - acum_x)).astype(BF16)
    x_dt_b = x_dt.astype(BF16)
    e_acum = jnp.exp(acum_x)
    e_alast = jnp.exp(alast_x)
    lane2 = lax.broadcasted_iota(jnp.int32, (q, 2 * SSD_HEAD_DIM), 1)
    heads_per_group = SSD_HEADS // SSD_GROUPS

    for g in range(SSD_GROUPS):
        bg = bm[:, g * SSD_STATE:(g + 1) * SSD_STATE]
        cg = cm[:, g * SSD_STATE:(g + 1) * SSD_STATE]
        cb = _nt_dot(cg, bg)
        h_in = state_ref[g]
        y_g = jnp.dot(cg, h_in.astype(BF16), preferred_element_type=F32) * e_acum[:, g * gw:(g + 1) * gw]
        parts = []
        for pr in range(heads_per_group // 2):
            xp = x_dt_b[:, g * gw + pr * 2 * SSD_HEAD_DIM:g * gw + (pr + 1) * 2 * SSD_HEAD_DIM]
            ys = []
            for hh in range(2):
                hd = g * heads_per_group + pr * 2 + hh
                seg = a_cum[:, hd:hd + 1] - a_cum_t[hd:hd + 1, :]
                lmat = jnp.exp(jnp.where(tril, seg, -jnp.inf))
                ys.append(jnp.dot((cb * lmat).astype(BF16), xp, preferred_element_type=F32))
            parts.append(jnp.where(lane2 < SSD_HEAD_DIM, ys[0], ys[1]))
        y_g = y_g + jnp.concatenate(parts, axis=1)
        st = jnp.dot(bg.astype(F32).T.astype(BF16), xd[:, g * gw:(g + 1) * gw], preferred_element_type=F32)
        state_ref[g] = h_in * e_alast[:, g * gw:(g + 1) * gw] + st

        sl = slice(g * gw, (g + 1) * gw)
        y_g = y_g + dskip_ref[:, sl] * xs[:, sl]
        y_g = y_g * zs_ref[0, :, sl].astype(F32)
        r = lax.rsqrt(jnp.mean(y_g * y_g, axis=-1, keepdims=True) + EPS)
        o_ref[0, :, sl] = ((y_g * r) * gn_ref[:, sl]).astype(o_ref.dtype)


def _ssd(xbc3, zs3, dt3, a_log, d_skip, ssd_norm):
    b, s, xw = xbc3.shape
    d_in = SSD_HEADS * SSD_HEAD_DIM
    nc = s // SSD_CHUNK
    pad = LANES - SSD_HEADS
    alog = jnp.pad(a_log.astype(F32), (0, pad)).reshape(1, LANES)
    dskip = jnp.repeat(d_skip.astype(F32), SSD_HEAD_DIM).reshape(1, d_in)
    expand = (jnp.arange(LANES)[:, None] == (jnp.arange(d_in) // SSD_HEAD_DIM)[None, :]).astype(BF16)
    const = lambda shape: pl.BlockSpec(shape, lambda bi, ci: (0,) * len(shape))
    return pl.pallas_call(
        _ssd_kernel,
        grid=(b, nc),
        in_specs=[
            pl.BlockSpec((1, SSD_CHUNK, xw), lambda bi, ci: (bi, ci, 0)),
            pl.BlockSpec((1, SSD_CHUNK, d_in), lambda bi, ci: (bi, ci, 0)),
            pl.BlockSpec((1, SSD_CHUNK, LANES), lambda bi, ci: (bi, ci, 0)),
            const((1, LANES)), const((1, d_in)), const((1, d_in)), const((LANES, d_in)),
        ],
        out_specs=pl.BlockSpec((1, SSD_CHUNK, d_in), lambda bi, ci: (bi, ci, 0)),
        out_shape=jax.ShapeDtypeStruct((b, s, d_in), BF16),
        scratch_shapes=[pltpu.VMEM((SSD_GROUPS, SSD_STATE, d_in // SSD_GROUPS), F32)],
        compiler_params=_params("parallel", "arbitrary"),
        name="ssd_scan",
    )(xbc3, zs3, dt3, alog, dskip, ssd_norm.reshape(1, d_in), expand)


def _memkv_kernel(mem_ref, g_ref, w_ref, kg_ref, km_ref, vm_ref):
    m = mem_ref[0]
    r = lax.rsqrt(jnp.mean(m * m, axis=-1, keepdims=True) + EPS)
    mn = ((m * r) * g_ref[...]).astype(BF16)
    kv = jnp.dot(mn, w_ref[...], preferred_element_type=F32)
    mw = MEM_HEADS * MEM_HEAD_DIM
    for hd in range(MEM_HEADS):
        sl = slice(hd * MEM_HEAD_DIM, (hd + 1) * MEM_HEAD_DIM)
        kh = kv[:, sl]
        rk = lax.rsqrt(jnp.mean(kh * kh, axis=-1, keepdims=True) + EPS)
        km_ref[0, :, sl] = ((kh * rk) * kg_ref[...]).astype(km_ref.dtype)
    vm_ref[0] = kv[:, mw:].astype(vm_ref.dtype)


def _memkv(mem, g_mem, w_kv, k_gain):
    b, m, d = mem.shape
    mw = MEM_HEADS * MEM_HEAD_DIM
    return pl.pallas_call(
        _memkv_kernel,
        grid=(b,),
        in_specs=[
            pl.BlockSpec((1, m, d), lambda bi: (bi, 0, 0)),
            pl.BlockSpec((1, d), lambda bi: (0, 0)),
            pl.BlockSpec((d, 2 * mw), lambda bi: (0, 0)),
            pl.BlockSpec((1, MEM_HEAD_DIM), lambda bi: (0, 0)),
        ],
        out_specs=[pl.BlockSpec((1, m, mw), lambda bi: (bi, 0, 0)), pl.BlockSpec((1, m, mw), lambda bi: (bi, 0, 0))],
        out_shape=[jax.ShapeDtypeStruct((b, m, mw), BF16), jax.ShapeDtypeStruct((b, m, mw), BF16)],
        compiler_params=_params("parallel"),
        name="mem_kv",
    )(mem, g_mem.reshape(1, d), w_kv, k_gain.reshape(1, MEM_HEAD_DIM))


def _memattn_kernel(h_ref, w_ref, qg_ref, km_ref, vm_ref, o_ref):
    qm = jnp.dot(h_ref[...], w_ref[...], preferred_element_type=F32)
    scale = MEM_HEAD_DIM ** -0.5
    for hd in range(MEM_HEADS):
        sl = slice(hd * MEM_HEAD_DIM, (hd + 1) * MEM_HEAD_DIM)
        qh = qm[:, sl]
        r = lax.rsqrt(jnp.mean(qh * qh, axis=-1, keepdims=True) + EPS)
        qn = ((qh * r) * qg_ref[...]).astype(BF16)
        s = _nt_dot(qn, km_ref[0, :, sl]) * scale
        p = jnp.exp(s - jnp.max(s, axis=-1, keepdims=True))
        l = jnp.sum(p, axis=-1, keepdims=True)
        o = jnp.dot(p.astype(BF16), vm_ref[0, :, sl], preferred_element_type=F32)
        o_ref[:, sl] = (o / l).astype(o_ref.dtype)


def _memattn(h, w_qm, q_gain, km, vm, seq, tm=1024):
    n, d = h.shape
    b, m, mw = km.shape
    tiles_per_seq = seq // tm
    return pl.pallas_call(
        _memattn_kernel,
        grid=(n // tm,),
        in_specs=[
            pl.BlockSpec((tm, d), lambda i: (i, 0)),
            pl.BlockSpec((d, mw), lambda i: (0, 0)),
            pl.BlockSpec((1, MEM_HEAD_DIM), lambda i: (0, 0)),
            pl.BlockSpec((1, m, mw), lambda i: (i // tiles_per_seq, 0, 0)),
            pl.BlockSpec((1, m, mw), lambda i: (i // tiles_per_seq, 0, 0)),
        ],
        out_specs=pl.BlockSpec((tm, mw), lambda i: (i, 0)),
        out_shape=jax.ShapeDtypeStruct((n, mw), BF16),
        compiler_params=_params("parallel"),
        name="mem_attention",
    )(h, w_qm, q_gain.reshape(1, MEM_HEAD_DIM), km, vm)


def _merge_kernel(x_ref, h_ref, oa_ref, os_ref, om_ref, wg_ref, wa_ref, ws_ref, wm_ref, wo_ref, gf_ref, wr_ref,
                  x1_ref, h2_ref, route_ref, cnt_ref):
    d = x_ref.shape[1]
    gates = _sigmoid(jnp.dot(h_ref[...], wg_ref[...], preferred_element_type=F32))
    merged = gates[:, :d] * jnp.dot(oa_ref[...], wa_ref[...], preferred_element_type=F32)
    merged = merged + gates[:, d:2 * d] * jnp.dot(os_ref[...], ws_ref[...], preferred_element_type=F32)
    merged = merged + gates[:, 2 * d:] * jnp.dot(om_ref[...], wm_ref[...], preferred_element_type=F32)
    x1 = x_ref[...] + jnp.dot(merged.astype(BF16), wo_ref[...], preferred_element_type=F32)
    x1_ref[...] = x1
    r = lax.rsqrt(jnp.mean(x1 * x1, axis=-1, keepdims=True) + EPS)
    h2 = (x1 * r) * gf_ref[...]
    h2_ref[...] = _rows_to_token_tiles(h2)

    h2_hi = h2.astype(BF16)
    h2_lo = (h2 - h2_hi.astype(F32)).astype(BF16)
    lg = (jnp.dot(h2_hi, wr_ref[0], preferred_element_type=F32) + jnp.dot(h2_hi, wr_ref[1], preferred_element_type=F32)
          + jnp.dot(h2_lo, wr_ref[0], preferred_element_type=F32))
    lanef = lax.broadcasted_iota(jnp.int32, lg.shape, 1).astype(F32)
    big = float(LANES)
    gmask = lanef < MOE_GROUPS
    gl = jnp.where(gmask, lg, -jnp.inf)
    gmax = jnp.max(gl, axis=-1, keepdims=True)
    p_g = 1.0 / jnp.sum(jnp.exp(gl - gmax), axis=-1, keepdims=True)
    g_sel = jnp.min(jnp.where(gl == gmax, lanef, big), axis=-1, keepdims=True)
    lo = MOE_GROUPS + MOE_EXPERTS_PER_GROUP * g_sel
    el = jnp.where((lanef >= lo) & (lanef < lo + MOE_EXPERTS_PER_GROUP), lg, -jnp.inf)
    m1 = jnp.max(el, axis=-1, keepdims=True)
    i1 = jnp.min(jnp.where(el == m1, lanef, big), axis=-1, keepdims=True)
    el2 = jnp.where(lanef == i1, -jnp.inf, el)
    m2 = jnp.max(el2, axis=-1, keepdims=True)
    i2 = jnp.min(jnp.where(el2 == m2, lanef, big), axis=-1, keepdims=True)
    e2 = jnp.exp(m2 - m1)
    w1 = 1.0 / (1.0 + e2)
    w2 = e2 / (1.0 + e2)
    route = jnp.where(lanef == 0, i1 - MOE_GROUPS, 0.0)
    route = jnp.where(lanef == 1, i2 - MOE_GROUPS, route)
    route = jnp.where(lanef == 2, p_g * w1, route)
    route = jnp.where(lanef == 3, p_g * w2, route)
    route_ref[...] = route

    @pl.when(pl.program_id(0) == 0)
    def _():
        cnt_ref[...] = jnp.zeros_like(cnt_ref)

    hits = jnp.where(lanef == i1 - MOE_GROUPS, 1.0, 0.0) + jnp.where(lanef == i2 - MOE_GROUPS, 1.0, 0.0)
    cnt_ref[...] += jnp.sum(hits.reshape(hits.shape[0] // SUBLANES, SUBLANES, LANES), axis=0)


def _merge(x2, h, o_a, o_s, o_m, w_gates, w_a, w_s, w_m, w_out, g_ffn, w_router, tm=512):
    n, d = x2.shape
    full = lambda arr: pl.BlockSpec(arr.shape, lambda i: (0,) * arr.ndim)
    tile = lambda arr: pl.BlockSpec((tm, arr.shape[1]), lambda i: (i, 0))
    gf = g_ffn.reshape(1, d)
    return pl.pallas_call(
        _merge_kernel,
        grid=(n // tm,),
        in_specs=[tile(x2), tile(h), tile(o_a), tile(o_s), tile(o_m), full(w_gates), full(w_a), full(w_s), full(w_m),
                  full(w_out), full(gf), full(w_router)],
        out_specs=[pl.BlockSpec((tm, d), lambda i: (i, 0)),
                   pl.BlockSpec((tm, d // LANES, LANES), lambda i: (i, 0, 0)),
                   pl.BlockSpec((tm, LANES), lambda i: (i, 0)),
                   pl.BlockSpec((SUBLANES, LANES), lambda i: (0, 0))],
        out_shape=[jax.ShapeDtypeStruct((n, d), F32), jax.ShapeDtypeStruct((n, d // LANES, LANES), F32),
                   jax.ShapeDtypeStruct((n, LANES), F32), jax.ShapeDtypeStruct((SUBLANES, LANES), F32)],
        compiler_params=_params("arbitrary"),
        name="merge_router",
    )(x2, h, o_a, o_s, o_m, w_gates, w_a, w_s, w_m, w_out, gf, w_router)


def _moe_kernel(be_ref, j0_ref, nv_ref, tok_ref, dst_ref, h2_hbm, wg_ref, wu_ref, wd_ref, y2_hbm, xbuf, ybuf, gsem,
                ssem, wgb, wub, wdb):
    rows = xbuf.shape[1]
    n_tok = h2_hbm.shape[0]
    nblocks = pl.num_programs(0)
    i = pl.program_id(0)
    nbuf = xbuf.shape[0]
    slot = lax.rem(i, nbuf)
    slot_next = lax.rem(i + 1, nbuf)
    slot_prev = lax.rem(i + 2, nbuf)
    dump0 = MOE_TOPK * n_tok

    def block_rows(blk):
        b = jnp.clip(blk, 0, nblocks - 1)
        return j0_ref[b], jnp.where(blk < 0, -1, nv_ref[b] - 1)

    n_assign = tok_ref.shape[0]

    def gather_row(span, sl, r, prio):
        j0, _ = span
        tok = tok_ref[jnp.minimum(j0 + r, n_assign - 1)]
        pltpu.make_async_copy(h2_hbm.at[pl.ds(tok, 1)], xbuf.at[sl, pl.ds(r, 1)], gsem.at[sl]).start(priority=prio)

    def scatter_row(span, sl, r, prio):
        j0, last = span
        dst = jnp.where(r <= last, dst_ref[jnp.minimum(j0 + r, n_assign - 1)], dump0 + sl * rows + r)
        pltpu.make_async_copy(ybuf.at[sl, pl.ds(r, 1)], y2_hbm.at[pl.ds(dst, 1)], ssem.at[sl]).start(priority=prio)

    def looped(fn, blk, sl):
        span = block_rows(blk)

        def body(c, carry):
            for k in range(MOE_DMA_UNROLL):
                fn(span, sl, c * MOE_DMA_UNROLL + k, k % 2)
            return carry
        lax.fori_loop(0, rows // MOE_DMA_UNROLL, body, 0)

    def gather_wait(sl):
        pltpu.make_async_copy(h2_hbm.at[pl.ds(0, rows)], xbuf.at[sl], gsem.at[sl]).wait()

    def scatter_wait(sl):
        pltpu.make_async_copy(ybuf.at[sl], y2_hbm.at[pl.ds(0, rows)], ssem.at[sl]).wait()

    @pl.when(i == 0)
    def _():
        looped(gather_row, 0, 0)
        looped(gather_row, 1, 1)
        ybuf[...] = jnp.zeros_like(ybuf)
        for sl in range(2):
            pltpu.make_async_copy(ybuf.at[sl], y2_hbm.at[pl.ds(dump0 + sl * rows, rows)], ssem.at[sl]).start()

    @pl.when((i == 0) | (be_ref[i] != be_ref[jnp.maximum(i - 1, 0)]))
    def _():
        wgb[...] = wg_ref[0].astype(BF16)
        wub[...] = wu_ref[0].astype(BF16)
        wdb[...] = wd_ref[0].astype(BF16)

    gather_wait(slot)
    scatter_wait(slot)
    nxt, prv = block_rows(i + 2), block_rows(i - 1)
    for r in range(rows):
        scatter_row(prv, slot_prev, r, r % 2)
    x = _token_tiles_to_rows(xbuf[slot]).astype(BF16)
    for r in range(rows):
        gather_row(nxt, slot_prev, r, r % 2)
    gate = jnp.dot(x, wgb[...], preferred_element_type=F32)
    up = jnp.dot(x, wub[...], preferred_element_type=F32)
    hid = (gate * _sigmoid(gate)) * up
    y = jnp.dot(hid.astype(BF16), wdb[...], preferred_element_type=F32)
    ybuf[slot] = _rows_to_token_tiles(y)

    @pl.when(i == nblocks - 1)
    def _():
        gather_wait(slot_next)
        gather_wait(slot_prev)
        scatter_wait(slot_next)
        scatter_wait(slot_prev)
        looped(scatter_row, i, slot)
        scatter_wait(slot)


def _moe_experts(h2, tables, w_gate, w_up, w_down):
    n = h2.shape[0]
    e, d, f = w_gate.shape
    nblocks = tables[0].shape[0]
    wmap = lambda i, be, j0, nv, tok, dst: (be[i], 0, 0)
    grid_spec = pltpu.PrefetchScalarGridSpec(
        num_scalar_prefetch=5,
        grid=(nblocks,),
        in_specs=[
            pl.BlockSpec(memory_space=pl.ANY),
            pl.BlockSpec((1, d, f), wmap),
            pl.BlockSpec((1, d, f), wmap),
            pl.BlockSpec((1, f, d), wmap),
        ],
        out_specs=pl.BlockSpec(memory_space=pl.ANY),
        scratch_shapes=[
            pltpu.VMEM((MOE_BUFFERS, MOE_ROWS, d // LANES, LANES), F32),
            pltpu.VMEM((MOE_BUFFERS, MOE_ROWS, d // LANES, LANES), F32),
            pltpu.SemaphoreType.DMA((MOE_BUFFERS,)),
            pltpu.SemaphoreType.DMA((MOE_BUFFERS,)),
            pltpu.VMEM((d, f), BF16), pltpu.VMEM((d, f), BF16), pltpu.VMEM((f, d), BF16),
        ],
    )
    return pl.pallas_call(
        _moe_kernel,
        grid_spec=grid_spec,
        out_shape=jax.ShapeDtypeStruct((n * MOE_TOPK + MOE_BUFFERS * MOE_ROWS, d // LANES, LANES), F32),
        compiler_params=_params("arbitrary"),
        name="moe_experts",
    )(*tables, h2, w_gate, w_up, w_down)


def _combine_kernel(x1_ref, y0_ref, y1_ref, route_ref, o_ref):
    route = route_ref[...]
    w0 = route[:, 2:3]
    w1 = route[:, 3:4]
    o_ref[...] = x1_ref[...] + (w0 * _token_tiles_to_rows(y0_ref[...]) + w1 * _token_tiles_to_rows(y1_ref[...]))


def _combine(x1, y2, route, tm=512):
    n, d = x1.shape
    tiles = n // tm
    ytile = (tm, d // LANES, LANES)
    return pl.pallas_call(
        _combine_kernel,
        grid=(tiles,),
        in_specs=[pl.BlockSpec((tm, d), lambda i: (i, 0)), pl.BlockSpec(ytile, lambda i: (i, 0, 0)),
                  pl.BlockSpec(ytile, lambda i: (tiles + i, 0, 0)), pl.BlockSpec((tm, LANES), lambda i: (i, 0))],
        out_specs=pl.BlockSpec((tm, d), lambda i: (i, 0)),
        out_shape=jax.ShapeDtypeStruct((n, d), F32),
        compiler_params=_params("parallel"),
        name="moe_combine",
    )(x1, y2, y2, route)


def _dispatch_tables(route, expert_counts, n_tok):
    n_assign = n_tok * MOE_TOPK
    e_flat = route[:, :MOE_TOPK].astype(jnp.int32).reshape(n_assign)
    order = jnp.argsort(e_flat).astype(jnp.int32)
    counts = jnp.sum(expert_counts, axis=0)[:MOE_EXPERTS].astype(jnp.int32)
    blocks_per_expert = (counts + MOE_ROWS - 1) // MOE_ROWS
    blk_end = jnp.cumsum(blocks_per_expert)
    raw_start = jnp.cumsum(counts) - counts
    nblocks = n_assign // MOE_ROWS + MOE_EXPERTS
    blk = jnp.arange(nblocks, dtype=jnp.int32)
    blk_expert = jnp.minimum(jnp.sum(blk_end[None, :] <= blk[:, None], axis=1), MOE_EXPERTS - 1).astype(jnp.int32)
    within = blk - (blk_end - blocks_per_expert)[blk_expert]
    blk_nvalid = jnp.clip(counts[blk_expert] - within * MOE_ROWS, 0, MOE_ROWS).astype(jnp.int32)
    blk_j0 = jnp.where(blk_nvalid > 0, raw_start[blk_expert] + within * MOE_ROWS, 0).astype(jnp.int32)
    tok_sorted = order // MOE_TOPK
    dst_sorted = (order % MOE_TOPK) * n_tok + tok_sorted
    return blk_expert, blk_j0, blk_nvalid, tok_sorted, dst_sorted


def kernel(x, mem, g_mix, w_in, moba_q_norm, moba_k_norm, conv_w, conv_b, dt_bias, a_log, d_skip, ssd_norm, g_mem,
           w_mem_kv, mem_q_norm, mem_k_norm, w_o_moba, w_o_ssd, w_o_mem, w_out, g_ffn, w_router_group,
           w_router_expert, w_gate, w_up, w_down):
    bsz, seq, d = x.shape
    assert seq % MOBA_BLOCK == 0 and seq % SSD_CHUNK == 0
    n_tok = bsz * seq
    moba_w = MOBA_HEADS * MOBA_HEAD_DIM
    d_in = SSD_HEADS * SSD_HEAD_DIM
    xbc_w = d_in + 2 * SSD_GROUPS * SSD_STATE
    mem_w = MEM_HEADS * MEM_HEAD_DIM
    sizes = (moba_w, moba_w, moba_w, d_in, xbc_w, SSD_HEADS, mem_w, 3 * d)
    offs = [0]
    for sz in sizes:
        offs.append(offs[-1] + sz)
    w_in_b = w_in.astype(BF16)
    w_qk = w_in_b[:, offs[0]:offs[2]]
    w_v = w_in_b[:, offs[2]:offs[3]]
    w_z = w_in_b[:, offs[3]:offs[4]]
    w_xbc = w_in_b[:, offs[4]:offs[5]]
    w_dt = jnp.pad(w_in_b[:, offs[5]:offs[6]], ((0, 0), (0, LANES - SSD_HEADS)))
    w_qm = w_in_b[:, offs[6]:offs[7]]
    w_gates = w_in_b[:, offs[7]:offs[8]]

    x2 = x.reshape(n_tok, d)

    half = MOBA_HEAD_DIM // 2
    inv = ROPE_THETA ** (-jnp.arange(half, dtype=F32) / half)
    ang = jnp.arange(seq, dtype=F32)[:, None] * inv[None, :]
    cos_t = jnp.tile(jnp.cos(ang), (1, LANES // half))
    sin_t = jnp.tile(jnp.concatenate([-jnp.sin(ang), jnp.sin(ang)], axis=1), (1, LANES // MOBA_HEAD_DIM))
    gains = jnp.stack([jnp.tile(moba_q_norm, MOBA_HEADS), jnp.tile(moba_k_norm, MOBA_HEADS)]).reshape(2, 1, moba_w)

    h, qk, v, zs, xbc, dt = _inproj(x2, g_mix, w_qk, w_v, w_z, w_xbc, w_dt, gains, cos_t, sin_t, conv_w, conv_b,
                                    dt_bias, seq)

    qk3 = qk.reshape(bsz, seq, 2 * moba_w)
    v3 = v.reshape(bsz, seq, moba_w)
    nb = seq // MOBA_BLOCK
    assert nb <= MOBA_MAX_BLOCKS and nb % MOBA_UNROLL == 0
    ka, vt, qa = _moba_prep(qk3, v3)
    o_a = _moba_attention(qa, ka, vt).reshape(n_tok, moba_w)

    o_s = _ssd(xbc.reshape(bsz, seq, xbc_w), zs.reshape(bsz, seq, d_in), dt.reshape(bsz, seq, LANES), a_log, d_skip,
               ssd_norm).reshape(n_tok, d_in)

    km, vm = _memkv(mem, g_mem, w_mem_kv.astype(BF16), mem_k_norm)
    o_m = _memattn(h, w_qm, mem_q_norm, km, vm, seq)

    w_router = jnp.pad(jnp.concatenate([w_router_group, w_router_expert], axis=1),
                       ((0, 0), (0, LANES - MOE_GROUPS - MOE_EXPERTS)))
    w_router_hi = w_router.astype(BF16)
    w_router = jnp.stack([w_router_hi, (w_router - w_router_hi.astype(F32)).astype(BF16)])
    x1, h2, route, expert_counts = _merge(x2, h, o_a, o_s, o_m, w_gates, w_o_moba.astype(BF16),
                                          w_o_ssd.astype(BF16), w_o_mem.astype(BF16), w_out.astype(BF16), g_ffn,
                                          w_router)

    tables = _dispatch_tables(route, expert_counts, n_tok)
    y2 = _moe_experts(h2, tables, w_gate, w_up, w_down)
    out = _combine(x1, y2, route)
    return out.reshape(bsz, seq, d)
```

```python
import functools

import jax
import jax.numpy as jnp
from jax import lax
from jax.experimental import pallas as pl
from jax.experimental.pallas import tpu as pltpu

F32 = jnp.float32
BF16 = jnp.bfloat16
HIGHEST = lax.Precision.HIGHEST

EPS = 1e-6
ROPE_THETA = 10000.0
MOBA_HEADS = 8
MOBA_HEAD_DIM = 64
MOBA_BLOCK = 256
MOBA_TOPK = 3
SSD_HEAD_DIM = 64
SSD_HEADS = 16
SSD_GROUPS = 4
SSD_STATE = 128
SSD_CONV = 4
SSD_CHUNK = 256
MEM_HEADS = 4
MEM_HEAD_DIM = 128
MOE_GROUPS = 4
MOE_EXPERTS_PER_GROUP = 8
MOE_EXPERTS = MOE_GROUPS * MOE_EXPERTS_PER_GROUP
MOE_TOPK = 2

LANES = 128
SUBLANES = 8
LOG2E = 1.4426950408889634
MASKED = -1e30
MOBA_MAX_BLOCKS = 32
MOBA_UNROLL = 2
MOBA_TILES_PER_STEP = 8
MOBA_VT_ROWS = MOBA_HEAD_DIM + 16
MOE_ROWS = 256
INPROJ_CONV_CHUNK = 512
MOE_BUFFERS = 3
MOE_DMA_UNROLL = 8
VMEM_LIMIT = 56 * 1024 * 1024


def _params(*sem):
    return pltpu.CompilerParams(dimension_semantics=sem, vmem_limit_bytes=VMEM_LIMIT)


def _sigmoid(x):
    return 1.0 / (1.0 + jnp.exp(-x))


def _split3_dot(x, sel):
    hi = x.astype(BF16)
    rest = x - hi.astype(F32)
    mid = rest.astype(BF16)
    lo = (rest - mid.astype(F32)).astype(BF16)
    return (jnp.dot(hi, sel, preferred_element_type=F32) + jnp.dot(mid, sel, preferred_element_type=F32)
            + jnp.dot(lo, sel, preferred_element_type=F32))


def _rows_to_token_tiles(x):
    rows, width = x.shape
    chunks = width // LANES
    x4 = jnp.stack([x[:, k * LANES:(k + 1) * LANES].reshape(rows // SUBLANES, SUBLANES, LANES)
                    for k in range(chunks)], axis=1)
    return jnp.swapaxes(x4, 1, 2).reshape(rows, chunks, LANES)


def _token_tiles_to_rows(x3):
    rows, chunks, _ = x3.shape
    x4 = jnp.swapaxes(x3.reshape(rows // SUBLANES, SUBLANES, chunks, LANES), 1, 2)
    return jnp.concatenate([x4[:, k].reshape(rows, LANES) for k in range(chunks)], axis=1)


def _nt_dot(a, b, precision=None):
    return lax.dot_general(a, b, (((1,), (1,)), ((), ())), precision=precision, preferred_element_type=F32)


def _softplus(x):
    return jnp.maximum(x, 0.0) + jnp.log1p(jnp.exp(-jnp.abs(x)))


def _inproj_kernel(x_ref, gmix_ref, wqk_ref, wv_ref, wz_ref, wxbc_ref, wdt_ref, gain_ref, gsum_ref, cos_ref, sin_ref,
                   cw_ref, cb_ref, dtb_ref, h_ref, qk_ref, v_ref, zs_ref, xbc_ref, dt_ref, *ext_refs,
                   tiles_per_seq):
    tm = x_ref.shape[0]
    x = x_ref[...]
    r = lax.rsqrt(jnp.mean(x * x, axis=-1, keepdims=True) + EPS)
    h = ((x * r) * gmix_ref[...]).astype(BF16)
    h_ref[...] = h

    width = gsum_ref.shape[0]
    half = MOBA_HEAD_DIM // 2
    lane = lax.broadcasted_iota(jnp.int32, (tm, width), 1)
    first = (lane & (MOBA_HEAD_DIM - 1)) < half
    reps = width // cos_ref.shape[1]
    cos = jnp.concatenate([cos_ref[...]] * reps, axis=1)
    sin = jnp.concatenate([sin_ref[...]] * reps, axis=1)
    for j in range(2):
        u = jnp.dot(h, wqk_ref[:, j * width:(j + 1) * width], preferred_element_type=F32)
        sq = u * u
        hi = sq.astype(BF16)
        lo = (sq - hi.astype(F32)).astype(BF16)
        ss = (jnp.dot(hi, gsum_ref[...], preferred_element_type=F32)
              + jnp.dot(lo, gsum_ref[...], preferred_element_type=F32))
        un = (u * lax.rsqrt(ss * (1.0 / MOBA_HEAD_DIM) + EPS)) * gain_ref[j]
        partner = jnp.where(first, pltpu.roll(un, width - half, axis=1), pltpu.roll(un, half, axis=1))
        qk_ref[:, j * width:(j + 1) * width] = un * cos + partner * sin

    v_ref[...] = jnp.dot(h, wv_ref[...], preferred_element_type=F32).astype(v_ref.dtype)
    z = jnp.dot(h, wz_ref[...], preferred_element_type=F32)
    zs_ref[...] = (z * _sigmoid(z)).astype(zs_ref.dtype)
    dt_ref[...] = _softplus(jnp.dot(h, wdt_ref[...], preferred_element_type=F32) + dtb_ref[...])

    halo = SUBLANES

    @pl.when(pl.program_id(0) % tiles_per_seq == 0)
    def _():
        for ext_ref in ext_refs:
            ext_ref[0:halo, :] = jnp.zeros((halo, ext_ref.shape[1]), F32)

    cchunk = ext_refs[0].shape[1]
    for ci, ext_ref in enumerate(ext_refs):
        cs = slice(ci * cchunk, (ci + 1) * cchunk)
        u = jnp.dot(h, wxbc_ref[:, cs], preferred_element_type=F32)
        ext_ref[halo:halo + tm, :] = u
        acc = cb_ref[:, cs] + cw_ref[SSD_CONV - 1:SSD_CONV, cs] * u
        for kk in range(SSD_CONV - 1):
            off = halo - (SSD_CONV - 1 - kk)
            acc = acc + cw_ref[kk:kk + 1, cs] * ext_ref[off:off + tm, :]
        ext_ref[0:halo, :] = u[tm - halo:tm, :]
        xbc_ref[:, cs] = (acc * _sigmoid(acc)).astype(xbc_ref.dtype)


def _inproj(x2, g_mix, w_qk, w_v, w_z, w_xbc, w_dt, gains, cos_t, sin_t, conv_w, conv_b, dt_bias, seq, tm=512):
    n, d = x2.shape
    width = w_qk.shape[1] // 2
    tiles_per_seq = seq // tm
    head = jnp.arange(width) // MOBA_HEAD_DIM
    gsum = (head[:, None] == head[None, :]).astype(BF16)
    dtb = jnp.pad(dt_bias.astype(F32), (0, LANES - dt_bias.shape[0])).reshape(1, LANES)
    full = lambda arr: pl.BlockSpec(arr.shape, lambda i: (0,) * arr.ndim)
    tile = lambda cols: pl.BlockSpec((tm, cols), lambda i: (i, 0))
    rope = pl.BlockSpec((tm, LANES), lambda i: (i % tiles_per_seq, 0))
    consts = (g_mix.reshape(1, d), w_qk, w_v, w_z, w_xbc, w_dt, gains, gsum)
    tail = (conv_w, conv_b.reshape(1, -1), dtb)
    outs = ((d, BF16), (2 * width, F32), (w_v.shape[1], BF16), (w_z.shape[1], BF16), (w_xbc.shape[1], BF16),
            (LANES, F32))
    return pl.pallas_call(
        functools.partial(_inproj_kernel, tiles_per_seq=tiles_per_seq),
        grid=(n // tm,),
        in_specs=[tile(d)] + [full(a) for a in consts] + [rope, rope] + [full(a) for a in tail],
        out_specs=[tile(c) for c, _ in outs],
        out_shape=[jax.ShapeDtypeStruct((n, c), dt) for c, dt in outs],
        scratch_shapes=[pltpu.VMEM((tm + SUBLANES, INPROJ_CONV_CHUNK), F32)] * (w_xbc.shape[1] // INPROJ_CONV_CHUNK),
        compiler_params=_params("arbitrary"),
        name="in_proj",
    )(x2, *consts, cos_t, sin_t, *tail)


def _moba_prep_kernel(q_ref, k_ref, v_ref, ka_ref, vt_ref, qa_ref, km_ref):
    n = pl.program_id(1)

    @pl.when(n == 0)
    def _():
        km_ref[...] = jnp.zeros_like(km_ref)

    _moba_select(q_ref, km_ref, qa_ref, n)

    ones = jnp.ones((MOBA_VT_ROWS - MOBA_HEAD_DIM, MOBA_BLOCK), BF16)
    k = k_ref[0]
    km_ref[pl.ds(n, 1), :] = jnp.mean(k, axis=0, keepdims=True)
    lane = lax.broadcasted_iota(jnp.int32, (MOBA_BLOCK, LANES), 1)
    own = lane < MOBA_HEAD_DIM
    onehot = jnp.where(lane == MOBA_HEAD_DIM + n, 1.0, 0.0)
    for pr in range(MOBA_HEADS // 2):
        pair = slice(pr * LANES, (pr + 1) * LANES)
        kp = k[:, pair]
        ka_ref[0, 2 * pr, 0] = jnp.where(own, kp, onehot).astype(BF16)
        ka_ref[0, 2 * pr + 1, 0] = jnp.where(own, pltpu.roll(kp, MOBA_HEAD_DIM, axis=1), onehot).astype(BF16)
        vp_t = v_ref[0, :, pair].astype(F32).T
        for hh in range(2):
            vt_ref[0, 2 * pr + hh, 0, :MOBA_HEAD_DIM, :] = vp_t[hh * MOBA_HEAD_DIM:(hh + 1) * MOBA_HEAD_DIM].astype(BF16)
            vt_ref[0, 2 * pr + hh, 0, MOBA_HEAD_DIM:, :] = ones


def _moba_prep(qk3, v3):
    b, s, w = v3.shape
    nb = s // MOBA_BLOCK
    return pl.pallas_call(
        _moba_prep_kernel,
        grid=(b, nb),
        in_specs=[pl.BlockSpec((1, MOBA_BLOCK, w), lambda bi, n: (bi, n, 0)),
                  pl.BlockSpec((1, MOBA_BLOCK, w), lambda bi, n: (bi, n, 1)),
                  pl.BlockSpec((1, MOBA_BLOCK, w), lambda bi, n: (bi, n, 0))],
        out_specs=[
            pl.BlockSpec((1, MOBA_HEADS, 1, MOBA_BLOCK, LANES), lambda bi, n: (bi, 0, n, 0, 0)),
            pl.BlockSpec((1, MOBA_HEADS, 1, MOBA_VT_ROWS, MOBA_BLOCK), lambda bi, n: (bi, 0, n, 0, 0)),
            pl.BlockSpec((1, MOBA_HEADS, 1, LANES, MOBA_BLOCK), lambda bi, n: (bi, 0, n, 0, 0)),
        ],
        out_shape=[jax.ShapeDtypeStruct((b, MOBA_HEADS, nb, MOBA_BLOCK, LANES), BF16),
                   jax.ShapeDtypeStruct((b, MOBA_HEADS, nb, MOBA_VT_ROWS, MOBA_BLOCK), BF16),
                   jax.ShapeDtypeStruct((b, MOBA_HEADS, nb, LANES, MOBA_BLOCK), BF16)],
        scratch_shapes=[pltpu.VMEM((MOBA_MAX_BLOCKS, w), F32)],
        compiler_params=_params("parallel", "arbitrary"),
        name="moba_prep",
    )(qk3, qk3, v3)


def _moba_select(q_ref, km_ref, qa_ref, i):
    blk = MOBA_BLOCK
    nsel = MOBA_MAX_BLOCKS
    qscale = (MOBA_HEAD_DIM ** -0.5) * LOG2E
    q_t = (q_ref[0] * qscale).T
    kmean = km_ref[...]
    lane = lax.broadcasted_iota(jnp.int32, (nsel, LANES), 1)
    blk_row = lax.broadcasted_iota(jnp.int32, (nsel, blk), 0)
    blk_rowf = blk_row.astype(F32)
    pad = jnp.zeros((LANES - MOBA_HEAD_DIM - nsel, blk), F32)
    for hd in range(MOBA_HEADS):
        pr, hh = hd // 2, hd % 2
        hmask = (lane >= hh * MOBA_HEAD_DIM) & (lane < (hh + 1) * MOBA_HEAD_DIM)
        km_h = jnp.where(hmask, kmean[:, pr * LANES:(pr + 1) * LANES], 0.0)
        gate = jnp.dot(km_h, q_t[pr * LANES:(pr + 1) * LANES], precision=HIGHEST, preferred_element_type=F32)
        g = jnp.where(blk_row < i, gate, -jnp.inf)
        bias = jnp.full((nsel, blk), MASKED, F32)
        for _ in range(MOBA_TOPK):
            m = jnp.max(g, axis=0, keepdims=True)
            hit = (g == m) & (m > -jnp.inf)
            first = jnp.min(jnp.where(hit, blk_rowf, float(nsel)), axis=0, keepdims=True)
            sel = blk_rowf == first
            bias = jnp.where(sel, 0.0, bias)
            g = jnp.where(sel, -jnp.inf, g)
        qs = q_t[hd * MOBA_HEAD_DIM:(hd + 1) * MOBA_HEAD_DIM]
        qa_ref[0, hd, 0] = jnp.concatenate([qs, bias, pad], axis=0).astype(BF16)


def _moba_kernel(qa_ref, qan_ref, ka_ref, vt_ref, o_ref, sa_ref, sb_ref, sc_ref, gma_ref, gmb_ref, gmc_ref, m_ref,
                 acc_ref):
    blk = MOBA_BLOCK
    nblk = ka_ref.shape[2]
    key_pos = lax.broadcasted_iota(jnp.int32, (blk, blk), 0)
    qry_pos = lax.broadcasted_iota(jnp.int32, (blk, blk), 1)
    causal = key_pos <= qry_pos
    feat_row = lax.broadcasted_iota(jnp.int32, (LANES, blk), 0) < MOBA_HEAD_DIM

    def fold(s, op):
        return op(s.reshape(blk // SUBLANES, SUBLANES, blk), axis=0)

    def score_tiles(tiles, s_ref, gm_ref):
        for hh in range(2):
            gmax = None
            for u, (n, q_of, is_own) in enumerate(tiles):
                s = jnp.dot(ka_ref[0, hh, n], q_of(hh), preferred_element_type=F32)
                if is_own:
                    s = jnp.where(causal, s, MASKED)
                s_ref[hh, u] = s
                gmax = fold(s, jnp.max) if gmax is None else jnp.maximum(gmax, fold(s, jnp.max))
            gm_ref[hh] = gmax

    def score_first(q_past, own, s_ref, gm_ref):
        q_own = lambda hh: jnp.where(feat_row, q_past(hh), jnp.zeros((), BF16))
        tiles = [(own, q_own, True)] + [(u - 1, q_past, False) for u in range(1, MOBA_UNROLL)]
        score_tiles(tiles, s_ref, gm_ref)

    def score_group(q_past, grp, s_ref, gm_ref):
        tiles = [(jnp.minimum(grp * MOBA_UNROLL + (u - 1), nblk - 1), q_past, False) for u in range(MOBA_UNROLL)]
        score_tiles(tiles, s_ref, gm_ref)

    def value_group(i, grp, s_ref, gm_ref):
        for hh in range(2):
            m_old = m_ref[hh][0:1]
            m_new = jnp.maximum(m_old, jnp.max(gm_ref[hh], axis=0, keepdims=True))
            alpha = jnp.exp2(m_old - m_new)
            acc = acc_ref[hh] * alpha
            for u in range(MOBA_UNROLL):
                n = jnp.minimum(grp * MOBA_UNROLL + (u - 1), nblk - 1)
                if u == 0:
                    n = jnp.where(grp == 0, i, n)
                p = jnp.exp2(s_ref[hh, u] - m_new).astype(BF16)
                acc = acc + jnp.dot(vt_ref[0, hh, n], p, preferred_element_type=F32)
            m_ref[hh] = jnp.broadcast_to(m_new, (SUBLANES, blk))
            acc_ref[hh] = acc

    first_tile = pl.program_id(2) * MOBA_TILES_PER_STEP

    @pl.when(first_tile == 0)
    def _():
        score_first(lambda hh: qa_ref[0, hh, 0], 0, sc_ref, gmc_ref)

    for sub in range(MOBA_TILES_PER_STEP):
        i = first_tile + sub
        ngroups = (i + MOBA_UNROLL) // MOBA_UNROLL
        q_cur = lambda hh, sub=sub: qa_ref[0, hh, sub]
        if sub + 1 < MOBA_TILES_PER_STEP:
            q_nxt = lambda hh, sub=sub: qa_ref[0, hh, sub + 1]
        else:
            q_nxt = lambda hh: qan_ref[0, hh, 0]
        for hh in range(2):
            m_ref[hh] = jnp.full((SUBLANES, blk), MASKED, F32)
            acc_ref[hh] = jnp.zeros((MOBA_VT_ROWS, blk), F32)

        score_group(q_cur, 1, sa_ref, gma_ref)
        value_group(i, 0, sc_ref, gmc_ref)

        rest = ngroups - 1
        npairs = jnp.maximum(rest - 1, 0) // 2

        def body(t, carry, i=i, q_cur=q_cur):
            score_group(q_cur, 2 * t + 2, sb_ref, gmb_ref)
            value_group(i, 2 * t + 1, sa_ref, gma_ref)
            score_group(q_cur, 2 * t + 3, sa_ref, gma_ref)
            value_group(i, 2 * t + 2, sb_ref, gmb_ref)
            return carry

        lax.fori_loop(0, npairs, body, 0)
        left = rest - 2 * npairs
        own_next = jnp.minimum(i + 1, nblk - 1)

        @pl.when(left == 2)
        def _(i=i, q_cur=q_cur, q_nxt=q_nxt, npairs=npairs, own_next=own_next):
            score_group(q_cur, 2 * npairs + 2, sb_ref, gmb_ref)
            value_group(i, 2 * npairs + 1, sa_ref, gma_ref)
            score_first(q_nxt, own_next, sc_ref, gmc_ref)
            value_group(i, 2 * npairs + 2, sb_ref, gmb_ref)

        @pl.when(left == 1)
        def _(i=i, q_nxt=q_nxt, npairs=npairs, own_next=own_next):
            score_first(q_nxt, own_next, sc_ref, gmc_ref)
            value_group(i, 2 * npairs + 1, sa_ref, gma_ref)

        @pl.when(left == 0)
        def _(q_nxt=q_nxt, own_next=own_next):
            score_first(q_nxt, own_next, sc_ref, gmc_ref)

        outs = [acc_ref[hh, :MOBA_HEAD_DIM, :] / acc_ref[hh, MOBA_HEAD_DIM:MOBA_HEAD_DIM + 1, :] for hh in range(2)]
        o_ref[0, sub * blk:(sub + 1) * blk, :] = jnp.concatenate(outs, axis=0).T.astype(o_ref.dtype)


def _moba_attention(qa, ka, vt):
    b, heads, nq = qa.shape[:3]
    s = nq * MOBA_BLOCK
    pairs = heads // 2
    tps = MOBA_TILES_PER_STEP
    assert nq % tps == 0
    return pl.pallas_call(
        _moba_kernel,
        grid=(b, pairs, nq // tps),
        in_specs=[
            pl.BlockSpec((1, 2, tps, LANES, MOBA_BLOCK), lambda bi, p, j: (bi, p, j, 0, 0)),
            pl.BlockSpec((1, 2, 1, LANES, MOBA_BLOCK),
                         lambda bi, p, j: (bi, p, jnp.minimum((j + 1) * tps, nq - 1), 0, 0)),
            pl.BlockSpec((1, 2, nq, MOBA_BLOCK, LANES), lambda bi, p, j: (bi, p, 0, 0, 0)),
            pl.BlockSpec((1, 2, nq, MOBA_VT_ROWS, MOBA_BLOCK), lambda bi, p, j: (bi, p, 0, 0, 0)),
        ],
        out_specs=pl.BlockSpec((1, tps * MOBA_BLOCK, LANES), lambda bi, p, j: (bi, j, p)),
        out_shape=jax.ShapeDtypeStruct((b, s, heads * MOBA_HEAD_DIM), BF16),
        scratch_shapes=[
            pltpu.VMEM((2, MOBA_UNROLL, MOBA_BLOCK, MOBA_BLOCK), F32),
            pltpu.VMEM((2, MOBA_UNROLL, MOBA_BLOCK, MOBA_BLOCK), F32),
            pltpu.VMEM((2, MOBA_UNROLL, MOBA_BLOCK, MOBA_BLOCK), F32),
            pltpu.VMEM((2, SUBLANES, MOBA_BLOCK), F32),
            pltpu.VMEM((2, SUBLANES, MOBA_BLOCK), F32),
            pltpu.VMEM((2, SUBLANES, MOBA_BLOCK), F32),
            pltpu.VMEM((2, SUBLANES, MOBA_BLOCK), F32),
            pltpu.VMEM((2, MOBA_VT_ROWS, MOBA_BLOCK), F32),
        ],
        compiler_params=_params("parallel", "parallel", "arbitrary"),
        name="moba_attention",
    )(qa, qa, ka, vt)


def _ssd_kernel(xbc_ref, zs_ref, dt_ref, alog_ref, dskip_ref, gn_ref, ex_ref, o_ref, state_ref):
    q = SSD_CHUNK
    d_in = SSD_HEADS * SSD_HEAD_DIM
    bc_w = SSD_GROUPS * SSD_STATE
    gw = d_in // SSD_GROUPS
    c = pl.program_id(1)

    @pl.when(c == 0)
    def _():
        state_ref[...] = jnp.zeros_like(state_ref)

    xs = xbc_ref[0, :, :d_in].astype(F32)
    bm = xbc_ref[0, :, d_in:d_in + bc_w]
    cm = xbc_ref[0, :, d_in + bc_w:]
    dt = dt_ref[0]
    a = dt * (-jnp.exp(alog_ref[...]))
    row = lax.broadcasted_iota(jnp.int32, (q, q), 0)
    col = lax.broadcasted_iota(jnp.int32, (q, q), 1)
    tril = col <= row
    a_cum = jnp.dot(tril.astype(F32), a, precision=HIGHEST, preferred_element_type=F32)
    a_cum_t = a_cum.T
    ex = ex_ref[...]
    dt_x = _split3_dot(dt, ex)
    acum_x = _split3_dot(a_cum, ex)
    alast_x = acum_x[q - 1:q, :]
    x_dt = xs * dt_x
    xd = (x_dt * jnp.exp(alast_x - acum_x)).astype(BF16)
    x_dt_b = x_dt.astype(BF16)
    e_acum = jnp.exp(acum_x)
    e_alast = jnp.exp(alast_x)
    lane2 = lax.broadcasted_iota(jnp.int32, (q, 2 * SSD_HEAD_DIM), 1)
    heads_per_group = SSD_HEADS // SSD_GROUPS

    for g in range(SSD_GROUPS):
        bg = bm[:, g * SSD_STATE:(g + 1) * SSD_STATE]
        cg = cm[:, g * SSD_STATE:(g + 1) * SSD_STATE]
        cb = _nt_dot(cg, bg)
        h_in = state_ref[g]
        y_g = jnp.dot(cg, h_in.astype(BF16), preferred_element_type=F32) * e_acum[:, g * gw:(g + 1) * gw]
        parts = []
        for pr in range(heads_per_group // 2):
            xp = x_dt_b[:, g * gw + pr * 2 * SSD_HEAD_DIM:g * gw + (pr + 1) * 2 * SSD_HEAD_DIM]
            ys = []
            for hh in range(2):
                hd = g * heads_per_group + pr * 2 + hh
                seg = a_cum[:, hd:hd + 1] - a_cum_t[hd:hd + 1, :]
                lmat = jnp.exp(jnp.where(tril, seg, -jnp.inf))
                ys.append(jnp.dot((cb * lmat).astype(BF16), xp, preferred_element_type=F32))
            parts.append(jnp.where(lane2 < SSD_HEAD_DIM, ys[0], ys[1]))
        y_g = y_g + jnp.concatenate(parts, axis=1)
        st = jnp.dot(bg.astype(F32).T.astype(BF16), xd[:, g * gw:(g + 1) * gw], preferred_element_type=F32)
        state_ref[g] = h_in * e_alast[:, g * gw:(g + 1) * gw] + st

        sl = slice(g * gw, (g + 1) * gw)
        y_g = y_g + dskip_ref[:, sl] * xs[:, sl]
        y_g = y_g * zs_ref[0, :, sl].astype(F32)
        r = lax.rsqrt(jnp.mean(y_g * y_g, axis=-1, keepdims=True) + EPS)
        o_ref[0, :, sl] = ((y_g * r) * gn_ref[:, sl]).astype(o_ref.dtype)


def _ssd(xbc3, zs3, dt3, a_log, d_skip, ssd_norm):
    b, s, xw = xbc3.shape
    d_in = SSD_HEADS * SSD_HEAD_DIM
    nc = s // SSD_CHUNK
    pad = LANES - SSD_HEADS
    alog = jnp.pad(a_log.astype(F32), (0, pad)).reshape(1, LANES)
    dskip = jnp.repeat(d_skip.astype(F32), SSD_HEAD_DIM).reshape(1, d_in)
    expand = (jnp.arange(LANES)[:, None] == (jnp.arange(d_in) // SSD_HEAD_DIM)[None, :]).astype(BF16)
    const = lambda shape: pl.BlockSpec(shape, lambda bi, ci: (0,) * len(shape))
    return pl.pallas_call(
        _ssd_kernel,
        grid=(b, nc),
        in_specs=[
            pl.BlockSpec((1, SSD_CHUNK, xw), lambda bi, ci: (bi, ci, 0)),
            pl.BlockSpec((1, SSD_CHUNK, d_in), lambda bi, ci: (bi, ci, 0)),
            pl.BlockSpec((1, SSD_CHUNK, LANES), lambda bi, ci: (bi, ci, 0)),
            const((1, LANES)), const((1, d_in)), const((1, d_in)), const((LANES, d_in)),
        ],
        out_specs=pl.BlockSpec((1, SSD_CHUNK, d_in), lambda bi, ci: (bi, ci, 0)),
        out_shape=jax.ShapeDtypeStruct((b, s, d_in), BF16),
        scratch_shapes=[pltpu.VMEM((SSD_GROUPS, SSD_STATE, d_in // SSD_GROUPS), F32)],
        compiler_params=_params("parallel", "arbitrary"),
        name="ssd_scan",
    )(xbc3, zs3, dt3, alog, dskip, ssd_norm.reshape(1, d_in), expand)


def _memkv_kernel(mem_ref, g_ref, w_ref, kg_ref, km_ref, vm_ref):
    m = mem_ref[0]
    r = lax.rsqrt(jnp.mean(m * m, axis=-1, keepdims=True) + EPS)
    mn = ((m * r) * g_ref[...]).astype(BF16)
    kv = jnp.dot(mn, w_ref[...], preferred_element_type=F32)
    mw = MEM_HEADS * MEM_HEAD_DIM
    for hd in range(MEM_HEADS):
        sl = slice(hd * MEM_HEAD_DIM, (hd + 1) * MEM_HEAD_DIM)
        kh = kv[:, sl]
        rk = lax.rsqrt(jnp.mean(kh * kh, axis=-1, keepdims=True) + EPS)
        km_ref[0, :, sl] = ((kh * rk) * kg_ref[...]).astype(km_ref.dtype)
    vm_ref[0] = kv[:, mw:].astype(vm_ref.dtype)


def _memkv(mem, g_mem, w_kv, k_gain):
    b, m, d = mem.shape
    mw = MEM_HEADS * MEM_HEAD_DIM
    return pl.pallas_call(
        _memkv_kernel,
        grid=(b,),
        in_specs=[
            pl.BlockSpec((1, m, d), lambda bi: (bi, 0, 0)),
            pl.BlockSpec((1, d), lambda bi: (0, 0)),
            pl.BlockSpec((d, 2 * mw), lambda bi: (0, 0)),
            pl.BlockSpec((1, MEM_HEAD_DIM), lambda bi: (0, 0)),
        ],
        out_specs=[pl.BlockSpec((1, m, mw), lambda bi: (bi, 0, 0)), pl.BlockSpec((1, m, mw), lambda bi: (bi, 0, 0))],
        out_shape=[jax.ShapeDtypeStruct((b, m, mw), BF16), jax.ShapeDtypeStruct((b, m, mw), BF16)],
        compiler_params=_params("parallel"),
        name="mem_kv",
    )(mem, g_mem.reshape(1, d), w_kv, k_gain.reshape(1, MEM_HEAD_DIM))


def _memattn_tile(h, w_ref, qg_ref, km_ref, vm_ref):
    qm = jnp.dot(h, w_ref[...], preferred_element_type=F32)
    scale = MEM_HEAD_DIM ** -0.5
    outs = []
    for hd in range(MEM_HEADS):
        sl = slice(hd * MEM_HEAD_DIM, (hd + 1) * MEM_HEAD_DIM)
        qh = qm[:, sl]
        r = lax.rsqrt(jnp.mean(qh * qh, axis=-1, keepdims=True) + EPS)
        qn = ((qh * r) * qg_ref[...]).astype(BF16)
        s = _nt_dot(qn, km_ref[0, :, sl]) * scale
        p = jnp.exp(s - jnp.max(s, axis=-1, keepdims=True))
        l = jnp.sum(p, axis=-1, keepdims=True)
        o = jnp.dot(p.astype(BF16), vm_ref[0, :, sl], preferred_element_type=F32)
        outs.append((o / l).astype(BF16))
    return jnp.concatenate(outs, axis=1)


def _merge_kernel(x_ref, h_ref, oa_ref, os_ref, wqm_ref, qg_ref, km_ref, vm_ref, wg_ref, wa_ref, ws_ref, wm_ref,
                  wo_ref, gf_ref, wr_ref, x1_ref, h2_ref, route_ref, cnt_ref):
    d = x_ref.shape[1]
    o_m = _memattn_tile(h_ref[...], wqm_ref, qg_ref, km_ref, vm_ref)
    gates = _sigmoid(jnp.dot(h_ref[...], wg_ref[...], preferred_element_type=F32))
    merged = gates[:, :d] * jnp.dot(oa_ref[...], wa_ref[...], preferred_element_type=F32)
    merged = merged + gates[:, d:2 * d] * jnp.dot(os_ref[...], ws_ref[...], preferred_element_type=F32)
    merged = merged + gates[:, 2 * d:] * jnp.dot(o_m, wm_ref[...], preferred_element_type=F32)
    x1 = x_ref[...] + jnp.dot(merged.astype(BF16), wo_ref[...], preferred_element_type=F32)
    x1_ref[...] = x1
    r = lax.rsqrt(jnp.mean(x1 * x1, axis=-1, keepdims=True) + EPS)
    h2 = (x1 * r) * gf_ref[...]
    h2_ref[...] = _rows_to_token_tiles(h2)

    h2_hi = h2.astype(BF16)
    h2_lo = (h2 - h2_hi.astype(F32)).astype(BF16)
    lg = (jnp.dot(h2_hi, wr_ref[0], preferred_element_type=F32) + jnp.dot(h2_hi, wr_ref[1], preferred_element_type=F32)
          + jnp.dot(h2_lo, wr_ref[0], preferred_element_type=F32))
    lanef = lax.broadcasted_iota(jnp.int32, lg.shape, 1).astype(F32)
    big = float(LANES)
    gmask = lanef < MOE_GROUPS
    gl = jnp.where(gmask, lg, -jnp.inf)
    gmax = jnp.max(gl, axis=-1, keepdims=True)
    p_g = 1.0 / jnp.sum(jnp.exp(gl - gmax), axis=-1, keepdims=True)
    g_sel = jnp.min(jnp.where(gl == gmax, lanef, big), axis=-1, keepdims=True)
    lo = MOE_GROUPS + MOE_EXPERTS_PER_GROUP * g_sel
    el = jnp.where((lanef >= lo) & (lanef < lo + MOE_EXPERTS_PER_GROUP), lg, -jnp.inf)
    m1 = jnp.max(el, axis=-1, keepdims=True)
    i1 = jnp.min(jnp.where(el == m1, lanef, big), axis=-1, keepdims=True)
    el2 = jnp.where(lanef == i1, -jnp.inf, el)
    m2 = jnp.max(el2, axis=-1, keepdims=True)
    i2 = jnp.min(jnp.where(el2 == m2, lanef, big), axis=-1, keepdims=True)
    e2 = jnp.exp(m2 - m1)
    w1 = 1.0 / (1.0 + e2)
    w2 = e2 / (1.0 + e2)
    route = jnp.where(lanef == 0, i1 - MOE_GROUPS, 0.0)
    route = jnp.where(lanef == 1, i2 - MOE_GROUPS, route)
    route = jnp.where(lanef == 2, p_g * w1, route)
    route = jnp.where(lanef == 3, p_g * w2, route)
    route_ref[...] = route

    @pl.when(pl.program_id(0) == 0)
    def _():
        cnt_ref[...] = jnp.zeros_like(cnt_ref)

    hits = jnp.where(lanef == i1 - MOE_GROUPS, 1.0, 0.0) + jnp.where(lanef == i2 - MOE_GROUPS, 1.0, 0.0)
    cnt_ref[...] += jnp.sum(hits.reshape(hits.shape[0] // SUBLANES, SUBLANES, LANES), axis=0)


def _merge(x2, h, o_a, o_s, w_qm, q_gain, km, vm, seq, w_gates, w_a, w_s, w_m, w_out, g_ffn, w_router, tm=512):
    n, d = x2.shape
    full = lambda arr: pl.BlockSpec(arr.shape, lambda i: (0,) * arr.ndim)
    tile = lambda arr: pl.BlockSpec((tm, arr.shape[1]), lambda i: (i, 0))
    tiles_per_seq = seq // tm
    per_batch = lambda arr: pl.BlockSpec((1,) + arr.shape[1:], lambda i: (i // tiles_per_seq, 0, 0))
    gf = g_ffn.reshape(1, d)
    qg = q_gain.reshape(1, MEM_HEAD_DIM)
    return pl.pallas_call(
        _merge_kernel,
        grid=(n // tm,),
        in_specs=[tile(x2), tile(h), tile(o_a), tile(o_s), full(w_qm), full(qg), per_batch(km), per_batch(vm),
                  full(w_gates), full(w_a), full(w_s), full(w_m), full(w_out), full(gf), full(w_router)],
        out_specs=[pl.BlockSpec((tm, d), lambda i: (i, 0)),
                   pl.BlockSpec((tm, d // LANES, LANES), lambda i: (i, 0, 0)),
                   pl.BlockSpec((tm, LANES), lambda i: (i, 0)),
                   pl.BlockSpec((SUBLANES, LANES), lambda i: (0, 0))],
        out_shape=[jax.ShapeDtypeStruct((n, d), F32), jax.ShapeDtypeStruct((n, d // LANES, LANES), F32),
                   jax.ShapeDtypeStruct((n, LANES), F32), jax.ShapeDtypeStruct((SUBLANES, LANES), F32)],
        compiler_params=_params("arbitrary"),
        name="merge_router",
    )(x2, h, o_a, o_s, w_qm, qg, km, vm, w_gates, w_a, w_s, w_m, w_out, gf, w_router)


def _moe_kernel(be_ref, j0_ref, nv_ref, tok_ref, dst_ref, h2_hbm, wg_ref, wu_ref, wd_ref, y2_hbm, xbuf, ybuf, gsem,
                ssem, wgb, wub, wdb):
    rows = xbuf.shape[1]
    n_tok = h2_hbm.shape[0]
    nblocks = pl.num_programs(0)
    i = pl.program_id(0)
    nbuf = xbuf.shape[0]
    slot = lax.rem(i, nbuf)
    slot_next = lax.rem(i + 1, nbuf)
    slot_prev = lax.rem(i + 2, nbuf)
    dump0 = MOE_TOPK * n_tok

    def block_rows(blk):
        b = jnp.clip(blk, 0, nblocks - 1)
        return j0_ref[b], jnp.where(blk < 0, -1, nv_ref[b] - 1)

    n_assign = tok_ref.shape[0]

    def gather_row(span, sl, r, prio):
        j0, _ = span
        tok = tok_ref[jnp.minimum(j0 + r, n_assign - 1)]
        pltpu.make_async_copy(h2_hbm.at[pl.ds(tok, 1)], xbuf.at[sl, pl.ds(r, 1)], gsem.at[sl]).start(priority=prio)

    def scatter_row(span, sl, r, prio):
        j0, last = span
        dst = jnp.where(r <= last, dst_ref[jnp.minimum(j0 + r, n_assign - 1)], dump0 + sl * rows + r)
        pltpu.make_async_copy(ybuf.at[sl, pl.ds(r, 1)], y2_hbm.at[pl.ds(dst, 1)], ssem.at[sl]).start(priority=prio)

    def looped(fn, blk, sl):
        span = block_rows(blk)

        def body(c, carry):
            for k in range(MOE_DMA_UNROLL):
                fn(span, sl, c * MOE_DMA_UNROLL + k, k % 2)
            return carry
        lax.fori_loop(0, rows // MOE_DMA_UNROLL, body, 0)

    def gather_wait(sl):
        pltpu.make_async_copy(h2_hbm.at[pl.ds(0, rows)], xbuf.at[sl], gsem.at[sl]).wait()

    def scatter_wait(sl):
        pltpu.make_async_copy(ybuf.at[sl], y2_hbm.at[pl.ds(0, rows)], ssem.at[sl]).wait()

    @pl.when(i == 0)
    def _():
        looped(gather_row, 0, 0)
        looped(gather_row, 1, 1)
        ybuf[...] = jnp.zeros_like(ybuf)
        for sl in range(2):
            pltpu.make_async_copy(ybuf.at[sl], y2_hbm.at[pl.ds(dump0 + sl * rows, rows)], ssem.at[sl]).start()

    @pl.when((i == 0) | (be_ref[i] != be_ref[jnp.maximum(i - 1, 0)]))
    def _():
        wgb[...] = wg_ref[0].astype(BF16)
        wub[...] = wu_ref[0].astype(BF16)
        wdb[...] = wd_ref[0].astype(BF16)

    gather_wait(slot)
    scatter_wait(slot)
    nxt, prv = block_rows(i + 2), block_rows(i - 1)
    for r in range(rows):
        scatter_row(prv, slot_prev, r, r % 2)
    x = _token_tiles_to_rows(xbuf[slot]).astype(BF16)
    for r in range(rows):
        gather_row(nxt, slot_prev, r, r % 2)
    gate = jnp.dot(x, wgb[...], preferred_element_type=F32)
    up = jnp.dot(x, wub[...], preferred_element_type=F32)
    hid = (gate * _sigmoid(gate)) * up
    y = jnp.dot(hid.astype(BF16), wdb[...], preferred_element_type=F32)
    ybuf[slot] = _rows_to_token_tiles(y)

    @pl.when(i == nblocks - 1)
    def _():
        gather_wait(slot_next)
        gather_wait(slot_prev)
        scatter_wait(slot_next)
        scatter_wait(slot_prev)
        looped(scatter_row, i, slot)
        scatter_wait(slot)


def _moe_experts(h2, tables, w_gate, w_up, w_down):
    n = h2.shape[0]
    e, d, f = w_gate.shape
    nblocks = tables[0].shape[0]
    wmap = lambda i, be, j0, nv, tok, dst: (be[i], 0, 0)
    grid_spec = pltpu.PrefetchScalarGridSpec(
        num_scalar_prefetch=5,
        grid=(nblocks,),
        in_specs=[
            pl.BlockSpec(memory_space=pl.ANY),
            pl.BlockSpec((1, d, f), wmap),
            pl.BlockSpec((1, d, f), wmap),
            pl.BlockSpec((1, f, d), wmap),
        ],
        out_specs=pl.BlockSpec(memory_space=pl.ANY),
        scratch_shapes=[
            pltpu.VMEM((MOE_BUFFERS, MOE_ROWS, d // LANES, LANES), F32),
            pltpu.VMEM((MOE_BUFFERS, MOE_ROWS, d // LANES, LANES), F32),
            pltpu.SemaphoreType.DMA((MOE_BUFFERS,)),
            pltpu.SemaphoreType.DMA((MOE_BUFFERS,)),
            pltpu.VMEM((d, f), BF16), pltpu.VMEM((d, f), BF16), pltpu.VMEM((f, d), BF16),
        ],
    )
    return pl.pallas_call(
        _moe_kernel,
        grid_spec=grid_spec,
        out_shape=jax.ShapeDtypeStruct((n * MOE_TOPK + MOE_BUFFERS * MOE_ROWS, d // LANES, LANES), F32),
        compiler_params=_params("arbitrary"),
        name="moe_experts",
    )(*tables, h2, w_gate, w_up, w_down)


def _combine_kernel(x1_ref, y0_ref, y1_ref, route_ref, o_ref):
    route = route_ref[...]
    w0 = route[:, 2:3]
    w1 = route[:, 3:4]
    o_ref[...] = x1_ref[...] + (w0 * _token_tiles_to_rows(y0_ref[...]) + w1 * _token_tiles_to_rows(y1_ref[...]))


def _combine(x1, y2, route, tm=512):
    n, d = x1.shape
    tiles = n // tm
    ytile = (tm, d // LANES, LANES)
    return pl.pallas_call(
        _combine_kernel,
        grid=(tiles,),
        in_specs=[pl.BlockSpec((tm, d), lambda i: (i, 0)), pl.BlockSpec(ytile, lambda i: (i, 0, 0)),
                  pl.BlockSpec(ytile, lambda i: (tiles + i, 0, 0)), pl.BlockSpec((tm, LANES), lambda i: (i, 0))],
        out_specs=pl.BlockSpec((tm, d), lambda i: (i, 0)),
        out_shape=jax.ShapeDtypeStruct((n, d), F32),
        compiler_params=_params("parallel"),
        name="moe_combine",
    )(x1, y2, y2, route)


def _dispatch_tables(route, expert_counts, n_tok):
    n_assign = n_tok * MOE_TOPK
    e_flat = route[:, :MOE_TOPK].astype(jnp.int32).reshape(n_assign)
    order = jnp.argsort(e_flat).astype(jnp.int32)
    counts = jnp.sum(expert_counts, axis=0)[:MOE_EXPERTS].astype(jnp.int32)
    blocks_per_expert = (counts + MOE_ROWS - 1) // MOE_ROWS
    blk_end = jnp.cumsum(blocks_per_expert)
    raw_start = jnp.cumsum(counts) - counts
    nblocks = n_assign // MOE_ROWS + MOE_EXPERTS
    blk = jnp.arange(nblocks, dtype=jnp.int32)
    blk_expert = jnp.minimum(jnp.sum(blk_end[None, :] <= blk[:, None], axis=1), MOE_EXPERTS - 1).astype(jnp.int32)
    within = blk - (blk_end - blocks_per_expert)[blk_expert]
    blk_nvalid = jnp.clip(counts[blk_expert] - within * MOE_ROWS, 0, MOE_ROWS).astype(jnp.int32)
    blk_j0 = jnp.where(blk_nvalid > 0, raw_start[blk_expert] + within * MOE_ROWS, 0).astype(jnp.int32)
    tok_sorted = order // MOE_TOPK
    dst_sorted = (order % MOE_TOPK) * n_tok + tok_sorted
    return blk_expert, blk_j0, blk_nvalid, tok_sorted, dst_sorted


def kernel(x, mem, g_mix, w_in, moba_q_norm, moba_k_norm, conv_w, conv_b, dt_bias, a_log, d_skip, ssd_norm, g_mem,
           w_mem_kv, mem_q_norm, mem_k_norm, w_o_moba, w_o_ssd, w_o_mem, w_out, g_ffn, w_router_group,
           w_router_expert, w_gate, w_up, w_down):
    bsz, seq, d = x.shape
    assert seq % MOBA_BLOCK == 0 and seq % SSD_CHUNK == 0
    n_tok = bsz * seq
    moba_w = MOBA_HEADS * MOBA_HEAD_DIM
    d_in = SSD_HEADS * SSD_HEAD_DIM
    xbc_w = d_in + 2 * SSD_GROUPS * SSD_STATE
    mem_w = MEM_HEADS * MEM_HEAD_DIM
    sizes = (moba_w, moba_w, moba_w, d_in, xbc_w, SSD_HEADS, mem_w, 3 * d)
    offs = [0]
    for sz in sizes:
        offs.append(offs[-1] + sz)
    w_in_b = w_in.astype(BF16)
    w_qk = w_in_b[:, offs[0]:offs[2]]
    w_v = w_in_b[:, offs[2]:offs[3]]
    w_z = w_in_b[:, offs[3]:offs[4]]
    w_xbc = w_in_b[:, offs[4]:offs[5]]
    w_dt = jnp.pad(w_in_b[:, offs[5]:offs[6]], ((0, 0), (0, LANES - SSD_HEADS)))
    w_qm = w_in_b[:, offs[6]:offs[7]]
    w_gates = w_in_b[:, offs[7]:offs[8]]

    x2 = x.reshape(n_tok, d)

    half = MOBA_HEAD_DIM // 2
    inv = ROPE_THETA ** (-jnp.arange(half, dtype=F32) / half)
    ang = jnp.arange(seq, dtype=F32)[:, None] * inv[None, :]
    cos_t = jnp.tile(jnp.cos(ang), (1, LANES // half))
    sin_t = jnp.tile(jnp.concatenate([-jnp.sin(ang), jnp.sin(ang)], axis=1), (1, LANES // MOBA_HEAD_DIM))
    gains = jnp.stack([jnp.tile(moba_q_norm, MOBA_HEADS), jnp.tile(moba_k_norm, MOBA_HEADS)]).reshape(2, 1, moba_w)

    h, qk, v, zs, xbc, dt = _inproj(x2, g_mix, w_qk, w_v, w_z, w_xbc, w_dt, gains, cos_t, sin_t, conv_w, conv_b,
                                    dt_bias, seq)

    qk3 = qk.reshape(bsz, seq, 2 * moba_w)
    v3 = v.reshape(bsz, seq, moba_w)
    nb = seq // MOBA_BLOCK
    assert nb <= MOBA_MAX_BLOCKS and nb % MOBA_UNROLL == 0
    ka, vt, qa = _moba_prep(qk3, v3)
    o_a = _moba_attention(qa, ka, vt).reshape(n_tok, moba_w)

    o_s = _ssd(xbc.reshape(bsz, seq, xbc_w), zs.reshape(bsz, seq, d_in), dt.reshape(bsz, seq, LANES), a_log, d_skip,
               ssd_norm).reshape(n_tok, d_in)

    km, vm = _memkv(mem, g_mem, w_mem_kv.astype(BF16), mem_k_norm)

    w_router = jnp.pad(jnp.concatenate([w_router_group, w_router_expert], axis=1),
                       ((0, 0), (0, LANES - MOE_GROUPS - MOE_EXPERTS)))
    w_router_hi = w_router.astype(BF16)
    w_router = jnp.stack([w_router_hi, (w_router - w_router_hi.astype(F32)).astype(BF16)])
    x1, h2, route, expert_counts = _merge(x2, h, o_a, o_s, w_qm, mem_q_norm, km, vm, seq, w_gates,
                                          w_o_moba.astype(BF16), w_o_ssd.astype(BF16), w_o_mem.astype(BF16),
                                          w_out.astype(BF16), g_ffn, w_router)

    tables = _dispatch_tables(route, expert_counts, n_tok)
    y2 = _moe_experts(h2, tables, w_gate, w_up, w_down)
    out = _combine(x1, y2, route)
    return out.reshape(bsz, seq, d)
```
